```python
import jax, jax.numpy as jnp
from jax import lax
import numpy as np

D_MODEL = 2048
BATCH = 8
SEQ = 2048
DEPTH = 1

N_META = 16
D_MIX = D_MODEL
HEAD_DIM = 64
D_ATTN = D_MIX // 2
N_Q_HEADS = D_ATTN // HEAD_DIM
N_KV_HEADS = 4
D_KV = N_KV_HEADS * HEAD_DIM
WINDOW = 128
BLOCK = 128
ROPE_THETA = 10000.0
D_LRU = D_MIX - D_ATTN
N_LRU_BLOCKS = 16
LRU_BLOCK = D_LRU // N_LRU_BLOCKS
CONV_WIDTH = 4
LRU_C = 8.0
D_IN_PROJ = D_ATTN + 2 * D_KV + 2 * D_LRU
N_EXPERTS = 32
TOP_K = 4
D_FF = D_MODEL
SWIGLU_LIMIT = 7.0
SWIGLU_ALPHA = 1.702
NORM_EPS = 1e-5
NEG_INF = -1e30

kernel_name = "hymba_swa_sink_rglru_moe"


def rms_norm(x, g):
    xf = x.astype(jnp.float32)
    y = xf * lax.rsqrt(jnp.mean(xf * xf, axis=-1, keepdims=True) + NORM_EPS)
    return (y * g.astype(jnp.float32)).astype(x.dtype)


def rotary(x, pos):
    half = HEAD_DIM // 2
    inv = 1.0 / (ROPE_THETA ** (jnp.arange(half, dtype=jnp.float32) / half))
    ang = pos.astype(jnp.float32)[:, None] * inv[None, :]
    cos = jnp.cos(ang)[None, :, None, :]
    sin = jnp.sin(ang)[None, :, None, :]
    xf = x.astype(jnp.float32)
    x1, x2 = xf[..., :half], xf[..., half:]
    out = jnp.concatenate([x1 * cos - x2 * sin, x2 * cos + x1 * sin], axis=-1)
    return out.astype(x.dtype)


def sliding_window_attention(q, k, v, sinks):
    B, T = q.shape[0], q.shape[1]
    pad_l = (-N_META) % BLOCK
    pad_r = (-(T + pad_l)) % BLOCK
    Tp = T + pad_l + pad_r
    nb = Tp // BLOCK
    G = N_Q_HEADS // N_KV_HEADS
    padw = ((0, 0), (pad_l, pad_r), (0, 0), (0, 0))
    pos = jnp.arange(Tp) - pad_l
    q = rotary(jnp.pad(q, padw), pos)
    k = rotary(jnp.pad(k, padw), pos)
    v = jnp.pad(v, padw)

    qb = q.reshape(B, nb, BLOCK, N_KV_HEADS, G, HEAD_DIM)
    kb = k.reshape(B, nb, BLOCK, N_KV_HEADS, HEAD_DIM)
    vb = v.reshape(B, nb, BLOCK, N_KV_HEADS, HEAD_DIM)
    shift = lambda t: jnp.concatenate([jnp.zeros_like(t[:, :1]), t[:, :-1]], axis=1)
    k_band = jnp.concatenate([shift(kb), kb], axis=2)
    v_band = jnp.concatenate([shift(vb), vb], axis=2)
    k_meta = k[:, pad_l:pad_l + N_META]
    v_meta = v[:, pad_l:pad_l + N_META]

    scale = HEAD_DIM ** -0.5
    s_band = jnp.einsum('bnqhgd,bnkhd->bnhgqk', qb, k_band,
                        preferred_element_type=jnp.float32) * scale
    s_meta = jnp.einsum('bnqhgd,bmhd->bnhgqm', qb, k_meta,
                        preferred_element_type=jnp.float32) * scale

    qi = jnp.arange(nb)[:, None] * BLOCK + jnp.arange(BLOCK)[None, :]
    ki = (jnp.arange(nb)[:, None] - 1) * BLOCK + jnp.arange(2 * BLOCK)[None, :]
    diff = qi[:, :, None] - ki[:, None, :]
    band_ok = (diff >= 0) & (diff < WINDOW) & (ki[:, None, :] >= pad_l + N_META)
    meta_ok = (pad_l + jnp.arange(N_META))[None, None, :] <= qi[:, :, None]
    s_band = jnp.where(band_ok[None, :, None, None], s_band, NEG_INF)
    s_meta = jnp.where(meta_ok[None, :, None, None], s_meta, NEG_INF)

    sink = jnp.broadcast_to(
        sinks.astype(jnp.float32).reshape(N_KV_HEADS, G)[None, None, :, :, None, None],
        s_band.shape[:-1] + (1,))
    p = jax.nn.softmax(jnp.concatenate([sink, s_meta, s_band], axis=-1), axis=-1)
    p_meta = p[..., 1:1 + N_META].astype(v.dtype)
    p_band = p[..., 1 + N_META:].astype(v.dtype)
    o = (jnp.einsum('bnhgqm,bmhd->bnqhgd', p_meta, v_meta)
         + jnp.einsum('bnhgqk,bnkhd->bnqhgd', p_band, v_band))
    o = o.reshape(B, Tp, D_ATTN)
    return o[:, pad_l:pad_l + T]


def rg_lru_branch(xr, gate, conv_w, conv_b, w_a, b_a, w_i, b_i, lam):
    B, T, C = xr.shape
    xc = lax.conv_general_dilated(
        xr, conv_w[:, None, :].astype(xr.dtype), window_strides=(1,),
        padding=((CONV_WIDTH - 1, 0),), dimension_numbers=('NWC', 'WIO', 'NWC'),
        feature_group_count=C) + conv_b
    xh = xc.reshape(B, T, N_LRU_BLOCKS, LRU_BLOCK)
    r = jax.nn.sigmoid((jnp.einsum('btnc,ncd->btnd', xh, w_a).reshape(B, T, C) + b_a).astype(jnp.float32))
    i = jax.nn.sigmoid((jnp.einsum('btnc,ncd->btnd', xh, w_i).reshape(B, T, C) + b_i).astype(jnp.float32))
    log_a = -LRU_C * r * jax.nn.softplus(-lam.astype(jnp.float32))
    a = jnp.exp(log_a)
    mult = jnp.sqrt(-jnp.expm1(2.0 * log_a))
    b = mult * i * xc.astype(jnp.float32)

    def combine(e1, e2):
        a1, b1 = e1
        a2, b2 = e2
        return a1 * a2, a2 * b1 + b2

    _, h = lax.associative_scan(combine, (a, b), axis=1)
    y = h * jax.nn.gelu(gate.astype(jnp.float32))
    return y.astype(xr.dtype)


def hybrid_mixer(h, w_in, b_in, sinks, conv_w, conv_b, w_a, b_a, w_i, b_i, lam,
                 g_attn_out, g_lru_out, w_out, b_out):
    B, T, _ = h.shape
    z = h @ w_in + b_in
    q, k, v, xr, gt = jnp.split(
        z, [D_ATTN, D_ATTN + D_KV, D_ATTN + 2 * D_KV, D_ATTN + 2 * D_KV + D_LRU], axis=-1)
    q = q.reshape(B, T, N_Q_HEADS, HEAD_DIM)
    k = k.reshape(B, T, N_KV_HEADS, HEAD_DIM)
    v = v.reshape(B, T, N_KV_HEADS, HEAD_DIM)
    attn = sliding_window_attention(q, k, v, sinks)
    lru = rg_lru_branch(xr, gt, conv_w, conv_b, w_a, b_a, w_i, b_i, lam)
    o = jnp.concatenate([rms_norm(attn, g_attn_out), rms_norm(lru, g_lru_out)], axis=-1)
    return o @ w_out + b_out


def moe_ffn(h, w_router, b_router, w_gate, b_gate, w_up, b_up, w_down, b_down):
    B, T, D = h.shape
    xf = h.reshape(-1, D)
    N = xf.shape[0]
    logits = (xf @ w_router + b_router).astype(jnp.float32)
    top_v, top_e = lax.top_k(logits, TOP_K)
    wts = jax.nn.softmax(top_v, axis=-1)
    e_flat = top_e.reshape(-1)
    order = jnp.argsort(e_flat)
    e_sorted = e_flat[order]
    tok = order // TOP_K
    xs = xf[tok]
    sizes = jnp.bincount(e_flat, length=N_EXPERTS).astype(jnp.int32)
    g = lax.ragged_dot(xs, w_gate, sizes) + b_gate[e_sorted]
    u = lax.ragged_dot(xs, w_up, sizes) + b_up[e_sorted]
    g = jnp.minimum(g, SWIGLU_LIMIT)
    u = jnp.clip(u, -SWIGLU_LIMIT, SWIGLU_LIMIT)
    act = g * jax.nn.sigmoid(SWIGLU_ALPHA * g) * (u + 1.0)
    y = lax.ragged_dot(act, w_down, sizes) + b_down[e_sorted]
    y = y * wts.reshape(-1)[order][:, None].astype(y.dtype)
    out = jax.ops.segment_sum(y, tok, num_segments=N)
    return out.reshape(B, T, D)


def setup_inputs(seed: int = 0) -> dict:
    key = jax.random.key(seed)
    ks = jax.random.split(key, 32)
    f32 = jnp.float32
    nrm = lambda k, shape, s: jax.random.normal(k, shape, f32) * s
    L = DEPTH
    u = jax.random.uniform(ks[12], (L, D_LRU), f32, 0.81, 0.998)
    s = u ** (1.0 / LRU_C)
    lru_lambda = jnp.log(s) - jnp.log1p(-s)
    return {
        "x": nrm(ks[0], (BATCH, SEQ, D_MODEL), 1.0),
        "meta_tokens": nrm(ks[1], (N_META, D_MODEL), 1.0),
        "norm_mix": 1.0 + nrm(ks[2], (L, D_MODEL), 0.02),
        "w_in": nrm(ks[3], (L, D_MODEL, D_IN_PROJ), D_MODEL ** -0.5),
        "b_in": nrm(ks[4], (L, D_IN_PROJ), 0.01),
        "sinks": nrm(ks[5], (L, N_Q_HEADS), 0.5),
        "conv_w": nrm(ks[6], (L, CONV_WIDTH, D_LRU), CONV_WIDTH ** -0.5),
        "conv_b": nrm(ks[7], (L, D_LRU), 0.01),
        "w_a": nrm(ks[8], (L, N_LRU_BLOCKS, LRU_BLOCK, LRU_BLOCK), LRU_BLOCK ** -0.5),
        "b_a": nrm(ks[9], (L, D_LRU), 0.01),
        "w_i": nrm(ks[10], (L, N_LRU_BLOCKS, LRU_BLOCK, LRU_BLOCK), LRU_BLOCK ** -0.5),
        "b_i": nrm(ks[11], (L, D_LRU), 0.01),
        "lru_lambda": lru_lambda,
        "g_attn_out": 1.0 + nrm(ks[13], (L, D_ATTN), 0.02),
        "g_lru_out": 1.0 + nrm(ks[14], (L, D_LRU), 0.02),
        "w_out": nrm(ks[15], (L, D_MIX, D_MODEL), D_MIX ** -0.5),
        "b_out": nrm(ks[16], (L, D_MODEL), 0.01),
        "norm_ffn": 1.0 + nrm(ks[17], (L, D_MODEL), 0.02),
        "w_router": nrm(ks[18], (L, D_MODEL, N_EXPERTS), D_MODEL ** -0.5),
        "b_router": nrm(ks[19], (L, N_EXPERTS), 0.01),
        "w_gate": nrm(ks[20], (L, N_EXPERTS, D_MODEL, D_FF), D_MODEL ** -0.5),
        "b_gate": nrm(ks[21], (L, N_EXPERTS, D_FF), 0.01),
        "w_up": nrm(ks[22], (L, N_EXPERTS, D_MODEL, D_FF), D_MODEL ** -0.5),
        "b_up": nrm(ks[23], (L, N_EXPERTS, D_FF), 0.01),
        "w_down": nrm(ks[24], (L, N_EXPERTS, D_FF, D_MODEL), D_FF ** -0.5),
        "b_down": nrm(ks[25], (L, N_EXPERTS, D_MODEL), 0.01),
        "final_norm": 1.0 + nrm(ks[26], (D_MODEL,), 0.02),
    }


def reference(x, meta_tokens, norm_mix, w_in, b_in, sinks, conv_w, conv_b, w_a, b_a,
              w_i, b_i, lru_lambda, g_attn_out, g_lru_out, w_out, b_out, norm_ffn,
              w_router, b_router, w_gate, b_gate, w_up, b_up, w_down, b_down, final_norm):
    B = x.shape[0]
    meta = jnp.broadcast_to(meta_tokens[None].astype(x.dtype), (B, N_META, x.shape[-1]))
    h = jnp.concatenate([meta, x], axis=1)
    for l in range(DEPTH):
        h = h + hybrid_mixer(rms_norm(h, norm_mix[l]), w_in[l], b_in[l], sinks[l],
                             conv_w[l], conv_b[l], w_a[l], b_a[l], w_i[l], b_i[l],
                             lru_lambda[l], g_attn_out[l], g_lru_out[l], w_out[l], b_out[l])
        h = h + moe_ffn(rms_norm(h, norm_ffn[l]), w_router[l], b_router[l], w_gate[l],
                        b_gate[l], w_up[l], b_up[l], w_down[l], b_down[l])
    h = rms_norm(h, final_norm)
    return h[:, N_META:]
```

```python
import jax
import jax.numpy as jnp
from jax import lax
from jax.experimental import pallas as pl
from jax.experimental.pallas import tpu as pltpu

F32 = jnp.float32
BF16 = jnp.bfloat16

N_META = 16
HEAD_DIM = 64
N_Q_HEADS = 16
N_KV_HEADS = 4
GROUP = N_Q_HEADS // N_KV_HEADS
D_ATTN = N_Q_HEADS * HEAD_DIM
D_KV = N_KV_HEADS * HEAD_DIM
BLOCK = 128
ROPE_THETA = 10000.0
D_LRU = 1024
LRU_BLOCK = 64
LRU_GROUP = 256
CONV_WIDTH = 4
LRU_C = 8.0
N_EXPERTS = 32
TOP_K = 4
SWIGLU_LIMIT = 7.0
SWIGLU_ALPHA = 1.702
NORM_EPS = 1e-5
NEG_INF = -1e30

LANES = 128
SUB_ROWS = 256
MOE_TM = 2304
VMEM_LIMIT = 56 * 1024 * 1024


def _cparams(sem):
    return pltpu.CompilerParams(dimension_semantics=sem, vmem_limit_bytes=VMEM_LIMIT)


def _rms(x, g):
    return x * lax.rsqrt(jnp.mean(x * x, axis=-1, keepdims=True) + NORM_EPS) * g


def _inproj_kernel(x_ref, g_ref, w_ref, b_ref, cos_ref, sin_ref, q_ref, kv_ref, xg_ref):
    xn = _rms(x_ref[...], g_ref[...]).astype(BF16)
    cos = cos_ref[...]
    sin = sin_ref[...]
    lane = lax.broadcasted_iota(jnp.int32, (1, LANES), 1)
    first_half = (lane % HEAD_DIM) < (HEAD_DIM // 2)

    def rope(z):
        partner = jnp.where(first_half, pltpu.roll(z, LANES - HEAD_DIM // 2, 1),
                            pltpu.roll(z, HEAD_DIM // 2, 1))
        return z * cos + partner * sin

    cw = 512
    for c in range(w_ref.shape[1] // cw):
        z = jnp.dot(xn, w_ref[:, c * cw:(c + 1) * cw], preferred_element_type=F32)
        z = z + b_ref[:, c * cw:(c + 1) * cw]
        if c < 2:
            for j in range(cw // LANES):
                zz = rope(z[:, j * LANES:(j + 1) * LANES]) * (HEAD_DIM ** -0.5)
                q_ref[:, c * cw + j * LANES:c * cw + (j + 1) * LANES] = zz.astype(BF16)
        elif c == 2:
            for j in range(D_KV // LANES):
                kv_ref[:, j * LANES:(j + 1) * LANES] = rope(z[:, j * LANES:(j + 1) * LANES]).astype(BF16)
            kv_ref[:, D_KV:] = z[:, D_KV:].astype(BF16)
        else:
            xg_ref[:, (c - 3) * cw:(c - 2) * cw] = z


def _inproj(x2d, g, w_bf, b, cos, sin, tm, pos_blocks):
    rows, d = x2d.shape
    dz = w_bf.shape[1]
    return pl.pallas_call(
        _inproj_kernel,
        grid=(rows // tm,),
        in_specs=[
            pl.BlockSpec((tm, d), lambda m: (m, 0)),
            pl.BlockSpec((1, d), lambda m: (0, 0)),
            pl.BlockSpec((d, dz), lambda m: (0, 0)),
            pl.BlockSpec((1, dz), lambda m: (0, 0)),
            pl.BlockSpec((tm, LANES), lambda m: (m % pos_blocks, 0)),
            pl.BlockSpec((tm, LANES), lambda m: (m % pos_blocks, 0)),
        ],
        out_specs=[
            pl.BlockSpec((tm, D_ATTN), lambda m: (m, 0)),
            pl.BlockSpec((tm, 2 * D_KV), lambda m: (m, 0)),
            pl.BlockSpec((tm, 2 * D_LRU), lambda m: (m, 0)),
        ],
        out_shape=[
            jax.ShapeDtypeStruct((rows, D_ATTN), BF16),
            jax.ShapeDtypeStruct((rows, 2 * D_KV), BF16),
            jax.ShapeDtypeStruct((rows, 2 * D_LRU), F32),
        ],
        compiler_params=_cparams(("arbitrary",)),
        name="inproj",
    )(x2d, g, w_bf, b, cos, sin)


def _attn_kernel(sink_ref, q_ref, kvc_ref, kvp_ref, kvm_ref, g_ref, o_ref):
    n = pl.program_id(1)
    row = lax.broadcasted_iota(jnp.int32, (GROUP * BLOCK, BLOCK), 0) % BLOCK
    col = lax.broadcasted_iota(jnp.int32, (GROUP * BLOCK, BLOCK), 1)
    ok_cur = col <= row
    ok_prev = jnp.logical_and(col > row, n > 0)
    q = q_ref[...]
    kvc = kvc_ref[...]
    kvp = kvp_ref[...]
    kvm = kvm_ref[...]
    nt = (((1,), (1,)), ((), ()))
    outs = []
    for h in range(N_KV_HEADS):
        ks = slice(h * HEAD_DIM, (h + 1) * HEAD_DIM)
        vs = slice(D_KV + h * HEAD_DIM, D_KV + (h + 1) * HEAD_DIM)
        qh = jnp.concatenate(
            [q[:, (h * GROUP + g) * HEAD_DIM:(h * GROUP + g + 1) * HEAD_DIM] for g in range(GROUP)], axis=0)
        s_c = jnp.where(ok_cur, lax.dot_general(qh, kvc[:, ks], nt, preferred_element_type=F32), NEG_INF)
        s_p = jnp.where(ok_prev, lax.dot_general(qh, kvp[:, ks], nt, preferred_element_type=F32), NEG_INF)
        s_m = lax.dot_general(qh, kvm[:, ks], nt, preferred_element_type=F32)
        sink = jnp.concatenate(
            [jnp.full((BLOCK, 1), sink_ref[h * GROUP + g], F32) for g in range(GROUP)], axis=0)
        m = jnp.maximum(jnp.maximum(jnp.max(s_c, axis=-1, keepdims=True), jnp.max(s_p, axis=-1, keepdims=True)),
                        jnp.maximum(jnp.max(s_m, axis=-1, keepdims=True), sink))
        p_c = jnp.exp(s_c - m)
        p_p = jnp.exp(s_p - m)
        p_m = jnp.exp(s_m - m)
        den = (jnp.sum(p_c, axis=-1, keepdims=True) + jnp.sum(p_p, axis=-1, keepdims=True)
               + jnp.sum(p_m, axis=-1, keepdims=True) + jnp.exp(sink - m))
        o = (jnp.dot(p_c.astype(BF16), kvc[:, vs], preferred_element_type=F32)
             + jnp.dot(p_p.astype(BF16), kvp[:, vs], preferred_element_type=F32)
             + jnp.dot(p_m.astype(BF16), kvm[:, vs], preferred_element_type=F32))
        o = o / den
        outs.extend(o[g * BLOCK:(g + 1) * BLOCK] for g in range(GROUP))
    o_all = jnp.concatenate(outs, axis=1)
    o_ref[...] = _rms(o_all, g_ref[...]).astype(BF16)


def _attention(sinks, q, kv, kvm, g_attn, batch, seq):
    nb = seq // BLOCK
    return pl.pallas_call(
        _attn_kernel,
        grid_spec=pltpu.PrefetchScalarGridSpec(
            num_scalar_prefetch=1,
            grid=(batch, nb),
            in_specs=[
                pl.BlockSpec((BLOCK, D_ATTN), lambda b, n, s: (b * nb + n, 0)),
                pl.BlockSpec((BLOCK, 2 * D_KV), lambda b, n, s: (b * nb + n, 0)),
                pl.BlockSpec((BLOCK, 2 * D_KV), lambda b, n, s: (b * nb + jnp.maximum(n - 1, 0), 0)),
                pl.BlockSpec((N_META, 2 * D_KV), lambda b, n, s: (0, 0)),
                pl.BlockSpec((1, D_ATTN), lambda b, n, s: (0, 0)),
            ],
            out_specs=pl.BlockSpec((BLOCK, D_ATTN), lambda b, n, s: (b * nb + n, 0)),
        ),
        out_shape=jax.ShapeDtypeStruct((batch * seq, D_ATTN), BF16),
        compiler_params=_cparams(("arbitrary", "arbitrary")),
        name="attention",
    )(sinks, q, kv, kv, kvm, g_attn)


def _expm1(y):
    p = 1.0 + y * (1.0 / 8.0)
    for k in range(7, 1, -1):
        p = 1.0 + y * p * (1.0 / k)
    return jnp.where(y > -0.25, y * p, jnp.exp(y) - 1.0)


def _lru_kernel(xg_ref, cw_ref, cb_ref, wa_ref, ba_ref, wi_ref, bi_ref, lam_ref, g_ref, h0_ref, tail0_ref,
                o_ref, hout_ref, tailout_ref, ext_ref, a_ref, b_ref, h_ref):
    tt = pl.program_id(1)
    rows = a_ref.shape[0]

    @pl.when(tt == 0)
    def _():
        h_ref[...] = h0_ref[...]
        ext_ref[0:8, :] = tail0_ref[...]

    ext_ref[8:, :] = xg_ref[:, :D_LRU]
    xc = cb_ref[...] + sum(cw_ref[j:j + 1, :] * ext_ref[5 + j:5 + j + rows, :] for j in range(CONV_WIDTH))
    xcb = xc.astype(BF16)
    sp = jax.nn.softplus(-lam_ref[...])
    for c in range(D_LRU // LRU_GROUP):
        cs = slice(c * LRU_GROUP, (c + 1) * LRU_GROUP)
        r = jax.nn.sigmoid(jnp.dot(xcb[:, cs], wa_ref[c], preferred_element_type=F32) + ba_ref[:, cs])
        i = jax.nn.sigmoid(jnp.dot(xcb[:, cs], wi_ref[c], preferred_element_type=F32) + bi_ref[:, cs])
        log_a = -LRU_C * r * sp[:, cs]
        a_ref[:, cs] = jnp.exp(log_a)
        b_ref[:, cs] = jnp.sqrt(-_expm1(2.0 * log_a)) * i * xc[:, cs]

    def step(t, h):
        h = a_ref[pl.ds(t, 1), :] * h + b_ref[pl.ds(t, 1), :]
        b_ref[pl.ds(t, 1), :] = h
        return h

    h_last = lax.fori_loop(0, rows, step, h_ref[...])
    h_ref[...] = h_last
    hout_ref[...] = h_last
    tail = ext_ref[rows:rows + 8, :]
    ext_ref[0:8, :] = tail
    tailout_ref[...] = tail
    y = b_ref[...] * jax.nn.gelu(xg_ref[:, D_LRU:])
    o_ref[...] = _rms(y, g_ref[...]).astype(BF16)


def _lru(xg, conv_w, conv_b, wa_bd, b_a, wi_bd, b_i, lam, g_lru, h0, tail0, batch, seq, tt):
    ntt = seq // tt
    vec = pl.BlockSpec((1, D_LRU), lambda b, t: (0, 0))
    wspec = pl.BlockSpec((D_LRU // LRU_GROUP, LRU_GROUP, LRU_GROUP), lambda b, t: (0, 0, 0))
    return pl.pallas_call(
        _lru_kernel,
        grid=(batch, ntt),
        in_specs=[
            pl.BlockSpec((tt, 2 * D_LRU), lambda b, t: (b * ntt + t, 0)),
            pl.BlockSpec((CONV_WIDTH, D_LRU), lambda b, t: (0, 0)),
            vec, wspec, vec, wspec, vec, vec, vec, vec,
            pl.BlockSpec((8, D_LRU), lambda b, t: (0, 0)),
        ],
        out_specs=[
            pl.BlockSpec((tt, D_LRU), lambda b, t: (b * ntt + t, 0)),
            pl.BlockSpec((1, D_LRU), lambda b, t: (0, 0)),
            pl.BlockSpec((8, D_LRU), lambda b, t: (0, 0)),
        ],
        out_shape=[
            jax.ShapeDtypeStruct((batch * seq, D_LRU), BF16),
            jax.ShapeDtypeStruct((1, D_LRU), F32),
            jax.ShapeDtypeStruct((8, D_LRU), F32),
        ],
        scratch_shapes=[
            pltpu.VMEM((8 + tt, D_LRU), F32),
            pltpu.VMEM((tt, D_LRU), F32),
            pltpu.VMEM((tt, D_LRU), F32),
            pltpu.VMEM((1, D_LRU), F32),
        ],
        compiler_params=_cparams(("arbitrary", "arbitrary")),
        name="rglru",
    )(xg, conv_w, conv_b, wa_bd, b_a, wi_bd, b_i, lam, g_lru, h0, tail0)


def _outproj_kernel(a_ref, l_ref, wo_ref, bo_ref, x_ref, gf_ref, wr_ref, br_ref, tri_ref,
                    h_ref, xp_ref, ids_ref, wts_ref, rank_ref, cnt_ref, carry_ref):
    m = pl.program_id(0)
    tm = a_ref.shape[0]

    @pl.when(m == 0)
    def _():
        carry_ref[...] = jnp.zeros_like(carry_ref)

    h = (jnp.dot(a_ref[...], wo_ref[:D_ATTN, :], preferred_element_type=F32)
         + jnp.dot(l_ref[...], wo_ref[D_ATTN:, :], preferred_element_type=F32)
         + bo_ref[...] + x_ref[...])
    h_ref[...] = h
    xn = _rms(h, gf_ref[...]).astype(BF16)
    half = xn.shape[1] // 2
    lo = pltpu.bitcast(xn[:, :half].astype(F32), jnp.uint32)
    hi = pltpu.bitcast(xn[:, half:].astype(F32), jnp.uint32)
    packed = (hi & jnp.uint32(0xFFFF0000)) | (lo >> 16)
    for s in range(half // LANES):
        xp_ref[:, s, :] = packed[:, s * LANES:(s + 1) * LANES]

    logits = lax.dot_general(wr_ref[...], xn, (((1,), (1,)), ((), ())), preferred_element_type=F32)
    logits = logits + br_ref[...]
    eidx = lax.broadcasted_iota(jnp.int32, (N_EXPERTS, tm), 0)
    work = logits
    vals, sels = [], []
    for k in range(TOP_K):
        v = jnp.max(work, axis=0, keepdims=True)
        idx = jnp.min(jnp.where(work == v, eidx, N_EXPERTS), axis=0, keepdims=True)
        sel = eidx == idx
        ids_ref[k:k + 1, :] = idx
        vals.append(v)
        sels.append(sel)
        work = jnp.where(sel, -jnp.inf, work)
    es = [jnp.exp(v - vals[0]) for v in vals]
    den = es[0] + es[1] + es[2] + es[3]
    for k in range(TOP_K):
        wts_ref[k:k + 1, :] = es[k] / den
    cnt = sum(s.astype(F32) for s in sels)
    incl = jnp.dot(cnt.astype(BF16), tri_ref[...], preferred_element_type=F32)
    before = incl - cnt + carry_ref[:, 0:1]
    for k in range(TOP_K):
        rk = jnp.sum(jnp.where(sels[k], before, 0.0), axis=0, keepdims=True)
        rank_ref[k:k + 1, :] = rk.astype(jnp.int32)
    carry_ref[...] = carry_ref[...] + incl[:, tm - 1:tm]
    cnt_ref[...] = carry_ref[...]


def _outproj(attn_n, lru_n, wo_bf, b_out, x2d, g_ffn, wr_t, br, tri, tm):
    rows, d = x2d.shape
    const = lambda shape: pl.BlockSpec(shape, lambda m: tuple(0 for _ in shape))
    return pl.pallas_call(
        _outproj_kernel,
        grid=(rows // tm,),
        in_specs=[
            pl.BlockSpec((tm, D_ATTN), lambda m: (m, 0)),
            pl.BlockSpec((tm, D_LRU), lambda m: (m, 0)),
            const((D_ATTN + D_LRU, d)), const((1, d)),
            pl.BlockSpec((tm, d), lambda m: (m, 0)),
            const((1, d)), const((N_EXPERTS, d)), const((N_EXPERTS, 1)), const((tm, tm)),
        ],
        out_specs=[
            pl.BlockSpec((tm, d), lambda m: (m, 0)),
            pl.BlockSpec((tm, d // 2 // LANES, LANES), lambda m: (m, 0, 0)),
            pl.BlockSpec((TOP_K, tm), lambda m: (0, m)),
            pl.BlockSpec((TOP_K, tm), lambda m: (0, m)),
            pl.BlockSpec((TOP_K, tm), lambda m: (0, m)),
            const((N_EXPERTS, LANES)),
        ],
        out_shape=[
            jax.ShapeDtypeStruct((rows, d), F32),
            jax.ShapeDtypeStruct((rows, d // 2 // LANES, LANES), jnp.uint32),
            jax.ShapeDtypeStruct((TOP_K, rows), jnp.int32),
            jax.ShapeDtypeStruct((TOP_K, rows), F32),
            jax.ShapeDtypeStruct((TOP_K, rows), jnp.int32),
            jax.ShapeDtypeStruct((N_EXPERTS, LANES), F32),
        ],
        scratch_shapes=[pltpu.VMEM((N_EXPERTS, LANES), F32)],
        compiler_params=_cparams(("arbitrary",)),
        name="outproj_router",
    )(attn_n, lru_n, wo_bf, b_out, x2d, g_ffn, wr_t, br, tri)


def _zero_fill_rows(zero_ref, dst_hbm, first_row, n_blocks, sem):
    def copy(j):
        row = pl.multiple_of(first_row + j * SUB_ROWS, SUB_ROWS)
        return pltpu.make_async_copy(zero_ref, dst_hbm.at[pl.ds(row, SUB_ROWS)], sem)

    def start(j, c):
        copy(j).start()
        return c

    def wait(j, c):
        copy(j).wait()
        return c

    lax.fori_loop(0, n_blocks, start, 0)
    lax.fori_loop(0, n_blocks, wait, 0)


def _dispatch_kernel(tails_ref, pos_hbm, xp_ref, xs_hbm, pos_smem, zero_ref, sem, psem):
    i = pl.program_id(0)
    tm = xp_ref.shape[0]
    n = TOP_K * tm
    pcopy = pltpu.make_async_copy(pos_hbm.at[pl.ds(i * n, n)], pos_smem, psem)
    pcopy.start()

    @pl.when(i == 0)
    def _():
        zero_ref[...] = jnp.zeros_like(zero_ref)
        for e in range(N_EXPERTS):
            pltpu.make_async_copy(zero_ref, xs_hbm.at[pl.ds(tails_ref[e], SUB_ROWS)], sem).start()
        for e in range(N_EXPERTS):
            pltpu.make_async_copy(zero_ref, xs_hbm.at[pl.ds(tails_ref[e], SUB_ROWS)], sem).wait()
        _zero_fill_rows(zero_ref, xs_hbm, tails_ref[N_EXPERTS], tails_ref[N_EXPERTS + 1], sem)

    pcopy.wait()

    def issue(j, c):
        pltpu.make_async_copy(xp_ref.at[j % tm], xs_hbm.at[pos_smem[j]], sem).start()
        return c

    lax.fori_loop(0, n, issue, 0)
    for _ in range(TOP_K):
        pltpu.make_async_copy(xp_ref, xs_hbm.at[pl.ds(0, tm)], sem).wait()


def _dispatch(tails, pos_tiles, xp, p_alloc, tm):
    rows, ns, _ = xp.shape
    return pl.pallas_call(
        _dispatch_kernel,
        grid_spec=pltpu.PrefetchScalarGridSpec(
            num_scalar_prefetch=1,
            grid=(rows // tm,),
            in_specs=[
                pl.BlockSpec(memory_space=pl.ANY),
                pl.BlockSpec((tm, ns, LANES), lambda i, s: (i, 0, 0)),
            ],
            out_specs=pl.BlockSpec(memory_space=pl.ANY),
            scratch_shapes=[
                pltpu.SMEM((TOP_K * tm,), jnp.int32),
                pltpu.VMEM((SUB_ROWS, ns, LANES), jnp.uint32),
                pltpu.SemaphoreType.DMA,
                pltpu.SemaphoreType.DMA,
            ],
        ),
        out_shape=jax.ShapeDtypeStruct((p_alloc, ns, LANES), jnp.uint32),
        compiler_params=_cparams(("arbitrary",)),
        name="dispatch",
    )(tails, pos_tiles, xp)


def _unpack_rows(w):
    lo = pltpu.bitcast(w << 16, F32).astype(BF16)
    hi = pltpu.bitcast(w & jnp.uint32(0xFFFF0000), F32).astype(BF16)
    return lo, hi


def _load_chunk(src_hbm, dst_ref, start, nsub, sem):
    def copy(j):
        r = pl.multiple_of(j * SUB_ROWS, SUB_ROWS)
        src = src_hbm.at[pl.ds(pl.multiple_of(start + r, SUB_ROWS), SUB_ROWS)]
        return pltpu.make_async_copy(src, dst_ref.at[pl.ds(r, SUB_ROWS)], sem)

    def begin(j, c):
        copy(j).start()
        return c

    def wait(j, c):
        copy(j).wait()
        return c

    lax.fori_loop(0, nsub, begin, 0)
    lax.fori_loop(0, nsub, wait, 0)


def _moe_up_kernel(ce_ref, cs_ref, cn_ref, tail_ref, xs_hbm, wg_ref, bg_ref, wu_ref, bu_ref, act_hbm,
                   land_ref, xb_ref, wgb_ref, wub_ref, ab_ref, zero_ref, sem, osem):
    c = pl.program_id(0)
    f = pl.program_id(1)
    nsub = cn_ref[c]
    start = pl.multiple_of(cs_ref[c], SUB_ROWS)
    tf = wg_ref.shape[2]
    ns = land_ref.shape[1]
    half = ns * LANES

    @pl.when(jnp.logical_and(c == 0, f == 0))
    def _():
        zero_ref[...] = jnp.zeros_like(zero_ref)
        _zero_fill_rows(zero_ref, act_hbm, tail_ref[0], tail_ref[1], osem)

    @pl.when(jnp.logical_and(f == 0, nsub > 0))
    def _():
        _load_chunk(xs_hbm, land_ref, start, nsub, sem)

        def unpack(j, carry):
            r = pl.multiple_of(j * SUB_ROWS, SUB_ROWS)
            for s in range(ns):
                lo, hi = _unpack_rows(land_ref[pl.ds(r, SUB_ROWS), s, :])
                xb_ref[pl.ds(r, SUB_ROWS), s * LANES:(s + 1) * LANES] = lo
                xb_ref[pl.ds(r, SUB_ROWS), half + s * LANES:half + (s + 1) * LANES] = hi
            return carry

        lax.fori_loop(0, nsub, unpack, 0)

    @pl.when(nsub > 0)
    def _():
        wgb_ref[...] = wg_ref[0].astype(BF16)
        wub_ref[...] = wu_ref[0].astype(BF16)
        bg = bg_ref[0]
        bu = bu_ref[0]

        def sub(j, carry):
            r = pl.multiple_of(j * SUB_ROWS, SUB_ROWS)
            x = xb_ref[pl.ds(r, SUB_ROWS), :]
            g = jnp.dot(x, wgb_ref[...], preferred_element_type=F32) + bg
            u = jnp.dot(x, wub_ref[...], preferred_element_type=F32) + bu
            g = jnp.minimum(g, SWIGLU_LIMIT)
            u = jnp.clip(u, -SWIGLU_LIMIT, SWIGLU_LIMIT)
            ab_ref[pl.ds(r, SUB_ROWS), :] = (g * jax.nn.sigmoid(SWIGLU_ALPHA * g) * (u + 1.0)).astype(BF16)
            pltpu.make_async_copy(
                ab_ref.at[pl.ds(r, SUB_ROWS)],
                act_hbm.at[pl.ds(start + r, SUB_ROWS), pl.ds(pl.multiple_of(f * tf, tf), tf)], osem).start()
            return carry

        lax.fori_loop(0, nsub, sub, 0)

        def drain(j, carry):
            pltpu.make_async_copy(ab_ref.at[pl.ds(0, SUB_ROWS)],
                                  act_hbm.at[pl.ds(0, SUB_ROWS), pl.ds(0, tf)], osem).wait()
            return carry

        lax.fori_loop(0, nsub, drain, 0)


def _moe_up(ch_e, ch_start, ch_nsub, tail, xs, w_gate, b_gate, w_up, b_up, tf):
    p_alloc, ns, _ = xs.shape
    d = 2 * ns * LANES
    dff = w_gate.shape[2]
    nch = ch_e.shape[0]
    nf = dff // tf
    wmap = lambda c, f, e, s, n, t: (e[c], 0, jnp.where(n[c] > 0, f, nf - 1))
    return pl.pallas_call(
        _moe_up_kernel,
        grid_spec=pltpu.PrefetchScalarGridSpec(
            num_scalar_prefetch=4,
            grid=(nch, nf),
            in_specs=[
                pl.BlockSpec(memory_space=pl.ANY),
                pl.BlockSpec((1, d, tf), wmap),
                pl.BlockSpec((1, 1, tf), wmap),
                pl.BlockSpec((1, d, tf), wmap),
                pl.BlockSpec((1, 1, tf), wmap),
            ],
            out_specs=pl.BlockSpec(memory_space=pl.ANY),
            scratch_shapes=[
                pltpu.VMEM((MOE_TM, ns, LANES), jnp.uint32),
                pltpu.VMEM((MOE_TM, d), BF16),
                pltpu.VMEM((d, tf), BF16),
                pltpu.VMEM((d, tf), BF16),
                pltpu.VMEM((MOE_TM, tf), BF16),
                pltpu.VMEM((SUB_ROWS, dff), BF16),
                pltpu.SemaphoreType.DMA,
                pltpu.SemaphoreType.DMA,
            ],
        ),
        out_shape=jax.ShapeDtypeStruct((p_alloc, dff), BF16),
        compiler_params=_cparams(("arbitrary", "arbitrary")),
        name="moe_up",
    )(ch_e, ch_start, ch_nsub, tail, xs, w_gate, b_gate.reshape(N_EXPERTS, 1, dff), w_up,
      b_up.reshape(N_EXPERTS, 1, dff))


def _moe_down_kernel(ce_ref, cs_ref, cn_ref, tail_ref, act_hbm, wd_ref, bd_ref, ys_hbm,
                     xb_ref, wdb_ref, yb_ref, zero_ref, sem, osem):
    c = pl.program_id(0)
    f = pl.program_id(1)
    nsub = cn_ref[c]
    start = pl.multiple_of(cs_ref[c], SUB_ROWS)
    tn = wd_ref.shape[2]

    @pl.when(jnp.logical_and(c == 0, f == 0))
    def _():
        zero_ref[...] = jnp.zeros_like(zero_ref)
        _zero_fill_rows(zero_ref, ys_hbm, tail_ref[0], tail_ref[1], sem)

    @pl.when(jnp.logical_and(f == 0, nsub > 0))
    def _():
        _load_chunk(act_hbm, xb_ref, start, nsub, sem)

    ns = tn // LANES

    def out_copy(slot, r):
        return pltpu.make_async_copy(
            yb_ref.at[slot],
            ys_hbm.at[pl.ds(start + r, SUB_ROWS), pl.ds(pl.multiple_of(f * ns, ns), ns), :], osem.at[slot])

    @pl.when(nsub > 0)
    def _():
        wdb_ref[...] = wd_ref[0].astype(BF16)
        bd = bd_ref[0]

        def sub(j, carry):
            r = pl.multiple_of(j * SUB_ROWS, SUB_ROWS)
            slot = j % 2

            @pl.when(j >= 2)
            def _():
                out_copy(slot, r).wait()

            y = jnp.dot(xb_ref[pl.ds(r, SUB_ROWS), :], wdb_ref[...], preferred_element_type=F32) + bd
            for s in range(ns):
                yb_ref[slot, :, s, :] = y[:, s * LANES:(s + 1) * LANES]
            out_copy(slot, r).start()
            return carry

        lax.fori_loop(0, nsub, sub, 0)
        out_copy(0, 0).wait()

        @pl.when(nsub >= 2)
        def _():
            out_copy(1, 0).wait()


def _moe_down(ch_e, ch_start, ch_nsub, tail, act, w_down, b_down, tn):
    p_alloc, dff = act.shape
    d = w_down.shape[2]
    nch = ch_e.shape[0]
    nn = d // tn
    wmap = lambda c, f, e, s, n, t: (e[c], 0, jnp.where(n[c] > 0, f, nn - 1))
    return pl.pallas_call(
        _moe_down_kernel,
        grid_spec=pltpu.PrefetchScalarGridSpec(
            num_scalar_prefetch=4,
            grid=(nch, nn),
            in_specs=[
                pl.BlockSpec(memory_space=pl.ANY),
                pl.BlockSpec((1, dff, tn), wmap),
                pl.BlockSpec((1, 1, tn), wmap),
            ],
            out_specs=pl.BlockSpec(memory_space=pl.ANY),
            scratch_shapes=[
                pltpu.VMEM((MOE_TM, dff), BF16),
                pltpu.VMEM((dff, tn), BF16),
                pltpu.VMEM((2, SUB_ROWS, tn // LANES, LANES), F32),
                pltpu.VMEM((SUB_ROWS, d // LANES, LANES), F32),
                pltpu.SemaphoreType.DMA,
                pltpu.SemaphoreType.DMA((2,)),
            ],
        ),
        out_shape=jax.ShapeDtypeStruct((p_alloc, d // LANES, LANES), F32),
        compiler_params=_cparams(("arbitrary", "arbitrary")),
        name="moe_down",
    )(ch_e, ch_start, ch_nsub, tail, act, w_down, b_down.reshape(N_EXPERTS, 1, d))


def _combine_kernel(pos_hbm, ys_hbm, h_ref, w_ref, g_ref, o_ref, pos_smem, yb_ref, sem, psem):
    i = pl.program_id(0)
    tm = h_ref.shape[0]
    n = TOP_K * tm
    pcopy = pltpu.make_async_copy(pos_hbm.at[pl.ds(i * n, n)], pos_smem, psem)
    pcopy.start()
    pcopy.wait()

    def issue(j, c):
        pltpu.make_async_copy(ys_hbm.at[pos_smem[j]], yb_ref.at[j], sem).start()
        return c

    lax.fori_loop(0, n, issue, 0)
    pltpu.make_async_copy(ys_hbm.at[pl.ds(0, n)], yb_ref, sem).wait()
    ssq = jnp.zeros((tm, 1), F32)
    for s in range(yb_ref.shape[1]):
        cs = slice(s * LANES, (s + 1) * LANES)
        acc = h_ref[:, cs]
        for k in range(TOP_K):
            acc = acc + w_ref[:, k:k + 1] * yb_ref[k * tm:(k + 1) * tm, s, :]
        o_ref[:, cs] = acc
        ssq = ssq + jnp.sum(acc * acc, axis=-1, keepdims=True)
    o_ref[...] = o_ref[...] * lax.rsqrt(ssq * (1.0 / o_ref.shape[1]) + NORM_EPS) * g_ref[...]


def _combine(pos_tiles, ys, h1, wts_t, g_final, tm):
    rows, d = h1.shape
    return pl.pallas_call(
        _combine_kernel,
        grid=(rows // tm,),
        in_specs=[
            pl.BlockSpec(memory_space=pl.ANY),
            pl.BlockSpec(memory_space=pl.ANY),
            pl.BlockSpec((tm, d), lambda i: (i, 0)),
            pl.BlockSpec((tm, TOP_K), lambda i: (i, 0)),
            pl.BlockSpec((1, d), lambda i: (0, 0)),
        ],
        out_specs=pl.BlockSpec((tm, d), lambda i: (i, 0)),
        out_shape=jax.ShapeDtypeStruct((rows, d), F32),
        scratch_shapes=[
            pltpu.SMEM((TOP_K * tm,), jnp.int32),
            pltpu.VMEM((TOP_K * tm, d // LANES, LANES), F32),
            pltpu.SemaphoreType.DMA,
            pltpu.SemaphoreType.DMA,
        ],
        compiler_params=_cparams(("arbitrary",)),
        name="combine",
    )(pos_tiles, ys, h1, wts_t, g_final)


def _rope_tables(n_pos):
    half = HEAD_DIM // 2
    inv = 1.0 / (ROPE_THETA ** (jnp.arange(half, dtype=F32) / half))
    ang = jnp.arange(n_pos, dtype=F32)[:, None] * inv[None, :]
    cos = jnp.tile(jnp.cos(ang), (1, LANES // half))
    sin = jnp.tile(jnp.concatenate([-jnp.sin(ang), jnp.sin(ang)], axis=1), (1, LANES // HEAD_DIM))
    return cos, sin


def _block_diag(w):
    per = LRU_GROUP // LRU_BLOCK
    w4 = w.reshape(D_LRU // LRU_GROUP, per, LRU_BLOCK, LRU_BLOCK)
    eye = jnp.eye(per, dtype=w.dtype)
    bd = jnp.einsum("gpcd,pq->gpcqd", w4, eye)
    return bd.reshape(D_LRU // LRU_GROUP, LRU_GROUP, LRU_GROUP).astype(BF16)


def _tile_positions(pos, tm):
    rows = pos.shape[1]
    return pos.reshape(TOP_K, rows // tm, tm).transpose(1, 0, 2).reshape(-1)


def _chunk_schedule(counts, n_rows):
    aligned = ((counts + SUB_ROWS - 1) // SUB_ROWS) * SUB_ROWS
    offs = jnp.cumsum(aligned) - aligned
    n_ch = (aligned + MOE_TM - 1) // MOE_TM
    cum = jnp.cumsum(n_ch)
    total = cum[-1]
    nch_max = N_EXPERTS + (n_rows + N_EXPERTS * SUB_ROWS) // MOE_TM
    c = jnp.arange(nch_max, dtype=jnp.int32)
    cc = jnp.minimum(c, total - 1)
    e = jnp.searchsorted(cum, cc, side="right").astype(jnp.int32)
    j = cc - (cum[e] - n_ch[e])
    start = offs[e] + j * MOE_TM
    nsub = jnp.where(c < total, jnp.minimum(MOE_TM, aligned[e] - j * MOE_TM) // SUB_ROWS, 0)
    tails = offs + (counts // SUB_ROWS) * SUB_ROWS
    used = jnp.sum(aligned)
    tail = jnp.stack([used, (n_rows + N_EXPERTS * SUB_ROWS - used) // SUB_ROWS])
    tails = jnp.concatenate([tails, tail])
    return (offs, e, start.astype(jnp.int32), nsub.astype(jnp.int32), tails.astype(jnp.int32),
            tail.astype(jnp.int32))


def kernel(x, meta_tokens, norm_mix, w_in, b_in, sinks, conv_w, conv_b, w_a, b_a, w_i, b_i, lru_lambda,
           g_attn_out, g_lru_out, w_out, b_out, norm_ffn, w_router, b_router, w_gate, b_gate, w_up, b_up,
           w_down, b_down, final_norm):
    batch, seq, d = x.shape
    rows = batch * seq
    x2d = x.reshape(rows, d)
    row = lambda v: v.reshape(1, -1)

    w_in_bf = w_in[0].astype(BF16)
    w_out_bf = w_out[0].astype(BF16)
    cos, sin = _rope_tables(N_META + seq)
    wa_bd, wi_bd = _block_diag(w_a[0]), _block_diag(w_i[0])
    lru_args = (conv_w[0], row(conv_b[0]), wa_bd, row(b_a[0]), wi_bd, row(b_i[0]), row(lru_lambda[0]),
                row(g_lru_out[0]))

    _, kvm, xgm = _inproj(meta_tokens, row(norm_mix[0]), w_in_bf, row(b_in[0]), cos[:N_META], sin[:N_META],
                          N_META, 1)
    _, h0, tail0 = _lru(xgm, *lru_args, jnp.zeros((1, D_LRU), F32), jnp.zeros((8, D_LRU), F32), 1, N_META,
                        N_META)

    tm = 512
    q, kv, xg = _inproj(x2d, row(norm_mix[0]), w_in_bf, row(b_in[0]), cos[N_META:], sin[N_META:], tm,
                        seq // tm)
    attn_n = _attention(sinks[0], q, kv, kvm, row(g_attn_out[0]), batch, seq)
    lru_n, _, _ = _lru(xg, *lru_args, h0, tail0, batch, seq, 512)

    tri = (jnp.arange(tm)[:, None] <= jnp.arange(tm)[None, :]).astype(BF16)
    h1, xp, ids, wts, rank, cnt = _outproj(
        attn_n, lru_n, w_out_bf, row(b_out[0]), x2d, row(norm_ffn[0]), w_router[0].T.astype(BF16),
        b_router[0].reshape(N_EXPERTS, 1), tri, tm)

    counts = cnt[:, 0].astype(jnp.int32)
    n_rows = rows * TOP_K
    offs, ch_e, ch_start, ch_nsub, tails, tail = _chunk_schedule(counts, n_rows)
    pos = offs[ids] + rank
    p_alloc = n_rows + N_EXPERTS * SUB_ROWS

    tmd = 512
    xs = _dispatch(tails, _tile_positions(pos, tmd), xp, p_alloc, tmd)
    act = _moe_up(ch_e, ch_start, ch_nsub, tail, xs, w_gate[0], b_gate[0], w_up[0], b_up[0], 256)
    ys = _moe_down(ch_e, ch_start, ch_nsub, tail, act, w_down[0], b_down[0], 1024)
    tmc = 256
    out = _combine(_tile_positions(pos, tmc), ys, h1, wts.T, row(final_norm), tmc)
    return out.reshape(batch, seq, d)
```

```python
import jax
import jax.numpy as jnp
from jax import lax
from jax.experimental import pallas as pl
from jax.experimental.pallas import tpu as pltpu

F32 = jnp.float32
BF16 = jnp.bfloat16

N_META = 16
HEAD_DIM = 64
N_Q_HEADS = 16
N_KV_HEADS = 4
GROUP = N_Q_HEADS // N_KV_HEADS
D_ATTN = N_Q_HEADS * HEAD_DIM
D_KV = N_KV_HEADS * HEAD_DIM
BLOCK = 128
ROPE_THETA = 10000.0
D_LRU = 1024
LRU_BLOCK = 64
LRU_GROUP = 256
CONV_WIDTH = 4
LRU_C = 8.0
N_EXPERTS = 32
TOP_K = 4
SWIGLU_LIMIT = 7.0
SWIGLU_ALPHA = 1.702
NORM_EPS = 1e-5
NEG_INF = -1e30

LANES = 128
D_MODEL = 2048
TOK_WORDS = D_MODEL // 2
SUB_ROWS = 256
MOE_TM = 2304
MOE_TF = 512
MOE_TK = 512
VMEM_LIMIT = 56 * 1024 * 1024


def _cparams(sem):
    return pltpu.CompilerParams(dimension_semantics=sem, vmem_limit_bytes=VMEM_LIMIT)


def _rms(x, g):
    return x * lax.rsqrt(jnp.mean(x * x, axis=-1, keepdims=True) + NORM_EPS) * g


def _inproj_kernel(x_ref, g_ref, w_ref, b_ref, cos_ref, sin_ref, q_ref, kv_ref, xg_ref):
    xn = _rms(x_ref[...], g_ref[...]).astype(BF16)
    cos = cos_ref[...]
    sin = sin_ref[...]
    lane = lax.broadcasted_iota(jnp.int32, (1, LANES), 1)
    first_half = (lane % HEAD_DIM) < (HEAD_DIM // 2)

    def rope(z):
        partner = jnp.where(first_half, pltpu.roll(z, LANES - HEAD_DIM // 2, 1),
                            pltpu.roll(z, HEAD_DIM // 2, 1))
        return z * cos + partner * sin

    cw = 512
    for c in range(w_ref.shape[1] // cw):
        z = jnp.dot(xn, w_ref[:, c * cw:(c + 1) * cw], preferred_element_type=F32)
        z = z + b_ref[:, c * cw:(c + 1) * cw]
        if c < 2:
            for j in range(cw // LANES):
                zz = rope(z[:, j * LANES:(j + 1) * LANES]) * (HEAD_DIM ** -0.5)
                q_ref[:, c * cw + j * LANES:c * cw + (j + 1) * LANES] = zz.astype(BF16)
        elif c == 2:
            for j in range(D_KV // LANES):
                kv_ref[:, j * LANES:(j + 1) * LANES] = rope(z[:, j * LANES:(j + 1) * LANES]).astype(BF16)
            kv_ref[:, D_KV:] = z[:, D_KV:].astype(BF16)
        else:
            xg_ref[:, (c - 3) * cw:(c - 2) * cw] = z


def _inproj(x2d, g, w_bf, b, cos, sin, tm, pos_blocks):
    rows, d = x2d.shape
    dz = w_bf.shape[1]
    return pl.pallas_call(
        _inproj_kernel,
        grid=(rows // tm,),
        in_specs=[
            pl.BlockSpec((tm, d), lambda m: (m, 0)),
            pl.BlockSpec((1, d), lambda m: (0, 0)),
            pl.BlockSpec((d, dz), lambda m: (0, 0)),
            pl.BlockSpec((1, dz), lambda m: (0, 0)),
            pl.BlockSpec((tm, LANES), lambda m: (m % pos_blocks, 0)),
            pl.BlockSpec((tm, LANES), lambda m: (m % pos_blocks, 0)),
        ],
        out_specs=[
            pl.BlockSpec((tm, D_ATTN), lambda m: (m, 0)),
            pl.BlockSpec((tm, 2 * D_KV), lambda m: (m, 0)),
            pl.BlockSpec((tm, 2 * D_LRU), lambda m: (m, 0)),
        ],
        out_shape=[
            jax.ShapeDtypeStruct((rows, D_ATTN), BF16),
            jax.ShapeDtypeStruct((rows, 2 * D_KV), BF16),
            jax.ShapeDtypeStruct((rows, 2 * D_LRU), F32),
        ],
        compiler_params=_cparams(("arbitrary",)),
        name="inproj",
    )(x2d, g, w_bf, b, cos, sin)


def _attn_kernel(sink_ref, q_ref, kvc_ref, kvp_ref, kvm_ref, g_ref, o_ref):
    n = pl.program_id(1)
    row = lax.broadcasted_iota(jnp.int32, (GROUP * BLOCK, BLOCK), 0) % BLOCK
    col = lax.broadcasted_iota(jnp.int32, (GROUP * BLOCK, BLOCK), 1)
    ok_cur = col <= row
    ok_prev = jnp.logical_and(col > row, n > 0)
    q = q_ref[...]
    kvc = kvc_ref[...]
    kvp = kvp_ref[...]
    kvm = kvm_ref[...]
    nt = (((1,), (1,)), ((), ()))
    outs = []
    for h in range(N_KV_HEADS):
        ks = slice(h * HEAD_DIM, (h + 1) * HEAD_DIM)
        vs = slice(D_KV + h * HEAD_DIM, D_KV + (h + 1) * HEAD_DIM)
        qh = jnp.concatenate(
            [q[:, (h * GROUP + g) * HEAD_DIM:(h * GROUP + g + 1) * HEAD_DIM] for g in range(GROUP)], axis=0)
        s_c = jnp.where(ok_cur, lax.dot_general(qh, kvc[:, ks], nt, preferred_element_type=F32), NEG_INF)
        s_p = jnp.where(ok_prev, lax.dot_general(qh, kvp[:, ks], nt, preferred_element_type=F32), NEG_INF)
        s_m = lax.dot_general(qh, kvm[:, ks], nt, preferred_element_type=F32)
        sink = jnp.concatenate(
            [jnp.full((BLOCK, 1), sink_ref[h * GROUP + g], F32) for g in range(GROUP)], axis=0)
        m = jnp.maximum(jnp.maximum(jnp.max(s_c, axis=-1, keepdims=True), jnp.max(s_p, axis=-1, keepdims=True)),
                        jnp.maximum(jnp.max(s_m, axis=-1, keepdims=True), sink))
        p_c = jnp.exp(s_c - m)
        p_p = jnp.exp(s_p - m)
        p_m = jnp.exp(s_m - m)
        den = (jnp.sum(p_c, axis=-1, keepdims=True) + jnp.sum(p_p, axis=-1, keepdims=True)
               + jnp.sum(p_m, axis=-1, keepdims=True) + jnp.exp(sink - m))
        o = (jnp.dot(p_c.astype(BF16), kvc[:, vs], preferred_element_type=F32)
             + jnp.dot(p_p.astype(BF16), kvp[:, vs], preferred_element_type=F32)
             + jnp.dot(p_m.astype(BF16), kvm[:, vs], preferred_element_type=F32))
        o = o / den
        outs.extend(o[g * BLOCK:(g + 1) * BLOCK] for g in range(GROUP))
    o_all = jnp.concatenate(outs, axis=1)
    o_ref[...] = _rms(o_all, g_ref[...]).astype(BF16)


def _attention(sinks, q, kv, kvm, g_attn, batch, seq):
    nb = seq // BLOCK
    return pl.pallas_call(
        _attn_kernel,
        grid_spec=pltpu.PrefetchScalarGridSpec(
            num_scalar_prefetch=1,
            grid=(batch, nb),
            in_specs=[
                pl.BlockSpec((BLOCK, D_ATTN), lambda b, n, s: (b * nb + n, 0)),
                pl.BlockSpec((BLOCK, 2 * D_KV), lambda b, n, s: (b * nb + n, 0)),
                pl.BlockSpec((BLOCK, 2 * D_KV), lambda b, n, s: (b * nb + jnp.maximum(n - 1, 0), 0)),
                pl.BlockSpec((N_META, 2 * D_KV), lambda b, n, s: (0, 0)),
                pl.BlockSpec((1, D_ATTN), lambda b, n, s: (0, 0)),
            ],
            out_specs=pl.BlockSpec((BLOCK, D_ATTN), lambda b, n, s: (b * nb + n, 0)),
        ),
        out_shape=jax.ShapeDtypeStruct((batch * seq, D_ATTN), BF16),
        compiler_params=_cparams(("arbitrary", "arbitrary")),
        name="attention",
    )(sinks, q, kv, kv, kvm, g_attn)


def _expm1(y):
    p = 1.0 + y * (1.0 / 8.0)
    for k in range(7, 1, -1):
        p = 1.0 + y * p * (1.0 / k)
    return jnp.where(y > -0.25, y * p, jnp.exp(y) - 1.0)


def _lru_kernel(xg_ref, cw_ref, cb_ref, wa_ref, ba_ref, wi_ref, bi_ref, lam_ref, g_ref, h0_ref, tail0_ref,
                o_ref, hout_ref, tailout_ref, ext_ref, a_ref, b_ref, h_ref):
    tt = pl.program_id(1)
    rows = a_ref.shape[0]

    @pl.when(tt == 0)
    def _():
        h_ref[...] = h0_ref[...]
        ext_ref[0:8, :] = tail0_ref[...]

    ext_ref[8:, :] = xg_ref[:, :D_LRU]
    xc = cb_ref[...] + sum(cw_ref[j:j + 1, :] * ext_ref[5 + j:5 + j + rows, :] for j in range(CONV_WIDTH))
    xcb = xc.astype(BF16)
    sp = jax.nn.softplus(-lam_ref[...])
    for c in range(D_LRU // LRU_GROUP):
        cs = slice(c * LRU_GROUP, (c + 1) * LRU_GROUP)
        r = jax.nn.sigmoid(jnp.dot(xcb[:, cs], wa_ref[c], preferred_element_type=F32) + ba_ref[:, cs])
        i = jax.nn.sigmoid(jnp.dot(xcb[:, cs], wi_ref[c], preferred_element_type=F32) + bi_ref[:, cs])
        log_a = -LRU_C * r * sp[:, cs]
        a_ref[:, cs] = jnp.exp(log_a)
        b_ref[:, cs] = jnp.sqrt(-_expm1(2.0 * log_a)) * i * xc[:, cs]

    def step(t, h):
        h = a_ref[pl.ds(t, 1), :] * h + b_ref[pl.ds(t, 1), :]
        b_ref[pl.ds(t, 1), :] = h
        return h

    h_last = lax.fori_loop(0, rows, step, h_ref[...])
    h_ref[...] = h_last
    hout_ref[...] = h_last
    tail = ext_ref[rows:rows + 8, :]
    ext_ref[0:8, :] = tail
    tailout_ref[...] = tail
    y = b_ref[...] * jax.nn.gelu(xg_ref[:, D_LRU:])
    o_ref[...] = _rms(y, g_ref[...]).astype(BF16)


def _lru(xg, conv_w, conv_b, wa_bd, b_a, wi_bd, b_i, lam, g_lru, h0, tail0, batch, seq, tt):
    ntt = seq // tt
    vec = pl.BlockSpec((1, D_LRU), lambda b, t: (0, 0))
    wspec = pl.BlockSpec((D_LRU // LRU_GROUP, LRU_GROUP, LRU_GROUP), lambda b, t: (0, 0, 0))
    return pl.pallas_call(
        _lru_kernel,
        grid=(batch, ntt),
        in_specs=[
            pl.BlockSpec((tt, 2 * D_LRU), lambda b, t: (b * ntt + t, 0)),
            pl.BlockSpec((CONV_WIDTH, D_LRU), lambda b, t: (0, 0)),
            vec, wspec, vec, wspec, vec, vec, vec, vec,
            pl.BlockSpec((8, D_LRU), lambda b, t: (0, 0)),
        ],
        out_specs=[
            pl.BlockSpec((tt, D_LRU), lambda b, t: (b * ntt + t, 0)),
            pl.BlockSpec((1, D_LRU), lambda b, t: (0, 0)),
            pl.BlockSpec((8, D_LRU), lambda b, t: (0, 0)),
        ],
        out_shape=[
            jax.ShapeDtypeStruct((batch * seq, D_LRU), BF16),
            jax.ShapeDtypeStruct((1, D_LRU), F32),
            jax.ShapeDtypeStruct((8, D_LRU), F32),
        ],
        scratch_shapes=[
            pltpu.VMEM((8 + tt, D_LRU), F32),
            pltpu.VMEM((tt, D_LRU), F32),
            pltpu.VMEM((tt, D_LRU), F32),
            pltpu.VMEM((1, D_LRU), F32),
        ],
        compiler_params=_cparams(("arbitrary", "arbitrary")),
        name="rglru",
    )(xg, conv_w, conv_b, wa_bd, b_a, wi_bd, b_i, lam, g_lru, h0, tail0)


def _outproj_kernel(a_ref, l_ref, wo_ref, bo_ref, x_ref, gf_ref, wr_ref, br_ref, tri_ref,
                    h_ref, xp_ref, ids_ref, wts_ref, rank_ref, cnt_ref, carry_ref):
    m = pl.program_id(0)
    tm = a_ref.shape[0]

    @pl.when(m == 0)
    def _():
        carry_ref[...] = jnp.zeros_like(carry_ref)

    h = (jnp.dot(a_ref[...], wo_ref[:D_ATTN, :], preferred_element_type=F32)
         + jnp.dot(l_ref[...], wo_ref[D_ATTN:, :], preferred_element_type=F32)
         + bo_ref[...] + x_ref[...])
    h_ref[...] = h
    xn = _rms(h, gf_ref[...]).astype(BF16)
    half = xn.shape[1] // 2
    lo = pltpu.bitcast(xn[:, :half].astype(F32), jnp.uint32)
    hi = pltpu.bitcast(xn[:, half:].astype(F32), jnp.uint32)
    packed = (hi & jnp.uint32(0xFFFF0000)) | (lo >> 16)
    ns = half // LANES
    for s in range(ns):
        xp_ref[pl.ds(s, tm, stride=ns), :] = packed[:, s * LANES:(s + 1) * LANES]

    logits = lax.dot_general(wr_ref[...], xn, (((1,), (1,)), ((), ())), preferred_element_type=F32)
    logits = logits + br_ref[...]
    eidx = lax.broadcasted_iota(jnp.int32, (N_EXPERTS, tm), 0)
    work = logits
    vals, sels = [], []
    for k in range(TOP_K):
        v = jnp.max(work, axis=0, keepdims=True)
        idx = jnp.min(jnp.where(work == v, eidx, N_EXPERTS), axis=0, keepdims=True)
        sel = eidx == idx
        ids_ref[k:k + 1, :] = idx
        vals.append(v)
        sels.append(sel)
        work = jnp.where(sel, -jnp.inf, work)
    es = [jnp.exp(v - vals[0]) for v in vals]
    den = es[0] + es[1] + es[2] + es[3]
    for k in range(TOP_K):
        wts_ref[k:k + 1, :] = es[k] / den
    cnt = sum(s.astype(F32) for s in sels)
    incl = jnp.dot(cnt.astype(BF16), tri_ref[...], preferred_element_type=F32)
    before = incl - cnt + carry_ref[:, 0:1]
    for k in range(TOP_K):
        rk = jnp.sum(jnp.where(sels[k], before, 0.0), axis=0, keepdims=True)
        rank_ref[k:k + 1, :] = rk.astype(jnp.int32)
    carry_ref[...] = carry_ref[...] + incl[:, tm - 1:tm]
    cnt_ref[...] = carry_ref[...]


def _outproj(attn_n, lru_n, wo_bf, b_out, x2d, g_ffn, wr_t, br, tri, tm):
    rows, d = x2d.shape
    const = lambda shape: pl.BlockSpec(shape, lambda m: tuple(0 for _ in shape))
    return pl.pallas_call(
        _outproj_kernel,
        grid=(rows // tm,),
        in_specs=[
            pl.BlockSpec((tm, D_ATTN), lambda m: (m, 0)),
            pl.BlockSpec((tm, D_LRU), lambda m: (m, 0)),
            const((D_ATTN + D_LRU, d)), const((1, d)),
            pl.BlockSpec((tm, d), lambda m: (m, 0)),
            const((1, d)), const((N_EXPERTS, d)), const((N_EXPERTS, 1)), const((tm, tm)),
        ],
        out_specs=[
            pl.BlockSpec((tm, d), lambda m: (m, 0)),
            pl.BlockSpec((tm * (d // 2 // LANES), LANES), lambda m: (m, 0)),
            pl.BlockSpec((TOP_K, tm), lambda m: (0, m)),
            pl.BlockSpec((TOP_K, tm), lambda m: (0, m)),
            pl.BlockSpec((TOP_K, tm), lambda m: (0, m)),
            const((N_EXPERTS, LANES)),
        ],
        out_shape=[
            jax.ShapeDtypeStruct((rows, d), F32),
            jax.ShapeDtypeStruct((rows * (d // 2 // LANES), LANES), jnp.uint32),
            jax.ShapeDtypeStruct((TOP_K, rows), jnp.int32),
            jax.ShapeDtypeStruct((TOP_K, rows), F32),
            jax.ShapeDtypeStruct((TOP_K, rows), jnp.int32),
            jax.ShapeDtypeStruct((N_EXPERTS, LANES), F32),
        ],
        scratch_shapes=[pltpu.VMEM((N_EXPERTS, LANES), F32)],
        compiler_params=_cparams(("arbitrary",)),
        name="outproj_router",
    )(attn_n, lru_n, wo_bf, b_out, x2d, g_ffn, wr_t, br, tri)


def _zero_fill_rows(zero_ref, dst_hbm, first_row, n_blocks, sem):
    blk = zero_ref.shape[0]
    per = blk // SUB_ROWS

    def copy(j):
        row = pl.multiple_of((first_row + j * SUB_ROWS) * per, blk)
        return pltpu.make_async_copy(zero_ref, dst_hbm.at[pl.ds(row, blk)], sem)

    def start(j, c):
        copy(j).start()
        return c

    def wait(j, c):
        copy(j).wait()
        return c

    lax.fori_loop(0, n_blocks, start, 0)
    lax.fori_loop(0, n_blocks, wait, 0)


def _dispatch_kernel(tails_ref, pos_hbm, xp_ref, xs_hbm, pos_smem, zero_ref, sem, psem):
    i = pl.program_id(0)
    ns = TOK_WORDS // LANES
    tm = xp_ref.shape[0] // ns
    n = TOP_K * tm
    pcopy = pltpu.make_async_copy(pos_hbm.at[pl.ds(i * n, n)], pos_smem, psem)
    pcopy.start()

    @pl.when(i == 0)
    def _():
        zero_ref[...] = jnp.zeros_like(zero_ref)
        for e in range(N_EXPERTS):
            _zero_fill_rows(zero_ref, xs_hbm, tails_ref[e], 1, sem)
        _zero_fill_rows(zero_ref, xs_hbm, tails_ref[N_EXPERTS], tails_ref[N_EXPERTS + 1], sem)

    pcopy.wait()

    def issue(jj, c):
        for u in range(2):
            j = jj * 2 + u
            src = xp_ref.at[pl.ds(pl.multiple_of((j % tm) * ns, ns), ns)]
            dst = xs_hbm.at[pl.ds(pl.multiple_of(pos_smem[j] * ns, ns), ns)]
            pltpu.make_async_copy(src, dst, sem).start(priority=u)
        return c

    lax.fori_loop(0, n // 2, issue, 0)
    for _ in range(TOP_K):
        pltpu.make_async_copy(xp_ref, xs_hbm.at[pl.ds(0, tm * ns)], sem).wait()


def _dispatch(tails, pos_tiles, xp, p_alloc, tm):
    ns = TOK_WORDS // LANES
    rows = xp.shape[0] // ns
    return pl.pallas_call(
        _dispatch_kernel,
        grid_spec=pltpu.PrefetchScalarGridSpec(
            num_scalar_prefetch=1,
            grid=(rows // tm,),
            in_specs=[
                pl.BlockSpec(memory_space=pl.ANY),
                pl.BlockSpec((tm * ns, LANES), lambda i, s: (i, 0)),
            ],
            out_specs=pl.BlockSpec(memory_space=pl.ANY),
            scratch_shapes=[
                pltpu.SMEM((TOP_K * tm,), jnp.int32),
                pltpu.VMEM((SUB_ROWS * ns, LANES), jnp.uint32),
                pltpu.SemaphoreType.DMA,
                pltpu.SemaphoreType.DMA,
            ],
        ),
        out_shape=jax.ShapeDtypeStruct((p_alloc * ns, LANES), jnp.uint32),
        compiler_params=_cparams(("arbitrary",)),
        name="dispatch",
    )(tails, pos_tiles, xp)


def _chunk_dma(cs_ref, cn_ref, src_hbm, dst_ref, per, sem):
    blk = SUB_ROWS * per
    nch = cs_ref.shape[0]

    def copy(ci, j):
        src = src_hbm.at[pl.ds(pl.multiple_of((cs_ref[ci] + j * SUB_ROWS) * per, blk), blk)]
        return pltpu.make_async_copy(src, dst_ref.at[pl.ds(pl.multiple_of(j * blk, blk), blk)], sem)

    def run(ci, wait):
        cc = jnp.minimum(ci, nch - 1)
        n = jnp.where(ci < nch, cn_ref[cc], 0)

        def body(j, c):
            if wait:
                copy(cc, j).wait()
            else:
                copy(cc, j).start()
            return c

        lax.fori_loop(0, n, body, 0)

    return (lambda ci: run(ci, False)), (lambda ci: run(ci, True))


def _tile_loop(nsub, tile_fn):
    n2 = nsub // 2

    @pl.when(nsub % 2 == 1)
    def _():
        tile_fn(pl.multiple_of(n2 * 2 * SUB_ROWS, SUB_ROWS), SUB_ROWS, 2, False)

    def pair(j, c):
        tile_fn(pl.multiple_of(j * 2 * SUB_ROWS, 2 * SUB_ROWS), 2 * SUB_ROWS, j % 2, j >= 2)
        return c

    lax.fori_loop(0, n2, pair, 0)
    return n2


def _moe_up_kernel(ce_ref, cs_ref, cn_ref, tail_ref, xs_hbm, wg_ref, bg_ref, wu_ref, bu_ref, act_hbm,
                   land_ref, xb_ref, wgb_ref, wub_ref, ab_ref, zero_ref, sem, osem):
    c = pl.program_id(0)
    f = pl.program_id(1)
    nsub = cn_ref[c]
    start = pl.multiple_of(cs_ref[c], SUB_ROWS)
    tf = wg_ref.shape[2]
    ns = TOK_WORDS // LANES
    start_chunk, wait_chunk = _chunk_dma(cs_ref, cn_ref, xs_hbm, land_ref, ns, sem)

    @pl.when(jnp.logical_and(c == 0, f == 0))
    def _():
        zero_ref[...] = jnp.zeros_like(zero_ref)
        _zero_fill_rows(zero_ref, act_hbm, tail_ref[0], tail_ref[1], osem.at[0])
        start_chunk(c)

    @pl.when(jnp.logical_and(f == 0, nsub > 0))
    def _():
        wait_chunk(c)

        def unpack(j, carry):
            r = pl.multiple_of(j * SUB_ROWS, SUB_ROWS)
            for s in range(ns):
                w = land_ref[pl.ds(r * ns + s, SUB_ROWS, stride=ns), :]
                xb_ref[pl.ds(r, SUB_ROWS), s * LANES:(s + 1) * LANES] = pltpu.bitcast(w << 16, F32).astype(BF16)
                xb_ref[pl.ds(r, SUB_ROWS), TOK_WORDS + s * LANES:TOK_WORDS + (s + 1) * LANES] = (
                    pltpu.bitcast(w & jnp.uint32(0xFFFF0000), F32).astype(BF16))
            return carry

        lax.fori_loop(0, nsub, unpack, 0)
        start_chunk(c + 1)

    def out_copy(r, rows, slot):
        return pltpu.make_async_copy(
            ab_ref.at[slot, pl.ds(0, rows)],
            act_hbm.at[pl.ds(pl.multiple_of(start + r, SUB_ROWS), rows), pl.ds(pl.multiple_of(f * tf, tf), tf)],
            osem.at[slot])

    @pl.when(nsub > 0)
    def _():
        wgb_ref[...] = wg_ref[0].astype(BF16)
        wub_ref[...] = wu_ref[0].astype(BF16)
        bg = bg_ref[0]
        bu = bu_ref[0]

        def tile(r, rows, slot, wait_before):
            x = xb_ref[pl.ds(r, rows), :]
            g = jnp.dot(x, wgb_ref[...], preferred_element_type=F32) + bg
            u = jnp.dot(x, wub_ref[...], preferred_element_type=F32) + bu
            g = jnp.minimum(g, SWIGLU_LIMIT)
            u = jnp.clip(u, -SWIGLU_LIMIT, SWIGLU_LIMIT)
            a = (g * jax.nn.sigmoid(SWIGLU_ALPHA * g) * (u + 1.0)).astype(BF16)
            if wait_before is not False:
                @pl.when(wait_before)
                def _():
                    out_copy(r, rows, slot).wait()
            ab_ref[slot, pl.ds(0, rows), :] = a
            out_copy(r, rows, slot).start()

        n2 = _tile_loop(nsub, tile)
        for k in range(2):
            @pl.when(n2 > k)
            def _():
                out_copy(0, 2 * SUB_ROWS, (n2 - 1 - k) % 2).wait()

        @pl.when(nsub % 2 == 1)
        def _():
            out_copy(0, SUB_ROWS, 2).wait()


def _moe_up(ch_e, ch_start, ch_nsub, tail, xs, w_gate, b_gate, w_up, b_up):
    ns = TOK_WORDS // LANES
    p_alloc = xs.shape[0] // ns
    d = 2 * TOK_WORDS
    dff = w_gate.shape[2]
    tf = MOE_TF
    nch = ch_e.shape[0]
    nf = dff // tf
    wmap = lambda c, f, e, s, n, t: (e[c], 0, jnp.where(n[c] > 0, f, nf - 1))
    return pl.pallas_call(
        _moe_up_kernel,
        grid_spec=pltpu.PrefetchScalarGridSpec(
            num_scalar_prefetch=4,
            grid=(nch, nf),
            in_specs=[
                pl.BlockSpec(memory_space=pl.ANY),
                pl.BlockSpec((1, d, tf), wmap),
                pl.BlockSpec((1, 1, tf), wmap),
                pl.BlockSpec((1, d, tf), wmap),
                pl.BlockSpec((1, 1, tf), wmap),
            ],
            out_specs=pl.BlockSpec(memory_space=pl.ANY),
            scratch_shapes=[
                pltpu.VMEM((MOE_TM * ns, LANES), jnp.uint32),
                pltpu.VMEM((MOE_TM, d), BF16),
                pltpu.VMEM((d, tf), BF16),
                pltpu.VMEM((d, tf), BF16),
                pltpu.VMEM((3, 2 * SUB_ROWS, tf), BF16),
                pltpu.VMEM((SUB_ROWS, dff), BF16),
                pltpu.SemaphoreType.DMA,
                pltpu.SemaphoreType.DMA((3,)),
            ],
        ),
        out_shape=jax.ShapeDtypeStruct((p_alloc, dff), BF16),
        compiler_params=_cparams(("arbitrary", "arbitrary")),
        name="moe_up",
    )(ch_e, ch_start, ch_nsub, tail, xs, w_gate, b_gate.reshape(N_EXPERTS, 1, dff), w_up,
      b_up.reshape(N_EXPERTS, 1, dff))


def _moe_down_kernel(ce_ref, cs_ref, cn_ref, tail_ref, act_hbm, wd_ref, bd_ref, ys_hbm,
                     xa_ref, xb_ref, wdb_ref, acc_ref, yb_ref, zero_ref, sem, osem):
    c = pl.program_id(0)
    kt = pl.program_id(1)
    nk = pl.num_programs(1)
    nsub = cn_ref[c]
    start = pl.multiple_of(cs_ref[c], SUB_ROWS)
    tk = wd_ref.shape[1]
    ns = TOK_WORDS // LANES
    bufs = (xa_ref, xb_ref)

    def in_copy(ci, k, p, j):
        src = act_hbm.at[pl.ds(pl.multiple_of(cs_ref[ci] + j * SUB_ROWS, SUB_ROWS), SUB_ROWS),
                         pl.ds(pl.multiple_of(k * tk, tk), tk)]
        return pltpu.make_async_copy(src, bufs[p].at[pl.ds(pl.multiple_of(j * SUB_ROWS, SUB_ROWS), SUB_ROWS)],
                                     sem.at[p])

    def run_in(ci, k, p, wait):
        cc = jnp.minimum(ci, cs_ref.shape[0] - 1)
        n = jnp.where(ci < cs_ref.shape[0], cn_ref[cc], 0)

        def body(j, carry):
            if wait:
                in_copy(cc, k, p, j).wait()
            else:
                in_copy(cc, k, p, j).start()
            return carry

        lax.fori_loop(0, n, body, 0)

    @pl.when(jnp.logical_and(c == 0, kt == 0))
    def _():
        zero_ref[...] = jnp.zeros_like(zero_ref)
        _zero_fill_rows(zero_ref, ys_hbm, tail_ref[0], tail_ref[1], osem.at[0])
        run_in(c, kt, 0, False)

    def out_copy(r, rows, slot):
        dst = ys_hbm.at[pl.ds(pl.multiple_of((start + r) * ns, SUB_ROWS * ns), rows * ns)]
        return pltpu.make_async_copy(yb_ref.at[slot, pl.ds(0, rows * ns)], dst, osem.at[slot])

    def step(p):
        run_in(c, kt, p, True)
        last = kt == nk - 1
        run_in(jnp.where(last, c + 1, c), jnp.where(last, 0, kt + 1), 1 - p, False)
        wdb_ref[...] = wd_ref[0].astype(BF16)
        x_ref = bufs[p]

        def partial_tile(r, rows, slot, wait_before):
            y = jnp.dot(x_ref[pl.ds(r, rows), :], wdb_ref[...], preferred_element_type=F32)

            @pl.when(kt == 0)
            def _():
                acc_ref[pl.ds(r, rows), :] = y

            @pl.when(kt > 0)
            def _():
                acc_ref[pl.ds(r, rows), :] += y

        def final_tile(r, rows, slot, wait_before):
            y = (acc_ref[pl.ds(r, rows), :] + bd_ref[0]
                 + jnp.dot(x_ref[pl.ds(r, rows), :], wdb_ref[...], preferred_element_type=F32))
            lo = pltpu.bitcast(y[:, :TOK_WORDS].astype(BF16).astype(F32), jnp.uint32)
            hi = pltpu.bitcast(y[:, TOK_WORDS:].astype(BF16).astype(F32), jnp.uint32)
            packed = (hi & jnp.uint32(0xFFFF0000)) | (lo >> 16)
            if wait_before is not False:
                @pl.when(wait_before)
                def _():
                    out_copy(r, rows, slot).wait()
            for s in range(ns):
                yb_ref[slot, pl.ds(s, rows, stride=ns), :] = packed[:, s * LANES:(s + 1) * LANES]
            out_copy(r, rows, slot).start()

        @pl.when(jnp.logical_not(last))
        def _():
            _tile_loop(nsub, partial_tile)

        @pl.when(last)
        def _():
            n2 = _tile_loop(nsub, final_tile)
            for k in range(2):
                @pl.when(n2 > k)
                def _():
                    out_copy(0, 2 * SUB_ROWS, (n2 - 1 - k) % 2).wait()

            @pl.when(nsub % 2 == 1)
            def _():
                out_copy(0, SUB_ROWS, 2).wait()

    for p in range(2):
        @pl.when(jnp.logical_and(nsub > 0, kt % 2 == p))
        def _():
            step(p)


def _moe_down(ch_e, ch_start, ch_nsub, tail, act, w_down, b_down):
    p_alloc, dff = act.shape
    d = w_down.shape[2]
    ns = TOK_WORDS // LANES
    tk = MOE_TK
    nch = ch_e.shape[0]
    nk = dff // tk
    assert nk % 2 == 0
    wmap = lambda c, k, e, s, n, t: (e[c], jnp.where(n[c] > 0, k, nk - 1), 0)
    return pl.pallas_call(
        _moe_down_kernel,
        grid_spec=pltpu.PrefetchScalarGridSpec(
            num_scalar_prefetch=4,
            grid=(nch, nk),
            in_specs=[
                pl.BlockSpec(memory_space=pl.ANY),
                pl.BlockSpec((1, tk, d), wmap),
                pl.BlockSpec((1, 1, d), lambda c, k, e, s, n, t: (e[c], 0, 0)),
            ],
            out_specs=pl.BlockSpec(memory_space=pl.ANY),
            scratch_shapes=[
                pltpu.VMEM((MOE_TM, tk), BF16),
                pltpu.VMEM((MOE_TM, tk), BF16),
                pltpu.VMEM((tk, d), BF16),
                pltpu.VMEM((MOE_TM, d), F32),
                pltpu.VMEM((3, 2 * SUB_ROWS * ns, LANES), jnp.uint32),
                pltpu.VMEM((SUB_ROWS * ns, LANES), jnp.uint32),
                pltpu.SemaphoreType.DMA((2,)),
                pltpu.SemaphoreType.DMA((3,)),
            ],
        ),
        out_shape=jax.ShapeDtypeStruct((p_alloc * ns, LANES), jnp.uint32),
        compiler_params=_cparams(("arbitrary", "arbitrary")),
        name="moe_down",
    )(ch_e, ch_start, ch_nsub, tail, act, w_down, b_down.reshape(N_EXPERTS, 1, d))


def _combine_kernel(pos_hbm, ys_hbm, h_ref, w_ref, g_ref, o_ref, pos_smem, yb_ref, sem, psem):
    i = pl.program_id(0)
    tm, d = h_ref.shape
    n = TOP_K * tm
    ns = TOK_WORDS // LANES

    def fetch(tile, slot):
        pcopy = pltpu.make_async_copy(pos_hbm.at[pl.ds(tile * n, n)], pos_smem.at[pl.ds(slot * n, n)], psem)
        pcopy.start()
        pcopy.wait()

        def issue(jj, c):
            for u in range(2):
                j = jj * 2 + u
                src = ys_hbm.at[pl.ds(pl.multiple_of(pos_smem[slot * n + j] * ns, ns), ns)]
                dst = yb_ref.at[slot, pl.ds(pl.multiple_of(j * ns, ns), ns)]
                pltpu.make_async_copy(src, dst, sem.at[slot]).start(priority=u)
            return c

        lax.fori_loop(0, n // 2, issue, 0)

    @pl.when(i == 0)
    def _():
        fetch(i, 0)

    @pl.when(i + 1 < pl.num_programs(0))
    def _():
        for p in range(2):
            @pl.when((i + 1) % 2 == p)
            def _():
                fetch(i + 1, p)

    def reduce(slot):
        pltpu.make_async_copy(ys_hbm.at[pl.ds(0, n * ns)], yb_ref.at[slot], sem.at[slot]).wait()
        wb = [jnp.broadcast_to(w_ref[:, k:k + 1], (tm, LANES)) for k in range(TOP_K)]
        ssq = jnp.zeros((tm, 1), F32)
        for s in range(ns):
            lo_cols = slice(s * LANES, (s + 1) * LANES)
            hi_cols = slice(TOK_WORDS + s * LANES, TOK_WORDS + (s + 1) * LANES)
            acc_lo = h_ref[:, lo_cols]
            acc_hi = h_ref[:, hi_cols]
            for k in range(TOP_K):
                w = yb_ref[slot, pl.ds(k * tm * ns + s, tm, stride=ns), :]
                acc_lo = acc_lo + wb[k] * pltpu.bitcast(w << 16, F32)
                acc_hi = acc_hi + wb[k] * pltpu.bitcast(w & jnp.uint32(0xFFFF0000), F32)
            o_ref[:, lo_cols] = acc_lo
            o_ref[:, hi_cols] = acc_hi
            ssq = ssq + jnp.sum(acc_lo * acc_lo + acc_hi * acc_hi, axis=-1, keepdims=True)
        o_ref[...] = o_ref[...] * lax.rsqrt(ssq * (1.0 / d) + NORM_EPS) * g_ref[...]

    for p in range(2):
        @pl.when(i % 2 == p)
        def _():
            reduce(p)


def _combine(pos_tiles, ys, h1, wts_t, g_final, tm):
    rows, d = h1.shape
    ns = TOK_WORDS // LANES
    return pl.pallas_call(
        _combine_kernel,
        grid=(rows // tm,),
        in_specs=[
            pl.BlockSpec(memory_space=pl.ANY),
            pl.BlockSpec(memory_space=pl.ANY),
            pl.BlockSpec((tm, d), lambda i: (i, 0)),
            pl.BlockSpec((tm, TOP_K), lambda i: (i, 0)),
            pl.BlockSpec((1, d), lambda i: (0, 0)),
        ],
        out_specs=pl.BlockSpec((tm, d), lambda i: (i, 0)),
        out_shape=jax.ShapeDtypeStruct((rows, d), F32),
        scratch_shapes=[
            pltpu.SMEM((2 * TOP_K * tm,), jnp.int32),
            pltpu.VMEM((2, TOP_K * tm * ns, LANES), jnp.uint32),
            pltpu.SemaphoreType.DMA((2,)),
            pltpu.SemaphoreType.DMA,
        ],
        compiler_params=_cparams(("arbitrary",)),
        name="combine",
    )(pos_tiles, ys, h1, wts_t, g_final)


def _rope_tables(n_pos):
    half = HEAD_DIM // 2
    inv = 1.0 / (ROPE_THETA ** (jnp.arange(half, dtype=F32) / half))
    ang = jnp.arange(n_pos, dtype=F32)[:, None] * inv[None, :]
    cos = jnp.tile(jnp.cos(ang), (1, LANES // half))
    sin = jnp.tile(jnp.concatenate([-jnp.sin(ang), jnp.sin(ang)], axis=1), (1, LANES // HEAD_DIM))
    return cos, sin


def _block_diag(w):
    per = LRU_GROUP // LRU_BLOCK
    w4 = w.reshape(D_LRU // LRU_GROUP, per, LRU_BLOCK, LRU_BLOCK)
    eye = jnp.eye(per, dtype=w.dtype)
    bd = jnp.einsum("gpcd,pq->gpcqd", w4, eye)
    return bd.reshape(D_LRU // LRU_GROUP, LRU_GROUP, LRU_GROUP).astype(BF16)


def _tile_positions(pos, tm):
    rows = pos.shape[1]
    return pos.reshape(TOP_K, rows // tm, tm).transpose(1, 0, 2).reshape(-1)


def _chunk_schedule(counts, n_rows):
    aligned = ((counts + SUB_ROWS - 1) // SUB_ROWS) * SUB_ROWS
    offs = jnp.cumsum(aligned) - aligned
    n_ch = (aligned + MOE_TM - 1) // MOE_TM
    cum = jnp.cumsum(n_ch)
    total = cum[-1]
    nch_max = N_EXPERTS + (n_rows + N_EXPERTS * SUB_ROWS) // MOE_TM
    c = jnp.arange(nch_max, dtype=jnp.int32)
    cc = jnp.minimum(c, total - 1)
    e = jnp.searchsorted(cum, cc, side="right").astype(jnp.int32)
    j = cc - (cum[e] - n_ch[e])
    start = offs[e] + j * MOE_TM
    nsub = jnp.where(c < total, jnp.minimum(MOE_TM, aligned[e] - j * MOE_TM) // SUB_ROWS, 0)
    tails = offs + (counts // SUB_ROWS) * SUB_ROWS
    used = jnp.sum(aligned)
    tail = jnp.stack([used, (n_rows + N_EXPERTS * SUB_ROWS - used) // SUB_ROWS])
    tails = jnp.concatenate([tails, tail])
    return (offs, e, start.astype(jnp.int32), nsub.astype(jnp.int32), tails.astype(jnp.int32),
            tail.astype(jnp.int32))


def kernel(x, meta_tokens, norm_mix, w_in, b_in, sinks, conv_w, conv_b, w_a, b_a, w_i, b_i, lru_lambda,
           g_attn_out, g_lru_out, w_out, b_out, norm_ffn, w_router, b_router, w_gate, b_gate, w_up, b_up,
           w_down, b_down, final_norm):
    batch, seq, d = x.shape
    rows = batch * seq
    x2d = x.reshape(rows, d)
    row = lambda v: v.reshape(1, -1)

    w_in_bf = w_in[0].astype(BF16)
    w_out_bf = w_out[0].astype(BF16)
    cos, sin = _rope_tables(N_META + seq)
    wa_bd, wi_bd = _block_diag(w_a[0]), _block_diag(w_i[0])
    lru_args = (conv_w[0], row(conv_b[0]), wa_bd, row(b_a[0]), wi_bd, row(b_i[0]), row(lru_lambda[0]),
                row(g_lru_out[0]))

    _, kvm, xgm = _inproj(meta_tokens, row(norm_mix[0]), w_in_bf, row(b_in[0]), cos[:N_META], sin[:N_META],
                          N_META, 1)
    _, h0, tail0 = _lru(xgm, *lru_args, jnp.zeros((1, D_LRU), F32), jnp.zeros((8, D_LRU), F32), 1, N_META,
                        N_META)

    tm = 512
    q, kv, xg = _inproj(x2d, row(norm_mix[0]), w_in_bf, row(b_in[0]), cos[N_META:], sin[N_META:], tm,
                        seq // tm)
    attn_n = _attention(sinks[0], q, kv, kvm, row(g_attn_out[0]), batch, seq)
    lru_n, _, _ = _lru(xg, *lru_args, h0, tail0, batch, seq, 512)

    tri = (jnp.arange(tm)[:, None] <= jnp.arange(tm)[None, :]).astype(BF16)
    h1, xp, ids, wts, rank, cnt = _outproj(
        attn_n, lru_n, w_out_bf, row(b_out[0]), x2d, row(norm_ffn[0]), w_router[0].T.astype(BF16),
        b_router[0].reshape(N_EXPERTS, 1), tri, tm)

    counts = cnt[:, 0].astype(jnp.int32)
    n_rows = rows * TOP_K
    offs, ch_e, ch_start, ch_nsub, tails, tail = _chunk_schedule(counts, n_rows)
    onehot = ids[..., None] == jnp.arange(N_EXPERTS, dtype=jnp.int32)
    pos = jnp.sum(jnp.where(onehot, offs, 0), axis=-1) + rank
    p_alloc = n_rows + N_EXPERTS * SUB_ROWS

    tmd = 512
    xs = _dispatch(tails, _tile_positions(pos, tmd), xp, p_alloc, tmd)
    act = _moe_up(ch_e, ch_start, ch_nsub, tail, xs, w_gate[0], b_gate[0], w_up[0], b_up[0])
    ys = _moe_down(ch_e, ch_start, ch_nsub, tail, act, w_down[0], b_down[0])
    tmc = 256
    out = _combine(_tile_positions(pos, tmc), ys, h1, wts.T, row(final_norm), tmc)
    return out.reshape(batch, seq, d)
```

```python
import jax
import jax.numpy as jnp
from jax import lax
from jax.experimental import pallas as pl
from jax.experimental.pallas import tpu as pltpu

F32 = jnp.float32
BF16 = jnp.bfloat16

N_META = 16
HEAD_DIM = 64
N_Q_HEADS = 16
N_KV_HEADS = 4
GROUP = N_Q_HEADS // N_KV_HEADS
D_ATTN = N_Q_HEADS * HEAD_DIM
D_KV = N_KV_HEADS * HEAD_DIM
BLOCK = 128
ROPE_THETA = 10000.0
D_LRU = 1024
LRU_BLOCK = 64
LRU_GROUP = 256
CONV_WIDTH = 4
LRU_C = 8.0
N_EXPERTS = 32
TOP_K = 4
SWIGLU_LIMIT = 7.0
SWIGLU_ALPHA = 1.702
NORM_EPS = 1e-5
NEG_INF = -1e30

LANES = 128
D_MODEL = 2048
TOK_WORDS = D_MODEL // 2
SUB_ROWS = 256
MOE_TM = 2304
MOE_TF = 512
MOE_TN = 1024
VMEM_LIMIT = 56 * 1024 * 1024


def _cparams(sem):
    return pltpu.CompilerParams(dimension_semantics=sem, vmem_limit_bytes=VMEM_LIMIT)


def _rms(x, g):
    return x * lax.rsqrt(jnp.mean(x * x, axis=-1, keepdims=True) + NORM_EPS) * g


def _inproj_kernel(x_ref, g_ref, w_ref, b_ref, cos_ref, sin_ref, q_ref, kv_ref, xg_ref):
    xn = _rms(x_ref[...], g_ref[...]).astype(BF16)
    cos = cos_ref[...]
    sin = sin_ref[...]
    lane = lax.broadcasted_iota(jnp.int32, (1, LANES), 1)
    first_half = (lane % HEAD_DIM) < (HEAD_DIM // 2)

    def rope(z):
        partner = jnp.where(first_half, pltpu.roll(z, LANES - HEAD_DIM // 2, 1),
                            pltpu.roll(z, HEAD_DIM // 2, 1))
        return z * cos + partner * sin

    cw = 512
    for c in range(w_ref.shape[1] // cw):
        z = jnp.dot(xn, w_ref[:, c * cw:(c + 1) * cw], preferred_element_type=F32)
        z = z + b_ref[:, c * cw:(c + 1) * cw]
        if c < 2:
            for j in range(cw // LANES):
                zz = rope(z[:, j * LANES:(j + 1) * LANES]) * (HEAD_DIM ** -0.5)
                q_ref[:, c * cw + j * LANES:c * cw + (j + 1) * LANES] = zz.astype(BF16)
        elif c == 2:
            for j in range(D_KV // LANES):
                kv_ref[:, j * LANES:(j + 1) * LANES] = rope(z[:, j * LANES:(j + 1) * LANES]).astype(BF16)
            kv_ref[:, D_KV:] = z[:, D_KV:].astype(BF16)
        else:
            xg_ref[:, (c - 3) * cw:(c - 2) * cw] = z


def _inproj(x2d, g, w_bf, b, cos, sin, tm, pos_blocks):
    rows, d = x2d.shape
    dz = w_bf.shape[1]
    return pl.pallas_call(
        _inproj_kernel,
        grid=(rows // tm,),
        in_specs=[
            pl.BlockSpec((tm, d), lambda m: (m, 0)),
            pl.BlockSpec((1, d), lambda m: (0, 0)),
            pl.BlockSpec((d, dz), lambda m: (0, 0)),
            pl.BlockSpec((1, dz), lambda m: (0, 0)),
            pl.BlockSpec((tm, LANES), lambda m: (m % pos_blocks, 0)),
            pl.BlockSpec((tm, LANES), lambda m: (m % pos_blocks, 0)),
        ],
        out_specs=[
            pl.BlockSpec((tm, D_ATTN), lambda m: (m, 0)),
            pl.BlockSpec((tm, 2 * D_KV), lambda m: (m, 0)),
            pl.BlockSpec((tm, 2 * D_LRU), lambda m: (m, 0)),
        ],
        out_shape=[
            jax.ShapeDtypeStruct((rows, D_ATTN), BF16),
            jax.ShapeDtypeStruct((rows, 2 * D_KV), BF16),
            jax.ShapeDtypeStruct((rows, 2 * D_LRU), F32),
        ],
        compiler_params=_cparams(("arbitrary",)),
        name="inproj",
    )(x2d, g, w_bf, b, cos, sin)


def _attn_kernel(sink_ref, q_ref, kvc_ref, kvp_ref, kvm_ref, g_ref, o_ref):
    n = pl.program_id(1)
    row = lax.broadcasted_iota(jnp.int32, (GROUP * BLOCK, BLOCK), 0) % BLOCK
    col = lax.broadcasted_iota(jnp.int32, (GROUP * BLOCK, BLOCK), 1)
    ok_cur = col <= row
    ok_prev = jnp.logical_and(col > row, n > 0)
    q = q_ref[...]
    kvc = kvc_ref[...]
    kvp = kvp_ref[...]
    kvm = kvm_ref[...]
    nt = (((1,), (1,)), ((), ()))
    outs = []
    for h in range(N_KV_HEADS):
        ks = slice(h * HEAD_DIM, (h + 1) * HEAD_DIM)
        vs = slice(D_KV + h * HEAD_DIM, D_KV + (h + 1) * HEAD_DIM)
        qh = jnp.concatenate(
            [q[:, (h * GROUP + g) * HEAD_DIM:(h * GROUP + g + 1) * HEAD_DIM] for g in range(GROUP)], axis=0)
        s_c = jnp.where(ok_cur, lax.dot_general(qh, kvc[:, ks], nt, preferred_element_type=F32), NEG_INF)
        s_p = jnp.where(ok_prev, lax.dot_general(qh, kvp[:, ks], nt, preferred_element_type=F32), NEG_INF)
        s_m = lax.dot_general(qh, kvm[:, ks], nt, preferred_element_type=F32)
        sink = jnp.concatenate(
            [jnp.full((BLOCK, 1), sink_ref[h * GROUP + g], F32) for g in range(GROUP)], axis=0)
        m = jnp.maximum(jnp.maximum(jnp.max(s_c, axis=-1, keepdims=True), jnp.max(s_p, axis=-1, keepdims=True)),
                        jnp.maximum(jnp.max(s_m, axis=-1, keepdims=True), sink))
        p_c = jnp.exp(s_c - m)
        p_p = jnp.exp(s_p - m)
        p_m = jnp.exp(s_m - m)
        den = (jnp.sum(p_c, axis=-1, keepdims=True) + jnp.sum(p_p, axis=-1, keepdims=True)
               + jnp.sum(p_m, axis=-1, keepdims=True) + jnp.exp(sink - m))
        o = (jnp.dot(p_c.astype(BF16), kvc[:, vs], preferred_element_type=F32)
             + jnp.dot(p_p.astype(BF16), kvp[:, vs], preferred_element_type=F32)
             + jnp.dot(p_m.astype(BF16), kvm[:, vs], preferred_element_type=F32))
        o = o / den
        outs.extend(o[g * BLOCK:(g + 1) * BLOCK] for g in range(GROUP))
    o_all = jnp.concatenate(outs, axis=1)
    o_ref[...] = _rms(o_all, g_ref[...]).astype(BF16)


def _attention(sinks, q, kv, kvm, g_attn, batch, seq):
    nb = seq // BLOCK
    return pl.pallas_call(
        _attn_kernel,
        grid_spec=pltpu.PrefetchScalarGridSpec(
            num_scalar_prefetch=1,
            grid=(batch, nb),
            in_specs=[
                pl.BlockSpec((BLOCK, D_ATTN), lambda b, n, s: (b * nb + n, 0)),
                pl.BlockSpec((BLOCK, 2 * D_KV), lambda b, n, s: (b * nb + n, 0)),
                pl.BlockSpec((BLOCK, 2 * D_KV), lambda b, n, s: (b * nb + jnp.maximum(n - 1, 0), 0)),
                pl.BlockSpec((N_META, 2 * D_KV), lambda b, n, s: (0, 0)),
                pl.BlockSpec((1, D_ATTN), lambda b, n, s: (0, 0)),
            ],
            out_specs=pl.BlockSpec((BLOCK, D_ATTN), lambda b, n, s: (b * nb + n, 0)),
        ),
        out_shape=jax.ShapeDtypeStruct((batch * seq, D_ATTN), BF16),
        compiler_params=_cparams(("arbitrary", "arbitrary")),
        name="attention",
    )(sinks, q, kv, kv, kvm, g_attn)


def _expm1(y):
    p = 1.0 + y * (1.0 / 8.0)
    for k in range(7, 1, -1):
        p = 1.0 + y * p * (1.0 / k)
    return jnp.where(y > -0.25, y * p, jnp.exp(y) - 1.0)


def _lru_kernel(xg_ref, cw_ref, cb_ref, wa_ref, ba_ref, wi_ref, bi_ref, lam_ref, g_ref, h0_ref, tail0_ref,
                o_ref, hout_ref, tailout_ref, ext_ref, a_ref, b_ref, h_ref):
    tt = pl.program_id(1)
    rows = a_ref.shape[0]

    @pl.when(tt == 0)
    def _():
        h_ref[...] = h0_ref[...]
        ext_ref[0:8, :] = tail0_ref[...]

    ext_ref[8:, :] = xg_ref[:, :D_LRU]
    xc = cb_ref[...] + sum(cw_ref[j:j + 1, :] * ext_ref[5 + j:5 + j + rows, :] for j in range(CONV_WIDTH))
    xcb = xc.astype(BF16)
    sp = jax.nn.softplus(-lam_ref[...])
    for c in range(D_LRU // LRU_GROUP):
        cs = slice(c * LRU_GROUP, (c + 1) * LRU_GROUP)
        r = jax.nn.sigmoid(jnp.dot(xcb[:, cs], wa_ref[c], preferred_element_type=F32) + ba_ref[:, cs])
        i = jax.nn.sigmoid(jnp.dot(xcb[:, cs], wi_ref[c], preferred_element_type=F32) + bi_ref[:, cs])
        log_a = -LRU_C * r * sp[:, cs]
        a_ref[:, cs] = jnp.exp(log_a)
        b_ref[:, cs] = jnp.sqrt(-_expm1(2.0 * log_a)) * i * xc[:, cs]

    def step(t, h):
        h = a_ref[pl.ds(t, 1), :] * h + b_ref[pl.ds(t, 1), :]
        b_ref[pl.ds(t, 1), :] = h
        return h

    h_last = lax.fori_loop(0, rows, step, h_ref[...])
    h_ref[...] = h_last
    hout_ref[...] = h_last
    tail = ext_ref[rows:rows + 8, :]
    ext_ref[0:8, :] = tail
    tailout_ref[...] = tail
    y = b_ref[...] * jax.nn.gelu(xg_ref[:, D_LRU:])
    o_ref[...] = _rms(y, g_ref[...]).astype(BF16)


def _lru(xg, conv_w, conv_b, wa_bd, b_a, wi_bd, b_i, lam, g_lru, h0, tail0, batch, seq, tt):
    ntt = seq // tt
    vec = pl.BlockSpec((1, D_LRU), lambda b, t: (0, 0))
    wspec = pl.BlockSpec((D_LRU // LRU_GROUP, LRU_GROUP, LRU_GROUP), lambda b, t: (0, 0, 0))
    return pl.pallas_call(
        _lru_kernel,
        grid=(batch, ntt),
        in_specs=[
            pl.BlockSpec((tt, 2 * D_LRU), lambda b, t: (b * ntt + t, 0)),
            pl.BlockSpec((CONV_WIDTH, D_LRU), lambda b, t: (0, 0)),
            vec, wspec, vec, wspec, vec, vec, vec, vec,
            pl.BlockSpec((8, D_LRU), lambda b, t: (0, 0)),
        ],
        out_specs=[
            pl.BlockSpec((tt, D_LRU), lambda b, t: (b * ntt + t, 0)),
            pl.BlockSpec((1, D_LRU), lambda b, t: (0, 0)),
            pl.BlockSpec((8, D_LRU), lambda b, t: (0, 0)),
        ],
        out_shape=[
            jax.ShapeDtypeStruct((batch * seq, D_LRU), BF16),
            jax.ShapeDtypeStruct((1, D_LRU), F32),
            jax.ShapeDtypeStruct((8, D_LRU), F32),
        ],
        scratch_shapes=[
            pltpu.VMEM((8 + tt, D_LRU), F32),
            pltpu.VMEM((tt, D_LRU), F32),
            pltpu.VMEM((tt, D_LRU), F32),
            pltpu.VMEM((1, D_LRU), F32),
        ],
        compiler_params=_cparams(("arbitrary", "arbitrary")),
        name="rglru",
    )(xg, conv_w, conv_b, wa_bd, b_a, wi_bd, b_i, lam, g_lru, h0, tail0)


def _outproj_kernel(a_ref, l_ref, wo_ref, bo_ref, x_ref, gf_ref, wr_ref, br_ref, tri_ref,
                    h_ref, xp_ref, ids_ref, wts_ref, rank_ref, cnt_ref, carry_ref):
    m = pl.program_id(0)
    tm = a_ref.shape[0]

    @pl.when(m == 0)
    def _():
        carry_ref[...] = jnp.zeros_like(carry_ref)

    h = (jnp.dot(a_ref[...], wo_ref[:D_ATTN, :], preferred_element_type=F32)
         + jnp.dot(l_ref[...], wo_ref[D_ATTN:, :], preferred_element_type=F32)
         + bo_ref[...] + x_ref[...])
    h_ref[...] = h
    xn = _rms(h, gf_ref[...]).astype(BF16)
    half = xn.shape[1] // 2
    lo = pltpu.bitcast(xn[:, :half].astype(F32), jnp.uint32)
    hi = pltpu.bitcast(xn[:, half:].astype(F32), jnp.uint32)
    packed = (hi & jnp.uint32(0xFFFF0000)) | (lo >> 16)
    ns = half // LANES
    for s in range(ns):
        xp_ref[pl.ds(s, tm, stride=ns), :] = packed[:, s * LANES:(s + 1) * LANES]

    logits = lax.dot_general(wr_ref[...], xn, (((1,), (1,)), ((), ())), preferred_element_type=F32)
    logits = logits + br_ref[...]
    eidx = lax.broadcasted_iota(jnp.int32, (N_EXPERTS, tm), 0)
    work = logits
    vals, sels = [], []
    for k in range(TOP_K):
        v = jnp.max(work, axis=0, keepdims=True)
        idx = jnp.min(jnp.where(work == v, eidx, N_EXPERTS), axis=0, keepdims=True)
        sel = eidx == idx
        ids_ref[k:k + 1, :] = idx
        vals.append(v)
        sels.append(sel)
        work = jnp.where(sel, -jnp.inf, work)
    es = [jnp.exp(v - vals[0]) for v in vals]
    den = es[0] + es[1] + es[2] + es[3]
    for k in range(TOP_K):
        wts_ref[k:k + 1, :] = es[k] / den
    cnt = sum(s.astype(F32) for s in sels)
    incl = jnp.dot(cnt.astype(BF16), tri_ref[...], preferred_element_type=F32)
    before = incl - cnt + carry_ref[:, 0:1]
    for k in range(TOP_K):
        rk = jnp.sum(jnp.where(sels[k], before, 0.0), axis=0, keepdims=True)
        rank_ref[k:k + 1, :] = rk.astype(jnp.int32)
    carry_ref[...] = carry_ref[...] + incl[:, tm - 1:tm]
    cnt_ref[...] = carry_ref[...]


def _outproj(attn_n, lru_n, wo_bf, b_out, x2d, g_ffn, wr_t, br, tri, tm):
    rows, d = x2d.shape
    const = lambda shape: pl.BlockSpec(shape, lambda m: tuple(0 for _ in shape))
    return pl.pallas_call(
        _outproj_kernel,
        grid=(rows // tm,),
        in_specs=[
            pl.BlockSpec((tm, D_ATTN), lambda m: (m, 0)),
            pl.BlockSpec((tm, D_LRU), lambda m: (m, 0)),
            const((D_ATTN + D_LRU, d)), const((1, d)),
            pl.BlockSpec((tm, d), lambda m: (m, 0)),
            const((1, d)), const((N_EXPERTS, d)), const((N_EXPERTS, 1)), const((tm, tm)),
        ],
        out_specs=[
            pl.BlockSpec((tm, d), lambda m: (m, 0)),
            pl.BlockSpec((tm * (d // 2 // LANES), LANES), lambda m: (m, 0)),
            pl.BlockSpec((TOP_K, tm), lambda m: (0, m)),
            pl.BlockSpec((TOP_K, tm), lambda m: (0, m)),
            pl.BlockSpec((TOP_K, tm), lambda m: (0, m)),
            const((N_EXPERTS, LANES)),
        ],
        out_shape=[
            jax.ShapeDtypeStruct((rows, d), F32),
            jax.ShapeDtypeStruct((rows * (d // 2 // LANES), LANES), jnp.uint32),
            jax.ShapeDtypeStruct((TOP_K, rows), jnp.int32),
            jax.ShapeDtypeStruct((TOP_K, rows), F32),
            jax.ShapeDtypeStruct((TOP_K, rows), jnp.int32),
            jax.ShapeDtypeStruct((N_EXPERTS, LANES), F32),
        ],
        scratch_shapes=[pltpu.VMEM((N_EXPERTS, LANES), F32)],
        compiler_params=_cparams(("arbitrary",)),
        name="outproj_router",
    )(attn_n, lru_n, wo_bf, b_out, x2d, g_ffn, wr_t, br, tri)


def _zero_fill_rows(zero_ref, dst_hbm, first_row, n_blocks, sem):
    blk = zero_ref.shape[0]
    per = blk // SUB_ROWS

    def copy(j):
        row = pl.multiple_of((first_row + j * SUB_ROWS) * per, blk)
        return pltpu.make_async_copy(zero_ref, dst_hbm.at[pl.ds(row, blk)], sem)

    def start(j, c):
        copy(j).start()
        return c

    def wait(j, c):
        copy(j).wait()
        return c

    lax.fori_loop(0, n_blocks, start, 0)
    lax.fori_loop(0, n_blocks, wait, 0)


def _dispatch_kernel(tails_ref, pos_hbm, xp_ref, xs_hbm, pos_smem, zero_ref, sem, psem):
    i = pl.program_id(0)
    ns = TOK_WORDS // LANES
    tm = xp_ref.shape[0] // ns
    n = TOP_K * tm
    pcopy = pltpu.make_async_copy(pos_hbm.at[pl.ds(i * n, n)], pos_smem, psem)
    pcopy.start()

    @pl.when(i == 0)
    def _():
        zero_ref[...] = jnp.zeros_like(zero_ref)
        for e in range(N_EXPERTS):
            _zero_fill_rows(zero_ref, xs_hbm, tails_ref[e], 1, sem)
        _zero_fill_rows(zero_ref, xs_hbm, tails_ref[N_EXPERTS], tails_ref[N_EXPERTS + 1], sem)

    pcopy.wait()

    def issue(jj, c):
        for u in range(2):
            j = jj * 2 + u
            src = xp_ref.at[pl.ds(pl.multiple_of((j % tm) * ns, ns), ns)]
            dst = xs_hbm.at[pl.ds(pl.multiple_of(pos_smem[j] * ns, ns), ns)]
            pltpu.make_async_copy(src, dst, sem).start(priority=u)
        return c

    lax.fori_loop(0, n // 2, issue, 0)
    for _ in range(TOP_K):
        pltpu.make_async_copy(xp_ref, xs_hbm.at[pl.ds(0, tm * ns)], sem).wait()


def _dispatch(tails, pos_tiles, xp, p_alloc, tm):
    ns = TOK_WORDS // LANES
    rows = xp.shape[0] // ns
    return pl.pallas_call(
        _dispatch_kernel,
        grid_spec=pltpu.PrefetchScalarGridSpec(
            num_scalar_prefetch=1,
            grid=(rows // tm,),
            in_specs=[
                pl.BlockSpec(memory_space=pl.ANY),
                pl.BlockSpec((tm * ns, LANES), lambda i, s: (i, 0)),
            ],
            out_specs=pl.BlockSpec(memory_space=pl.ANY),
            scratch_shapes=[
                pltpu.SMEM((TOP_K * tm,), jnp.int32),
                pltpu.VMEM((SUB_ROWS * ns, LANES), jnp.uint32),
                pltpu.SemaphoreType.DMA,
                pltpu.SemaphoreType.DMA,
            ],
        ),
        out_shape=jax.ShapeDtypeStruct((p_alloc * ns, LANES), jnp.uint32),
        compiler_params=_cparams(("arbitrary",)),
        name="dispatch",
    )(tails, pos_tiles, xp)


def _chunk_dma(cs_ref, cn_ref, src_hbm, dst_ref, per, sem):
    blk = SUB_ROWS * per
    nch = cs_ref.shape[0]

    def copy(ci, j):
        src = src_hbm.at[pl.ds(pl.multiple_of((cs_ref[ci] + j * SUB_ROWS) * per, blk), blk)]
        return pltpu.make_async_copy(src, dst_ref.at[pl.ds(pl.multiple_of(j * blk, blk), blk)], sem)

    def run(ci, wait):
        cc = jnp.minimum(ci, nch - 1)
        n = jnp.where(ci < nch, cn_ref[cc], 0)

        def body(j, c):
            if wait:
                copy(cc, j).wait()
            else:
                copy(cc, j).start()
            return c

        lax.fori_loop(0, n, body, 0)

    return (lambda ci: run(ci, False)), (lambda ci: run(ci, True))


def _tile_loop(nsub, tile_fn):
    n2 = nsub // 2

    @pl.when(nsub % 2 == 1)
    def _():
        tile_fn(pl.multiple_of(n2 * 2 * SUB_ROWS, SUB_ROWS), SUB_ROWS, 2, False)

    def pair(j, c):
        tile_fn(pl.multiple_of(j * 2 * SUB_ROWS, 2 * SUB_ROWS), 2 * SUB_ROWS, j % 2, j >= 2)
        return c

    lax.fori_loop(0, n2, pair, 0)
    return n2


def _moe_up_kernel(ce_ref, cs_ref, cn_ref, tail_ref, xs_hbm, wg_ref, bg_ref, wu_ref, bu_ref, act_hbm,
                   land_ref, xb_ref, wgb_ref, wub_ref, ab_ref, zero_ref, sem, osem):
    c = pl.program_id(0)
    f = pl.program_id(1)
    nsub = cn_ref[c]
    start = pl.multiple_of(cs_ref[c], SUB_ROWS)
    tf = wg_ref.shape[2]
    ns = TOK_WORDS // LANES
    start_chunk, wait_chunk = _chunk_dma(cs_ref, cn_ref, xs_hbm, land_ref, ns, sem)

    @pl.when(jnp.logical_and(c == 0, f == 0))
    def _():
        zero_ref[...] = jnp.zeros_like(zero_ref)
        _zero_fill_rows(zero_ref, act_hbm, tail_ref[0], tail_ref[1], osem.at[0])
        start_chunk(c)

    @pl.when(jnp.logical_and(f == 0, nsub > 0))
    def _():
        wait_chunk(c)

        def unpack(j, carry):
            r = pl.multiple_of(j * SUB_ROWS, SUB_ROWS)
            for s in range(ns):
                w = land_ref[pl.ds(r * ns + s, SUB_ROWS, stride=ns), :]
                xb_ref[pl.ds(r, SUB_ROWS), s * LANES:(s + 1) * LANES] = pltpu.bitcast(w << 16, F32).astype(BF16)
                xb_ref[pl.ds(r, SUB_ROWS), TOK_WORDS + s * LANES:TOK_WORDS + (s + 1) * LANES] = (
                    pltpu.bitcast(w & jnp.uint32(0xFFFF0000), F32).astype(BF16))
            return carry

        lax.fori_loop(0, nsub, unpack, 0)
        start_chunk(c + 1)

    def out_copy(r, rows, slot):
        return pltpu.make_async_copy(
            ab_ref.at[slot, pl.ds(0, rows)],
            act_hbm.at[pl.ds(pl.multiple_of(start + r, SUB_ROWS), rows), pl.ds(pl.multiple_of(f * tf, tf), tf)],
            osem.at[slot])

    @pl.when(nsub > 0)
    def _():
        wgb_ref[...] = wg_ref[0].astype(BF16)
        wub_ref[...] = wu_ref[0].astype(BF16)
        bg = bg_ref[0]
        bu = bu_ref[0]

        def tile(r, rows, slot, wait_before):
            x = xb_ref[pl.ds(r, rows), :]
            g = jnp.dot(x, wgb_ref[...], preferred_element_type=F32) + bg
            u = jnp.dot(x, wub_ref[...], preferred_element_type=F32) + bu
            g = jnp.minimum(g, SWIGLU_LIMIT)
            u = jnp.clip(u, -SWIGLU_LIMIT, SWIGLU_LIMIT)
            a = (g * jax.nn.sigmoid(SWIGLU_ALPHA * g) * (u + 1.0)).astype(BF16)
            if wait_before is not False:
                @pl.when(wait_before)
                def _():
                    out_copy(r, rows, slot).wait()
            ab_ref[slot, pl.ds(0, rows), :] = a
            out_copy(r, rows, slot).start()

        n2 = _tile_loop(nsub, tile)
        for k in range(2):
            @pl.when(n2 > k)
            def _():
                out_copy(0, 2 * SUB_ROWS, (n2 - 1 - k) % 2).wait()

        @pl.when(nsub % 2 == 1)
        def _():
            out_copy(0, SUB_ROWS, 2).wait()


def _moe_up(ch_e, ch_start, ch_nsub, tail, xs, w_gate, b_gate, w_up, b_up):
    ns = TOK_WORDS // LANES
    p_alloc = xs.shape[0] // ns
    d = 2 * TOK_WORDS
    dff = w_gate.shape[2]
    tf = MOE_TF
    nch = ch_e.shape[0]
    nf = dff // tf
    wmap = lambda c, f, e, s, n, t: (e[c], 0, jnp.where(n[c] > 0, f, nf - 1))
    return pl.pallas_call(
        _moe_up_kernel,
        grid_spec=pltpu.PrefetchScalarGridSpec(
            num_scalar_prefetch=4,
            grid=(nch, nf),
            in_specs=[
                pl.BlockSpec(memory_space=pl.ANY),
                pl.BlockSpec((1, d, tf), wmap),
                pl.BlockSpec((1, 1, tf), wmap),
                pl.BlockSpec((1, d, tf), wmap),
                pl.BlockSpec((1, 1, tf), wmap),
            ],
            out_specs=pl.BlockSpec(memory_space=pl.ANY),
            scratch_shapes=[
                pltpu.VMEM((MOE_TM * ns, LANES), jnp.uint32),
                pltpu.VMEM((MOE_TM, d), BF16),
                pltpu.VMEM((d, tf), BF16),
                pltpu.VMEM((d, tf), BF16),
                pltpu.VMEM((3, 2 * SUB_ROWS, tf), BF16),
                pltpu.VMEM((SUB_ROWS, dff), BF16),
                pltpu.SemaphoreType.DMA,
                pltpu.SemaphoreType.DMA((3,)),
            ],
        ),
        out_shape=jax.ShapeDtypeStruct((p_alloc, dff), BF16),
        compiler_params=_cparams(("arbitrary", "arbitrary")),
        name="moe_up",
    )(ch_e, ch_start, ch_nsub, tail, xs, w_gate, b_gate.reshape(N_EXPERTS, 1, dff), w_up,
      b_up.reshape(N_EXPERTS, 1, dff))


def _moe_down_kernel(ce_ref, cs_ref, cn_ref, tail_ref, act_hbm, wd_ref, bd_ref, ys_hbm,
                     xa_ref, xb_ref, wdb_ref, yb_ref, zero_ref, sem, osem):
    c = pl.program_id(0)
    f = pl.program_id(1)
    nf = pl.num_programs(1)
    nsub = cn_ref[c]
    start = pl.multiple_of(cs_ref[c], SUB_ROWS)
    tn = wd_ref.shape[2]
    ns = TOK_WORDS // LANES
    nw = tn // 2 // LANES
    copies = [_chunk_dma(cs_ref, cn_ref, act_hbm, ref, 1, sem.at[i]) for i, ref in enumerate((xa_ref, xb_ref))]

    def on_parity(ci, fn):
        for p in range(2):
            @pl.when(ci % 2 == p)
            def _():
                fn(p)

    @pl.when(jnp.logical_and(c == 0, f == 0))
    def _():
        zero_ref[...] = jnp.zeros_like(zero_ref)
        _zero_fill_rows(zero_ref, ys_hbm, tail_ref[0], tail_ref[1], osem)
        copies[0][0](c)

    @pl.when(jnp.logical_and(f == 0, nsub > 0))
    def _():
        on_parity(c, lambda p: copies[p][1](c))
        on_parity(c + 1, lambda p: copies[p][0](c + 1))

    def out_copy(r, rows):
        src = yb_ref.at[pl.ds(pl.multiple_of(r * ns, SUB_ROWS * ns), rows * ns)]
        dst = ys_hbm.at[pl.ds(pl.multiple_of((start + r) * ns, SUB_ROWS * ns), rows * ns)]
        return pltpu.make_async_copy(src, dst, osem)

    def compute(x_ref):
        wdb_ref[...] = wd_ref[0].astype(BF16)
        bd = bd_ref[0]
        last = f == nf - 1

        def tile(r, rows, slot, wait_before):
            y = jnp.dot(x_ref[pl.ds(r, rows), :], wdb_ref[...], preferred_element_type=F32) + bd
            lo = pltpu.bitcast(y[:, :tn // 2].astype(BF16).astype(F32), jnp.uint32)
            hi = pltpu.bitcast(y[:, tn // 2:].astype(BF16).astype(F32), jnp.uint32)
            packed = (hi & jnp.uint32(0xFFFF0000)) | (lo >> 16)
            for s in range(nw):
                yb_ref[pl.ds(r * ns + f * nw + s, rows, stride=ns), :] = packed[:, s * LANES:(s + 1) * LANES]

            @pl.when(last)
            def _():
                out_copy(r, rows).start()

        _tile_loop(nsub, tile)

        @pl.when(last)
        def _():
            def drain(j, carry):
                out_copy(0, 2 * SUB_ROWS).wait()
                return carry

            lax.fori_loop(0, nsub // 2, drain, 0)

            @pl.when(nsub % 2 == 1)
            def _():
                out_copy(0, SUB_ROWS).wait()

    @pl.when(nsub > 0)
    def _():
        on_parity(c, lambda p: compute((xa_ref, xb_ref)[p]))


def _moe_down(ch_e, ch_start, ch_nsub, tail, act, w_down, b_down):
    p_alloc, dff = act.shape
    d = w_down.shape[2]
    ns = TOK_WORDS // LANES
    tn = MOE_TN
    nch = ch_e.shape[0]
    nn = d // tn
    wmap = lambda c, f, e, s, n, t: (e[c], 0, jnp.where(n[c] > 0, f, nn - 1))
    return pl.pallas_call(
        _moe_down_kernel,
        grid_spec=pltpu.PrefetchScalarGridSpec(
            num_scalar_prefetch=4,
            grid=(nch, nn),
            in_specs=[
                pl.BlockSpec(memory_space=pl.ANY),
                pl.BlockSpec((1, dff, tn), wmap),
                pl.BlockSpec((1, 1, tn), wmap),
            ],
            out_specs=pl.BlockSpec(memory_space=pl.ANY),
            scratch_shapes=[
                pltpu.VMEM((MOE_TM, dff), BF16),
                pltpu.VMEM((MOE_TM, dff), BF16),
                pltpu.VMEM((dff, tn), BF16),
                pltpu.VMEM((MOE_TM * ns, LANES), jnp.uint32),
                pltpu.VMEM((SUB_ROWS * ns, LANES), jnp.uint32),
                pltpu.SemaphoreType.DMA((2,)),
                pltpu.SemaphoreType.DMA,
            ],
        ),
        out_shape=jax.ShapeDtypeStruct((p_alloc * ns, LANES), jnp.uint32),
        compiler_params=_cparams(("arbitrary", "arbitrary")),
        name="moe_down",
    )(ch_e, ch_start, ch_nsub, tail, act, w_down, b_down.reshape(N_EXPERTS, 1, d))


def _combine_kernel(pos_hbm, ys_hbm, h_ref, w_ref, g_ref, o_ref, pos_smem, yb_ref, sem, psem):
    i = pl.program_id(0)
    tm, d = h_ref.shape
    n = TOP_K * tm
    ns = TOK_WORDS // LANES

    def fetch(tile, slot):
        pcopy = pltpu.make_async_copy(pos_hbm.at[pl.ds(tile * n, n)], pos_smem.at[pl.ds(slot * n, n)], psem)
        pcopy.start()
        pcopy.wait()

        def issue(jj, c):
            for u in range(2):
                j = jj * 2 + u
                src = ys_hbm.at[pl.ds(pl.multiple_of(pos_smem[slot * n + j] * ns, ns), ns)]
                dst = yb_ref.at[slot, pl.ds(pl.multiple_of(j * ns, ns), ns)]
                pltpu.make_async_copy(src, dst, sem.at[slot]).start(priority=u)
            return c

        lax.fori_loop(0, n // 2, issue, 0)

    @pl.when(i == 0)
    def _():
        fetch(i, 0)

    @pl.when(i + 1 < pl.num_programs(0))
    def _():
        for p in range(2):
            @pl.when((i + 1) % 2 == p)
            def _():
                fetch(i + 1, p)

    def reduce(slot):
        pltpu.make_async_copy(ys_hbm.at[pl.ds(0, n * ns)], yb_ref.at[slot], sem.at[slot]).wait()
        wb = [jnp.broadcast_to(w_ref[:, k:k + 1], (tm, LANES)) for k in range(TOP_K)]
        ssq = jnp.zeros((tm, 1), F32)
        nw = MOE_TN // 2 // LANES
        for s in range(ns):
            c0 = (s // nw) * MOE_TN + (s % nw) * LANES
            lo_cols = slice(c0, c0 + LANES)
            hi_cols = slice(c0 + MOE_TN // 2, c0 + MOE_TN // 2 + LANES)
            acc_lo = h_ref[:, lo_cols]
            acc_hi = h_ref[:, hi_cols]
            for k in range(TOP_K):
                w = yb_ref[slot, pl.ds(k * tm * ns + s, tm, stride=ns), :]
                acc_lo = acc_lo + wb[k] * pltpu.bitcast(w << 16, F32)
                acc_hi = acc_hi + wb[k] * pltpu.bitcast(w & jnp.uint32(0xFFFF0000), F32)
            o_ref[:, lo_cols] = acc_lo
            o_ref[:, hi_cols] = acc_hi
            ssq = ssq + jnp.sum(acc_lo * acc_lo + acc_hi * acc_hi, axis=-1, keepdims=True)
        o_ref[...] = o_ref[...] * lax.rsqrt(ssq * (1.0 / d) + NORM_EPS) * g_ref[...]

    for p in range(2):
        @pl.when(i % 2 == p)
        def _():
            reduce(p)


def _combine(pos_tiles, ys, h1, wts_t, g_final, tm):
    rows, d = h1.shape
    ns = TOK_WORDS // LANES
    return pl.pallas_call(
        _combine_kernel,
        grid=(rows // tm,),
        in_specs=[
            pl.BlockSpec(memory_space=pl.ANY),
            pl.BlockSpec(memory_space=pl.ANY),
            pl.BlockSpec((tm, d), lambda i: (i, 0)),
            pl.BlockSpec((tm, TOP_K), lambda i: (i, 0)),
            pl.BlockSpec((1, d), lambda i: (0, 0)),
        ],
        out_specs=pl.BlockSpec((tm, d), lambda i: (i, 0)),
        out_shape=jax.ShapeDtypeStruct((rows, d), F32),
        scratch_shapes=[
            pltpu.SMEM((2 * TOP_K * tm,), jnp.int32),
            pltpu.VMEM((2, TOP_K * tm * ns, LANES), jnp.uint32),
            pltpu.SemaphoreType.DMA((2,)),
            pltpu.SemaphoreType.DMA,
        ],
        compiler_params=_cparams(("arbitrary",)),
        name="combine",
    )(pos_tiles, ys, h1, wts_t, g_final)


def _rope_tables(n_pos):
    half = HEAD_DIM // 2
    inv = 1.0 / (ROPE_THETA ** (jnp.arange(half, dtype=F32) / half))
    ang = jnp.arange(n_pos, dtype=F32)[:, None] * inv[None, :]
    cos = jnp.tile(jnp.cos(ang), (1, LANES // half))
    sin = jnp.tile(jnp.concatenate([-jnp.sin(ang), jnp.sin(ang)], axis=1), (1, LANES // HEAD_DIM))
    return cos, sin


def _block_diag(w):
    per = LRU_GROUP // LRU_BLOCK
    w4 = w.reshape(D_LRU // LRU_GROUP, per, LRU_BLOCK, LRU_BLOCK)
    eye = jnp.eye(per, dtype=w.dtype)
    bd = jnp.einsum("gpcd,pq->gpcqd", w4, eye)
    return bd.reshape(D_LRU // LRU_GROUP, LRU_GROUP, LRU_GROUP).astype(BF16)


def _tile_positions(pos, tm):
    rows = pos.shape[1]
    return pos.reshape(TOP_K, rows // tm, tm).transpose(1, 0, 2).reshape(-1)


def _chunk_schedule(counts, n_rows):
    aligned = ((counts + SUB_ROWS - 1) // SUB_ROWS) * SUB_ROWS
    offs = jnp.cumsum(aligned) - aligned
    n_ch = (aligned + MOE_TM - 1) // MOE_TM
    cum = jnp.cumsum(n_ch)
    total = cum[-1]
    nch_max = N_EXPERTS + (n_rows + N_EXPERTS * SUB_ROWS) // MOE_TM
    c = jnp.arange(nch_max, dtype=jnp.int32)
    cc = jnp.minimum(c, total - 1)
    e = jnp.searchsorted(cum, cc, side="right").astype(jnp.int32)
    j = cc - (cum[e] - n_ch[e])
    start = offs[e] + j * MOE_TM
    nsub = jnp.where(c < total, jnp.minimum(MOE_TM, aligned[e] - j * MOE_TM) // SUB_ROWS, 0)
    tails = offs + (counts // SUB_ROWS) * SUB_ROWS
    used = jnp.sum(aligned)
    tail = jnp.stack([used, (n_rows + N_EXPERTS * SUB_ROWS - used) // SUB_ROWS])
    tails = jnp.concatenate([tails, tail])
    return (offs, e, start.astype(jnp.int32), nsub.astype(jnp.int32), tails.astype(jnp.int32),
            tail.astype(jnp.int32))


def kernel(x, meta_tokens, norm_mix, w_in, b_in, sinks, conv_w, conv_b, w_a, b_a, w_i, b_i, lru_lambda,
           g_attn_out, g_lru_out, w_out, b_out, norm_ffn, w_router, b_router, w_gate, b_gate, w_up, b_up,
           w_down, b_down, final_norm):
    batch, seq, d = x.shape
    rows = batch * seq
    x2d = x.reshape(rows, d)
    row = lambda v: v.reshape(1, -1)

    w_in_bf = w_in[0].astype(BF16)
    w_out_bf = w_out[0].astype(BF16)
    cos, sin = _rope_tables(N_META + seq)
    wa_bd, wi_bd = _block_diag(w_a[0]), _block_diag(w_i[0])
    lru_args = (conv_w[0], row(conv_b[0]), wa_bd, row(b_a[0]), wi_bd, row(b_i[0]), row(lru_lambda[0]),
                row(g_lru_out[0]))

    _, kvm, xgm = _inproj(meta_tokens, row(norm_mix[0]), w_in_bf, row(b_in[0]), cos[:N_META], sin[:N_META],
                          N_META, 1)
    _, h0, tail0 = _lru(xgm, *lru_args, jnp.zeros((1, D_LRU), F32), jnp.zeros((8, D_LRU), F32), 1, N_META,
                        N_META)

    tm = 512
    q, kv, xg = _inproj(x2d, row(norm_mix[0]), w_in_bf, row(b_in[0]), cos[N_META:], sin[N_META:], tm,
                        seq // tm)
    attn_n = _attention(sinks[0], q, kv, kvm, row(g_attn_out[0]), batch, seq)
    lru_n, _, _ = _lru(xg, *lru_args, h0, tail0, batch, seq, 512)

    tri = (jnp.arange(tm)[:, None] <= jnp.arange(tm)[None, :]).astype(BF16)
    h1, xp, ids, wts, rank, cnt = _outproj(
        attn_n, lru_n, w_out_bf, row(b_out[0]), x2d, row(norm_ffn[0]), w_router[0].T.astype(BF16),
        b_router[0].reshape(N_EXPERTS, 1), tri, tm)

    counts = cnt[:, 0].astype(jnp.int32)
    n_rows = rows * TOP_K
    offs, ch_e, ch_start, ch_nsub, tails, tail = _chunk_schedule(counts, n_rows)
    onehot = ids[..., None] == jnp.arange(N_EXPERTS, dtype=jnp.int32)
    pos = jnp.sum(jnp.where(onehot, offs, 0), axis=-1) + rank
    p_alloc = n_rows + N_EXPERTS * SUB_ROWS

    tmd = 512
    xs = _dispatch(tails, _tile_positions(pos, tmd), xp, p_alloc, tmd)
    act = _moe_up(ch_e, ch_start, ch_nsub, tail, xs, w_gate[0], b_gate[0], w_up[0], b_up[0])
    ys = _moe_down(ch_e, ch_start, ch_nsub, tail, act, w_down[0], b_down[0])
    tmc = 256
    out = _combine(_tile_positions(pos, tmc), ys, h1, wts.T, row(final_norm), tmc)
    return out.reshape(batch, seq, d)
```

```python
import jax
import jax.numpy as jnp
from jax import lax
from jax.experimental import pallas as pl
from jax.experimental.pallas import tpu as pltpu

F32 = jnp.float32
BF16 = jnp.bfloat16

N_META = 16
HEAD_DIM = 64
N_Q_HEADS = 16
N_KV_HEADS = 4
GROUP = N_Q_HEADS // N_KV_HEADS
D_ATTN = N_Q_HEADS * HEAD_DIM
D_KV = N_KV_HEADS * HEAD_DIM
KV_COLS = 4 * D_KV
BLOCK = 128
ROPE_THETA = 10000.0
D_LRU = 1024
LRU_BLOCK = 64
LRU_GROUP = 256
CONV_WIDTH = 4
LRU_C = 8.0
N_EXPERTS = 32
TOP_K = 4
SWIGLU_LIMIT = 7.0
SWIGLU_ALPHA = 1.702
NORM_EPS = 1e-5
NEG_INF = -1e30

LANES = 128
D_MODEL = 2048
TOK_WORDS = D_MODEL // 2
SUB_ROWS = 256
MOE_TM = 2304
MOE_TF = 512
MOE_TN = 1024
VMEM_LIMIT = 56 * 1024 * 1024


def _cparams(sem):
    return pltpu.CompilerParams(dimension_semantics=sem, vmem_limit_bytes=VMEM_LIMIT)


def _rms(x, g):
    return x * lax.rsqrt(jnp.mean(x * x, axis=-1, keepdims=True) + NORM_EPS) * g


def _inproj_kernel(x_ref, g_ref, w_ref, b_ref, cos_ref, sin_ref, q_ref, kv_ref, xg_ref):
    xn = _rms(x_ref[...], g_ref[...]).astype(BF16)
    cos = cos_ref[...]
    sin = sin_ref[...]
    lane = lax.broadcasted_iota(jnp.int32, (1, LANES), 1)
    first_half = (lane % HEAD_DIM) < (HEAD_DIM // 2)

    def rope(z):
        partner = jnp.where(first_half, pltpu.roll(z, LANES - HEAD_DIM // 2, 1),
                            pltpu.roll(z, HEAD_DIM // 2, 1))
        return z * cos + partner * sin

    cw = 512
    for c in range(w_ref.shape[1] // cw):
        z = jnp.dot(xn, w_ref[:, c * cw:(c + 1) * cw], preferred_element_type=F32)
        z = z + b_ref[:, c * cw:(c + 1) * cw]
        if c < 2:
            for j in range(cw // LANES):
                zz = rope(z[:, j * LANES:(j + 1) * LANES]) * (HEAD_DIM ** -0.5)
                q_ref[:, c * cw + j * LANES:c * cw + (j + 1) * LANES] = zz.astype(BF16)
        elif c == 2:
            low = lane < HEAD_DIM
            for j in range(2 * D_KV // LANES):
                zz = z[:, j * LANES:(j + 1) * LANES]
                zz = rope(zz) if j < D_KV // LANES else zz
                sw = pltpu.roll(zz, HEAD_DIM, 1)
                kv_ref[:, 2 * j * LANES:(2 * j + 1) * LANES] = jnp.where(low, zz, sw).astype(BF16)
                kv_ref[:, (2 * j + 1) * LANES:(2 * j + 2) * LANES] = jnp.where(low, sw, zz).astype(BF16)
        else:
            xg_ref[:, (c - 3) * cw:(c - 2) * cw] = z


def _inproj(x2d, g, w_bf, b, cos, sin, tm, pos_blocks):
    rows, d = x2d.shape
    dz = w_bf.shape[1]
    return pl.pallas_call(
        _inproj_kernel,
        grid=(rows // tm,),
        in_specs=[
            pl.BlockSpec((tm, d), lambda m: (m, 0)),
            pl.BlockSpec((1, d), lambda m: (0, 0)),
            pl.BlockSpec((d, dz), lambda m: (0, 0)),
            pl.BlockSpec((1, dz), lambda m: (0, 0)),
            pl.BlockSpec((tm, LANES), lambda m: (m % pos_blocks, 0)),
            pl.BlockSpec((tm, LANES), lambda m: (m % pos_blocks, 0)),
        ],
        out_specs=[
            pl.BlockSpec((tm, D_ATTN), lambda m: (m, 0)),
            pl.BlockSpec((tm, KV_COLS), lambda m: (m, 0)),
            pl.BlockSpec((tm, 2 * D_LRU), lambda m: (m, 0)),
        ],
        out_shape=[
            jax.ShapeDtypeStruct((rows, D_ATTN), BF16),
            jax.ShapeDtypeStruct((rows, KV_COLS), BF16),
            jax.ShapeDtypeStruct((rows, 2 * D_LRU), F32),
        ],
        compiler_params=_cparams(("arbitrary",)),
        name="inproj",
    )(x2d, g, w_bf, b, cos, sin)


def _attn_kernel(sink_ref, q_ref, kvc_ref, kvp_ref, kvm_ref, g_ref, o_ref):
    n = pl.program_id(1)
    row = lax.broadcasted_iota(jnp.int32, (GROUP * BLOCK, BLOCK), 0) % BLOCK
    col = lax.broadcasted_iota(jnp.int32, (GROUP * BLOCK, BLOCK), 1)
    in_cur = col <= row
    in_prev = jnp.logical_and(col > row, n > 0)
    is_meta = col < N_META
    low = lax.broadcasted_iota(jnp.int32, (BLOCK, LANES), 1) < HEAD_DIM
    nt = (((1,), (1,)), ((), ()))
    zero = jnp.zeros((), BF16)
    outs = []
    for h in range(N_KV_HEADS):
        ks = slice(h * LANES, (h + 1) * LANES)
        vs = slice(N_KV_HEADS * LANES + h * LANES, N_KV_HEADS * LANES + (h + 1) * LANES)
        parts = []
        for j in range(GROUP // 2):
            qg = q_ref[:, (h * GROUP // 2 + j) * LANES:(h * GROUP // 2 + j + 1) * LANES]
            parts += [jnp.where(low, qg, zero), jnp.where(low, zero, qg)]
        qs = jnp.concatenate(parts, axis=0)
        s_c = lax.dot_general(qs, kvc_ref[:, ks], nt, preferred_element_type=F32)
        s_p = lax.dot_general(qs, kvp_ref[:, ks], nt, preferred_element_type=F32)
        s_m = lax.dot_general(qs, kvm_ref[:, ks], nt, preferred_element_type=F32)
        s_b = jnp.where(in_prev, s_p, jnp.where(in_cur, s_c, NEG_INF))
        s_m = jnp.where(is_meta, s_m, NEG_INF)
        sink = jnp.concatenate(
            [jnp.full((BLOCK, 1), sink_ref[h * GROUP + g], F32) for g in range(GROUP)], axis=0)
        m = jnp.maximum(jnp.max(jnp.maximum(s_b, s_m), axis=-1, keepdims=True), sink)
        p_b = jnp.exp(s_b - m)
        p_m = jnp.exp(s_m - m)
        den = jnp.sum(p_b + p_m, axis=-1, keepdims=True) + jnp.exp(sink - m)
        r = (jnp.dot(jnp.where(in_cur, p_b, 0.0).astype(BF16), kvc_ref[:, vs], preferred_element_type=F32)
             + jnp.dot(jnp.where(in_cur, 0.0, p_b).astype(BF16), kvp_ref[:, vs], preferred_element_type=F32)
             + jnp.dot(p_m.astype(BF16), kvm_ref[:, vs], preferred_element_type=F32))
        r = r / den
        for j in range(GROUP // 2):
            outs.append(jnp.where(low, r[2 * j * BLOCK:(2 * j + 1) * BLOCK], r[(2 * j + 1) * BLOCK:(2 * j + 2) * BLOCK]))
    o_all = jnp.concatenate(outs, axis=1)
    o_ref[...] = _rms(o_all, g_ref[...]).astype(BF16)


def _attention(sinks, q, kv, kvm, g_attn, batch, seq):
    nb = seq // BLOCK
    return pl.pallas_call(
        _attn_kernel,
        grid_spec=pltpu.PrefetchScalarGridSpec(
            num_scalar_prefetch=1,
            grid=(batch, nb),
            in_specs=[
                pl.BlockSpec((BLOCK, D_ATTN), lambda b, n, s: (b * nb + n, 0)),
                pl.BlockSpec((BLOCK, KV_COLS), lambda b, n, s: (b * nb + n, 0)),
                pl.BlockSpec((BLOCK, KV_COLS), lambda b, n, s: (b * nb + jnp.maximum(n - 1, 0), 0)),
                pl.BlockSpec((BLOCK, KV_COLS), lambda b, n, s: (0, 0)),
                pl.BlockSpec((1, D_ATTN), lambda b, n, s: (0, 0)),
            ],
            out_specs=pl.BlockSpec((BLOCK, D_ATTN), lambda b, n, s: (b * nb + n, 0)),
        ),
        out_shape=jax.ShapeDtypeStruct((batch * seq, D_ATTN), BF16),
        compiler_params=_cparams(("arbitrary", "arbitrary")),
        name="attention",
    )(sinks, q, kv, kv, kvm, g_attn)


def _expm1(y):
    p = 1.0 + y * (1.0 / 8.0)
    for k in range(7, 1, -1):
        p = 1.0 + y * p * (1.0 / k)
    return jnp.where(y > -0.25, y * p, jnp.exp(y) - 1.0)


def _lru_kernel(xg_ref, cw_ref, cb_ref, wa_ref, ba_ref, wi_ref, bi_ref, lam_ref, g_ref, h0_ref, tail0_ref,
                o_ref, hout_ref, tailout_ref, ext_ref, a_ref, b_ref, h_ref):
    tt = pl.program_id(1)
    rows = a_ref.shape[0]

    @pl.when(tt == 0)
    def _():
        h_ref[...] = h0_ref[...]
        ext_ref[0:8, :] = tail0_ref[...]

    ext_ref[8:, :] = xg_ref[:, :D_LRU]
    xc = cb_ref[...] + sum(cw_ref[j:j + 1, :] * ext_ref[5 + j:5 + j + rows, :] for j in range(CONV_WIDTH))
    xcb = xc.astype(BF16)
    sp = jax.nn.softplus(-lam_ref[...])
    for c in range(D_LRU // LRU_GROUP):
        cs = slice(c * LRU_GROUP, (c + 1) * LRU_GROUP)
        r = jax.nn.sigmoid(jnp.dot(xcb[:, cs], wa_ref[c], preferred_element_type=F32) + ba_ref[:, cs])
        i = jax.nn.sigmoid(jnp.dot(xcb[:, cs], wi_ref[c], preferred_element_type=F32) + bi_ref[:, cs])
        log_a = -LRU_C * r * sp[:, cs]
        a_ref[:, cs] = jnp.exp(log_a)
        b_ref[:, cs] = jnp.sqrt(-_expm1(2.0 * log_a)) * i * xc[:, cs]

    def step(t, h):
        h = a_ref[pl.ds(t, 1), :] * h + b_ref[pl.ds(t, 1), :]
        b_ref[pl.ds(t, 1), :] = h
        return h

    h_last = lax.fori_loop(0, rows, step, h_ref[...])
    h_ref[...] = h_last
    hout_ref[...] = h_last
    tail = ext_ref[rows:rows + 8, :]
    ext_ref[0:8, :] = tail
    tailout_ref[...] = tail
    y = b_ref[...] * jax.nn.gelu(xg_ref[:, D_LRU:])
    o_ref[...] = _rms(y, g_ref[...]).astype(BF16)


def _lru(xg, conv_w, conv_b, wa_bd, b_a, wi_bd, b_i, lam, g_lru, h0, tail0, batch, seq, tt):
    ntt = seq // tt
    vec = pl.BlockSpec((1, D_LRU), lambda b, t: (0, 0))
    wspec = pl.BlockSpec((D_LRU // LRU_GROUP, LRU_GROUP, LRU_GROUP), lambda b, t: (0, 0, 0))
    return pl.pallas_call(
        _lru_kernel,
        grid=(batch, ntt),
        in_specs=[
            pl.BlockSpec((tt, 2 * D_LRU), lambda b, t: (b * ntt + t, 0)),
            pl.BlockSpec((CONV_WIDTH, D_LRU), lambda b, t: (0, 0)),
            vec, wspec, vec, wspec, vec, vec, vec, vec,
            pl.BlockSpec((8, D_LRU), lambda b, t: (0, 0)),
        ],
        out_specs=[
            pl.BlockSpec((tt, D_LRU), lambda b, t: (b * ntt + t, 0)),
            pl.BlockSpec((1, D_LRU), lambda b, t: (0, 0)),
            pl.BlockSpec((8, D_LRU), lambda b, t: (0, 0)),
        ],
        out_shape=[
            jax.ShapeDtypeStruct((batch * seq, D_LRU), BF16),
            jax.ShapeDtypeStruct((1, D_LRU), F32),
            jax.ShapeDtypeStruct((8, D_LRU), F32),
        ],
        scratch_shapes=[
            pltpu.VMEM((8 + tt, D_LRU), F32),
            pltpu.VMEM((tt, D_LRU), F32),
            pltpu.VMEM((tt, D_LRU), F32),
            pltpu.VMEM((1, D_LRU), F32),
        ],
        compiler_params=_cparams(("arbitrary", "arbitrary")),
        name="rglru",
    )(xg, conv_w, conv_b, wa_bd, b_a, wi_bd, b_i, lam, g_lru, h0, tail0)


def _outproj_kernel(a_ref, l_ref, wo_ref, bo_ref, x_ref, gf_ref, wr_ref, br_ref, tri_ref,
                    h_ref, xp_ref, ids_ref, wts_ref, rank_ref, cnt_ref, carry_ref):
    m = pl.program_id(0)
    tm = a_ref.shape[0]

    @pl.when(m == 0)
    def _():
        carry_ref[...] = jnp.zeros_like(carry_ref)

    h = (jnp.dot(a_ref[...], wo_ref[:D_ATTN, :], preferred_element_type=F32)
         + jnp.dot(l_ref[...], wo_ref[D_ATTN:, :], preferred_element_type=F32)
         + bo_ref[...] + x_ref[...])
    h_ref[...] = h
    xn = _rms(h, gf_ref[...]).astype(BF16)
    half = xn.shape[1] // 2
    lo = pltpu.bitcast(xn[:, :half].astype(F32), jnp.uint32)
    hi = pltpu.bitcast(xn[:, half:].astype(F32), jnp.uint32)
    packed = (hi & jnp.uint32(0xFFFF0000)) | (lo >> 16)
    ns = half // LANES
    for s in range(ns):
        xp_ref[pl.ds(s, tm, stride=ns), :] = packed[:, s * LANES:(s + 1) * LANES]

    logits = lax.dot_general(wr_ref[...], xn, (((1,), (1,)), ((), ())), preferred_element_type=F32)
    logits = logits + br_ref[...]
    eidx = lax.broadcasted_iota(jnp.int32, (N_EXPERTS, tm), 0)
    work = logits
    vals, sels = [], []
    for k in range(TOP_K):
        v = jnp.max(work, axis=0, keepdims=True)
        idx = jnp.min(jnp.where(work == v, eidx, N_EXPERTS), axis=0, keepdims=True)
        sel = eidx == idx
        ids_ref[k:k + 1, :] = idx
        vals.append(v)
        sels.append(sel)
        work = jnp.where(sel, -jnp.inf, work)
    es = [jnp.exp(v - vals[0]) for v in vals]
    den = es[0] + es[1] + es[2] + es[3]
    for k in range(TOP_K):
        wts_ref[k:k + 1, :] = es[k] / den
    cnt = sum(s.astype(F32) for s in sels)
    incl = jnp.dot(cnt.astype(BF16), tri_ref[...], preferred_element_type=F32)
    before = incl - cnt + carry_ref[:, 0:1]
    for k in range(TOP_K):
        rk = jnp.sum(jnp.where(sels[k], before, 0.0), axis=0, keepdims=True)
        rank_ref[k:k + 1, :] = rk.astype(jnp.int32)
    carry_ref[...] = carry_ref[...] + incl[:, tm - 1:tm]
    cnt_ref[...] = carry_ref[...]


def _outproj(attn_n, lru_n, wo_bf, b_out, x2d, g_ffn, wr_t, br, tri, tm):
    rows, d = x2d.shape
    const = lambda shape: pl.BlockSpec(shape, lambda m: tuple(0 for _ in shape))
    return pl.pallas_call(
        _outproj_kernel,
        grid=(rows // tm,),
        in_specs=[
            pl.BlockSpec((tm, D_ATTN), lambda m: (m, 0)),
            pl.BlockSpec((tm, D_LRU), lambda m: (m, 0)),
            const((D_ATTN + D_LRU, d)), const((1, d)),
            pl.BlockSpec((tm, d), lambda m: (m, 0)),
            const((1, d)), const((N_EXPERTS, d)), const((N_EXPERTS, 1)), const((tm, tm)),
        ],
        out_specs=[
            pl.BlockSpec((tm, d), lambda m: (m, 0)),
            pl.BlockSpec((tm * (d // 2 // LANES), LANES), lambda m: (m, 0)),
            pl.BlockSpec((TOP_K, tm), lambda m: (0, m)),
            pl.BlockSpec((TOP_K, tm), lambda m: (0, m)),
            pl.BlockSpec((TOP_K, tm), lambda m: (0, m)),
            const((N_EXPERTS, LANES)),
        ],
        out_shape=[
            jax.ShapeDtypeStruct((rows, d), F32),
            jax.ShapeDtypeStruct((rows * (d // 2 // LANES), LANES), jnp.uint32),
            jax.ShapeDtypeStruct((TOP_K, rows), jnp.int32),
            jax.ShapeDtypeStruct((TOP_K, rows), F32),
            jax.ShapeDtypeStruct((TOP_K, rows), jnp.int32),
            jax.ShapeDtypeStruct((N_EXPERTS, LANES), F32),
        ],
        scratch_shapes=[pltpu.VMEM((N_EXPERTS, LANES), F32)],
        compiler_params=_cparams(("arbitrary",)),
        name="outproj_router",
    )(attn_n, lru_n, wo_bf, b_out, x2d, g_ffn, wr_t, br, tri)


def _zero_fill_rows(zero_ref, dst_hbm, first_row, n_blocks, sem):
    blk = zero_ref.shape[0]
    per = blk // SUB_ROWS

    def copy(j):
        row = pl.multiple_of((first_row + j * SUB_ROWS) * per, blk)
        return pltpu.make_async_copy(zero_ref, dst_hbm.at[pl.ds(row, blk)], sem)

    def start(j, c):
        copy(j).start()
        return c

    def wait(j, c):
        copy(j).wait()
        return c

    lax.fori_loop(0, n_blocks, start, 0)
    lax.fori_loop(0, n_blocks, wait, 0)


def _dispatch_kernel(tails_ref, pos_hbm, xp_ref, xs_hbm, pos_smem, zero_ref, sem, psem):
    i = pl.program_id(0)
    ns = TOK_WORDS // LANES
    tm = xp_ref.shape[0] // ns
    n = TOP_K * tm
    pcopy = pltpu.make_async_copy(pos_hbm.at[pl.ds(i * n, n)], pos_smem, psem)
    pcopy.start()

    @pl.when(i == 0)
    def _():
        zero_ref[...] = jnp.zeros_like(zero_ref)
        for e in range(N_EXPERTS):
            _zero_fill_rows(zero_ref, xs_hbm, tails_ref[e], 1, sem)
        _zero_fill_rows(zero_ref, xs_hbm, tails_ref[N_EXPERTS], tails_ref[N_EXPERTS + 1], sem)

    pcopy.wait()

    def issue(jj, c):
        for u in range(2):
            j = jj * 2 + u
            src = xp_ref.at[pl.ds(pl.multiple_of((j % tm) * ns, ns), ns)]
            dst = xs_hbm.at[pl.ds(pl.multiple_of(pos_smem[j] * ns, ns), ns)]
            pltpu.make_async_copy(src, dst, sem).start(priority=u)
        return c

    lax.fori_loop(0, n // 2, issue, 0)
    for _ in range(TOP_K):
        pltpu.make_async_copy(xp_ref, xs_hbm.at[pl.ds(0, tm * ns)], sem).wait()


def _dispatch(tails, pos_tiles, xp, p_alloc, tm):
    ns = TOK_WORDS // LANES
    rows = xp.shape[0] // ns
    return pl.pallas_call(
        _dispatch_kernel,
        grid_spec=pltpu.PrefetchScalarGridSpec(
            num_scalar_prefetch=1,
            grid=(rows // tm,),
            in_specs=[
                pl.BlockSpec(memory_space=pl.ANY),
                pl.BlockSpec((tm * ns, LANES), lambda i, s: (i, 0)),
            ],
            out_specs=pl.BlockSpec(memory_space=pl.ANY),
            scratch_shapes=[
                pltpu.SMEM((TOP_K * tm,), jnp.int32),
                pltpu.VMEM((SUB_ROWS * ns, LANES), jnp.uint32),
                pltpu.SemaphoreType.DMA,
                pltpu.SemaphoreType.DMA,
            ],
        ),
        out_shape=jax.ShapeDtypeStruct((p_alloc * ns, LANES), jnp.uint32),
        compiler_params=_cparams(("arbitrary",)),
        name="dispatch",
    )(tails, pos_tiles, xp)


def _chunk_dma(cs_ref, cn_ref, src_hbm, dst_ref, per, sem):
    blk = SUB_ROWS * per
    nch = cs_ref.shape[0]

    def copy(ci, j):
        src = src_hbm.at[pl.ds(pl.multiple_of((cs_ref[ci] + j * SUB_ROWS) * per, blk), blk)]
        return pltpu.make_async_copy(src, dst_ref.at[pl.ds(pl.multiple_of(j * blk, blk), blk)], sem)

    def run(ci, wait):
        cc = jnp.minimum(ci, nch - 1)
        n = jnp.where(ci < nch, cn_ref[cc], 0)

        def body(j, c):
            if wait:
                copy(cc, j).wait()
            else:
                copy(cc, j).start()
            return c

        lax.fori_loop(0, n, body, 0)

    return (lambda ci: run(ci, False)), (lambda ci: run(ci, True))


def _tile_loop(nsub, tile_fn):
    n2 = nsub // 2

    @pl.when(nsub % 2 == 1)
    def _():
        tile_fn(pl.multiple_of(n2 * 2 * SUB_ROWS, SUB_ROWS), SUB_ROWS, 2, False)

    def pair(j, c):
        tile_fn(pl.multiple_of(j * 2 * SUB_ROWS, 2 * SUB_ROWS), 2 * SUB_ROWS, j % 2, j >= 2)
        return c

    lax.fori_loop(0, n2, pair, 0)
    return n2


def _moe_up_kernel(ce_ref, cs_ref, cn_ref, tail_ref, xs_hbm, wg_ref, bg_ref, wu_ref, bu_ref, act_hbm,
                   land_ref, xb_ref, wgb_ref, wub_ref, ab_ref, zero_ref, sem, osem):
    c = pl.program_id(0)
    f = pl.program_id(1)
    nsub = cn_ref[c]
    start = pl.multiple_of(cs_ref[c], SUB_ROWS)
    tf = wg_ref.shape[2]
    ns = TOK_WORDS // LANES
    start_chunk, wait_chunk = _chunk_dma(cs_ref, cn_ref, xs_hbm, land_ref, ns, sem)

    @pl.when(jnp.logical_and(c == 0, f == 0))
    def _():
        zero_ref[...] = jnp.zeros_like(zero_ref)
        _zero_fill_rows(zero_ref, act_hbm, tail_ref[0], tail_ref[1], osem.at[0])
        start_chunk(c)

    @pl.when(jnp.logical_and(f == 0, nsub > 0))
    def _():
        wait_chunk(c)

        def unpack(j, carry):
            r = pl.multiple_of(j * SUB_ROWS, SUB_ROWS)
            for s in range(ns):
                w = land_ref[pl.ds(r * ns + s, SUB_ROWS, stride=ns), :]
                xb_ref[pl.ds(r, SUB_ROWS), s * LANES:(s + 1) * LANES] = pltpu.bitcast(w << 16, F32).astype(BF16)
                xb_ref[pl.ds(r, SUB_ROWS), TOK_WORDS + s * LANES:TOK_WORDS + (s + 1) * LANES] = (
                    pltpu.bitcast(w & jnp.uint32(0xFFFF0000), F32).astype(BF16))
            return carry

        lax.fori_loop(0, nsub, unpack, 0)
        start_chunk(c + 1)

    def out_copy(r, rows, slot):
        return pltpu.make_async_copy(
            ab_ref.at[slot, pl.ds(0, rows)],
            act_hbm.at[pl.ds(pl.multiple_of(start + r, SUB_ROWS), rows), pl.ds(pl.multiple_of(f * tf, tf), tf)],
            osem.at[slot])

    @pl.when(nsub > 0)
    def _():
        wgb_ref[...] = wg_ref[0].astype(BF16)
        wub_ref[...] = wu_ref[0].astype(BF16)
        bg = bg_ref[0]
        bu = bu_ref[0]

        def tile(r, rows, slot, wait_before):
            x = xb_ref[pl.ds(r, rows), :]
            g = jnp.dot(x, wgb_ref[...], preferred_element_type=F32) + bg
            u = jnp.dot(x, wub_ref[...], preferred_element_type=F32) + bu
            g = jnp.minimum(g, SWIGLU_LIMIT)
            u = jnp.clip(u, -SWIGLU_LIMIT, SWIGLU_LIMIT)
            a = (g * jax.nn.sigmoid(SWIGLU_ALPHA * g) * (u + 1.0)).astype(BF16)
            if wait_before is not False:
                @pl.when(wait_before)
                def _():
                    out_copy(r, rows, slot).wait()
            ab_ref[slot, pl.ds(0, rows), :] = a
            out_copy(r, rows, slot).start()

        n2 = _tile_loop(nsub, tile)
        for k in range(2):
            @pl.when(n2 > k)
            def _():
                out_copy(0, 2 * SUB_ROWS, (n2 - 1 - k) % 2).wait()

        @pl.when(nsub % 2 == 1)
        def _():
            out_copy(0, SUB_ROWS, 2).wait()


def _moe_up(ch_e, ch_start, ch_nsub, tail, xs, w_gate, b_gate, w_up, b_up):
    ns = TOK_WORDS // LANES
    p_alloc = xs.shape[0] // ns
    d = 2 * TOK_WORDS
    dff = w_gate.shape[2]
    tf = MOE_TF
    nch = ch_e.shape[0]
    nf = dff // tf
    wmap = lambda c, f, e, s, n, t: (e[c], 0, jnp.where(n[c] > 0, f, nf - 1))
    return pl.pallas_call(
        _moe_up_kernel,
        grid_spec=pltpu.PrefetchScalarGridSpec(
            num_scalar_prefetch=4,
            grid=(nch, nf),
            in_specs=[
                pl.BlockSpec(memory_space=pl.ANY),
                pl.BlockSpec((1, d, tf), wmap),
                pl.BlockSpec((1, 1, tf), wmap),
                pl.BlockSpec((1, d, tf), wmap),
                pl.BlockSpec((1, 1, tf), wmap),
            ],
            out_specs=pl.BlockSpec(memory_space=pl.ANY),
            scratch_shapes=[
                pltpu.VMEM((MOE_TM * ns, LANES), jnp.uint32),
                pltpu.VMEM((MOE_TM, d), BF16),
                pltpu.VMEM((d, tf), BF16),
                pltpu.VMEM((d, tf), BF16),
                pltpu.VMEM((3, 2 * SUB_ROWS, tf), BF16),
                pltpu.VMEM((SUB_ROWS, dff), BF16),
                pltpu.SemaphoreType.DMA,
                pltpu.SemaphoreType.DMA((3,)),
            ],
        ),
        out_shape=jax.ShapeDtypeStruct((p_alloc, dff), BF16),
        compiler_params=_cparams(("arbitrary", "arbitrary")),
        name="moe_up",
    )(ch_e, ch_start, ch_nsub, tail, xs, w_gate, b_gate.reshape(N_EXPERTS, 1, dff), w_up,
      b_up.reshape(N_EXPERTS, 1, dff))


def _moe_down_kernel(ce_ref, cs_ref, cn_ref, tail_ref, act_hbm, wd_ref, bd_ref, ys_hbm,
                     xa_ref, xb_ref, wdb_ref, yb_ref, zero_ref, sem, osem):
    c = pl.program_id(0)
    f = pl.program_id(1)
    nf = pl.num_programs(1)
    nsub = cn_ref[c]
    start = pl.multiple_of(cs_ref[c], SUB_ROWS)
    tn = wd_ref.shape[2]
    ns = TOK_WORDS // LANES
    nw = tn // 2 // LANES
    copies = [_chunk_dma(cs_ref, cn_ref, act_hbm, ref, 1, sem.at[i]) for i, ref in enumerate((xa_ref, xb_ref))]

    def on_parity(ci, fn):
        for p in range(2):
            @pl.when(ci % 2 == p)
            def _():
                fn(p)

    @pl.when(jnp.logical_and(c == 0, f == 0))
    def _():
        zero_ref[...] = jnp.zeros_like(zero_ref)
        _zero_fill_rows(zero_ref, ys_hbm, tail_ref[0], tail_ref[1], osem)
        copies[0][0](c)

    @pl.when(jnp.logical_and(f == 0, nsub > 0))
    def _():
        on_parity(c, lambda p: copies[p][1](c))
        on_parity(c + 1, lambda p: copies[p][0](c + 1))

    def out_copy(r, rows):
        src = yb_ref.at[pl.ds(pl.multiple_of(r * ns, SUB_ROWS * ns), rows * ns)]
        dst = ys_hbm.at[pl.ds(pl.multiple_of((start + r) * ns, SUB_ROWS * ns), rows * ns)]
        return pltpu.make_async_copy(src, dst, osem)

    def compute(x_ref):
        wdb_ref[...] = wd_ref[0].astype(BF16)
        bd = bd_ref[0]
        last = f == nf - 1

        def tile(r, rows, slot, wait_before):
            y = jnp.dot(x_ref[pl.ds(r, rows), :], wdb_ref[...], preferred_element_type=F32) + bd
            lo = pltpu.bitcast(y[:, :tn // 2].astype(BF16).astype(F32), jnp.uint32)
            hi = pltpu.bitcast(y[:, tn // 2:].astype(BF16).astype(F32), jnp.uint32)
            packed = (hi & jnp.uint32(0xFFFF0000)) | (lo >> 16)
            for s in range(nw):
                yb_ref[pl.ds(r * ns + f * nw + s, rows, stride=ns), :] = packed[:, s * LANES:(s + 1) * LANES]

            @pl.when(last)
            def _():
                out_copy(r, rows).start()

        _tile_loop(nsub, tile)

        @pl.when(last)
        def _():
            def drain(j, carry):
                out_copy(0, 2 * SUB_ROWS).wait()
                return carry

            lax.fori_loop(0, nsub // 2, drain, 0)

            @pl.when(nsub % 2 == 1)
            def _():
                out_copy(0, SUB_ROWS).wait()

    @pl.when(nsub > 0)
    def _():
        on_parity(c, lambda p: compute((xa_ref, xb_ref)[p]))


def _moe_down(ch_e, ch_start, ch_nsub, tail, act, w_down, b_down):
    p_alloc, dff = act.shape
    d = w_down.shape[2]
    ns = TOK_WORDS // LANES
    tn = MOE_TN
    nch = ch_e.shape[0]
    nn = d // tn
    wmap = lambda c, f, e, s, n, t: (e[c], 0, jnp.where(n[c] > 0, f, nn - 1))
    return pl.pallas_call(
        _moe_down_kernel,
        grid_spec=pltpu.PrefetchScalarGridSpec(
            num_scalar_prefetch=4,
            grid=(nch, nn),
            in_specs=[
                pl.BlockSpec(memory_space=pl.ANY),
                pl.BlockSpec((1, dff, tn), wmap),
                pl.BlockSpec((1, 1, tn), wmap),
            ],
            out_specs=pl.BlockSpec(memory_space=pl.ANY),
            scratch_shapes=[
                pltpu.VMEM((MOE_TM, dff), BF16),
                pltpu.VMEM((MOE_TM, dff), BF16),
                pltpu.VMEM((dff, tn), BF16),
                pltpu.VMEM((MOE_TM * ns, LANES), jnp.uint32),
                pltpu.VMEM((SUB_ROWS * ns, LANES), jnp.uint32),
                pltpu.SemaphoreType.DMA((2,)),
                pltpu.SemaphoreType.DMA,
            ],
        ),
        out_shape=jax.ShapeDtypeStruct((p_alloc * ns, LANES), jnp.uint32),
        compiler_params=_cparams(("arbitrary", "arbitrary")),
        name="moe_down",
    )(ch_e, ch_start, ch_nsub, tail, act, w_down, b_down.reshape(N_EXPERTS, 1, d))


def _combine_kernel(pos_hbm, ys_hbm, h_ref, w_ref, g_ref, o_ref, pos_smem, yb_ref, sem, psem):
    i = pl.program_id(0)
    tm, d = h_ref.shape
    n = TOP_K * tm
    ns = TOK_WORDS // LANES

    def fetch(tile, slot):
        pcopy = pltpu.make_async_copy(pos_hbm.at[pl.ds(tile * n, n)], pos_smem.at[pl.ds(slot * n, n)], psem)
        pcopy.start()
        pcopy.wait()

        def issue(jj, c):
            for u in range(2):
                j = jj * 2 + u
                src = ys_hbm.at[pl.ds(pl.multiple_of(pos_smem[slot * n + j] * ns, ns), ns)]
                dst = yb_ref.at[slot, pl.ds(pl.multiple_of(j * ns, ns), ns)]
                pltpu.make_async_copy(src, dst, sem.at[slot]).start(priority=u)
            return c

        lax.fori_loop(0, n // 2, issue, 0)

    @pl.when(i == 0)
    def _():
        fetch(i, 0)

    @pl.when(i + 1 < pl.num_programs(0))
    def _():
        for p in range(2):
            @pl.when((i + 1) % 2 == p)
            def _():
                fetch(i + 1, p)

    def reduce(slot):
        pltpu.make_async_copy(ys_hbm.at[pl.ds(0, n * ns)], yb_ref.at[slot], sem.at[slot]).wait()
        wb = [jnp.broadcast_to(w_ref[:, k:k + 1], (tm, LANES)) for k in range(TOP_K)]
        ssq = jnp.zeros((tm, 1), F32)
        nw = MOE_TN // 2 // LANES
        for s in range(ns):
            c0 = (s // nw) * MOE_TN + (s % nw) * LANES
            lo_cols = slice(c0, c0 + LANES)
            hi_cols = slice(c0 + MOE_TN // 2, c0 + MOE_TN // 2 + LANES)
            acc_lo = h_ref[:, lo_cols]
            acc_hi = h_ref[:, hi_cols]
            for k in range(TOP_K):
                w = yb_ref[slot, pl.ds(k * tm * ns + s, tm, stride=ns), :]
                acc_lo = acc_lo + wb[k] * pltpu.bitcast(w << 16, F32)
                acc_hi = acc_hi + wb[k] * pltpu.bitcast(w & jnp.uint32(0xFFFF0000), F32)
            o_ref[:, lo_cols] = acc_lo
            o_ref[:, hi_cols] = acc_hi
            ssq = ssq + jnp.sum(acc_lo * acc_lo + acc_hi * acc_hi, axis=-1, keepdims=True)
        o_ref[...] = o_ref[...] * lax.rsqrt(ssq * (1.0 / d) + NORM_EPS) * g_ref[...]

    for p in range(2):
        @pl.when(i % 2 == p)
        def _():
            reduce(p)


def _combine(pos_tiles, ys, h1, wts_t, g_final, tm):
    rows, d = h1.shape
    ns = TOK_WORDS // LANES
    return pl.pallas_call(
        _combine_kernel,
        grid=(rows // tm,),
        in_specs=[
            pl.BlockSpec(memory_space=pl.ANY),
            pl.BlockSpec(memory_space=pl.ANY),
            pl.BlockSpec((tm, d), lambda i: (i, 0)),
            pl.BlockSpec((tm, TOP_K), lambda i: (i, 0)),
            pl.BlockSpec((1, d), lambda i: (0, 0)),
        ],
        out_specs=pl.BlockSpec((tm, d), lambda i: (i, 0)),
        out_shape=jax.ShapeDtypeStruct((rows, d), F32),
        scratch_shapes=[
            pltpu.SMEM((2 * TOP_K * tm,), jnp.int32),
            pltpu.VMEM((2, TOP_K * tm * ns, LANES), jnp.uint32),
            pltpu.SemaphoreType.DMA((2,)),
            pltpu.SemaphoreType.DMA,
        ],
        compiler_params=_cparams(("arbitrary",)),
        name="combine",
    )(pos_tiles, ys, h1, wts_t, g_final)


def _rope_tables(n_pos):
    half = HEAD_DIM // 2
    inv = 1.0 / (ROPE_THETA ** (jnp.arange(half, dtype=F32) / half))
    ang = jnp.arange(n_pos, dtype=F32)[:, None] * inv[None, :]
    cos = jnp.tile(jnp.cos(ang), (1, LANES // half))
    sin = jnp.tile(jnp.concatenate([-jnp.sin(ang), jnp.sin(ang)], axis=1), (1, LANES // HEAD_DIM))
    return cos, sin


def _block_diag(w):
    per = LRU_GROUP // LRU_BLOCK
    w4 = w.reshape(D_LRU // LRU_GROUP, per, LRU_BLOCK, LRU_BLOCK)
    eye = jnp.eye(per, dtype=w.dtype)
    bd = jnp.einsum("gpcd,pq->gpcqd", w4, eye)
    return bd.reshape(D_LRU // LRU_GROUP, LRU_GROUP, LRU_GROUP).astype(BF16)


def _tile_positions(pos, tm):
    rows = pos.shape[1]
    return pos.reshape(TOP_K, rows // tm, tm).transpose(1, 0, 2).reshape(-1)


def _chunk_schedule(counts, n_rows):
    aligned = ((counts + SUB_ROWS - 1) // SUB_ROWS) * SUB_ROWS
    offs = jnp.cumsum(aligned) - aligned
    n_ch = (aligned + MOE_TM - 1) // MOE_TM
    cum = jnp.cumsum(n_ch)
    total = cum[-1]
    nch_max = N_EXPERTS + (n_rows + N_EXPERTS * SUB_ROWS) // MOE_TM
    c = jnp.arange(nch_max, dtype=jnp.int32)
    cc = jnp.minimum(c, total - 1)
    e = jnp.searchsorted(cum, cc, side="right").astype(jnp.int32)
    j = cc - (cum[e] - n_ch[e])
    start = offs[e] + j * MOE_TM
    nsub = jnp.where(c < total, jnp.minimum(MOE_TM, aligned[e] - j * MOE_TM) // SUB_ROWS, 0)
    tails = offs + (counts // SUB_ROWS) * SUB_ROWS
    used = jnp.sum(aligned)
    tail = jnp.stack([used, (n_rows + N_EXPERTS * SUB_ROWS - used) // SUB_ROWS])
    tails = jnp.concatenate([tails, tail])
    return (offs, e, start.astype(jnp.int32), nsub.astype(jnp.int32), tails.astype(jnp.int32),
            tail.astype(jnp.int32))


def kernel(x, meta_tokens, norm_mix, w_in, b_in, sinks, conv_w, conv_b, w_a, b_a, w_i, b_i, lru_lambda,
           g_attn_out, g_lru_out, w_out, b_out, norm_ffn, w_router, b_router, w_gate, b_gate, w_up, b_up,
           w_down, b_down, final_norm):
    batch, seq, d = x.shape
    rows = batch * seq
    x2d = x.reshape(rows, d)
    row = lambda v: v.reshape(1, -1)

    w_in_bf = w_in[0].astype(BF16)
    w_out_bf = w_out[0].astype(BF16)
    cos, sin = _rope_tables(N_META + seq)
    wa_bd, wi_bd = _block_diag(w_a[0]), _block_diag(w_i[0])
    lru_args = (conv_w[0], row(conv_b[0]), wa_bd, row(b_a[0]), wi_bd, row(b_i[0]), row(lru_lambda[0]),
                row(g_lru_out[0]))

    _, kvm, xgm = _inproj(meta_tokens, row(norm_mix[0]), w_in_bf, row(b_in[0]), cos[:N_META], sin[:N_META],
                          N_META, 1)
    _, h0, tail0 = _lru(xgm, *lru_args, jnp.zeros((1, D_LRU), F32), jnp.zeros((8, D_LRU), F32), 1, N_META,
                        N_META)

    tm = 512
    q, kv, xg = _inproj(x2d, row(norm_mix[0]), w_in_bf, row(b_in[0]), cos[N_META:], sin[N_META:], tm,
                        seq // tm)
    kvm = jnp.pad(kvm, ((0, BLOCK - N_META), (0, 0)))
    attn_n = _attention(sinks[0], q, kv, kvm, row(g_attn_out[0]), batch, seq)
    lru_n, _, _ = _lru(xg, *lru_args, h0, tail0, batch, seq, 512)

    tri = (jnp.arange(tm)[:, None] <= jnp.arange(tm)[None, :]).astype(BF16)
    h1, xp, ids, wts, rank, cnt = _outproj(
        attn_n, lru_n, w_out_bf, row(b_out[0]), x2d, row(norm_ffn[0]), w_router[0].T.astype(BF16),
        b_router[0].reshape(N_EXPERTS, 1), tri, tm)

    counts = cnt[:, 0].astype(jnp.int32)
    n_rows = rows * TOP_K
    offs, ch_e, ch_start, ch_nsub, tails, tail = _chunk_schedule(counts, n_rows)
    onehot = ids[..., None] == jnp.arange(N_EXPERTS, dtype=jnp.int32)
    pos = jnp.sum(jnp.where(onehot, offs, 0), axis=-1) + rank
    p_alloc = n_rows + N_EXPERTS * SUB_ROWS

    tmd = 512
    xs = _dispatch(tails, _tile_positions(pos, tmd), xp, p_alloc, tmd)
    act = _moe_up(ch_e, ch_start, ch_nsub, tail, xs, w_gate[0], b_gate[0], w_up[0], b_up[0])
    ys = _moe_down(ch_e, ch_start, ch_nsub, tail, act, w_down[0], b_down[0])
    tmc = 256
    out = _combine(_tile_positions(pos, tmc), ys, h1, wts.T, row(final_norm), tmc)
    return out.reshape(batch, seq, d)
```

```python
import jax
import jax.numpy as jnp
from jax import lax
from jax.experimental import pallas as pl
from jax.experimental.pallas import tpu as pltpu

F32 = jnp.float32
BF16 = jnp.bfloat16

N_META = 16
HEAD_DIM = 64
N_Q_HEADS = 16
N_KV_HEADS = 4
GROUP = N_Q_HEADS // N_KV_HEADS
D_ATTN = N_Q_HEADS * HEAD_DIM
D_KV = N_KV_HEADS * HEAD_DIM
KV_COLS = 4 * D_KV
BLOCK = 128
ROPE_THETA = 10000.0
D_LRU = 1024
LRU_BLOCK = 64
LRU_GROUP = 256
CONV_WIDTH = 4
LRU_C = 8.0
N_EXPERTS = 32
TOP_K = 4
SWIGLU_LIMIT = 7.0
SWIGLU_ALPHA = 1.702
NORM_EPS = 1e-5
NEG_INF = -1e30

LANES = 128
D_MODEL = 2048
TOK_WORDS = D_MODEL // 2
SUB_ROWS = 128
MOE_TM = 2304
MOE_TF = 512
MOE_TN = 1024
VMEM_LIMIT = 56 * 1024 * 1024


def _cparams(sem):
    return pltpu.CompilerParams(dimension_semantics=sem, vmem_limit_bytes=VMEM_LIMIT)


def _rms(x, g):
    return x * lax.rsqrt(jnp.mean(x * x, axis=-1, keepdims=True) + NORM_EPS) * g


def _inproj_kernel(x_ref, g_ref, w_ref, b_ref, cos_ref, sin_ref, q_ref, kv_ref, xg_ref):
    xn = _rms(x_ref[...], g_ref[...]).astype(BF16)
    cos = cos_ref[...]
    sin = sin_ref[...]
    lane = lax.broadcasted_iota(jnp.int32, (1, LANES), 1)
    first_half = (lane % HEAD_DIM) < (HEAD_DIM // 2)

    def rope(z):
        partner = jnp.where(first_half, pltpu.roll(z, LANES - HEAD_DIM // 2, 1),
                            pltpu.roll(z, HEAD_DIM // 2, 1))
        return z * cos + partner * sin

    cw = 512
    for c in range(w_ref.shape[1] // cw):
        z = jnp.dot(xn, w_ref[:, c * cw:(c + 1) * cw], preferred_element_type=F32)
        z = z + b_ref[:, c * cw:(c + 1) * cw]
        if c < 2:
            for j in range(cw // LANES):
                zz = rope(z[:, j * LANES:(j + 1) * LANES]) * (HEAD_DIM ** -0.5)
                q_ref[:, c * cw + j * LANES:c * cw + (j + 1) * LANES] = zz.astype(BF16)
        elif c == 2:
            low = lane < HEAD_DIM
            for j in range(2 * D_KV // LANES):
                zz = z[:, j * LANES:(j + 1) * LANES]
                zz = rope(zz) if j < D_KV // LANES else zz
                sw = pltpu.roll(zz, HEAD_DIM, 1)
                kv_ref[:, 2 * j * LANES:(2 * j + 1) * LANES] = jnp.where(low, zz, sw).astype(BF16)
                kv_ref[:, (2 * j + 1) * LANES:(2 * j + 2) * LANES] = jnp.where(low, sw, zz).astype(BF16)
        else:
            xg_ref[:, (c - 3) * cw:(c - 2) * cw] = z


def _inproj(x2d, g, w_bf, b, cos, sin, tm, pos_blocks):
    rows, d = x2d.shape
    dz = w_bf.shape[1]
    return pl.pallas_call(
        _inproj_kernel,
        grid=(rows // tm,),
        in_specs=[
            pl.BlockSpec((tm, d), lambda m: (m, 0)),
            pl.BlockSpec((1, d), lambda m: (0, 0)),
            pl.BlockSpec((d, dz), lambda m: (0, 0)),
            pl.BlockSpec((1, dz), lambda m: (0, 0)),
            pl.BlockSpec((tm, LANES), lambda m: (m % pos_blocks, 0)),
            pl.BlockSpec((tm, LANES), lambda m: (m % pos_blocks, 0)),
        ],
        out_specs=[
            pl.BlockSpec((tm, D_ATTN), lambda m: (m, 0)),
            pl.BlockSpec((tm, KV_COLS), lambda m: (m, 0)),
            pl.BlockSpec((tm, 2 * D_LRU), lambda m: (m, 0)),
        ],
        out_shape=[
            jax.ShapeDtypeStruct((rows, D_ATTN), BF16),
            jax.ShapeDtypeStruct((rows, KV_COLS), BF16),
            jax.ShapeDtypeStruct((rows, 2 * D_LRU), F32),
        ],
        compiler_params=_cparams(("arbitrary",)),
        name="inproj",
    )(x2d, g, w_bf, b, cos, sin)


def _attn_kernel(sink_ref, q_ref, kvc_ref, kvp_ref, kvm_ref, g_ref, o_ref):
    n = pl.program_id(1)
    row = lax.broadcasted_iota(jnp.int32, (GROUP * BLOCK, BLOCK), 0) % BLOCK
    col = lax.broadcasted_iota(jnp.int32, (GROUP * BLOCK, BLOCK), 1)
    in_cur = col <= row
    in_prev = jnp.logical_and(col > row, n > 0)
    is_meta = col < N_META
    low = lax.broadcasted_iota(jnp.int32, (BLOCK, LANES), 1) < HEAD_DIM
    nt = (((1,), (1,)), ((), ()))
    zero = jnp.zeros((), BF16)
    outs = []
    for h in range(N_KV_HEADS):
        ks = slice(h * LANES, (h + 1) * LANES)
        vs = slice(N_KV_HEADS * LANES + h * LANES, N_KV_HEADS * LANES + (h + 1) * LANES)
        parts = []
        for j in range(GROUP // 2):
            qg = q_ref[:, (h * GROUP // 2 + j) * LANES:(h * GROUP // 2 + j + 1) * LANES]
            parts += [jnp.where(low, qg, zero), jnp.where(low, zero, qg)]
        qs = jnp.concatenate(parts, axis=0)
        s_c = lax.dot_general(qs, kvc_ref[:, ks], nt, preferred_element_type=F32)
        s_p = lax.dot_general(qs, kvp_ref[:, ks], nt, preferred_element_type=F32)
        s_m = lax.dot_general(qs, kvm_ref[:, ks], nt, preferred_element_type=F32)
        s_b = jnp.where(in_prev, s_p, jnp.where(in_cur, s_c, NEG_INF))
        s_m = jnp.where(is_meta, s_m, NEG_INF)
        sink = jnp.concatenate(
            [jnp.full((BLOCK, 1), sink_ref[h * GROUP + g], F32) for g in range(GROUP)], axis=0)
        m = jnp.maximum(jnp.max(jnp.maximum(s_b, s_m), axis=-1, keepdims=True), sink)
        p_b = jnp.exp(s_b - m)
        p_m = jnp.exp(s_m - m)
        den = jnp.sum(p_b + p_m, axis=-1, keepdims=True) + jnp.exp(sink - m)
        r = (jnp.dot(jnp.where(in_cur, p_b, 0.0).astype(BF16), kvc_ref[:, vs], preferred_element_type=F32)
             + jnp.dot(jnp.where(in_cur, 0.0, p_b).astype(BF16), kvp_ref[:, vs], preferred_element_type=F32)
             + jnp.dot(p_m.astype(BF16), kvm_ref[:, vs], preferred_element_type=F32))
        r = r / den
        for j in range(GROUP // 2):
            outs.append(jnp.where(low, r[2 * j * BLOCK:(2 * j + 1) * BLOCK], r[(2 * j + 1) * BLOCK:(2 * j + 2) * BLOCK]))
    o_all = jnp.concatenate(outs, axis=1)
    o_ref[...] = _rms(o_all, g_ref[...]).astype(BF16)


def _attention(sinks, q, kv, kvm, g_attn, batch, seq):
    nb = seq // BLOCK
    return pl.pallas_call(
        _attn_kernel,
        grid_spec=pltpu.PrefetchScalarGridSpec(
            num_scalar_prefetch=1,
            grid=(batch, nb),
            in_specs=[
                pl.BlockSpec((BLOCK, D_ATTN), lambda b, n, s: (b * nb + n, 0)),
                pl.BlockSpec((BLOCK, KV_COLS), lambda b, n, s: (b * nb + n, 0)),
                pl.BlockSpec((BLOCK, KV_COLS), lambda b, n, s: (b * nb + jnp.maximum(n - 1, 0), 0)),
                pl.BlockSpec((BLOCK, KV_COLS), lambda b, n, s: (0, 0)),
                pl.BlockSpec((1, D_ATTN), lambda b, n, s: (0, 0)),
            ],
            out_specs=pl.BlockSpec((BLOCK, D_ATTN), lambda b, n, s: (b * nb + n, 0)),
        ),
        out_shape=jax.ShapeDtypeStruct((batch * seq, D_ATTN), BF16),
        compiler_params=_cparams(("arbitrary", "arbitrary")),
        name="attention",
    )(sinks, q, kv, kv, kvm, g_attn)


def _expm1(y):
    p = 1.0 + y * (1.0 / 8.0)
    for k in range(7, 1, -1):
        p = 1.0 + y * p * (1.0 / k)
    return jnp.where(y > -0.25, y * p, jnp.exp(y) - 1.0)


def _lru_kernel(xg_ref, cw_ref, cb_ref, wa_ref, ba_ref, wi_ref, bi_ref, lam_ref, g_ref, h0_ref, tail0_ref,
                o_ref, hout_ref, tailout_ref, ext_ref, a_ref, b_ref, h_ref):
    tt = pl.program_id(1)
    rows = a_ref.shape[0]

    @pl.when(tt == 0)
    def _():
        h_ref[...] = h0_ref[...]
        ext_ref[0:8, :] = tail0_ref[...]

    ext_ref[8:, :] = xg_ref[:, :D_LRU]
    xc = cb_ref[...] + sum(cw_ref[j:j + 1, :] * ext_ref[5 + j:5 + j + rows, :] for j in range(CONV_WIDTH))
    xcb = xc.astype(BF16)
    sp = jax.nn.softplus(-lam_ref[...])
    for c in range(D_LRU // LRU_GROUP):
        cs = slice(c * LRU_GROUP, (c + 1) * LRU_GROUP)
        r = jax.nn.sigmoid(jnp.dot(xcb[:, cs], wa_ref[c], preferred_element_type=F32) + ba_ref[:, cs])
        i = jax.nn.sigmoid(jnp.dot(xcb[:, cs], wi_ref[c], preferred_element_type=F32) + bi_ref[:, cs])
        log_a = -LRU_C * r * sp[:, cs]
        a_ref[:, cs] = jnp.exp(log_a)
        b_ref[:, cs] = jnp.sqrt(-_expm1(2.0 * log_a)) * i * xc[:, cs]

    def step(t, h):
        h = a_ref[pl.ds(t, 1), :] * h + b_ref[pl.ds(t, 1), :]
        b_ref[pl.ds(t, 1), :] = h
        return h

    h_last = lax.fori_loop(0, rows, step, h_ref[...])
    h_ref[...] = h_last
    hout_ref[...] = h_last
    tail = ext_ref[rows:rows + 8, :]
    ext_ref[0:8, :] = tail
    tailout_ref[...] = tail
    y = b_ref[...] * jax.nn.gelu(xg_ref[:, D_LRU:])
    o_ref[...] = _rms(y, g_ref[...]).astype(BF16)


def _lru(xg, conv_w, conv_b, wa_bd, b_a, wi_bd, b_i, lam, g_lru, h0, tail0, batch, seq, tt):
    ntt = seq // tt
    vec = pl.BlockSpec((1, D_LRU), lambda b, t: (0, 0))
    wspec = pl.BlockSpec((D_LRU // LRU_GROUP, LRU_GROUP, LRU_GROUP), lambda b, t: (0, 0, 0))
    return pl.pallas_call(
        _lru_kernel,
        grid=(batch, ntt),
        in_specs=[
            pl.BlockSpec((tt, 2 * D_LRU), lambda b, t: (b * ntt + t, 0)),
            pl.BlockSpec((CONV_WIDTH, D_LRU), lambda b, t: (0, 0)),
            vec, wspec, vec, wspec, vec, vec, vec, vec,
            pl.BlockSpec((8, D_LRU), lambda b, t: (0, 0)),
        ],
        out_specs=[
            pl.BlockSpec((tt, D_LRU), lambda b, t: (b * ntt + t, 0)),
            pl.BlockSpec((1, D_LRU), lambda b, t: (0, 0)),
            pl.BlockSpec((8, D_LRU), lambda b, t: (0, 0)),
        ],
        out_shape=[
            jax.ShapeDtypeStruct((batch * seq, D_LRU), BF16),
            jax.ShapeDtypeStruct((1, D_LRU), F32),
            jax.ShapeDtypeStruct((8, D_LRU), F32),
        ],
        scratch_shapes=[
            pltpu.VMEM((8 + tt, D_LRU), F32),
            pltpu.VMEM((tt, D_LRU), F32),
            pltpu.VMEM((tt, D_LRU), F32),
            pltpu.VMEM((1, D_LRU), F32),
        ],
        compiler_params=_cparams(("arbitrary", "arbitrary")),
        name="rglru",
    )(xg, conv_w, conv_b, wa_bd, b_a, wi_bd, b_i, lam, g_lru, h0, tail0)


def _outproj_kernel(a_ref, l_ref, wo_ref, bo_ref, x_ref, gf_ref, wr_ref, br_ref, tri_ref,
                    h_ref, xp_ref, ids_ref, wts_ref, rank_ref, cnt_ref, carry_ref):
    m = pl.program_id(0)
    tm = a_ref.shape[0]

    @pl.when(m == 0)
    def _():
        carry_ref[...] = jnp.zeros_like(carry_ref)

    h = (jnp.dot(a_ref[...], wo_ref[:D_ATTN, :], preferred_element_type=F32)
         + jnp.dot(l_ref[...], wo_ref[D_ATTN:, :], preferred_element_type=F32)
         + bo_ref[...] + x_ref[...])
    h_ref[...] = h
    xn = _rms(h, gf_ref[...]).astype(BF16)
    half = xn.shape[1] // 2
    lo = pltpu.bitcast(xn[:, :half].astype(F32), jnp.uint32)
    hi = pltpu.bitcast(xn[:, half:].astype(F32), jnp.uint32)
    packed = (hi & jnp.uint32(0xFFFF0000)) | (lo >> 16)
    ns = half // LANES
    for s in range(ns):
        xp_ref[pl.ds(s, tm, stride=ns), :] = packed[:, s * LANES:(s + 1) * LANES]

    logits = lax.dot_general(wr_ref[...], xn, (((1,), (1,)), ((), ())), preferred_element_type=F32)
    logits = logits + br_ref[...]
    eidx = lax.broadcasted_iota(jnp.int32, (N_EXPERTS, tm), 0)
    work = logits
    vals, sels = [], []
    for k in range(TOP_K):
        v = jnp.max(work, axis=0, keepdims=True)
        idx = jnp.min(jnp.where(work == v, eidx, N_EXPERTS), axis=0, keepdims=True)
        sel = eidx == idx
        ids_ref[k:k + 1, :] = idx
        vals.append(v)
        sels.append(sel)
        work = jnp.where(sel, -jnp.inf, work)
    es = [jnp.exp(v - vals[0]) for v in vals]
    den = es[0] + es[1] + es[2] + es[3]
    for k in range(TOP_K):
        wts_ref[k:k + 1, :] = es[k] / den
    cnt = sum(s.astype(F32) for s in sels)
    incl = jnp.dot(cnt.astype(BF16), tri_ref[...], preferred_element_type=F32)
    before = incl - cnt + carry_ref[:, 0:1]
    for k in range(TOP_K):
        rk = jnp.sum(jnp.where(sels[k], before, 0.0), axis=0, keepdims=True)
        rank_ref[k:k + 1, :] = rk.astype(jnp.int32)
    carry_ref[...] = carry_ref[...] + incl[:, tm - 1:tm]
    cnt_ref[...] = carry_ref[...]


def _outproj(attn_n, lru_n, wo_bf, b_out, x2d, g_ffn, wr_t, br, tri, tm):
    rows, d = x2d.shape
    const = lambda shape: pl.BlockSpec(shape, lambda m: tuple(0 for _ in shape))
    return pl.pallas_call(
        _outproj_kernel,
        grid=(rows // tm,),
        in_specs=[
            pl.BlockSpec((tm, D_ATTN), lambda m: (m, 0)),
            pl.BlockSpec((tm, D_LRU), lambda m: (m, 0)),
            const((D_ATTN + D_LRU, d)), const((1, d)),
            pl.BlockSpec((tm, d), lambda m: (m, 0)),
            const((1, d)), const((N_EXPERTS, d)), const((N_EXPERTS, 1)), const((tm, tm)),
        ],
        out_specs=[
            pl.BlockSpec((tm, d), lambda m: (m, 0)),
            pl.BlockSpec((tm * (d // 2 // LANES), LANES), lambda m: (m, 0)),
            pl.BlockSpec((TOP_K, tm), lambda m: (0, m)),
            pl.BlockSpec((TOP_K, tm), lambda m: (0, m)),
            pl.BlockSpec((TOP_K, tm), lambda m: (0, m)),
            const((N_EXPERTS, LANES)),
        ],
        out_shape=[
            jax.ShapeDtypeStruct((rows, d), F32),
            jax.ShapeDtypeStruct((rows * (d // 2 // LANES), LANES), jnp.uint32),
            jax.ShapeDtypeStruct((TOP_K, rows), jnp.int32),
            jax.ShapeDtypeStruct((TOP_K, rows), F32),
            jax.ShapeDtypeStruct((TOP_K, rows), jnp.int32),
            jax.ShapeDtypeStruct((N_EXPERTS, LANES), F32),
        ],
        scratch_shapes=[pltpu.VMEM((N_EXPERTS, LANES), F32)],
        compiler_params=_cparams(("arbitrary",)),
        name="outproj_router",
    )(attn_n, lru_n, wo_bf, b_out, x2d, g_ffn, wr_t, br, tri)


def _zero_fill_rows(zero_ref, dst_hbm, first_row, n_blocks, sem):
    blk = zero_ref.shape[0]
    per = blk // SUB_ROWS

    def copy(j):
        row = pl.multiple_of((first_row + j * SUB_ROWS) * per, blk)
        return pltpu.make_async_copy(zero_ref, dst_hbm.at[pl.ds(row, blk)], sem)

    def start(j, c):
        copy(j).start()
        return c

    def wait(j, c):
        copy(j).wait()
        return c

    lax.fori_loop(0, n_blocks, start, 0)
    lax.fori_loop(0, n_blocks, wait, 0)


def _dispatch_kernel(tails_ref, pos_hbm, xp_ref, xs_hbm, pos_smem, zero_ref, sem, psem):
    i = pl.program_id(0)
    ns = TOK_WORDS // LANES
    tm = xp_ref.shape[0] // ns
    n = TOP_K * tm
    pcopy = pltpu.make_async_copy(pos_hbm.at[pl.ds(i * n, n)], pos_smem, psem)
    pcopy.start()

    @pl.when(i == 0)
    def _():
        zero_ref[...] = jnp.zeros_like(zero_ref)
        for e in range(N_EXPERTS):
            _zero_fill_rows(zero_ref, xs_hbm, tails_ref[e], 1, sem)
        _zero_fill_rows(zero_ref, xs_hbm, tails_ref[N_EXPERTS], tails_ref[N_EXPERTS + 1], sem)

    pcopy.wait()

    def issue(jj, c):
        for u in range(2):
            j = jj * 2 + u
            src = xp_ref.at[pl.ds(pl.multiple_of((j % tm) * ns, ns), ns)]
            dst = xs_hbm.at[pl.ds(pl.multiple_of(pos_smem[j] * ns, ns), ns)]
            pltpu.make_async_copy(src, dst, sem).start(priority=u)
        return c

    lax.fori_loop(0, n // 2, issue, 0)
    for _ in range(TOP_K):
        pltpu.make_async_copy(xp_ref, xs_hbm.at[pl.ds(0, tm * ns)], sem).wait()


def _dispatch(tails, pos_tiles, xp, p_alloc, tm):
    ns = TOK_WORDS // LANES
    rows = xp.shape[0] // ns
    return pl.pallas_call(
        _dispatch_kernel,
        grid_spec=pltpu.PrefetchScalarGridSpec(
            num_scalar_prefetch=1,
            grid=(rows // tm,),
            in_specs=[
                pl.BlockSpec(memory_space=pl.ANY),
                pl.BlockSpec((tm * ns, LANES), lambda i, s: (i, 0)),
            ],
            out_specs=pl.BlockSpec(memory_space=pl.ANY),
            scratch_shapes=[
                pltpu.SMEM((TOP_K * tm,), jnp.int32),
                pltpu.VMEM((SUB_ROWS * ns, LANES), jnp.uint32),
                pltpu.SemaphoreType.DMA,
                pltpu.SemaphoreType.DMA,
            ],
        ),
        out_shape=jax.ShapeDtypeStruct((p_alloc * ns, LANES), jnp.uint32),
        compiler_params=_cparams(("arbitrary",)),
        name="dispatch",
    )(tails, pos_tiles, xp)


def _chunk_dma(cs_ref, cn_ref, src_hbm, dst_ref, per, sem):
    blk = SUB_ROWS * per
    nch = cs_ref.shape[0]

    def copy(ci, j):
        src = src_hbm.at[pl.ds(pl.multiple_of((cs_ref[ci] + j * SUB_ROWS) * per, blk), blk)]
        return pltpu.make_async_copy(src, dst_ref.at[pl.ds(pl.multiple_of(j * blk, blk), blk)], sem)

    def run(ci, wait):
        cc = jnp.minimum(ci, nch - 1)
        n = jnp.where(ci < nch, cn_ref[cc], 0)

        def body(j, c):
            if wait:
                copy(cc, j).wait()
            else:
                copy(cc, j).start()
            return c

        lax.fori_loop(0, n, body, 0)

    return (lambda ci: run(ci, False)), (lambda ci: run(ci, True))


def _tile_loop(nsub, tile_fn):
    n4 = nsub // 4
    rem = nsub % 4
    base = n4 * 4 * SUB_ROWS

    @pl.when(rem >= 2)
    def _():
        tile_fn(pl.multiple_of(base, SUB_ROWS), 2 * SUB_ROWS, 2, False)

    @pl.when(rem % 2 == 1)
    def _():
        tile_fn(pl.multiple_of(base + (rem // 2) * 2 * SUB_ROWS, SUB_ROWS), SUB_ROWS, 3, False)

    def full(j, c):
        tile_fn(pl.multiple_of(j * 4 * SUB_ROWS, 4 * SUB_ROWS), 4 * SUB_ROWS, j % 2, j >= 2)
        return c

    lax.fori_loop(0, n4, full, 0)
    return n4, rem


def _moe_up_kernel(ce_ref, cs_ref, cn_ref, tail_ref, xs_hbm, wg_ref, bg_ref, wu_ref, bu_ref, act_hbm,
                   land_ref, xb_ref, wgu_ref, ab_ref, zero_ref, sem, osem):
    c = pl.program_id(0)
    f = pl.program_id(1)
    nsub = cn_ref[c]
    start = pl.multiple_of(cs_ref[c], SUB_ROWS)
    tf = wg_ref.shape[2]
    ns = TOK_WORDS // LANES
    start_chunk, wait_chunk = _chunk_dma(cs_ref, cn_ref, xs_hbm, land_ref, ns, sem)

    @pl.when(jnp.logical_and(c == 0, f == 0))
    def _():
        zero_ref[...] = jnp.zeros_like(zero_ref)
        _zero_fill_rows(zero_ref, act_hbm, tail_ref[0], tail_ref[1], osem.at[0])
        start_chunk(c)

    @pl.when(jnp.logical_and(f == 0, nsub > 0))
    def _():
        wait_chunk(c)

        def unpack(j, carry):
            r = pl.multiple_of(j * SUB_ROWS, SUB_ROWS)
            for s in range(ns):
                w = land_ref[pl.ds(r * ns + s, SUB_ROWS, stride=ns), :]
                xb_ref[pl.ds(r, SUB_ROWS), s * LANES:(s + 1) * LANES] = pltpu.bitcast(w << 16, F32).astype(BF16)
                xb_ref[pl.ds(r, SUB_ROWS), TOK_WORDS + s * LANES:TOK_WORDS + (s + 1) * LANES] = (
                    pltpu.bitcast(w & jnp.uint32(0xFFFF0000), F32).astype(BF16))
            return carry

        lax.fori_loop(0, nsub, unpack, 0)
        start_chunk(c + 1)

    def out_copy(r, rows, slot):
        return pltpu.make_async_copy(
            ab_ref.at[slot, pl.ds(0, rows)],
            act_hbm.at[pl.ds(pl.multiple_of(start + r, SUB_ROWS), rows), pl.ds(pl.multiple_of(f * tf, tf), tf)],
            osem.at[slot])

    @pl.when(nsub > 0)
    def _():
        wgu_ref[:, :tf] = wg_ref[0].astype(BF16)
        wgu_ref[:, tf:] = wu_ref[0].astype(BF16)
        bg = bg_ref[0]
        bu = bu_ref[0]

        def tile(r, rows, slot, wait_before):
            z = jnp.dot(xb_ref[pl.ds(r, rows), :], wgu_ref[...], preferred_element_type=F32)
            g = z[:, :tf] + bg
            u = z[:, tf:] + bu
            g = jnp.minimum(g, SWIGLU_LIMIT)
            u = jnp.clip(u, -SWIGLU_LIMIT, SWIGLU_LIMIT)
            a = (g * jax.nn.sigmoid(SWIGLU_ALPHA * g) * (u + 1.0)).astype(BF16)
            if wait_before is not False:
                @pl.when(wait_before)
                def _():
                    out_copy(r, rows, slot).wait()
            ab_ref[slot, pl.ds(0, rows), :] = a
            out_copy(r, rows, slot).start()

        n4, rem = _tile_loop(nsub, tile)
        for k in range(2):
            @pl.when(n4 > k)
            def _():
                out_copy(0, 4 * SUB_ROWS, (n4 - 1 - k) % 2).wait()

        @pl.when(rem >= 2)
        def _():
            out_copy(0, 2 * SUB_ROWS, 2).wait()

        @pl.when(rem % 2 == 1)
        def _():
            out_copy(0, SUB_ROWS, 3).wait()


def _moe_up(ch_e, ch_start, ch_nsub, tail, xs, w_gate, b_gate, w_up, b_up):
    ns = TOK_WORDS // LANES
    p_alloc = xs.shape[0] // ns
    d = 2 * TOK_WORDS
    dff = w_gate.shape[2]
    tf = MOE_TF
    nch = ch_e.shape[0]
    nf = dff // tf
    wmap = lambda c, f, e, s, n, t: (e[c], 0, jnp.where(n[c] > 0, f, nf - 1))
    return pl.pallas_call(
        _moe_up_kernel,
        grid_spec=pltpu.PrefetchScalarGridSpec(
            num_scalar_prefetch=4,
            grid=(nch, nf),
            in_specs=[
                pl.BlockSpec(memory_space=pl.ANY),
                pl.BlockSpec((1, d, tf), wmap),
                pl.BlockSpec((1, 1, tf), wmap),
                pl.BlockSpec((1, d, tf), wmap),
                pl.BlockSpec((1, 1, tf), wmap),
            ],
            out_specs=pl.BlockSpec(memory_space=pl.ANY),
            scratch_shapes=[
                pltpu.VMEM((MOE_TM * ns, LANES), jnp.uint32),
                pltpu.VMEM((MOE_TM, d), BF16),
                pltpu.VMEM((d, 2 * tf), BF16),
                pltpu.VMEM((4, 4 * SUB_ROWS, tf), BF16),
                pltpu.VMEM((SUB_ROWS, dff), BF16),
                pltpu.SemaphoreType.DMA,
                pltpu.SemaphoreType.DMA((4,)),
            ],
        ),
        out_shape=jax.ShapeDtypeStruct((p_alloc, dff), BF16),
        compiler_params=_cparams(("arbitrary", "arbitrary")),
        name="moe_up",
    )(ch_e, ch_start, ch_nsub, tail, xs, w_gate, b_gate.reshape(N_EXPERTS, 1, dff), w_up,
      b_up.reshape(N_EXPERTS, 1, dff))


def _moe_down_kernel(ce_ref, cs_ref, cn_ref, tail_ref, act_hbm, wd_ref, bd_ref, ys_hbm,
                     xa_ref, xb_ref, wdb_ref, yb_ref, zero_ref, sem, osem):
    c = pl.program_id(0)
    f = pl.program_id(1)
    nf = pl.num_programs(1)
    nsub = cn_ref[c]
    start = pl.multiple_of(cs_ref[c], SUB_ROWS)
    tn = wd_ref.shape[2]
    ns = TOK_WORDS // LANES
    nw = tn // 2 // LANES
    copies = [_chunk_dma(cs_ref, cn_ref, act_hbm, ref, 1, sem.at[i]) for i, ref in enumerate((xa_ref, xb_ref))]

    def on_parity(ci, fn):
        for p in range(2):
            @pl.when(ci % 2 == p)
            def _():
                fn(p)

    @pl.when(jnp.logical_and(c == 0, f == 0))
    def _():
        zero_ref[...] = jnp.zeros_like(zero_ref)
        _zero_fill_rows(zero_ref, ys_hbm, tail_ref[0], tail_ref[1], osem)
        copies[0][0](c)

    @pl.when(jnp.logical_and(f == 0, nsub > 0))
    def _():
        on_parity(c, lambda p: copies[p][1](c))
        on_parity(c + 1, lambda p: copies[p][0](c + 1))

    def out_copy(r, rows):
        src = yb_ref.at[pl.ds(pl.multiple_of(r * ns, SUB_ROWS * ns), rows * ns)]
        dst = ys_hbm.at[pl.ds(pl.multiple_of((start + r) * ns, SUB_ROWS * ns), rows * ns)]
        return pltpu.make_async_copy(src, dst, osem)

    def compute(x_ref):
        wdb_ref[...] = wd_ref[0].astype(BF16)
        bd = bd_ref[0]
        last = f == nf - 1

        def tile(r, rows, slot, wait_before):
            y = jnp.dot(x_ref[pl.ds(r, rows), :], wdb_ref[...], preferred_element_type=F32) + bd
            lo = pltpu.bitcast(y[:, :tn // 2].astype(BF16).astype(F32), jnp.uint32)
            hi = pltpu.bitcast(y[:, tn // 2:].astype(BF16).astype(F32), jnp.uint32)
            packed = (hi & jnp.uint32(0xFFFF0000)) | (lo >> 16)
            for s in range(nw):
                yb_ref[pl.ds(r * ns + f * nw + s, rows, stride=ns), :] = packed[:, s * LANES:(s + 1) * LANES]

            @pl.when(last)
            def _():
                out_copy(r, rows).start()

        _tile_loop(nsub, tile)

        @pl.when(last)
        def _():
            def drain(j, carry):
                out_copy(0, 4 * SUB_ROWS).wait()
                return carry

            lax.fori_loop(0, nsub // 4, drain, 0)

            @pl.when(nsub % 4 >= 2)
            def _():
                out_copy(0, 2 * SUB_ROWS).wait()

            @pl.when(nsub % 2 == 1)
            def _():
                out_copy(0, SUB_ROWS).wait()

    @pl.when(nsub > 0)
    def _():
        on_parity(c, lambda p: compute((xa_ref, xb_ref)[p]))


def _moe_down(ch_e, ch_start, ch_nsub, tail, act, w_down, b_down):
    p_alloc, dff = act.shape
    d = w_down.shape[2]
    ns = TOK_WORDS // LANES
    tn = MOE_TN
    nch = ch_e.shape[0]
    nn = d // tn
    wmap = lambda c, f, e, s, n, t: (e[c], 0, jnp.where(n[c] > 0, f, nn - 1))
    return pl.pallas_call(
        _moe_down_kernel,
        grid_spec=pltpu.PrefetchScalarGridSpec(
            num_scalar_prefetch=4,
            grid=(nch, nn),
            in_specs=[
                pl.BlockSpec(memory_space=pl.ANY),
                pl.BlockSpec((1, dff, tn), wmap),
                pl.BlockSpec((1, 1, tn), wmap),
            ],
            out_specs=pl.BlockSpec(memory_space=pl.ANY),
            scratch_shapes=[
                pltpu.VMEM((MOE_TM, dff), BF16),
                pltpu.VMEM((MOE_TM, dff), BF16),
                pltpu.VMEM((dff, tn), BF16),
                pltpu.VMEM((MOE_TM * ns, LANES), jnp.uint32),
                pltpu.VMEM((SUB_ROWS * ns, LANES), jnp.uint32),
                pltpu.SemaphoreType.DMA((2,)),
                pltpu.SemaphoreType.DMA,
            ],
        ),
        out_shape=jax.ShapeDtypeStruct((p_alloc * ns, LANES), jnp.uint32),
        compiler_params=_cparams(("arbitrary", "arbitrary")),
        name="moe_down",
    )(ch_e, ch_start, ch_nsub, tail, act, w_down, b_down.reshape(N_EXPERTS, 1, d))


def _combine_kernel(pos_hbm, ys_hbm, h_ref, w_ref, g_ref, o_ref, pos_smem, yb_ref, sem, psem):
    i = pl.program_id(0)
    tm, d = h_ref.shape
    n = TOP_K * tm
    ns = TOK_WORDS // LANES

    def fetch(tile, slot):
        pcopy = pltpu.make_async_copy(pos_hbm.at[pl.ds(tile * n, n)], pos_smem.at[pl.ds(slot * n, n)], psem)
        pcopy.start()
        pcopy.wait()

        def issue(jj, c):
            for u in range(2):
                j = jj * 2 + u
                src = ys_hbm.at[pl.ds(pl.multiple_of(pos_smem[slot * n + j] * ns, ns), ns)]
                dst = yb_ref.at[slot, pl.ds(pl.multiple_of(j * ns, ns), ns)]
                pltpu.make_async_copy(src, dst, sem.at[slot]).start(priority=u)
            return c

        lax.fori_loop(0, n // 2, issue, 0)

    @pl.when(i == 0)
    def _():
        fetch(i, 0)

    @pl.when(i + 1 < pl.num_programs(0))
    def _():
        for p in range(2):
            @pl.when((i + 1) % 2 == p)
            def _():
                fetch(i + 1, p)

    def reduce(slot):
        pltpu.make_async_copy(ys_hbm.at[pl.ds(0, n * ns)], yb_ref.at[slot], sem.at[slot]).wait()
        wb = [jnp.broadcast_to(w_ref[:, k:k + 1], (tm, LANES)) for k in range(TOP_K)]
        ssq = jnp.zeros((tm, 1), F32)
        nw = MOE_TN // 2 // LANES
        for s in range(ns):
            c0 = (s // nw) * MOE_TN + (s % nw) * LANES
            lo_cols = slice(c0, c0 + LANES)
            hi_cols = slice(c0 + MOE_TN // 2, c0 + MOE_TN // 2 + LANES)
            acc_lo = h_ref[:, lo_cols]
            acc_hi = h_ref[:, hi_cols]
            for k in range(TOP_K):
                w = yb_ref[slot, pl.ds(k * tm * ns + s, tm, stride=ns), :]
                acc_lo = acc_lo + wb[k] * pltpu.bitcast(w << 16, F32)
                acc_hi = acc_hi + wb[k] * pltpu.bitcast(w & jnp.uint32(0xFFFF0000), F32)
            o_ref[:, lo_cols] = acc_lo
            o_ref[:, hi_cols] = acc_hi
            ssq = ssq + jnp.sum(acc_lo * acc_lo + acc_hi * acc_hi, axis=-1, keepdims=True)
        o_ref[...] = o_ref[...] * lax.rsqrt(ssq * (1.0 / d) + NORM_EPS) * g_ref[...]

    for p in range(2):
        @pl.when(i % 2 == p)
        def _():
            reduce(p)


def _combine(pos_tiles, ys, h1, wts_t, g_final, tm):
    rows, d = h1.shape
    ns = TOK_WORDS // LANES
    return pl.pallas_call(
        _combine_kernel,
        grid=(rows // tm,),
        in_specs=[
            pl.BlockSpec(memory_space=pl.ANY),
            pl.BlockSpec(memory_space=pl.ANY),
            pl.BlockSpec((tm, d), lambda i: (i, 0)),
            pl.BlockSpec((tm, TOP_K), lambda i: (i, 0)),
            pl.BlockSpec((1, d), lambda i: (0, 0)),
        ],
        out_specs=pl.BlockSpec((tm, d), lambda i: (i, 0)),
        out_shape=jax.ShapeDtypeStruct((rows, d), F32),
        scratch_shapes=[
            pltpu.SMEM((2 * TOP_K * tm,), jnp.int32),
            pltpu.VMEM((2, TOP_K * tm * ns, LANES), jnp.uint32),
            pltpu.SemaphoreType.DMA((2,)),
            pltpu.SemaphoreType.DMA,
        ],
        compiler_params=_cparams(("arbitrary",)),
        name="combine",
    )(pos_tiles, ys, h1, wts_t, g_final)


def _rope_tables(n_pos):
    half = HEAD_DIM // 2
    inv = 1.0 / (ROPE_THETA ** (jnp.arange(half, dtype=F32) / half))
    ang = jnp.arange(n_pos, dtype=F32)[:, None] * inv[None, :]
    cos = jnp.tile(jnp.cos(ang), (1, LANES // half))
    sin = jnp.tile(jnp.concatenate([-jnp.sin(ang), jnp.sin(ang)], axis=1), (1, LANES // HEAD_DIM))
    return cos, sin


def _block_diag(w):
    per = LRU_GROUP // LRU_BLOCK
    w4 = w.reshape(D_LRU // LRU_GROUP, per, LRU_BLOCK, LRU_BLOCK)
    eye = jnp.eye(per, dtype=w.dtype)
    bd = jnp.einsum("gpcd,pq->gpcqd", w4, eye)
    return bd.reshape(D_LRU // LRU_GROUP, LRU_GROUP, LRU_GROUP).astype(BF16)


def _tile_positions(pos, tm):
    rows = pos.shape[1]
    return pos.reshape(TOP_K, rows // tm, tm).transpose(1, 0, 2).reshape(-1)


def _chunk_schedule(counts, n_rows):
    aligned = ((counts + SUB_ROWS - 1) // SUB_ROWS) * SUB_ROWS
    offs = jnp.cumsum(aligned) - aligned
    n_ch = (aligned + MOE_TM - 1) // MOE_TM
    cum = jnp.cumsum(n_ch)
    total = cum[-1]
    nch_max = N_EXPERTS + (n_rows + N_EXPERTS * SUB_ROWS) // MOE_TM
    c = jnp.arange(nch_max, dtype=jnp.int32)
    cc = jnp.minimum(c, total - 1)
    e = jnp.searchsorted(cum, cc, side="right").astype(jnp.int32)
    j = cc - (cum[e] - n_ch[e])
    start = offs[e] + j * MOE_TM
    nsub = jnp.where(c < total, jnp.minimum(MOE_TM, aligned[e] - j * MOE_TM) // SUB_ROWS, 0)
    tails = offs + (counts // SUB_ROWS) * SUB_ROWS
    used = jnp.sum(aligned)
    tail = jnp.stack([used, (n_rows + N_EXPERTS * SUB_ROWS - used) // SUB_ROWS])
    tails = jnp.concatenate([tails, tail])
    return (offs, e, start.astype(jnp.int32), nsub.astype(jnp.int32), tails.astype(jnp.int32),
            tail.astype(jnp.int32))


def kernel(x, meta_tokens, norm_mix, w_in, b_in, sinks, conv_w, conv_b, w_a, b_a, w_i, b_i, lru_lambda,
           g_attn_out, g_lru_out, w_out, b_out, norm_ffn, w_router, b_router, w_gate, b_gate, w_up, b_up,
           w_down, b_down, final_norm):
    batch, seq, d = x.shape
    rows = batch * seq
    x2d = x.reshape(rows, d)
    row = lambda v: v.reshape(1, -1)

    w_in_bf = w_in[0].astype(BF16)
    w_out_bf = w_out[0].astype(BF16)
    cos, sin = _rope_tables(N_META + seq)
    wa_bd, wi_bd = _block_diag(w_a[0]), _block_diag(w_i[0])
    lru_args = (conv_w[0], row(conv_b[0]), wa_bd, row(b_a[0]), wi_bd, row(b_i[0]), row(lru_lambda[0]),
                row(g_lru_out[0]))

    _, kvm, xgm = _inproj(meta_tokens, row(norm_mix[0]), w_in_bf, row(b_in[0]), cos[:N_META], sin[:N_META],
                          N_META, 1)
    _, h0, tail0 = _lru(xgm, *lru_args, jnp.zeros((1, D_LRU), F32), jnp.zeros((8, D_LRU), F32), 1, N_META,
                        N_META)

    tm = 512
    q, kv, xg = _inproj(x2d, row(norm_mix[0]), w_in_bf, row(b_in[0]), cos[N_META:], sin[N_META:], tm,
                        seq // tm)
    kvm = jnp.pad(kvm, ((0, BLOCK - N_META), (0, 0)))
    attn_n = _attention(sinks[0], q, kv, kvm, row(g_attn_out[0]), batch, seq)
    lru_n, _, _ = _lru(xg, *lru_args, h0, tail0, batch, seq, 512)

    tri = (jnp.arange(tm)[:, None] <= jnp.arange(tm)[None, :]).astype(BF16)
    h1, xp, ids, wts, rank, cnt = _outproj(
        attn_n, lru_n, w_out_bf, row(b_out[0]), x2d, row(norm_ffn[0]), w_router[0].T.astype(BF16),
        b_router[0].reshape(N_EXPERTS, 1), tri, tm)

    counts = cnt[:, 0].astype(jnp.int32)
    n_rows = rows * TOP_K
    offs, ch_e, ch_start, ch_nsub, tails, tail = _chunk_schedule(counts, n_rows)
    onehot = ids[..., None] == jnp.arange(N_EXPERTS, dtype=jnp.int32)
    pos = jnp.sum(jnp.where(onehot, offs, 0), axis=-1) + rank
    p_alloc = n_rows + N_EXPERTS * SUB_ROWS

    tmd = 512
    xs = _dispatch(tails, _tile_positions(pos, tmd), xp, p_alloc, tmd)
    act = _moe_up(ch_e, ch_start, ch_nsub, tail, xs, w_gate[0], b_gate[0], w_up[0], b_up[0])
    ys = _moe_down(ch_e, ch_start, ch_nsub, tail, act, w_down[0], b_down[0])
    tmc = 256
    out = _combine(_tile_positions(pos, tmc), ys, h1, wts.T, row(final_norm), tmc)
    return out.reshape(batch, seq, d)
```

```python
import jax
import jax.numpy as jnp
from jax import lax
from jax.experimental import pallas as pl
from jax.experimental.pallas import tpu as pltpu

F32 = jnp.float32
BF16 = jnp.bfloat16

N_META = 16
HEAD_DIM = 64
N_Q_HEADS = 16
N_KV_HEADS = 4
GROUP = N_Q_HEADS // N_KV_HEADS
D_ATTN = N_Q_HEADS * HEAD_DIM
D_KV = N_KV_HEADS * HEAD_DIM
KV_COLS = 4 * D_KV
BLOCK = 128
ROPE_THETA = 10000.0
D_LRU = 1024
LRU_BLOCK = 64
LRU_GROUP = 256
CONV_WIDTH = 4
LRU_C = 8.0
N_EXPERTS = 32
TOP_K = 4
SWIGLU_LIMIT = 7.0
SWIGLU_ALPHA = 1.702
NORM_EPS = 1e-5
NEG_INF = -1e30

LANES = 128
D_MODEL = 2048
TOK_WORDS = D_MODEL // 2
SUB_ROWS = 128
MOE_TM = 2304
MOE_TF = 512
MOE_TN = 1024
VMEM_LIMIT = 56 * 1024 * 1024


def _cparams(sem):
    return pltpu.CompilerParams(dimension_semantics=sem, vmem_limit_bytes=VMEM_LIMIT)


def _rms(x, g):
    return x * lax.rsqrt(jnp.mean(x * x, axis=-1, keepdims=True) + NORM_EPS) * g


def _inproj_kernel(x_ref, g_ref, w_ref, b_ref, cos_ref, sin_ref, q_ref, kv_ref, xg_ref):
    xn = _rms(x_ref[...], g_ref[...]).astype(BF16)
    cos = cos_ref[...]
    sin = sin_ref[...]
    lane = lax.broadcasted_iota(jnp.int32, (1, LANES), 1)
    first_half = (lane % HEAD_DIM) < (HEAD_DIM // 2)

    def rope(z):
        partner = jnp.where(first_half, pltpu.roll(z, LANES - HEAD_DIM // 2, 1),
                            pltpu.roll(z, HEAD_DIM // 2, 1))
        return z * cos + partner * sin

    cw = 512
    for c in range(w_ref.shape[1] // cw):
        z = jnp.dot(xn, w_ref[:, c * cw:(c + 1) * cw], preferred_element_type=F32)
        z = z + b_ref[:, c * cw:(c + 1) * cw]
        if c < 2:
            for j in range(cw // LANES):
                zz = rope(z[:, j * LANES:(j + 1) * LANES]) * (HEAD_DIM ** -0.5)
                q_ref[:, c * cw + j * LANES:c * cw + (j + 1) * LANES] = zz.astype(BF16)
        elif c == 2:
            low = lane < HEAD_DIM
            for j in range(2 * D_KV // LANES):
                zz = z[:, j * LANES:(j + 1) * LANES]
                zz = rope(zz) if j < D_KV // LANES else zz
                sw = pltpu.roll(zz, HEAD_DIM, 1)
                kv_ref[:, 2 * j * LANES:(2 * j + 1) * LANES] = jnp.where(low, zz, sw).astype(BF16)
                kv_ref[:, (2 * j + 1) * LANES:(2 * j + 2) * LANES] = jnp.where(low, sw, zz).astype(BF16)
        else:
            xg_ref[:, (c - 3) * cw:(c - 2) * cw] = z


def _inproj(x2d, g, w_bf, b, cos, sin, tm, pos_blocks):
    rows, d = x2d.shape
    dz = w_bf.shape[1]
    return pl.pallas_call(
        _inproj_kernel,
        grid=(rows // tm,),
        in_specs=[
            pl.BlockSpec((tm, d), lambda m: (m, 0)),
            pl.BlockSpec((1, d), lambda m: (0, 0)),
            pl.BlockSpec((d, dz), lambda m: (0, 0)),
            pl.BlockSpec((1, dz), lambda m: (0, 0)),
            pl.BlockSpec((tm, LANES), lambda m: (m % pos_blocks, 0)),
            pl.BlockSpec((tm, LANES), lambda m: (m % pos_blocks, 0)),
        ],
        out_specs=[
            pl.BlockSpec((tm, D_ATTN), lambda m: (m, 0)),
            pl.BlockSpec((tm, KV_COLS), lambda m: (m, 0)),
            pl.BlockSpec((tm, 2 * D_LRU), lambda m: (m, 0)),
        ],
        out_shape=[
            jax.ShapeDtypeStruct((rows, D_ATTN), BF16),
            jax.ShapeDtypeStruct((rows, KV_COLS), BF16),
            jax.ShapeDtypeStruct((rows, 2 * D_LRU), F32),
        ],
        compiler_params=_cparams(("arbitrary",)),
        name="inproj",
    )(x2d, g, w_bf, b, cos, sin)


def _attn_kernel(sink_ref, q_ref, kvc_ref, kvp_ref, kvm_ref, g_ref, o_ref):
    n = pl.program_id(1)
    row = lax.broadcasted_iota(jnp.int32, (GROUP * BLOCK, BLOCK), 0) % BLOCK
    col = lax.broadcasted_iota(jnp.int32, (GROUP * BLOCK, BLOCK), 1)
    in_cur = col <= row
    in_prev = jnp.logical_and(col > row, n > 0)
    is_meta = col < N_META
    low = lax.broadcasted_iota(jnp.int32, (BLOCK, LANES), 1) < HEAD_DIM
    nt = (((1,), (1,)), ((), ()))
    zero = jnp.zeros((), BF16)
    outs = []
    for h in range(N_KV_HEADS):
        ks = slice(h * LANES, (h + 1) * LANES)
        vs = slice(N_KV_HEADS * LANES + h * LANES, N_KV_HEADS * LANES + (h + 1) * LANES)
        parts = []
        for j in range(GROUP // 2):
            qg = q_ref[:, (h * GROUP // 2 + j) * LANES:(h * GROUP // 2 + j + 1) * LANES]
            parts += [jnp.where(low, qg, zero), jnp.where(low, zero, qg)]
        qs = jnp.concatenate(parts, axis=0)
        s_c = lax.dot_general(qs, kvc_ref[:, ks], nt, preferred_element_type=F32)
        s_p = lax.dot_general(qs, kvp_ref[:, ks], nt, preferred_element_type=F32)
        s_m = lax.dot_general(qs, kvm_ref[:, ks], nt, preferred_element_type=F32)
        s_b = jnp.where(in_prev, s_p, jnp.where(in_cur, s_c, NEG_INF))
        s_m = jnp.where(is_meta, s_m, NEG_INF)
        sink = jnp.concatenate(
            [jnp.full((BLOCK, 1), sink_ref[h * GROUP + g], F32) for g in range(GROUP)], axis=0)
        m = jnp.maximum(jnp.max(jnp.maximum(s_b, s_m), axis=-1, keepdims=True), sink)
        p_b = jnp.exp(s_b - m)
        p_m = jnp.exp(s_m - m)
        den = jnp.sum(p_b + p_m, axis=-1, keepdims=True) + jnp.exp(sink - m)
        r = (jnp.dot(jnp.where(in_cur, p_b, 0.0).astype(BF16), kvc_ref[:, vs], preferred_element_type=F32)
             + jnp.dot(jnp.where(in_cur, 0.0, p_b).astype(BF16), kvp_ref[:, vs], preferred_element_type=F32)
             + jnp.dot(p_m.astype(BF16), kvm_ref[:, vs], preferred_element_type=F32))
        r = r / den
        for j in range(GROUP // 2):
            outs.append(jnp.where(low, r[2 * j * BLOCK:(2 * j + 1) * BLOCK], r[(2 * j + 1) * BLOCK:(2 * j + 2) * BLOCK]))
    o_all = jnp.concatenate(outs, axis=1)
    o_ref[...] = _rms(o_all, g_ref[...]).astype(BF16)


def _attention(sinks, q, kv, kvm, g_attn, batch, seq):
    nb = seq // BLOCK
    return pl.pallas_call(
        _attn_kernel,
        grid_spec=pltpu.PrefetchScalarGridSpec(
            num_scalar_prefetch=1,
            grid=(batch, nb),
            in_specs=[
                pl.BlockSpec((BLOCK, D_ATTN), lambda b, n, s: (b * nb + n, 0)),
                pl.BlockSpec((BLOCK, KV_COLS), lambda b, n, s: (b * nb + n, 0)),
                pl.BlockSpec((BLOCK, KV_COLS), lambda b, n, s: (b * nb + jnp.maximum(n - 1, 0), 0)),
                pl.BlockSpec((BLOCK, KV_COLS), lambda b, n, s: (0, 0)),
                pl.BlockSpec((1, D_ATTN), lambda b, n, s: (0, 0)),
            ],
            out_specs=pl.BlockSpec((BLOCK, D_ATTN), lambda b, n, s: (b * nb + n, 0)),
        ),
        out_shape=jax.ShapeDtypeStruct((batch * seq, D_ATTN), BF16),
        compiler_params=_cparams(("arbitrary", "arbitrary")),
        name="attention",
    )(sinks, q, kv, kv, kvm, g_attn)


def _expm1(y):
    p = 1.0 + y * (1.0 / 8.0)
    for k in range(7, 1, -1):
        p = 1.0 + y * p * (1.0 / k)
    return jnp.where(y > -0.25, y * p, jnp.exp(y) - 1.0)


def _lru_kernel(xg_ref, cw_ref, cb_ref, wa_ref, ba_ref, wi_ref, bi_ref, lam_ref, g_ref, h0_ref, tail0_ref,
                o_ref, hout_ref, tailout_ref, ext_ref, a_ref, b_ref, h_ref):
    tt = pl.program_id(1)
    rows = a_ref.shape[0]

    @pl.when(tt == 0)
    def _():
        h_ref[...] = h0_ref[...]
        ext_ref[0:8, :] = tail0_ref[...]

    ext_ref[8:, :] = xg_ref[:, :D_LRU]
    xc = cb_ref[...] + sum(cw_ref[j:j + 1, :] * ext_ref[5 + j:5 + j + rows, :] for j in range(CONV_WIDTH))
    xcb = xc.astype(BF16)
    sp = jax.nn.softplus(-lam_ref[...])
    for c in range(D_LRU // LRU_GROUP):
        cs = slice(c * LRU_GROUP, (c + 1) * LRU_GROUP)
        r = jax.nn.sigmoid(jnp.dot(xcb[:, cs], wa_ref[c], preferred_element_type=F32) + ba_ref[:, cs])
        i = jax.nn.sigmoid(jnp.dot(xcb[:, cs], wi_ref[c], preferred_element_type=F32) + bi_ref[:, cs])
        log_a = -LRU_C * r * sp[:, cs]
        a_ref[:, cs] = jnp.exp(log_a)
        b_ref[:, cs] = jnp.sqrt(-_expm1(2.0 * log_a)) * i * xc[:, cs]

    def step(t, h):
        h = a_ref[pl.ds(t, 1), :] * h + b_ref[pl.ds(t, 1), :]
        b_ref[pl.ds(t, 1), :] = h
        return h

    h_last = lax.fori_loop(0, rows, step, h_ref[...])
    h_ref[...] = h_last
    hout_ref[...] = h_last
    tail = ext_ref[rows:rows + 8, :]
    ext_ref[0:8, :] = tail
    tailout_ref[...] = tail
    y = b_ref[...] * jax.nn.gelu(xg_ref[:, D_LRU:])
    o_ref[...] = _rms(y, g_ref[...]).astype(BF16)


def _lru(xg, conv_w, conv_b, wa_bd, b_a, wi_bd, b_i, lam, g_lru, h0, tail0, batch, seq, tt):
    ntt = seq // tt
    vec = pl.BlockSpec((1, D_LRU), lambda b, t: (0, 0))
    wspec = pl.BlockSpec((D_LRU // LRU_GROUP, LRU_GROUP, LRU_GROUP), lambda b, t: (0, 0, 0))
    return pl.pallas_call(
        _lru_kernel,
        grid=(batch, ntt),
        in_specs=[
            pl.BlockSpec((tt, 2 * D_LRU), lambda b, t: (b * ntt + t, 0)),
            pl.BlockSpec((CONV_WIDTH, D_LRU), lambda b, t: (0, 0)),
            vec, wspec, vec, wspec, vec, vec, vec, vec,
            pl.BlockSpec((8, D_LRU), lambda b, t: (0, 0)),
        ],
        out_specs=[
            pl.BlockSpec((tt, D_LRU), lambda b, t: (b * ntt + t, 0)),
            pl.BlockSpec((1, D_LRU), lambda b, t: (0, 0)),
            pl.BlockSpec((8, D_LRU), lambda b, t: (0, 0)),
        ],
        out_shape=[
            jax.ShapeDtypeStruct((batch * seq, D_LRU), BF16),
            jax.ShapeDtypeStruct((1, D_LRU), F32),
            jax.ShapeDtypeStruct((8, D_LRU), F32),
        ],
        scratch_shapes=[
            pltpu.VMEM((8 + tt, D_LRU), F32),
            pltpu.VMEM((tt, D_LRU), F32),
            pltpu.VMEM((tt, D_LRU), F32),
            pltpu.VMEM((1, D_LRU), F32),
        ],
        compiler_params=_cparams(("arbitrary", "arbitrary")),
        name="rglru",
    )(xg, conv_w, conv_b, wa_bd, b_a, wi_bd, b_i, lam, g_lru, h0, tail0)


def _outproj_kernel(a_ref, l_ref, wo_ref, bo_ref, x_ref, gf_ref, wr_ref, br_ref, tri_ref,
                    h_ref, xp_ref, ids_ref, wts_ref, rank_ref, cnt_ref, carry_ref):
    m = pl.program_id(0)
    tm = a_ref.shape[0]

    @pl.when(m == 0)
    def _():
        carry_ref[...] = jnp.zeros_like(carry_ref)

    h = (jnp.dot(a_ref[...], wo_ref[:D_ATTN, :], preferred_element_type=F32)
         + jnp.dot(l_ref[...], wo_ref[D_ATTN:, :], preferred_element_type=F32)
         + bo_ref[...] + x_ref[...])
    h_ref[...] = h
    xn = _rms(h, gf_ref[...]).astype(BF16)
    half = xn.shape[1] // 2
    lo = pltpu.bitcast(xn[:, :half].astype(F32), jnp.uint32)
    hi = pltpu.bitcast(xn[:, half:].astype(F32), jnp.uint32)
    packed = (hi & jnp.uint32(0xFFFF0000)) | (lo >> 16)
    ns = half // LANES
    for s in range(ns):
        xp_ref[pl.ds(s, tm, stride=ns), :] = packed[:, s * LANES:(s + 1) * LANES]

    logits = lax.dot_general(wr_ref[...], xn, (((1,), (1,)), ((), ())), preferred_element_type=F32)
    logits = logits + br_ref[...]
    eidx = lax.broadcasted_iota(jnp.int32, (N_EXPERTS, tm), 0)
    work = logits
    vals, sels = [], []
    for k in range(TOP_K):
        v = jnp.max(work, axis=0, keepdims=True)
        idx = jnp.min(jnp.where(work == v, eidx, N_EXPERTS), axis=0, keepdims=True)
        sel = eidx == idx
        ids_ref[k:k + 1, :] = idx
        vals.append(v)
        sels.append(sel)
        work = jnp.where(sel, -jnp.inf, work)
    es = [jnp.exp(v - vals[0]) for v in vals]
    den = es[0] + es[1] + es[2] + es[3]
    for k in range(TOP_K):
        wts_ref[k:k + 1, :] = es[k] / den
    cnt = sum(s.astype(F32) for s in sels)
    incl = jnp.dot(cnt.astype(BF16), tri_ref[...], preferred_element_type=F32)
    before = incl - cnt + carry_ref[:, 0:1]
    for k in range(TOP_K):
        rk = jnp.sum(jnp.where(sels[k], before, 0.0), axis=0, keepdims=True)
        rank_ref[k:k + 1, :] = rk.astype(jnp.int32)
    carry_ref[...] = carry_ref[...] + incl[:, tm - 1:tm]
    cnt_ref[...] = carry_ref[...]


def _outproj(attn_n, lru_n, wo_bf, b_out, x2d, g_ffn, wr_t, br, tri, tm):
    rows, d = x2d.shape
    const = lambda shape: pl.BlockSpec(shape, lambda m: tuple(0 for _ in shape))
    return pl.pallas_call(
        _outproj_kernel,
        grid=(rows // tm,),
        in_specs=[
            pl.BlockSpec((tm, D_ATTN), lambda m: (m, 0)),
            pl.BlockSpec((tm, D_LRU), lambda m: (m, 0)),
            const((D_ATTN + D_LRU, d)), const((1, d)),
            pl.BlockSpec((tm, d), lambda m: (m, 0)),
            const((1, d)), const((N_EXPERTS, d)), const((N_EXPERTS, 1)), const((tm, tm)),
        ],
        out_specs=[
            pl.BlockSpec((tm, d), lambda m: (m, 0)),
            pl.BlockSpec((tm * (d // 2 // LANES), LANES), lambda m: (m, 0)),
            pl.BlockSpec((TOP_K, tm), lambda m: (0, m)),
            pl.BlockSpec((TOP_K, tm), lambda m: (0, m)),
            pl.BlockSpec((TOP_K, tm), lambda m: (0, m)),
            const((N_EXPERTS, LANES)),
        ],
        out_shape=[
            jax.ShapeDtypeStruct((rows, d), F32),
            jax.ShapeDtypeStruct((rows * (d // 2 // LANES), LANES), jnp.uint32),
            jax.ShapeDtypeStruct((TOP_K, rows), jnp.int32),
            jax.ShapeDtypeStruct((TOP_K, rows), F32),
            jax.ShapeDtypeStruct((TOP_K, rows), jnp.int32),
            jax.ShapeDtypeStruct((N_EXPERTS, LANES), F32),
        ],
        scratch_shapes=[pltpu.VMEM((N_EXPERTS, LANES), F32)],
        compiler_params=_cparams(("arbitrary",)),
        name="outproj_router",
    )(attn_n, lru_n, wo_bf, b_out, x2d, g_ffn, wr_t, br, tri)


def _zero_fill_rows(zero_ref, dst_hbm, first_row, n_blocks, sem):
    blk = zero_ref.shape[0]
    per = blk // SUB_ROWS

    def copy(j):
        row = pl.multiple_of((first_row + j * SUB_ROWS) * per, blk)
        return pltpu.make_async_copy(zero_ref, dst_hbm.at[pl.ds(row, blk)], sem)

    def start(j, c):
        copy(j).start()
        return c

    def wait(j, c):
        copy(j).wait()
        return c

    lax.fori_loop(0, n_blocks, start, 0)
    lax.fori_loop(0, n_blocks, wait, 0)


def _dispatch_kernel(tails_ref, pos_hbm, xp_ref, xs_hbm, pos_smem, zero_ref, sem, psem):
    i = pl.program_id(0)
    ns = TOK_WORDS // LANES
    tm = xp_ref.shape[0] // ns
    n = TOP_K * tm
    pcopy = pltpu.make_async_copy(pos_hbm.at[pl.ds(i * n, n)], pos_smem, psem)
    pcopy.start()

    @pl.when(i == 0)
    def _():
        zero_ref[...] = jnp.zeros_like(zero_ref)
        for e in range(N_EXPERTS):
            _zero_fill_rows(zero_ref, xs_hbm, tails_ref[e], 1, sem)
        _zero_fill_rows(zero_ref, xs_hbm, tails_ref[N_EXPERTS], tails_ref[N_EXPERTS + 1], sem)

    pcopy.wait()

    def issue(jj, c):
        for u in range(2):
            j = jj * 2 + u
            src = xp_ref.at[pl.ds(pl.multiple_of((j % tm) * ns, ns), ns)]
            dst = xs_hbm.at[pl.ds(pl.multiple_of(pos_smem[j] * ns, ns), ns)]
            pltpu.make_async_copy(src, dst, sem).start(priority=u)
        return c

    lax.fori_loop(0, n // 2, issue, 0)
    for _ in range(TOP_K):
        pltpu.make_async_copy(xp_ref, xs_hbm.at[pl.ds(0, tm * ns)], sem).wait()


def _dispatch(tails, pos_tiles, xp, p_alloc, tm):
    ns = TOK_WORDS // LANES
    rows = xp.shape[0] // ns
    return pl.pallas_call(
        _dispatch_kernel,
        grid_spec=pltpu.PrefetchScalarGridSpec(
            num_scalar_prefetch=1,
            grid=(rows // tm,),
            in_specs=[
                pl.BlockSpec(memory_space=pl.ANY),
                pl.BlockSpec((tm * ns, LANES), lambda i, s: (i, 0)),
            ],
            out_specs=pl.BlockSpec(memory_space=pl.ANY),
            scratch_shapes=[
                pltpu.SMEM((TOP_K * tm,), jnp.int32),
                pltpu.VMEM((SUB_ROWS * ns, LANES), jnp.uint32),
                pltpu.SemaphoreType.DMA,
                pltpu.SemaphoreType.DMA,
            ],
        ),
        out_shape=jax.ShapeDtypeStruct((p_alloc * ns, LANES), jnp.uint32),
        compiler_params=_cparams(("arbitrary",)),
        name="dispatch",
    )(tails, pos_tiles, xp)


def _chunk_dma(cs_ref, cn_ref, src_hbm, dst_ref, per, sem):
    blk = SUB_ROWS * per
    nch = cs_ref.shape[0]

    def copy(ci, j):
        src = src_hbm.at[pl.ds(pl.multiple_of((cs_ref[ci] + j * SUB_ROWS) * per, blk), blk)]
        return pltpu.make_async_copy(src, dst_ref.at[pl.ds(pl.multiple_of(j * blk, blk), blk)], sem)

    def run(ci, wait):
        cc = jnp.minimum(ci, nch - 1)
        n = jnp.where(ci < nch, cn_ref[cc], 0)

        def body(j, c):
            if wait:
                copy(cc, j).wait()
            else:
                copy(cc, j).start()
            return c

        lax.fori_loop(0, n, body, 0)

    return (lambda ci: run(ci, False)), (lambda ci: run(ci, True))


def _tile_loop(nsub, tile_fn):
    n4 = nsub // 4
    rem = nsub % 4
    base = n4 * 4 * SUB_ROWS

    @pl.when(rem >= 2)
    def _():
        tile_fn(pl.multiple_of(base, SUB_ROWS), 2 * SUB_ROWS, 0, 1)

    @pl.when(rem % 2 == 1)
    def _():
        tile_fn(pl.multiple_of(base + (rem // 2) * 2 * SUB_ROWS, SUB_ROWS), SUB_ROWS, 0, 2)

    def full(j, c):
        tile_fn(pl.multiple_of(j * 4 * SUB_ROWS, 4 * SUB_ROWS), 4 * SUB_ROWS, j, 0)
        return c

    lax.fori_loop(0, n4, full, 0)


def _moe_up_kernel(ce_ref, cs_ref, cn_ref, tail_ref, xs_hbm, wg_ref, bg_ref, wu_ref, bu_ref, act_hbm,
                   land_ref, xb_ref, wgu_ref, ab_ref, zero_ref, pend_ref, sem, osem):
    c = pl.program_id(0)
    f = pl.program_id(1)
    nsub = cn_ref[c]
    start = pl.multiple_of(cs_ref[c], SUB_ROWS)
    tf = wg_ref.shape[2]
    ns = TOK_WORDS // LANES
    start_chunk, wait_chunk = _chunk_dma(cs_ref, cn_ref, xs_hbm, land_ref, ns, sem)

    @pl.when(jnp.logical_and(c == 0, f == 0))
    def _():
        zero_ref[...] = jnp.zeros_like(zero_ref)
        _zero_fill_rows(zero_ref, act_hbm, tail_ref[0], tail_ref[1], osem.at[0])
        for slot in range(4):
            pend_ref[slot] = 0
        start_chunk(c)

    @pl.when(jnp.logical_and(f == 0, nsub > 0))
    def _():
        wait_chunk(c)

        def unpack(j, carry):
            r = pl.multiple_of(j * SUB_ROWS, SUB_ROWS)
            for s in range(ns):
                w = land_ref[pl.ds(r * ns + s, SUB_ROWS, stride=ns), :]
                xb_ref[pl.ds(r, SUB_ROWS), s * LANES:(s + 1) * LANES] = pltpu.bitcast(w << 16, F32).astype(BF16)
                xb_ref[pl.ds(r, SUB_ROWS), TOK_WORDS + s * LANES:TOK_WORDS + (s + 1) * LANES] = (
                    pltpu.bitcast(w & jnp.uint32(0xFFFF0000), F32).astype(BF16))
            return carry

        lax.fori_loop(0, nsub, unpack, 0)
        start_chunk(c + 1)

    def out_copy(r, rows, slot):
        return pltpu.make_async_copy(
            ab_ref.at[slot, pl.ds(0, rows)],
            act_hbm.at[pl.ds(pl.multiple_of(start + r, SUB_ROWS), rows), pl.ds(pl.multiple_of(f * tf, tf), tf)],
            osem.at[slot])

    @pl.when(nsub > 0)
    def _():
        wgu_ref[:, :tf] = wg_ref[0].astype(BF16)
        wgu_ref[:, tf:] = wu_ref[0].astype(BF16)
        bg = bg_ref[0]
        bu = bu_ref[0]

        def tile(r, rows, j, kind):
            z = jnp.dot(xb_ref[pl.ds(r, rows), :], wgu_ref[...], preferred_element_type=F32)
            g = z[:, :tf] + bg
            u = z[:, tf:] + bu
            g = jnp.minimum(g, SWIGLU_LIMIT)
            u = jnp.clip(u, -SWIGLU_LIMIT, SWIGLU_LIMIT)
            a = (g * jax.nn.sigmoid(SWIGLU_ALPHA * g) * (u + 1.0)).astype(BF16)
            slot = j % 2 if kind == 0 else 1 + kind

            @pl.when(pend_ref[slot] == 1)
            def _():
                out_copy(r, rows, slot).wait()

            ab_ref[slot, pl.ds(0, rows), :] = a
            out_copy(r, rows, slot).start()
            pend_ref[slot] = 1

        _tile_loop(nsub, tile)

    @pl.when(jnp.logical_and(c == pl.num_programs(0) - 1, f == pl.num_programs(1) - 1))
    def _():
        for slot, rows in enumerate((4 * SUB_ROWS, 4 * SUB_ROWS, 2 * SUB_ROWS, SUB_ROWS)):
            @pl.when(pend_ref[slot] == 1)
            def _():
                out_copy(0, rows, slot).wait()


def _moe_up(ch_e, ch_start, ch_nsub, tail, xs, w_gate, b_gate, w_up, b_up):
    ns = TOK_WORDS // LANES
    p_alloc = xs.shape[0] // ns
    d = 2 * TOK_WORDS
    dff = w_gate.shape[2]
    tf = MOE_TF
    nch = ch_e.shape[0]
    nf = dff // tf
    wmap = lambda c, f, e, s, n, t: (e[c], 0, jnp.where(n[c] > 0, f, nf - 1))
    return pl.pallas_call(
        _moe_up_kernel,
        grid_spec=pltpu.PrefetchScalarGridSpec(
            num_scalar_prefetch=4,
            grid=(nch, nf),
            in_specs=[
                pl.BlockSpec(memory_space=pl.ANY),
                pl.BlockSpec((1, d, tf), wmap),
                pl.BlockSpec((1, 1, tf), wmap),
                pl.BlockSpec((1, d, tf), wmap),
                pl.BlockSpec((1, 1, tf), wmap),
            ],
            out_specs=pl.BlockSpec(memory_space=pl.ANY),
            scratch_shapes=[
                pltpu.VMEM((MOE_TM * ns, LANES), jnp.uint32),
                pltpu.VMEM((MOE_TM, d), BF16),
                pltpu.VMEM((d, 2 * tf), BF16),
                pltpu.VMEM((4, 4 * SUB_ROWS, tf), BF16),
                pltpu.VMEM((SUB_ROWS, dff), BF16),
                pltpu.SMEM((4,), jnp.int32),
                pltpu.SemaphoreType.DMA,
                pltpu.SemaphoreType.DMA((4,)),
            ],
        ),
        out_shape=jax.ShapeDtypeStruct((p_alloc, dff), BF16),
        compiler_params=_cparams(("arbitrary", "arbitrary")),
        name="moe_up",
    )(ch_e, ch_start, ch_nsub, tail, xs, w_gate, b_gate.reshape(N_EXPERTS, 1, dff), w_up,
      b_up.reshape(N_EXPERTS, 1, dff))


def _moe_down_kernel(ce_ref, cs_ref, cn_ref, tail_ref, act_hbm, wd_ref, bd_ref, ys_hbm,
                     xa_ref, xb_ref, wdb_ref, yb_ref, zero_ref, pend_ref, sem, osem):
    c = pl.program_id(0)
    f = pl.program_id(1)
    nf = pl.num_programs(1)
    nsub = cn_ref[c]
    start = pl.multiple_of(cs_ref[c], SUB_ROWS)
    tn = wd_ref.shape[2]
    ns = TOK_WORDS // LANES
    nw = tn // 2 // LANES
    copies = [_chunk_dma(cs_ref, cn_ref, act_hbm, ref, 1, sem.at[i]) for i, ref in enumerate((xa_ref, xb_ref))]

    def on_parity(ci, fn):
        for p in range(2):
            @pl.when(ci % 2 == p)
            def _():
                fn(p)

    @pl.when(jnp.logical_and(c == 0, f == 0))
    def _():
        zero_ref[...] = jnp.zeros_like(zero_ref)
        _zero_fill_rows(zero_ref, ys_hbm, tail_ref[0], tail_ref[1], osem)
        for kind in range(3):
            pend_ref[kind] = 0
        copies[0][0](c)

    @pl.when(jnp.logical_and(f == 0, nsub > 0))
    def _():
        on_parity(c, lambda p: copies[p][1](c))
        on_parity(c + 1, lambda p: copies[p][0](c + 1))

    def out_copy(r, rows):
        src = yb_ref.at[pl.ds(pl.multiple_of(r * ns, SUB_ROWS * ns), rows * ns)]
        dst = ys_hbm.at[pl.ds(pl.multiple_of((start + r) * ns, SUB_ROWS * ns), rows * ns)]
        return pltpu.make_async_copy(src, dst, osem)

    def drain():
        for kind, rows in enumerate((4 * SUB_ROWS, 2 * SUB_ROWS, SUB_ROWS)):
            def wait(j, carry):
                out_copy(0, rows).wait()
                return carry

            lax.fori_loop(0, pend_ref[kind], wait, 0)
            pend_ref[kind] = 0

    def compute(x_ref):
        wdb_ref[...] = wd_ref[0].astype(BF16)
        bd = bd_ref[0]
        last = f == nf - 1

        def tile(r, rows, j, kind):
            y = jnp.dot(x_ref[pl.ds(r, rows), :], wdb_ref[...], preferred_element_type=F32) + bd
            lo = pltpu.bitcast(y[:, :tn // 2].astype(BF16).astype(F32), jnp.uint32)
            hi = pltpu.bitcast(y[:, tn // 2:].astype(BF16).astype(F32), jnp.uint32)
            packed = (hi & jnp.uint32(0xFFFF0000)) | (lo >> 16)

            @pl.when(f == 0)
            def _():
                drain()

            for s in range(nw):
                yb_ref[pl.ds(r * ns + f * nw + s, rows, stride=ns), :] = packed[:, s * LANES:(s + 1) * LANES]

            @pl.when(last)
            def _():
                out_copy(r, rows).start()
                pend_ref[kind] = pend_ref[kind] + 1

        _tile_loop(nsub, tile)

    @pl.when(nsub > 0)
    def _():
        on_parity(c, lambda p: compute((xa_ref, xb_ref)[p]))

    @pl.when(jnp.logical_and(c == pl.num_programs(0) - 1, f == nf - 1))
    def _():
        drain()


def _moe_down(ch_e, ch_start, ch_nsub, tail, act, w_down, b_down):
    p_alloc, dff = act.shape
    d = w_down.shape[2]
    ns = TOK_WORDS // LANES
    tn = MOE_TN
    nch = ch_e.shape[0]
    nn = d // tn
    wmap = lambda c, f, e, s, n, t: (e[c], 0, jnp.where(n[c] > 0, f, nn - 1))
    return pl.pallas_call(
        _moe_down_kernel,
        grid_spec=pltpu.PrefetchScalarGridSpec(
            num_scalar_prefetch=4,
            grid=(nch, nn),
            in_specs=[
                pl.BlockSpec(memory_space=pl.ANY),
                pl.BlockSpec((1, dff, tn), wmap),
                pl.BlockSpec((1, 1, tn), wmap),
            ],
            out_specs=pl.BlockSpec(memory_space=pl.ANY),
            scratch_shapes=[
                pltpu.VMEM((MOE_TM, dff), BF16),
                pltpu.VMEM((MOE_TM, dff), BF16),
                pltpu.VMEM((dff, tn), BF16),
                pltpu.VMEM((MOE_TM * ns, LANES), jnp.uint32),
                pltpu.VMEM((SUB_ROWS * ns, LANES), jnp.uint32),
                pltpu.SMEM((3,), jnp.int32),
                pltpu.SemaphoreType.DMA((2,)),
                pltpu.SemaphoreType.DMA,
            ],
        ),
        out_shape=jax.ShapeDtypeStruct((p_alloc * ns, LANES), jnp.uint32),
        compiler_params=_cparams(("arbitrary", "arbitrary")),
        name="moe_down",
    )(ch_e, ch_start, ch_nsub, tail, act, w_down, b_down.reshape(N_EXPERTS, 1, d))


def _combine_kernel(pos_hbm, ys_hbm, h_ref, w_ref, g_ref, o_ref, pos_smem, yb_ref, sem, psem):
    i = pl.program_id(0)
    tm, d = h_ref.shape
    n = TOP_K * tm
    ns = TOK_WORDS // LANES

    def fetch(tile, slot):
        pcopy = pltpu.make_async_copy(pos_hbm.at[pl.ds(tile * n, n)], pos_smem.at[pl.ds(slot * n, n)], psem)
        pcopy.start()
        pcopy.wait()

        def issue(jj, c):
            for u in range(2):
                j = jj * 2 + u
                src = ys_hbm.at[pl.ds(pl.multiple_of(pos_smem[slot * n + j] * ns, ns), ns)]
                dst = yb_ref.at[slot, pl.ds(pl.multiple_of(j * ns, ns), ns)]
                pltpu.make_async_copy(src, dst, sem.at[slot]).start(priority=u)
            return c

        lax.fori_loop(0, n // 2, issue, 0)

    @pl.when(i == 0)
    def _():
        fetch(i, 0)

    @pl.when(i + 1 < pl.num_programs(0))
    def _():
        for p in range(2):
            @pl.when((i + 1) % 2 == p)
            def _():
                fetch(i + 1, p)

    def reduce(slot):
        pltpu.make_async_copy(ys_hbm.at[pl.ds(0, n * ns)], yb_ref.at[slot], sem.at[slot]).wait()
        wb = [jnp.broadcast_to(w_ref[:, k:k + 1], (tm, LANES)) for k in range(TOP_K)]
        ssq = jnp.zeros((tm, 1), F32)
        nw = MOE_TN // 2 // LANES
        for s in range(ns):
            c0 = (s // nw) * MOE_TN + (s % nw) * LANES
            lo_cols = slice(c0, c0 + LANES)
            hi_cols = slice(c0 + MOE_TN // 2, c0 + MOE_TN // 2 + LANES)
            acc_lo = h_ref[:, lo_cols]
            acc_hi = h_ref[:, hi_cols]
            for k in range(TOP_K):
                w = yb_ref[slot, pl.ds(k * tm * ns + s, tm, stride=ns), :]
                acc_lo = acc_lo + wb[k] * pltpu.bitcast(w << 16, F32)
                acc_hi = acc_hi + wb[k] * pltpu.bitcast(w & jnp.uint32(0xFFFF0000), F32)
            o_ref[:, lo_cols] = acc_lo
            o_ref[:, hi_cols] = acc_hi
            ssq = ssq + jnp.sum(acc_lo * acc_lo + acc_hi * acc_hi, axis=-1, keepdims=True)
        o_ref[...] = o_ref[...] * lax.rsqrt(ssq * (1.0 / d) + NORM_EPS) * g_ref[...]

    for p in range(2):
        @pl.when(i % 2 == p)
        def _():
            reduce(p)


def _combine(pos_tiles, ys, h1, wts_t, g_final, tm):
    rows, d = h1.shape
    ns = TOK_WORDS // LANES
    return pl.pallas_call(
        _combine_kernel,
        grid=(rows // tm,),
        in_specs=[
            pl.BlockSpec(memory_space=pl.ANY),
            pl.BlockSpec(memory_space=pl.ANY),
            pl.BlockSpec((tm, d), lambda i: (i, 0)),
            pl.BlockSpec((tm, TOP_K), lambda i: (i, 0)),
            pl.BlockSpec((1, d), lambda i: (0, 0)),
        ],
        out_specs=pl.BlockSpec((tm, d), lambda i: (i, 0)),
        out_shape=jax.ShapeDtypeStruct((rows, d), F32),
        scratch_shapes=[
            pltpu.SMEM((2 * TOP_K * tm,), jnp.int32),
            pltpu.VMEM((2, TOP_K * tm * ns, LANES), jnp.uint32),
            pltpu.SemaphoreType.DMA((2,)),
            pltpu.SemaphoreType.DMA,
        ],
        compiler_params=_cparams(("arbitrary",)),
        name="combine",
    )(pos_tiles, ys, h1, wts_t, g_final)


def _rope_tables(n_pos):
    half = HEAD_DIM // 2
    inv = 1.0 / (ROPE_THETA ** (jnp.arange(half, dtype=F32) / half))
    ang = jnp.arange(n_pos, dtype=F32)[:, None] * inv[None, :]
    cos = jnp.tile(jnp.cos(ang), (1, LANES // half))
    sin = jnp.tile(jnp.concatenate([-jnp.sin(ang), jnp.sin(ang)], axis=1), (1, LANES // HEAD_DIM))
    return cos, sin


def _block_diag(w):
    per = LRU_GROUP // LRU_BLOCK
    w4 = w.reshape(D_LRU // LRU_GROUP, per, LRU_BLOCK, LRU_BLOCK)
    eye = jnp.eye(per, dtype=w.dtype)
    bd = jnp.einsum("gpcd,pq->gpcqd", w4, eye)
    return bd.reshape(D_LRU // LRU_GROUP, LRU_GROUP, LRU_GROUP).astype(BF16)


def _tile_positions(pos, tm):
    rows = pos.shape[1]
    return pos.reshape(TOP_K, rows // tm, tm).transpose(1, 0, 2).reshape(-1)


def _chunk_schedule(counts, n_rows):
    aligned = ((counts + SUB_ROWS - 1) // SUB_ROWS) * SUB_ROWS
    offs = jnp.cumsum(aligned) - aligned
    n_ch = (aligned + MOE_TM - 1) // MOE_TM
    cum = jnp.cumsum(n_ch)
    total = cum[-1]
    nch_max = N_EXPERTS + (n_rows + N_EXPERTS * SUB_ROWS) // MOE_TM
    c = jnp.arange(nch_max, dtype=jnp.int32)
    cc = jnp.minimum(c, total - 1)
    e = jnp.searchsorted(cum, cc, side="right").astype(jnp.int32)
    j = cc - (cum[e] - n_ch[e])
    start = offs[e] + j * MOE_TM
    nsub = jnp.where(c < total, jnp.minimum(MOE_TM, aligned[e] - j * MOE_TM) // SUB_ROWS, 0)
    tails = offs + (counts // SUB_ROWS) * SUB_ROWS
    used = jnp.sum(aligned)
    tail = jnp.stack([used, (n_rows + N_EXPERTS * SUB_ROWS - used) // SUB_ROWS])
    tails = jnp.concatenate([tails, tail])
    return (offs, e, start.astype(jnp.int32), nsub.astype(jnp.int32), tails.astype(jnp.int32),
            tail.astype(jnp.int32))


def kernel(x, meta_tokens, norm_mix, w_in, b_in, sinks, conv_w, conv_b, w_a, b_a, w_i, b_i, lru_lambda,
           g_attn_out, g_lru_out, w_out, b_out, norm_ffn, w_router, b_router, w_gate, b_gate, w_up, b_up,
           w_down, b_down, final_norm):
    batch, seq, d = x.shape
    rows = batch * seq
    x2d = x.reshape(rows, d)
    row = lambda v: v.reshape(1, -1)

    w_in_bf = w_in[0].astype(BF16)
    w_out_bf = w_out[0].astype(BF16)
    cos, sin = _rope_tables(N_META + seq)
    wa_bd, wi_bd = _block_diag(w_a[0]), _block_diag(w_i[0])
    lru_args = (conv_w[0], row(conv_b[0]), wa_bd, row(b_a[0]), wi_bd, row(b_i[0]), row(lru_lambda[0]),
                row(g_lru_out[0]))

    _, kvm, xgm = _inproj(meta_tokens, row(norm_mix[0]), w_in_bf, row(b_in[0]), cos[:N_META], sin[:N_META],
                          N_META, 1)
    _, h0, tail0 = _lru(xgm, *lru_args, jnp.zeros((1, D_LRU), F32), jnp.zeros((8, D_LRU), F32), 1, N_META,
                        N_META)

    tm = 512
    q, kv, xg = _inproj(x2d, row(norm_mix[0]), w_in_bf, row(b_in[0]), cos[N_META:], sin[N_META:], tm,
                        seq // tm)
    kvm = jnp.pad(kvm, ((0, BLOCK - N_META), (0, 0)))
    attn_n = _attention(sinks[0], q, kv, kvm, row(g_attn_out[0]), batch, seq)
    lru_n, _, _ = _lru(xg, *lru_args, h0, tail0, batch, seq, 512)

    tri = (jnp.arange(tm)[:, None] <= jnp.arange(tm)[None, :]).astype(BF16)
    h1, xp, ids, wts, rank, cnt = _outproj(
        attn_n, lru_n, w_out_bf, row(b_out[0]), x2d, row(norm_ffn[0]), w_router[0].T.astype(BF16),
        b_router[0].reshape(N_EXPERTS, 1), tri, tm)

    counts = cnt[:, 0].astype(jnp.int32)
    n_rows = rows * TOP_K
    offs, ch_e, ch_start, ch_nsub, tails, tail = _chunk_schedule(counts, n_rows)
    onehot = ids[..., None] == jnp.arange(N_EXPERTS, dtype=jnp.int32)
    pos = jnp.sum(jnp.where(onehot, offs, 0), axis=-1) + rank
    p_alloc = n_rows + N_EXPERTS * SUB_ROWS

    tmd = 512
    xs = _dispatch(tails, _tile_positions(pos, tmd), xp, p_alloc, tmd)
    act = _moe_up(ch_e, ch_start, ch_nsub, tail, xs, w_gate[0], b_gate[0], w_up[0], b_up[0])
    ys = _moe_down(ch_e, ch_start, ch_nsub, tail, act, w_down[0], b_down[0])
    tmc = 256
    out = _combine(_tile_positions(pos, tmc), ys, h1, wts.T, row(final_norm), tmc)
    return out.reshape(batch, seq, d)
```

```python
import jax
import jax.numpy as jnp
from jax import lax
from jax.experimental import pallas as pl
from jax.experimental.pallas import tpu as pltpu

F32 = jnp.float32
BF16 = jnp.bfloat16

N_META = 16
HEAD_DIM = 64
N_Q_HEADS = 16
N_KV_HEADS = 4
GROUP = N_Q_HEADS // N_KV_HEADS
D_ATTN = N_Q_HEADS * HEAD_DIM
D_KV = N_KV_HEADS * HEAD_DIM
KV_COLS = 4 * D_KV
BLOCK = 128
ROPE_THETA = 10000.0
D_LRU = 1024
LRU_BLOCK = 64
LRU_GROUP = 256
CONV_WIDTH = 4
LRU_C = 8.0
N_EXPERTS = 32
TOP_K = 4
SWIGLU_LIMIT = 7.0
SWIGLU_ALPHA = 1.702
NORM_EPS = 1e-5
NEG_INF = -1e30

LANES = 128
D_MODEL = 2048
TOK_WORDS = D_MODEL // 2
SUB_ROWS = 128
MOE_TM = 2304
MOE_TF = 512
MOE_TN = 1024
ROW_DMA_UNROLL = 8
VMEM_LIMIT = 56 * 1024 * 1024


def _cparams(sem):
    return pltpu.CompilerParams(dimension_semantics=sem, vmem_limit_bytes=VMEM_LIMIT)


def _rms(x, g):
    return x * lax.rsqrt(jnp.mean(x * x, axis=-1, keepdims=True) + NORM_EPS) * g


def _inproj_kernel(x_ref, g_ref, w_ref, b_ref, cos_ref, sin_ref, q_ref, kv_ref, xg_ref):
    xn = _rms(x_ref[...], g_ref[...]).astype(BF16)
    cos = cos_ref[...]
    sin = sin_ref[...]
    lane = lax.broadcasted_iota(jnp.int32, (1, LANES), 1)
    first_half = (lane % HEAD_DIM) < (HEAD_DIM // 2)

    def rope(z):
        partner = jnp.where(first_half, pltpu.roll(z, LANES - HEAD_DIM // 2, 1),
                            pltpu.roll(z, HEAD_DIM // 2, 1))
        return z * cos + partner * sin

    cw = 512
    for c in range(w_ref.shape[1] // cw):
        z = jnp.dot(xn, w_ref[:, c * cw:(c + 1) * cw], preferred_element_type=F32)
        z = z + b_ref[:, c * cw:(c + 1) * cw]
        if c < 2:
            for j in range(cw // LANES):
                zz = rope(z[:, j * LANES:(j + 1) * LANES]) * (HEAD_DIM ** -0.5)
                q_ref[:, c * cw + j * LANES:c * cw + (j + 1) * LANES] = zz.astype(BF16)
        elif c == 2:
            low = lane < HEAD_DIM
            for j in range(2 * D_KV // LANES):
                zz = z[:, j * LANES:(j + 1) * LANES]
                zz = rope(zz) if j < D_KV // LANES else zz
                sw = pltpu.roll(zz, HEAD_DIM, 1)
                kv_ref[:, 2 * j * LANES:(2 * j + 1) * LANES] = jnp.where(low, zz, sw).astype(BF16)
                kv_ref[:, (2 * j + 1) * LANES:(2 * j + 2) * LANES] = jnp.where(low, sw, zz).astype(BF16)
        else:
            xg_ref[:, (c - 3) * cw:(c - 2) * cw] = z


def _inproj(x2d, g, w_bf, b, cos, sin, tm, pos_blocks):
    rows, d = x2d.shape
    dz = w_bf.shape[1]
    return pl.pallas_call(
        _inproj_kernel,
        grid=(rows // tm,),
        in_specs=[
            pl.BlockSpec((tm, d), lambda m: (m, 0)),
            pl.BlockSpec((1, d), lambda m: (0, 0)),
            pl.BlockSpec((d, dz), lambda m: (0, 0)),
            pl.BlockSpec((1, dz), lambda m: (0, 0)),
            pl.BlockSpec((tm, LANES), lambda m: (m % pos_blocks, 0)),
            pl.BlockSpec((tm, LANES), lambda m: (m % pos_blocks, 0)),
        ],
        out_specs=[
            pl.BlockSpec((tm, D_ATTN), lambda m: (m, 0)),
            pl.BlockSpec((tm, KV_COLS), lambda m: (m, 0)),
            pl.BlockSpec((tm, 2 * D_LRU), lambda m: (m, 0)),
        ],
        out_shape=[
            jax.ShapeDtypeStruct((rows, D_ATTN), BF16),
            jax.ShapeDtypeStruct((rows, KV_COLS), BF16),
            jax.ShapeDtypeStruct((rows, 2 * D_LRU), F32),
        ],
        compiler_params=_cparams(("arbitrary",)),
        name="inproj",
    )(x2d, g, w_bf, b, cos, sin)


def _attn_kernel(sink_ref, q_ref, kvc_ref, kvp_ref, kvm_ref, g_ref, o_ref):
    n = pl.program_id(1)
    row = lax.broadcasted_iota(jnp.int32, (GROUP * BLOCK, BLOCK), 0) % BLOCK
    col = lax.broadcasted_iota(jnp.int32, (GROUP * BLOCK, BLOCK), 1)
    in_cur = col <= row
    in_prev = jnp.logical_and(col > row, n > 0)
    is_meta = col < N_META
    low = lax.broadcasted_iota(jnp.int32, (BLOCK, LANES), 1) < HEAD_DIM
    nt = (((1,), (1,)), ((), ()))
    zero = jnp.zeros((), BF16)
    outs = []
    for h in range(N_KV_HEADS):
        ks = slice(h * LANES, (h + 1) * LANES)
        vs = slice(N_KV_HEADS * LANES + h * LANES, N_KV_HEADS * LANES + (h + 1) * LANES)
        parts = []
        for j in range(GROUP // 2):
            qg = q_ref[:, (h * GROUP // 2 + j) * LANES:(h * GROUP // 2 + j + 1) * LANES]
            parts += [jnp.where(low, qg, zero), jnp.where(low, zero, qg)]
        qs = jnp.concatenate(parts, axis=0)
        s_c = lax.dot_general(qs, kvc_ref[:, ks], nt, preferred_element_type=F32)
        s_p = lax.dot_general(qs, kvp_ref[:, ks], nt, preferred_element_type=F32)
        s_m = lax.dot_general(qs, kvm_ref[:, ks], nt, preferred_element_type=F32)
        s_b = jnp.where(in_prev, s_p, jnp.where(in_cur, s_c, NEG_INF))
        s_m = jnp.where(is_meta, s_m, NEG_INF)
        sink = jnp.concatenate(
            [jnp.full((BLOCK, 1), sink_ref[h * GROUP + g], F32) for g in range(GROUP)], axis=0)
        m = jnp.maximum(jnp.max(jnp.maximum(s_b, s_m), axis=-1, keepdims=True), sink)
        p_b = jnp.exp(s_b - m)
        p_m = jnp.exp(s_m - m)
        den = jnp.sum(p_b + p_m, axis=-1, keepdims=True) + jnp.exp(sink - m)
        r = (jnp.dot(jnp.where(in_cur, p_b, 0.0).astype(BF16), kvc_ref[:, vs], preferred_element_type=F32)
             + jnp.dot(jnp.where(in_cur, 0.0, p_b).astype(BF16), kvp_ref[:, vs], preferred_element_type=F32)
             + jnp.dot(p_m.astype(BF16), kvm_ref[:, vs], preferred_element_type=F32))
        r = r / den
        for j in range(GROUP // 2):
            outs.append(jnp.where(low, r[2 * j * BLOCK:(2 * j + 1) * BLOCK], r[(2 * j + 1) * BLOCK:(2 * j + 2) * BLOCK]))
    o_all = jnp.concatenate(outs, axis=1)
    o_ref[...] = _rms(o_all, g_ref[...]).astype(BF16)


def _attention(sinks, q, kv, kvm, g_attn, batch, seq):
    nb = seq // BLOCK
    return pl.pallas_call(
        _attn_kernel,
        grid_spec=pltpu.PrefetchScalarGridSpec(
            num_scalar_prefetch=1,
            grid=(batch, nb),
            in_specs=[
                pl.BlockSpec((BLOCK, D_ATTN), lambda b, n, s: (b * nb + n, 0)),
                pl.BlockSpec((BLOCK, KV_COLS), lambda b, n, s: (b * nb + n, 0)),
                pl.BlockSpec((BLOCK, KV_COLS), lambda b, n, s: (b * nb + jnp.maximum(n - 1, 0), 0)),
                pl.BlockSpec((BLOCK, KV_COLS), lambda b, n, s: (0, 0)),
                pl.BlockSpec((1, D_ATTN), lambda b, n, s: (0, 0)),
            ],
            out_specs=pl.BlockSpec((BLOCK, D_ATTN), lambda b, n, s: (b * nb + n, 0)),
        ),
        out_shape=jax.ShapeDtypeStruct((batch * seq, D_ATTN), BF16),
        compiler_params=_cparams(("arbitrary", "arbitrary")),
        name="attention",
    )(sinks, q, kv, kv, kvm, g_attn)


def _expm1(y):
    p = 1.0 + y * (1.0 / 8.0)
    for k in range(7, 1, -1):
        p = 1.0 + y * p * (1.0 / k)
    return jnp.where(y > -0.25, y * p, jnp.exp(y) - 1.0)


def _lru_kernel(xg_ref, cw_ref, cb_ref, wa_ref, ba_ref, wi_ref, bi_ref, lam_ref, g_ref, h0_ref, tail0_ref,
                o_ref, hout_ref, tailout_ref, ext_ref, a_ref, b_ref, h_ref):
    tt = pl.program_id(1)
    rows = a_ref.shape[0]

    @pl.when(tt == 0)
    def _():
        h_ref[...] = h0_ref[...]
        ext_ref[0:8, :] = tail0_ref[...]

    ext_ref[8:, :] = xg_ref[:, :D_LRU]
    xc = cb_ref[...] + sum(cw_ref[j:j + 1, :] * ext_ref[5 + j:5 + j + rows, :] for j in range(CONV_WIDTH))
    xcb = xc.astype(BF16)
    sp = jax.nn.softplus(-lam_ref[...])
    for c in range(D_LRU // LRU_GROUP):
        cs = slice(c * LRU_GROUP, (c + 1) * LRU_GROUP)
        r = jax.nn.sigmoid(jnp.dot(xcb[:, cs], wa_ref[c], preferred_element_type=F32) + ba_ref[:, cs])
        i = jax.nn.sigmoid(jnp.dot(xcb[:, cs], wi_ref[c], preferred_element_type=F32) + bi_ref[:, cs])
        log_a = -LRU_C * r * sp[:, cs]
        a_ref[:, cs] = jnp.exp(log_a)
        b_ref[:, cs] = jnp.sqrt(-_expm1(2.0 * log_a)) * i * xc[:, cs]

    def step(t, h):
        h = a_ref[pl.ds(t, 1), :] * h + b_ref[pl.ds(t, 1), :]
        b_ref[pl.ds(t, 1), :] = h
        return h

    h_last = lax.fori_loop(0, rows, step, h_ref[...])
    h_ref[...] = h_last
    hout_ref[...] = h_last
    tail = ext_ref[rows:rows + 8, :]
    ext_ref[0:8, :] = tail
    tailout_ref[...] = tail
    y = b_ref[...] * jax.nn.gelu(xg_ref[:, D_LRU:])
    o_ref[...] = _rms(y, g_ref[...]).astype(BF16)


def _lru(xg, conv_w, conv_b, wa_bd, b_a, wi_bd, b_i, lam, g_lru, h0, tail0, batch, seq, tt):
    ntt = seq // tt
    vec = pl.BlockSpec((1, D_LRU), lambda b, t: (0, 0))
    wspec = pl.BlockSpec((D_LRU // LRU_GROUP, LRU_GROUP, LRU_GROUP), lambda b, t: (0, 0, 0))
    return pl.pallas_call(
        _lru_kernel,
        grid=(batch, ntt),
        in_specs=[
            pl.BlockSpec((tt, 2 * D_LRU), lambda b, t: (b * ntt + t, 0)),
            pl.BlockSpec((CONV_WIDTH, D_LRU), lambda b, t: (0, 0)),
            vec, wspec, vec, wspec, vec, vec, vec, vec,
            pl.BlockSpec((8, D_LRU), lambda b, t: (0, 0)),
        ],
        out_specs=[
            pl.BlockSpec((tt, D_LRU), lambda b, t: (b * ntt + t, 0)),
            pl.BlockSpec((1, D_LRU), lambda b, t: (0, 0)),
            pl.BlockSpec((8, D_LRU), lambda b, t: (0, 0)),
        ],
        out_shape=[
            jax.ShapeDtypeStruct((batch * seq, D_LRU), BF16),
            jax.ShapeDtypeStruct((1, D_LRU), F32),
            jax.ShapeDtypeStruct((8, D_LRU), F32),
        ],
        scratch_shapes=[
            pltpu.VMEM((8 + tt, D_LRU), F32),
            pltpu.VMEM((tt, D_LRU), F32),
            pltpu.VMEM((tt, D_LRU), F32),
            pltpu.VMEM((1, D_LRU), F32),
        ],
        compiler_params=_cparams(("arbitrary", "arbitrary")),
        name="rglru",
    )(xg, conv_w, conv_b, wa_bd, b_a, wi_bd, b_i, lam, g_lru, h0, tail0)


def _outproj_kernel(a_ref, l_ref, wo_ref, bo_ref, x_ref, gf_ref, wr_ref, br_ref, tri_ref,
                    h_ref, xp_ref, ids_ref, wts_ref, rank_ref, cnt_ref, carry_ref):
    m = pl.program_id(0)
    tm = a_ref.shape[0]

    @pl.when(m == 0)
    def _():
        carry_ref[...] = jnp.zeros_like(carry_ref)

    h = (jnp.dot(a_ref[...], wo_ref[:D_ATTN, :], preferred_element_type=F32)
         + jnp.dot(l_ref[...], wo_ref[D_ATTN:, :], preferred_element_type=F32)
         + bo_ref[...] + x_ref[...])
    h_ref[...] = h
    xn = _rms(h, gf_ref[...]).astype(BF16)
    half = xn.shape[1] // 2
    lo = pltpu.bitcast(xn[:, :half].astype(F32), jnp.uint32)
    hi = pltpu.bitcast(xn[:, half:].astype(F32), jnp.uint32)
    packed = (hi & jnp.uint32(0xFFFF0000)) | (lo >> 16)
    ns = half // LANES
    for s in range(ns):
        xp_ref[pl.ds(s, tm, stride=ns), :] = packed[:, s * LANES:(s + 1) * LANES]

    logits = lax.dot_general(wr_ref[...], xn, (((1,), (1,)), ((), ())), preferred_element_type=F32)
    logits = logits + br_ref[...]
    eidx = lax.broadcasted_iota(jnp.int32, (N_EXPERTS, tm), 0)
    work = logits
    vals, sels = [], []
    for k in range(TOP_K):
        v = jnp.max(work, axis=0, keepdims=True)
        idx = jnp.min(jnp.where(work == v, eidx, N_EXPERTS), axis=0, keepdims=True)
        sel = eidx == idx
        ids_ref[k:k + 1, :] = idx
        vals.append(v)
        sels.append(sel)
        work = jnp.where(sel, -jnp.inf, work)
    es = [jnp.exp(v - vals[0]) for v in vals]
    den = es[0] + es[1] + es[2] + es[3]
    for k in range(TOP_K):
        wts_ref[k:k + 1, :] = es[k] / den
    cnt = sum(s.astype(F32) for s in sels)
    incl = jnp.dot(cnt.astype(BF16), tri_ref[...], preferred_element_type=F32)
    before = incl - cnt + carry_ref[:, 0:1]
    for k in range(TOP_K):
        rk = jnp.sum(jnp.where(sels[k], before, 0.0), axis=0, keepdims=True)
        rank_ref[k:k + 1, :] = rk.astype(jnp.int32)
    carry_ref[...] = carry_ref[...] + incl[:, tm - 1:tm]
    cnt_ref[...] = carry_ref[...]


def _outproj(attn_n, lru_n, wo_bf, b_out, x2d, g_ffn, wr_t, br, tri, tm):
    rows, d = x2d.shape
    const = lambda shape: pl.BlockSpec(shape, lambda m: tuple(0 for _ in shape))
    return pl.pallas_call(
        _outproj_kernel,
        grid=(rows // tm,),
        in_specs=[
            pl.BlockSpec((tm, D_ATTN), lambda m: (m, 0)),
            pl.BlockSpec((tm, D_LRU), lambda m: (m, 0)),
            const((D_ATTN + D_LRU, d)), const((1, d)),
            pl.BlockSpec((tm, d), lambda m: (m, 0)),
            const((1, d)), const((N_EXPERTS, d)), const((N_EXPERTS, 1)), const((tm, tm)),
        ],
        out_specs=[
            pl.BlockSpec((tm, d), lambda m: (m, 0)),
            pl.BlockSpec((tm * (d // 2 // LANES), LANES), lambda m: (m, 0)),
            pl.BlockSpec((TOP_K, tm), lambda m: (0, m)),
            pl.BlockSpec((TOP_K, tm), lambda m: (0, m)),
            pl.BlockSpec((TOP_K, tm), lambda m: (0, m)),
            const((N_EXPERTS, LANES)),
        ],
        out_shape=[
            jax.ShapeDtypeStruct((rows, d), F32),
            jax.ShapeDtypeStruct((rows * (d // 2 // LANES), LANES), jnp.uint32),
            jax.ShapeDtypeStruct((TOP_K, rows), jnp.int32),
            jax.ShapeDtypeStruct((TOP_K, rows), F32),
            jax.ShapeDtypeStruct((TOP_K, rows), jnp.int32),
            jax.ShapeDtypeStruct((N_EXPERTS, LANES), F32),
        ],
        scratch_shapes=[pltpu.VMEM((N_EXPERTS, LANES), F32)],
        compiler_params=_cparams(("arbitrary",)),
        name="outproj_router",
    )(attn_n, lru_n, wo_bf, b_out, x2d, g_ffn, wr_t, br, tri)


def _zero_fill_rows(zero_ref, dst_hbm, first_row, n_blocks, sem):
    blk = zero_ref.shape[0]
    per = blk // SUB_ROWS

    def copy(j):
        row = pl.multiple_of((first_row + j * SUB_ROWS) * per, blk)
        return pltpu.make_async_copy(zero_ref, dst_hbm.at[pl.ds(row, blk)], sem)

    def start(j, c):
        copy(j).start()
        return c

    def wait(j, c):
        copy(j).wait()
        return c

    lax.fori_loop(0, n_blocks, start, 0)
    lax.fori_loop(0, n_blocks, wait, 0)


def _dispatch_kernel(tails_ref, pos_hbm, xp_ref, xs_hbm, pos_smem, zero_ref, sem, psem):
    i = pl.program_id(0)
    ns = TOK_WORDS // LANES
    tm = xp_ref.shape[0] // ns
    n = TOP_K * tm
    pcopy = pltpu.make_async_copy(pos_hbm.at[pl.ds(i * n, n)], pos_smem, psem)
    pcopy.start()

    @pl.when(i == 0)
    def _():
        zero_ref[...] = jnp.zeros_like(zero_ref)
        for e in range(N_EXPERTS):
            _zero_fill_rows(zero_ref, xs_hbm, tails_ref[e], 1, sem)
        _zero_fill_rows(zero_ref, xs_hbm, tails_ref[N_EXPERTS], tails_ref[N_EXPERTS + 1], sem)

    pcopy.wait()

    for k in range(TOP_K):
        def issue(tt, c):
            for u in range(ROW_DMA_UNROLL):
                t = tt * ROW_DMA_UNROLL + u
                src = xp_ref.at[pl.ds(pl.multiple_of(t * ns, ns), ns)]
                dst = xs_hbm.at[pl.ds(pl.multiple_of(pos_smem[k * tm + t] * ns, ns), ns)]
                pltpu.make_async_copy(src, dst, sem).start(priority=u % 2)
            return c

        lax.fori_loop(0, tm // ROW_DMA_UNROLL, issue, 0)
    for _ in range(TOP_K):
        pltpu.make_async_copy(xp_ref, xs_hbm.at[pl.ds(0, tm * ns)], sem).wait()


def _dispatch(tails, pos_tiles, xp, p_alloc, tm):
    ns = TOK_WORDS // LANES
    rows = xp.shape[0] // ns
    return pl.pallas_call(
        _dispatch_kernel,
        grid_spec=pltpu.PrefetchScalarGridSpec(
            num_scalar_prefetch=1,
            grid=(rows // tm,),
            in_specs=[
                pl.BlockSpec(memory_space=pl.ANY),
                pl.BlockSpec((tm * ns, LANES), lambda i, s: (i, 0)),
            ],
            out_specs=pl.BlockSpec(memory_space=pl.ANY),
            scratch_shapes=[
                pltpu.SMEM((TOP_K * tm,), jnp.int32),
                pltpu.VMEM((SUB_ROWS * ns, LANES), jnp.uint32),
                pltpu.SemaphoreType.DMA,
                pltpu.SemaphoreType.DMA,
            ],
        ),
        out_shape=jax.ShapeDtypeStruct((p_alloc * ns, LANES), jnp.uint32),
        compiler_params=_cparams(("arbitrary",)),
        name="dispatch",
    )(tails, pos_tiles, xp)


def _chunk_dma(cs_ref, cn_ref, src_hbm, dst_ref, per, sem):
    blk = SUB_ROWS * per
    nch = cs_ref.shape[0]

    def copy(ci, j):
        src = src_hbm.at[pl.ds(pl.multiple_of((cs_ref[ci] + j * SUB_ROWS) * per, blk), blk)]
        return pltpu.make_async_copy(src, dst_ref.at[pl.ds(pl.multiple_of(j * blk, blk), blk)], sem)

    def run(ci, wait):
        cc = jnp.minimum(ci, nch - 1)
        n = jnp.where(ci < nch, cn_ref[cc], 0)

        def body(j, c):
            if wait:
                copy(cc, j).wait()
            else:
                copy(cc, j).start()
            return c

        lax.fori_loop(0, n, body, 0)

    return (lambda ci: run(ci, False)), (lambda ci: run(ci, True))


def _tile_loop(nsub, tile_fn):
    n4 = nsub // 4
    rem = nsub % 4
    base = n4 * 4 * SUB_ROWS

    @pl.when(rem >= 2)
    def _():
        tile_fn(pl.multiple_of(base, SUB_ROWS), 2 * SUB_ROWS, 0, 1)

    @pl.when(rem % 2 == 1)
    def _():
        tile_fn(pl.multiple_of(base + (rem // 2) * 2 * SUB_ROWS, SUB_ROWS), SUB_ROWS, 0, 2)

    def full(j, c):
        tile_fn(pl.multiple_of(j * 4 * SUB_ROWS, 4 * SUB_ROWS), 4 * SUB_ROWS, j, 0)
        return c

    lax.fori_loop(0, n4, full, 0)


def _moe_up_kernel(ce_ref, cs_ref, cn_ref, tail_ref, xs_hbm, wg_ref, bg_ref, wu_ref, bu_ref, act_hbm,
                   land_ref, xb_ref, wgu_ref, ab_ref, zero_ref, pend_ref, sem, osem):
    c = pl.program_id(0)
    f = pl.program_id(1)
    nsub = cn_ref[c]
    start = pl.multiple_of(cs_ref[c], SUB_ROWS)
    tf = wg_ref.shape[2]
    ns = TOK_WORDS // LANES
    start_chunk, wait_chunk = _chunk_dma(cs_ref, cn_ref, xs_hbm, land_ref, ns, sem)

    @pl.when(jnp.logical_and(c == 0, f == 0))
    def _():
        zero_ref[...] = jnp.zeros_like(zero_ref)
        _zero_fill_rows(zero_ref, act_hbm, tail_ref[0], tail_ref[1], osem.at[0])
        for slot in range(4):
            pend_ref[slot] = 0
        start_chunk(c)

    @pl.when(jnp.logical_and(f == 0, nsub > 0))
    def _():
        wait_chunk(c)

        def unpack(j, carry):
            r = pl.multiple_of(j * SUB_ROWS, SUB_ROWS)
            for s in range(ns):
                w = land_ref[pl.ds(r * ns + s, SUB_ROWS, stride=ns), :]
                xb_ref[pl.ds(r, SUB_ROWS), s * LANES:(s + 1) * LANES] = pltpu.bitcast(w << 16, F32).astype(BF16)
                xb_ref[pl.ds(r, SUB_ROWS), TOK_WORDS + s * LANES:TOK_WORDS + (s + 1) * LANES] = (
                    pltpu.bitcast(w & jnp.uint32(0xFFFF0000), F32).astype(BF16))
            return carry

        lax.fori_loop(0, nsub, unpack, 0)
        start_chunk(c + 1)

    def out_copy(r, rows, slot):
        return pltpu.make_async_copy(
            ab_ref.at[slot, pl.ds(0, rows)],
            act_hbm.at[pl.ds(pl.multiple_of(start + r, SUB_ROWS), rows), pl.ds(pl.multiple_of(f * tf, tf), tf)],
            osem.at[slot])

    @pl.when(nsub > 0)
    def _():
        wgu_ref[:, :tf] = wg_ref[0].astype(BF16)
        wgu_ref[:, tf:] = wu_ref[0].astype(BF16)
        bg = bg_ref[0]
        bu = bu_ref[0]

        def tile(r, rows, j, kind):
            z = jnp.dot(xb_ref[pl.ds(r, rows), :], wgu_ref[...], preferred_element_type=F32)
            g = z[:, :tf] + bg
            u = z[:, tf:] + bu
            g = jnp.minimum(g, SWIGLU_LIMIT)
            u = jnp.clip(u, -SWIGLU_LIMIT, SWIGLU_LIMIT)
            a = (g * jax.nn.sigmoid(SWIGLU_ALPHA * g) * (u + 1.0)).astype(BF16)
            slot = j % 2 if kind == 0 else 1 + kind

            @pl.when(pend_ref[slot] == 1)
            def _():
                out_copy(r, rows, slot).wait()

            ab_ref[slot, pl.ds(0, rows), :] = a
            out_copy(r, rows, slot).start()
            pend_ref[slot] = 1

        _tile_loop(nsub, tile)

    @pl.when(jnp.logical_and(c == pl.num_programs(0) - 1, f == pl.num_programs(1) - 1))
    def _():
        for slot, rows in enumerate((4 * SUB_ROWS, 4 * SUB_ROWS, 2 * SUB_ROWS, SUB_ROWS)):
            @pl.when(pend_ref[slot] == 1)
            def _():
                out_copy(0, rows, slot).wait()


def _moe_up(ch_e, ch_start, ch_nsub, tail, xs, w_gate, b_gate, w_up, b_up):
    ns = TOK_WORDS // LANES
    p_alloc = xs.shape[0] // ns
    d = 2 * TOK_WORDS
    dff = w_gate.shape[2]
    tf = MOE_TF
    nch = ch_e.shape[0]
    nf = dff // tf
    wmap = lambda c, f, e, s, n, t: (e[c], 0, jnp.where(n[c] > 0, f, nf - 1))
    return pl.pallas_call(
        _moe_up_kernel,
        grid_spec=pltpu.PrefetchScalarGridSpec(
            num_scalar_prefetch=4,
            grid=(nch, nf),
            in_specs=[
                pl.BlockSpec(memory_space=pl.ANY),
                pl.BlockSpec((1, d, tf), wmap),
                pl.BlockSpec((1, 1, tf), wmap),
                pl.BlockSpec((1, d, tf), wmap),
                pl.BlockSpec((1, 1, tf), wmap),
            ],
            out_specs=pl.BlockSpec(memory_space=pl.ANY),
            scratch_shapes=[
                pltpu.VMEM((MOE_TM * ns, LANES), jnp.uint32),
                pltpu.VMEM((MOE_TM, d), BF16),
                pltpu.VMEM((d, 2 * tf), BF16),
                pltpu.VMEM((4, 4 * SUB_ROWS, tf), BF16),
                pltpu.VMEM((SUB_ROWS, dff), BF16),
                pltpu.SMEM((4,), jnp.int32),
                pltpu.SemaphoreType.DMA,
                pltpu.SemaphoreType.DMA((4,)),
            ],
        ),
        out_shape=jax.ShapeDtypeStruct((p_alloc, dff), BF16),
        compiler_params=_cparams(("arbitrary", "arbitrary")),
        name="moe_up",
    )(ch_e, ch_start, ch_nsub, tail, xs, w_gate, b_gate.reshape(N_EXPERTS, 1, dff), w_up,
      b_up.reshape(N_EXPERTS, 1, dff))


def _moe_down_kernel(ce_ref, cs_ref, cn_ref, tail_ref, act_hbm, wd_ref, bd_ref, ys_hbm,
                     xa_ref, xb_ref, wdb_ref, yb_ref, zero_ref, pend_ref, sem, osem):
    c = pl.program_id(0)
    f = pl.program_id(1)
    nf = pl.num_programs(1)
    nsub = cn_ref[c]
    start = pl.multiple_of(cs_ref[c], SUB_ROWS)
    tn = wd_ref.shape[2]
    ns = TOK_WORDS // LANES
    nw = tn // 2 // LANES
    copies = [_chunk_dma(cs_ref, cn_ref, act_hbm, ref, 1, sem.at[i]) for i, ref in enumerate((xa_ref, xb_ref))]

    def on_parity(ci, fn):
        for p in range(2):
            @pl.when(ci % 2 == p)
            def _():
                fn(p)

    @pl.when(jnp.logical_and(c == 0, f == 0))
    def _():
        zero_ref[...] = jnp.zeros_like(zero_ref)
        _zero_fill_rows(zero_ref, ys_hbm, tail_ref[0], tail_ref[1], osem)
        for kind in range(3):
            pend_ref[kind] = 0
        copies[0][0](c)

    @pl.when(jnp.logical_and(f == 0, nsub > 0))
    def _():
        on_parity(c, lambda p: copies[p][1](c))
        on_parity(c + 1, lambda p: copies[p][0](c + 1))

    def out_copy(r, rows):
        src = yb_ref.at[pl.ds(pl.multiple_of(r * ns, SUB_ROWS * ns), rows * ns)]
        dst = ys_hbm.at[pl.ds(pl.multiple_of((start + r) * ns, SUB_ROWS * ns), rows * ns)]
        return pltpu.make_async_copy(src, dst, osem)

    def drain():
        for kind, rows in enumerate((4 * SUB_ROWS, 2 * SUB_ROWS, SUB_ROWS)):
            def wait(j, carry):
                out_copy(0, rows).wait()
                return carry

            lax.fori_loop(0, pend_ref[kind], wait, 0)
            pend_ref[kind] = 0

    def compute(x_ref):
        wdb_ref[...] = wd_ref[0].astype(BF16)
        bd = bd_ref[0]
        last = f == nf - 1

        @pl.when(f == 0)
        def _():
            drain()

        def tile(r, rows, j, kind):
            y = jnp.dot(x_ref[pl.ds(r, rows), :], wdb_ref[...], preferred_element_type=F32) + bd
            lo = pltpu.bitcast(y[:, :tn // 2].astype(BF16).astype(F32), jnp.uint32)
            hi = pltpu.bitcast(y[:, tn // 2:].astype(BF16).astype(F32), jnp.uint32)
            packed = (hi & jnp.uint32(0xFFFF0000)) | (lo >> 16)

            for s in range(nw):
                yb_ref[pl.ds(r * ns + f * nw + s, rows, stride=ns), :] = packed[:, s * LANES:(s + 1) * LANES]

            @pl.when(last)
            def _():
                out_copy(r, rows).start()
                pend_ref[kind] = pend_ref[kind] + 1

        _tile_loop(nsub, tile)

    @pl.when(nsub > 0)
    def _():
        on_parity(c, lambda p: compute((xa_ref, xb_ref)[p]))

    @pl.when(jnp.logical_and(c == pl.num_programs(0) - 1, f == nf - 1))
    def _():
        drain()


def _moe_down(ch_e, ch_start, ch_nsub, tail, act, w_down, b_down):
    p_alloc, dff = act.shape
    d = w_down.shape[2]
    ns = TOK_WORDS // LANES
    tn = MOE_TN
    nch = ch_e.shape[0]
    nn = d // tn
    wmap = lambda c, f, e, s, n, t: (e[c], 0, jnp.where(n[c] > 0, f, nn - 1))
    return pl.pallas_call(
        _moe_down_kernel,
        grid_spec=pltpu.PrefetchScalarGridSpec(
            num_scalar_prefetch=4,
            grid=(nch, nn),
            in_specs=[
                pl.BlockSpec(memory_space=pl.ANY),
                pl.BlockSpec((1, dff, tn), wmap),
                pl.BlockSpec((1, 1, tn), wmap),
            ],
            out_specs=pl.BlockSpec(memory_space=pl.ANY),
            scratch_shapes=[
                pltpu.VMEM((MOE_TM, dff), BF16),
                pltpu.VMEM((MOE_TM, dff), BF16),
                pltpu.VMEM((dff, tn), BF16),
                pltpu.VMEM((MOE_TM * ns, LANES), jnp.uint32),
                pltpu.VMEM((SUB_ROWS * ns, LANES), jnp.uint32),
                pltpu.SMEM((3,), jnp.int32),
                pltpu.SemaphoreType.DMA((2,)),
                pltpu.SemaphoreType.DMA,
            ],
        ),
        out_shape=jax.ShapeDtypeStruct((p_alloc * ns, LANES), jnp.uint32),
        compiler_params=_cparams(("arbitrary", "arbitrary")),
        name="moe_down",
    )(ch_e, ch_start, ch_nsub, tail, act, w_down, b_down.reshape(N_EXPERTS, 1, d))


def _combine_kernel(pos_hbm, ys_hbm, h_ref, w_ref, g_ref, o_ref, pos_smem, yb_ref, sem, psem):
    i = pl.program_id(0)
    tm, d = h_ref.shape
    n = TOP_K * tm
    ns = TOK_WORDS // LANES

    def fetch(tile, slot):
        pcopy = pltpu.make_async_copy(pos_hbm.at[pl.ds(tile * n, n)], pos_smem.at[pl.ds(slot * n, n)], psem)
        pcopy.start()
        pcopy.wait()

        def issue(jj, c):
            for u in range(ROW_DMA_UNROLL):
                j = jj * ROW_DMA_UNROLL + u
                src = ys_hbm.at[pl.ds(pl.multiple_of(pos_smem[slot * n + j] * ns, ns), ns)]
                dst = yb_ref.at[slot, pl.ds(pl.multiple_of(j * ns, ns), ns)]
                pltpu.make_async_copy(src, dst, sem.at[slot]).start(priority=u % 2)
            return c

        lax.fori_loop(0, n // ROW_DMA_UNROLL, issue, 0)

    @pl.when(i == 0)
    def _():
        fetch(i, 0)

    @pl.when(i + 1 < pl.num_programs(0))
    def _():
        for p in range(2):
            @pl.when((i + 1) % 2 == p)
            def _():
                fetch(i + 1, p)

    def reduce(slot):
        pltpu.make_async_copy(ys_hbm.at[pl.ds(0, n * ns)], yb_ref.at[slot], sem.at[slot]).wait()
        wb = [jnp.broadcast_to(w_ref[:, k:k + 1], (tm, LANES)) for k in range(TOP_K)]
        ssq = jnp.zeros((tm, 1), F32)
        nw = MOE_TN // 2 // LANES
        for s in range(ns):
            c0 = (s // nw) * MOE_TN + (s % nw) * LANES
            lo_cols = slice(c0, c0 + LANES)
            hi_cols = slice(c0 + MOE_TN // 2, c0 + MOE_TN // 2 + LANES)
            acc_lo = h_ref[:, lo_cols]
            acc_hi = h_ref[:, hi_cols]
            for k in range(TOP_K):
                w = yb_ref[slot, pl.ds(k * tm * ns + s, tm, stride=ns), :]
                acc_lo = acc_lo + wb[k] * pltpu.bitcast(w << 16, F32)
                acc_hi = acc_hi + wb[k] * pltpu.bitcast(w & jnp.uint32(0xFFFF0000), F32)
            o_ref[:, lo_cols] = acc_lo
            o_ref[:, hi_cols] = acc_hi
            ssq = ssq + jnp.sum(acc_lo * acc_lo + acc_hi * acc_hi, axis=-1, keepdims=True)
        o_ref[...] = o_ref[...] * lax.rsqrt(ssq * (1.0 / d) + NORM_EPS) * g_ref[...]

    for p in range(2):
        @pl.when(i % 2 == p)
        def _():
            reduce(p)


def _combine(pos_tiles, ys, h1, wts_t, g_final, tm):
    rows, d = h1.shape
    ns = TOK_WORDS // LANES
    return pl.pallas_call(
        _combine_kernel,
        grid=(rows // tm,),
        in_specs=[
            pl.BlockSpec(memory_space=pl.ANY),
            pl.BlockSpec(memory_space=pl.ANY),
            pl.BlockSpec((tm, d), lambda i: (i, 0)),
            pl.BlockSpec((tm, TOP_K), lambda i: (i, 0)),
            pl.BlockSpec((1, d), lambda i: (0, 0)),
        ],
        out_specs=pl.BlockSpec((tm, d), lambda i: (i, 0)),
        out_shape=jax.ShapeDtypeStruct((rows, d), F32),
        scratch_shapes=[
            pltpu.SMEM((2 * TOP_K * tm,), jnp.int32),
            pltpu.VMEM((2, TOP_K * tm * ns, LANES), jnp.uint32),
            pltpu.SemaphoreType.DMA((2,)),
            pltpu.SemaphoreType.DMA,
        ],
        compiler_params=_cparams(("arbitrary",)),
        name="combine",
    )(pos_tiles, ys, h1, wts_t, g_final)


def _rope_tables(n_pos):
    half = HEAD_DIM // 2
    inv = 1.0 / (ROPE_THETA ** (jnp.arange(half, dtype=F32) / half))
    ang = jnp.arange(n_pos, dtype=F32)[:, None] * inv[None, :]
    cos = jnp.tile(jnp.cos(ang), (1, LANES // half))
    sin = jnp.tile(jnp.concatenate([-jnp.sin(ang), jnp.sin(ang)], axis=1), (1, LANES // HEAD_DIM))
    return cos, sin


def _block_diag(w):
    per = LRU_GROUP // LRU_BLOCK
    w4 = w.reshape(D_LRU // LRU_GROUP, per, LRU_BLOCK, LRU_BLOCK)
    eye = jnp.eye(per, dtype=w.dtype)
    bd = jnp.einsum("gpcd,pq->gpcqd", w4, eye)
    return bd.reshape(D_LRU // LRU_GROUP, LRU_GROUP, LRU_GROUP).astype(BF16)


def _tile_positions(pos, tm):
    rows = pos.shape[1]
    return pos.reshape(TOP_K, rows // tm, tm).transpose(1, 0, 2).reshape(-1)


def _chunk_schedule(counts, n_rows):
    aligned = ((counts + SUB_ROWS - 1) // SUB_ROWS) * SUB_ROWS
    offs = jnp.cumsum(aligned) - aligned
    n_ch = (aligned + MOE_TM - 1) // MOE_TM
    cum = jnp.cumsum(n_ch)
    total = cum[-1]
    nch_max = N_EXPERTS + (n_rows + N_EXPERTS * SUB_ROWS) // MOE_TM
    c = jnp.arange(nch_max, dtype=jnp.int32)
    cc = jnp.minimum(c, total - 1)
    e = jnp.searchsorted(cum, cc, side="right").astype(jnp.int32)
    j = cc - (cum[e] - n_ch[e])
    start = offs[e] + j * MOE_TM
    nsub = jnp.where(c < total, jnp.minimum(MOE_TM, aligned[e] - j * MOE_TM) // SUB_ROWS, 0)
    tails = offs + (counts // SUB_ROWS) * SUB_ROWS
    used = jnp.sum(aligned)
    tail = jnp.stack([used, (n_rows + N_EXPERTS * SUB_ROWS - used) // SUB_ROWS])
    tails = jnp.concatenate([tails, tail])
    return (offs, e, start.astype(jnp.int32), nsub.astype(jnp.int32), tails.astype(jnp.int32),
            tail.astype(jnp.int32))


def kernel(x, meta_tokens, norm_mix, w_in, b_in, sinks, conv_w, conv_b, w_a, b_a, w_i, b_i, lru_lambda,
           g_attn_out, g_lru_out, w_out, b_out, norm_ffn, w_router, b_router, w_gate, b_gate, w_up, b_up,
           w_down, b_down, final_norm):
    batch, seq, d = x.shape
    rows = batch * seq
    x2d = x.reshape(rows, d)
    row = lambda v: v.reshape(1, -1)

    w_in_bf = w_in[0].astype(BF16)
    w_out_bf = w_out[0].astype(BF16)
    cos, sin = _rope_tables(N_META + seq)
    wa_bd, wi_bd = _block_diag(w_a[0]), _block_diag(w_i[0])
    lru_args = (conv_w[0], row(conv_b[0]), wa_bd, row(b_a[0]), wi_bd, row(b_i[0]), row(lru_lambda[0]),
                row(g_lru_out[0]))

    _, kvm, xgm = _inproj(meta_tokens, row(norm_mix[0]), w_in_bf, row(b_in[0]), cos[:N_META], sin[:N_META],
                          N_META, 1)
    _, h0, tail0 = _lru(xgm, *lru_args, jnp.zeros((1, D_LRU), F32), jnp.zeros((8, D_LRU), F32), 1, N_META,
                        N_META)

    tm = 512
    q, kv, xg = _inproj(x2d, row(norm_mix[0]), w_in_bf, row(b_in[0]), cos[N_META:], sin[N_META:], tm,
                        seq // tm)
    kvm = jnp.pad(kvm, ((0, BLOCK - N_META), (0, 0)))
    attn_n = _attention(sinks[0], q, kv, kvm, row(g_attn_out[0]), batch, seq)
    lru_n, _, _ = _lru(xg, *lru_args, h0, tail0, batch, seq, 512)

    tri = (jnp.arange(tm)[:, None] <= jnp.arange(tm)[None, :]).astype(BF16)
    h1, xp, ids, wts, rank, cnt = _outproj(
        attn_n, lru_n, w_out_bf, row(b_out[0]), x2d, row(norm_ffn[0]), w_router[0].T.astype(BF16),
        b_router[0].reshape(N_EXPERTS, 1), tri, tm)

    counts = cnt[:, 0].astype(jnp.int32)
    n_rows = rows * TOP_K
    offs, ch_e, ch_start, ch_nsub, tails, tail = _chunk_schedule(counts, n_rows)
    onehot = ids[..., None] == jnp.arange(N_EXPERTS, dtype=jnp.int32)
    pos = jnp.sum(jnp.where(onehot, offs, 0), axis=-1) + rank
    p_alloc = n_rows + N_EXPERTS * SUB_ROWS

    tmd = 512
    xs = _dispatch(tails, _tile_positions(pos, tmd), xp, p_alloc, tmd)
    act = _moe_up(ch_e, ch_start, ch_nsub, tail, xs, w_gate[0], b_gate[0], w_up[0], b_up[0])
    ys = _moe_down(ch_e, ch_start, ch_nsub, tail, act, w_down[0], b_down[0])
    tmc = 256
    out = _combine(_tile_positions(pos, tmc), ys, h1, wts.T, row(final_norm), tmc)
    return out.reshape(batch, seq, d)
```

```python
import jax
import jax.numpy as jnp
from jax import lax
from jax.experimental import pallas as pl
from jax.experimental.pallas import tpu as pltpu

F32 = jnp.float32
BF16 = jnp.bfloat16

N_META = 16
HEAD_DIM = 64
N_Q_HEADS = 16
N_KV_HEADS = 4
GROUP = N_Q_HEADS // N_KV_HEADS
D_ATTN = N_Q_HEADS * HEAD_DIM
D_KV = N_KV_HEADS * HEAD_DIM
KV_COLS = 4 * D_KV
BLOCK = 128
ROPE_THETA = 10000.0
D_LRU = 1024
LRU_BLOCK = 64
LRU_GROUP = 256
CONV_WIDTH = 4
LRU_C = 8.0
N_EXPERTS = 32
TOP_K = 4
SWIGLU_LIMIT = 7.0
SWIGLU_ALPHA = 1.702
NORM_EPS = 1e-5
NEG_INF = -1e30

LANES = 128
D_MODEL = 2048
TOK_WORDS = D_MODEL // 2
SUB_ROWS = 128
MOE_TM = 2304
MOE_TF = 512
MOE_TN = 1024
ROW_DMA_UNROLL = 8
SLOT_ROWS = (4 * SUB_ROWS, 4 * SUB_ROWS, 2 * SUB_ROWS, SUB_ROWS)
VMEM_LIMIT = 56 * 1024 * 1024


def _cparams(sem):
    return pltpu.CompilerParams(dimension_semantics=sem, vmem_limit_bytes=VMEM_LIMIT)


def _rms(x, g):
    return x * lax.rsqrt(jnp.mean(x * x, axis=-1, keepdims=True) + NORM_EPS) * g


def _inproj_kernel(x_ref, g_ref, w_ref, b_ref, cos_ref, sin_ref, q_ref, kv_ref, xg_ref):
    xn = _rms(x_ref[...], g_ref[...]).astype(BF16)
    cos = cos_ref[...]
    sin = sin_ref[...]
    lane = lax.broadcasted_iota(jnp.int32, (1, LANES), 1)
    first_half = (lane % HEAD_DIM) < (HEAD_DIM // 2)

    def rope(z):
        partner = jnp.where(first_half, pltpu.roll(z, LANES - HEAD_DIM // 2, 1),
                            pltpu.roll(z, HEAD_DIM // 2, 1))
        return z * cos + partner * sin

    cw = 512
    for c in range(w_ref.shape[1] // cw):
        z = jnp.dot(xn, w_ref[:, c * cw:(c + 1) * cw], preferred_element_type=F32)
        z = z + b_ref[:, c * cw:(c + 1) * cw]
        if c < 2:
            for j in range(cw // LANES):
                zz = rope(z[:, j * LANES:(j + 1) * LANES]) * (HEAD_DIM ** -0.5)
                q_ref[:, c * cw + j * LANES:c * cw + (j + 1) * LANES] = zz.astype(BF16)
        elif c == 2:
            low = lane < HEAD_DIM
            for j in range(2 * D_KV // LANES):
                zz = z[:, j * LANES:(j + 1) * LANES]
                zz = rope(zz) if j < D_KV // LANES else zz
                sw = pltpu.roll(zz, HEAD_DIM, 1)
                kv_ref[:, 2 * j * LANES:(2 * j + 1) * LANES] = jnp.where(low, zz, sw).astype(BF16)
                kv_ref[:, (2 * j + 1) * LANES:(2 * j + 2) * LANES] = jnp.where(low, sw, zz).astype(BF16)
        else:
            xg_ref[:, (c - 3) * cw:(c - 2) * cw] = z


def _inproj(x2d, g, w_bf, b, cos, sin, tm, pos_blocks):
    rows, d = x2d.shape
    dz = w_bf.shape[1]
    return pl.pallas_call(
        _inproj_kernel,
        grid=(rows // tm,),
        in_specs=[
            pl.BlockSpec((tm, d), lambda m: (m, 0)),
            pl.BlockSpec((1, d), lambda m: (0, 0)),
            pl.BlockSpec((d, dz), lambda m: (0, 0)),
            pl.BlockSpec((1, dz), lambda m: (0, 0)),
            pl.BlockSpec((tm, LANES), lambda m: (m % pos_blocks, 0)),
            pl.BlockSpec((tm, LANES), lambda m: (m % pos_blocks, 0)),
        ],
        out_specs=[
            pl.BlockSpec((tm, D_ATTN), lambda m: (m, 0)),
            pl.BlockSpec((tm, KV_COLS), lambda m: (m, 0)),
            pl.BlockSpec((tm, 2 * D_LRU), lambda m: (m, 0)),
        ],
        out_shape=[
            jax.ShapeDtypeStruct((rows, D_ATTN), BF16),
            jax.ShapeDtypeStruct((rows, KV_COLS), BF16),
            jax.ShapeDtypeStruct((rows, 2 * D_LRU), F32),
        ],
        compiler_params=_cparams(("arbitrary",)),
        name="inproj",
    )(x2d, g, w_bf, b, cos, sin)


def _attn_kernel(sink_ref, q_ref, kvc_ref, kvp_ref, kvm_ref, g_ref, o_ref):
    n = pl.program_id(1)
    row = lax.broadcasted_iota(jnp.int32, (GROUP * BLOCK, BLOCK), 0) % BLOCK
    col = lax.broadcasted_iota(jnp.int32, (GROUP * BLOCK, BLOCK), 1)
    in_cur = col <= row
    in_prev = jnp.logical_and(col > row, n > 0)
    is_meta = col < N_META
    low = lax.broadcasted_iota(jnp.int32, (BLOCK, LANES), 1) < HEAD_DIM
    nt = (((1,), (1,)), ((), ()))
    zero = jnp.zeros((), BF16)
    outs = []
    for h in range(N_KV_HEADS):
        ks = slice(h * LANES, (h + 1) * LANES)
        vs = slice(N_KV_HEADS * LANES + h * LANES, N_KV_HEADS * LANES + (h + 1) * LANES)
        parts = []
        for j in range(GROUP // 2):
            qg = q_ref[:, (h * GROUP // 2 + j) * LANES:(h * GROUP // 2 + j + 1) * LANES]
            parts += [jnp.where(low, qg, zero), jnp.where(low, zero, qg)]
        qs = jnp.concatenate(parts, axis=0)
        s_c = lax.dot_general(qs, kvc_ref[:, ks], nt, preferred_element_type=F32)
        s_p = lax.dot_general(qs, kvp_ref[:, ks], nt, preferred_element_type=F32)
        s_m = lax.dot_general(qs, kvm_ref[:, ks], nt, preferred_element_type=F32)
        s_b = jnp.where(in_prev, s_p, jnp.where(in_cur, s_c, NEG_INF))
        s_m = jnp.where(is_meta, s_m, NEG_INF)
        sink = jnp.concatenate(
            [jnp.full((BLOCK, 1), sink_ref[h * GROUP + g], F32) for g in range(GROUP)], axis=0)
        m = jnp.maximum(jnp.max(jnp.maximum(s_b, s_m), axis=-1, keepdims=True), sink)
        p_b = jnp.exp(s_b - m)
        p_m = jnp.exp(s_m - m)
        den = jnp.sum(p_b + p_m, axis=-1, keepdims=True) + jnp.exp(sink - m)
        r = (jnp.dot(jnp.where(in_cur, p_b, 0.0).astype(BF16), kvc_ref[:, vs], preferred_element_type=F32)
             + jnp.dot(jnp.where(in_cur, 0.0, p_b).astype(BF16), kvp_ref[:, vs], preferred_element_type=F32)
             + jnp.dot(p_m.astype(BF16), kvm_ref[:, vs], preferred_element_type=F32))
        r = r / den
        for j in range(GROUP // 2):
            outs.append(jnp.where(low, r[2 * j * BLOCK:(2 * j + 1) * BLOCK], r[(2 * j + 1) * BLOCK:(2 * j + 2) * BLOCK]))
    o_all = jnp.concatenate(outs, axis=1)
    o_ref[...] = _rms(o_all, g_ref[...]).astype(BF16)


def _attention(sinks, q, kv, kvm, g_attn, batch, seq):
    nb = seq // BLOCK
    return pl.pallas_call(
        _attn_kernel,
        grid_spec=pltpu.PrefetchScalarGridSpec(
            num_scalar_prefetch=1,
            grid=(batch, nb),
            in_specs=[
                pl.BlockSpec((BLOCK, D_ATTN), lambda b, n, s: (b * nb + n, 0)),
                pl.BlockSpec((BLOCK, KV_COLS), lambda b, n, s: (b * nb + n, 0)),
                pl.BlockSpec((BLOCK, KV_COLS), lambda b, n, s: (b * nb + jnp.maximum(n - 1, 0), 0)),
                pl.BlockSpec((BLOCK, KV_COLS), lambda b, n, s: (0, 0)),
                pl.BlockSpec((1, D_ATTN), lambda b, n, s: (0, 0)),
            ],
            out_specs=pl.BlockSpec((BLOCK, D_ATTN), lambda b, n, s: (b * nb + n, 0)),
        ),
        out_shape=jax.ShapeDtypeStruct((batch * seq, D_ATTN), BF16),
        compiler_params=_cparams(("arbitrary", "arbitrary")),
        name="attention",
    )(sinks, q, kv, kv, kvm, g_attn)


def _one_minus_sq(a, log_a):
    y = 2.0 * log_a
    p = 1.0 + y * (1.0 / 4.0)
    for k in (3, 2):
        p = 1.0 + y * p * (1.0 / k)
    return jnp.where(y > -1.0 / 64.0, -(y * p), 1.0 - a * a)


def _lru_kernel(xg_ref, cw_ref, cb_ref, wa_ref, ba_ref, wi_ref, bi_ref, lam_ref, g_ref, h0_ref, tail0_ref,
                o_ref, hout_ref, tailout_ref, ext_ref, tail_ref, a_ref, b_ref, h_ref):
    tt = pl.program_id(0)
    batch, rows = xg_ref.shape[0], xg_ref.shape[1]

    @pl.when(tt == 0)
    def _():
        for g in range(D_LRU // LANES):
            h_ref[g] = jnp.broadcast_to(h0_ref[:, g * LANES:(g + 1) * LANES], (batch, LANES))
        for s in range(batch):
            tail_ref[s] = tail0_ref[...]

    sp = jax.nn.softplus(-lam_ref[...])

    def gates(s, carry):
        ext_ref[0:8, :] = tail_ref[s]
        ext_ref[8:, :] = xg_ref[s, :, :D_LRU]
        tail_ref[s] = ext_ref[rows:rows + 8, :]
        xc = cb_ref[...] + sum(cw_ref[j:j + 1, :] * ext_ref[5 + j:5 + j + rows, :] for j in range(CONV_WIDTH))
        xcb = xc.astype(BF16)
        for c in range(D_LRU // LRU_GROUP):
            cs = slice(c * LRU_GROUP, (c + 1) * LRU_GROUP)
            r = jax.nn.sigmoid(jnp.dot(xcb[:, cs], wa_ref[c], preferred_element_type=F32) + ba_ref[:, cs])
            i = jax.nn.sigmoid(jnp.dot(xcb[:, cs], wi_ref[c], preferred_element_type=F32) + bi_ref[:, cs])
            log_a = -LRU_C * r * sp[:, cs]
            a = jnp.exp(log_a)
            b = jnp.sqrt(_one_minus_sq(a, log_a)) * i * xc[:, cs]
            for g in range(LRU_GROUP // LANES):
                lanes = slice(g * LANES, (g + 1) * LANES)
                a_ref[c * (LRU_GROUP // LANES) + g, pl.ds(s, rows, stride=batch), :] = a[:, lanes]
                b_ref[c * (LRU_GROUP // LANES) + g, pl.ds(s, rows, stride=batch), :] = b[:, lanes]
        return carry

    lax.fori_loop(0, batch, gates, 0)

    def step(t, h):
        r0 = pl.multiple_of(t * batch, batch)
        new = []
        for g in range(D_LRU // LANES):
            hg = a_ref[g, pl.ds(r0, batch), :] * h[g] + b_ref[g, pl.ds(r0, batch), :]
            b_ref[g, pl.ds(r0, batch), :] = hg
            new.append(hg)
        return tuple(new)

    h_last = lax.fori_loop(0, rows, step, tuple(h_ref[g] for g in range(D_LRU // LANES)))
    for g in range(D_LRU // LANES):
        h_ref[g] = h_last[g]
    hout_ref[...] = jnp.concatenate([h_last[g][0:1] for g in range(D_LRU // LANES)], axis=1)
    tailout_ref[...] = tail_ref[0]

    def finish(s, carry):
        hs = jnp.concatenate([b_ref[g, pl.ds(s, rows, stride=batch), :] for g in range(D_LRU // LANES)], axis=1)
        y = hs * jax.nn.gelu(xg_ref[s, :, D_LRU:])
        o_ref[s] = _rms(y, g_ref[...]).astype(BF16)
        return carry

    lax.fori_loop(0, batch, finish, 0)


def _lru(xg, conv_w, conv_b, wa_bd, b_a, wi_bd, b_i, lam, g_lru, h0, tail0, batch, seq, tt):
    ntt = seq // tt
    ng = D_LRU // LANES
    vec = pl.BlockSpec((1, D_LRU), lambda t: (0, 0))
    wspec = pl.BlockSpec((D_LRU // LRU_GROUP, LRU_GROUP, LRU_GROUP), lambda t: (0, 0, 0))
    out, h_last, tail = pl.pallas_call(
        _lru_kernel,
        grid=(ntt,),
        in_specs=[
            pl.BlockSpec((batch, tt, 2 * D_LRU), lambda t: (0, t, 0)),
            pl.BlockSpec((CONV_WIDTH, D_LRU), lambda t: (0, 0)),
            vec, wspec, vec, wspec, vec, vec, vec, vec,
            pl.BlockSpec((8, D_LRU), lambda t: (0, 0)),
        ],
        out_specs=[
            pl.BlockSpec((batch, tt, D_LRU), lambda t: (0, t, 0)),
            pl.BlockSpec((1, D_LRU), lambda t: (0, 0)),
            pl.BlockSpec((8, D_LRU), lambda t: (0, 0)),
        ],
        out_shape=[
            jax.ShapeDtypeStruct((batch, seq, D_LRU), BF16),
            jax.ShapeDtypeStruct((1, D_LRU), F32),
            jax.ShapeDtypeStruct((8, D_LRU), F32),
        ],
        scratch_shapes=[
            pltpu.VMEM((8 + tt, D_LRU), F32),
            pltpu.VMEM((batch, 8, D_LRU), F32),
            pltpu.VMEM((ng, tt * batch, LANES), F32),
            pltpu.VMEM((ng, tt * batch, LANES), F32),
            pltpu.VMEM((ng, batch, LANES), F32),
        ],
        compiler_params=_cparams(("arbitrary",)),
        name="rglru",
    )(xg.reshape(batch, seq, 2 * D_LRU), conv_w, conv_b, wa_bd, b_a, wi_bd, b_i, lam, g_lru, h0, tail0)
    return out.reshape(batch * seq, D_LRU), h_last, tail


def _outproj_kernel(a_ref, l_ref, wo_ref, bo_ref, x_ref, gf_ref, wr_ref, br_ref, tri_ref,
                    h_ref, xp_ref, ids_ref, wts_ref, rank_ref, cnt_ref, carry_ref):
    m = pl.program_id(0)
    tm = a_ref.shape[0]

    @pl.when(m == 0)
    def _():
        carry_ref[...] = jnp.zeros_like(carry_ref)

    h = (jnp.dot(a_ref[...], wo_ref[:D_ATTN, :], preferred_element_type=F32)
         + jnp.dot(l_ref[...], wo_ref[D_ATTN:, :], preferred_element_type=F32)
         + bo_ref[...] + x_ref[...])
    h_ref[...] = h
    xn = _rms(h, gf_ref[...]).astype(BF16)
    half = xn.shape[1] // 2
    lo = pltpu.bitcast(xn[:, :half].astype(F32), jnp.uint32)
    hi = pltpu.bitcast(xn[:, half:].astype(F32), jnp.uint32)
    packed = (hi & jnp.uint32(0xFFFF0000)) | (lo >> 16)
    ns = half // LANES
    for s in range(ns):
        xp_ref[pl.ds(s, tm, stride=ns), :] = packed[:, s * LANES:(s + 1) * LANES]

    logits = lax.dot_general(wr_ref[...], xn, (((1,), (1,)), ((), ())), preferred_element_type=F32)
    logits = logits + br_ref[...]
    eidx = lax.broadcasted_iota(jnp.int32, (N_EXPERTS, tm), 0)
    work = logits
    vals, sels = [], []
    for k in range(TOP_K):
        v = jnp.max(work, axis=0, keepdims=True)
        idx = jnp.min(jnp.where(work == v, eidx, N_EXPERTS), axis=0, keepdims=True)
        sel = eidx == idx
        ids_ref[k:k + 1, :] = idx
        vals.append(v)
        sels.append(sel)
        work = jnp.where(sel, -jnp.inf, work)
    es = [jnp.exp(v - vals[0]) for v in vals]
    den = es[0] + es[1] + es[2] + es[3]
    for k in range(TOP_K):
        wts_ref[k:k + 1, :] = es[k] / den
    cnt = sum(s.astype(F32) for s in sels)
    incl = jnp.dot(cnt.astype(BF16), tri_ref[...], preferred_element_type=F32)
    before = incl - cnt + carry_ref[:, 0:1]
    for k in range(TOP_K):
        rk = jnp.sum(jnp.where(sels[k], before, 0.0), axis=0, keepdims=True)
        rank_ref[k:k + 1, :] = rk.astype(jnp.int32)
    carry_ref[...] = carry_ref[...] + incl[:, tm - 1:tm]
    cnt_ref[...] = carry_ref[...]


def _outproj(attn_n, lru_n, wo_bf, b_out, x2d, g_ffn, wr_t, br, tri, tm):
    rows, d = x2d.shape
    const = lambda shape: pl.BlockSpec(shape, lambda m: tuple(0 for _ in shape))
    return pl.pallas_call(
        _outproj_kernel,
        grid=(rows // tm,),
        in_specs=[
            pl.BlockSpec((tm, D_ATTN), lambda m: (m, 0)),
            pl.BlockSpec((tm, D_LRU), lambda m: (m, 0)),
            const((D_ATTN + D_LRU, d)), const((1, d)),
            pl.BlockSpec((tm, d), lambda m: (m, 0)),
            const((1, d)), const((N_EXPERTS, d)), const((N_EXPERTS, 1)), const((tm, tm)),
        ],
        out_specs=[
            pl.BlockSpec((tm, d), lambda m: (m, 0)),
            pl.BlockSpec((tm * (d // 2 // LANES), LANES), lambda m: (m, 0)),
            pl.BlockSpec((TOP_K, tm), lambda m: (0, m)),
            pl.BlockSpec((TOP_K, tm), lambda m: (0, m)),
            pl.BlockSpec((TOP_K, tm), lambda m: (0, m)),
            const((N_EXPERTS, LANES)),
        ],
        out_shape=[
            jax.ShapeDtypeStruct((rows, d), F32),
            jax.ShapeDtypeStruct((rows * (d // 2 // LANES), LANES), jnp.uint32),
            jax.ShapeDtypeStruct((TOP_K, rows), jnp.int32),
            jax.ShapeDtypeStruct((TOP_K, rows), F32),
            jax.ShapeDtypeStruct((TOP_K, rows), jnp.int32),
            jax.ShapeDtypeStruct((N_EXPERTS, LANES), F32),
        ],
        scratch_shapes=[pltpu.VMEM((N_EXPERTS, LANES), F32)],
        compiler_params=_cparams(("arbitrary",)),
        name="outproj_router",
    )(attn_n, lru_n, wo_bf, b_out, x2d, g_ffn, wr_t, br, tri)


def _zero_fill_rows(zero_ref, dst_hbm, first_row, n_blocks, sem):
    blk = zero_ref.shape[0]
    per = blk // SUB_ROWS

    def copy(j):
        row = pl.multiple_of((first_row + j * SUB_ROWS) * per, blk)
        return pltpu.make_async_copy(zero_ref, dst_hbm.at[pl.ds(row, blk)], sem)

    def start(j, c):
        copy(j).start()
        return c

    def wait(j, c):
        copy(j).wait()
        return c

    lax.fori_loop(0, n_blocks, start, 0)
    lax.fori_loop(0, n_blocks, wait, 0)


def _dispatch_kernel(tails_ref, pos_hbm, xp_ref, xs_hbm, pos_smem, zero_ref, sem, psem):
    i = pl.program_id(0)
    ns = TOK_WORDS // LANES
    tm = xp_ref.shape[0] // ns
    n = TOP_K * tm
    pcopy = pltpu.make_async_copy(pos_hbm.at[pl.ds(i * n, n)], pos_smem, psem)
    pcopy.start()

    @pl.when(i == 0)
    def _():
        zero_ref[...] = jnp.zeros_like(zero_ref)
        for e in range(N_EXPERTS):
            _zero_fill_rows(zero_ref, xs_hbm, tails_ref[e], 1, sem)
        _zero_fill_rows(zero_ref, xs_hbm, tails_ref[N_EXPERTS], tails_ref[N_EXPERTS + 1], sem)

    pcopy.wait()

    for k in range(TOP_K):
        def issue(tt, c):
            for u in range(ROW_DMA_UNROLL):
                t = tt * ROW_DMA_UNROLL + u
                src = xp_ref.at[pl.ds(pl.multiple_of(t * ns, ns), ns)]
                dst = xs_hbm.at[pl.ds(pl.multiple_of(pos_smem[k * tm + t] * ns, ns), ns)]
                pltpu.make_async_copy(src, dst, sem).start(priority=u % 2)
            return c

        lax.fori_loop(0, tm // ROW_DMA_UNROLL, issue, 0)
    for _ in range(TOP_K):
        pltpu.make_async_copy(xp_ref, xs_hbm.at[pl.ds(0, tm * ns)], sem).wait()


def _dispatch(tails, pos_tiles, xp, p_alloc, tm):
    ns = TOK_WORDS // LANES
    rows = xp.shape[0] // ns
    return pl.pallas_call(
        _dispatch_kernel,
        grid_spec=pltpu.PrefetchScalarGridSpec(
            num_scalar_prefetch=1,
            grid=(rows // tm,),
            in_specs=[
                pl.BlockSpec(memory_space=pl.ANY),
                pl.BlockSpec((tm * ns, LANES), lambda i, s: (i, 0)),
            ],
            out_specs=pl.BlockSpec(memory_space=pl.ANY),
            scratch_shapes=[
                pltpu.SMEM((TOP_K * tm,), jnp.int32),
                pltpu.VMEM((SUB_ROWS * ns, LANES), jnp.uint32),
                pltpu.SemaphoreType.DMA,
                pltpu.SemaphoreType.DMA,
            ],
        ),
        out_shape=jax.ShapeDtypeStruct((p_alloc * ns, LANES), jnp.uint32),
        compiler_params=_cparams(("arbitrary",)),
        name="dispatch",
    )(tails, pos_tiles, xp)


def _chunk_dma(cs_ref, cn_ref, src_hbm, dst_ref, per, sem):
    blk = SUB_ROWS * per
    nch = cs_ref.shape[0]

    def copy(ci, j):
        src = src_hbm.at[pl.ds(pl.multiple_of((cs_ref[ci] + j * SUB_ROWS) * per, blk), blk)]
        return pltpu.make_async_copy(src, dst_ref.at[pl.ds(pl.multiple_of(j * blk, blk), blk)], sem)

    def run(ci, wait):
        cc = jnp.minimum(ci, nch - 1)
        n = jnp.where(ci < nch, cn_ref[cc], 0)

        def body(j, c):
            if wait:
                copy(cc, j).wait()
            else:
                copy(cc, j).start()
            return c

        lax.fori_loop(0, n, body, 0)

    return (lambda ci: run(ci, False)), (lambda ci: run(ci, True))


def _tile_loop(nsub, prepare, compute, commit):
    def run(tiles):
        prepare([t[2] for t in tiles])
        for t in tiles:
            compute(*t)
        commit(tiles)

    n8 = nsub // 8
    rem = nsub % 8
    base = n8 * 8 * SUB_ROWS
    r2 = base + (rem // 4) * 4 * SUB_ROWS
    r1 = r2 + ((rem % 4) // 2) * 2 * SUB_ROWS

    @pl.when(rem % 4 >= 2)
    def _():
        run([(pl.multiple_of(r2, SUB_ROWS), 2 * SUB_ROWS, 2)])

    @pl.when(rem % 2 == 1)
    def _():
        run([(pl.multiple_of(r1, SUB_ROWS), SUB_ROWS, 3)])

    @pl.when(rem >= 4)
    def _():
        run([(pl.multiple_of(base, 4 * SUB_ROWS), 4 * SUB_ROWS, 0)])

    def pair(j, c):
        r = pl.multiple_of(j * 8 * SUB_ROWS, 8 * SUB_ROWS)
        run([(r, 4 * SUB_ROWS, 0), (r + 4 * SUB_ROWS, 4 * SUB_ROWS, 1)])
        return c

    lax.fori_loop(0, n8, pair, 0)


def _moe_up_kernel(ce_ref, cs_ref, cn_ref, tail_ref, xs_hbm, wg_ref, bg_ref, wu_ref, bu_ref, act_hbm,
                   land_ref, xb_ref, wgu_ref, ab_ref, zero_ref, pend_ref, sem, osem):
    c = pl.program_id(0)
    f = pl.program_id(1)
    nsub = cn_ref[c]
    start = pl.multiple_of(cs_ref[c], SUB_ROWS)
    tf = wg_ref.shape[2]
    ns = TOK_WORDS // LANES
    start_chunk, wait_chunk = _chunk_dma(cs_ref, cn_ref, xs_hbm, land_ref, ns, sem)

    @pl.when(jnp.logical_and(c == 0, f == 0))
    def _():
        zero_ref[...] = jnp.zeros_like(zero_ref)
        _zero_fill_rows(zero_ref, act_hbm, tail_ref[0], tail_ref[1], osem.at[0])
        for slot in range(4):
            pend_ref[slot] = 0
        start_chunk(c)

    @pl.when(jnp.logical_and(f == 0, nsub > 0))
    def _():
        wait_chunk(c)

        def unpack(j, carry):
            r = pl.multiple_of(j * SUB_ROWS, SUB_ROWS)
            for s in range(ns):
                w = land_ref[pl.ds(r * ns + s, SUB_ROWS, stride=ns), :]
                xb_ref[pl.ds(r, SUB_ROWS), s * LANES:(s + 1) * LANES] = pltpu.bitcast(w << 16, F32).astype(BF16)
                xb_ref[pl.ds(r, SUB_ROWS), TOK_WORDS + s * LANES:TOK_WORDS + (s + 1) * LANES] = (
                    pltpu.bitcast(w & jnp.uint32(0xFFFF0000), F32).astype(BF16))
            return carry

        lax.fori_loop(0, nsub, unpack, 0)
        start_chunk(c + 1)

    def out_copy(r, rows, slot):
        return pltpu.make_async_copy(
            ab_ref.at[slot, pl.ds(0, rows)],
            act_hbm.at[pl.ds(pl.multiple_of(start + r, SUB_ROWS), rows), pl.ds(pl.multiple_of(f * tf, tf), tf)],
            osem.at[slot])

    @pl.when(nsub > 0)
    def _():
        wgu_ref[:, :tf] = wg_ref[0].astype(BF16)
        wgu_ref[:, tf:] = wu_ref[0].astype(BF16)
        bg = bg_ref[0]
        bu = bu_ref[0]

        def prepare(slots):
            for slot in slots:
                @pl.when(pend_ref[slot] == 1)
                def _():
                    out_copy(0, SLOT_ROWS[slot], slot).wait()

        def compute(r, rows, slot):
            z = jnp.dot(xb_ref[pl.ds(r, rows), :], wgu_ref[...], preferred_element_type=F32)
            g = z[:, :tf] + bg
            u = z[:, tf:] + bu
            g = jnp.minimum(g, SWIGLU_LIMIT)
            u = jnp.clip(u, -SWIGLU_LIMIT, SWIGLU_LIMIT)
            ab_ref[slot, pl.ds(0, rows), :] = (g * jax.nn.sigmoid(SWIGLU_ALPHA * g) * (u + 1.0)).astype(BF16)

        def commit(tiles):
            for r, rows, slot in tiles:
                out_copy(r, rows, slot).start()
                pend_ref[slot] = 1

        _tile_loop(nsub, prepare, compute, commit)

    @pl.when(jnp.logical_and(c == pl.num_programs(0) - 1, f == pl.num_programs(1) - 1))
    def _():
        for slot, rows in enumerate(SLOT_ROWS):
            @pl.when(pend_ref[slot] == 1)
            def _():
                out_copy(0, rows, slot).wait()


def _moe_up(ch_e, ch_start, ch_nsub, tail, xs, w_gate, b_gate, w_up, b_up):
    ns = TOK_WORDS // LANES
    p_alloc = xs.shape[0] // ns
    d = 2 * TOK_WORDS
    dff = w_gate.shape[2]
    tf = MOE_TF
    nch = ch_e.shape[0]
    nf = dff // tf
    wmap = lambda c, f, e, s, n, t: (e[c], 0, jnp.where(n[c] > 0, f, nf - 1))
    return pl.pallas_call(
        _moe_up_kernel,
        grid_spec=pltpu.PrefetchScalarGridSpec(
            num_scalar_prefetch=4,
            grid=(nch, nf),
            in_specs=[
                pl.BlockSpec(memory_space=pl.ANY),
                pl.BlockSpec((1, d, tf), wmap),
                pl.BlockSpec((1, 1, tf), wmap),
                pl.BlockSpec((1, d, tf), wmap),
                pl.BlockSpec((1, 1, tf), wmap),
            ],
            out_specs=pl.BlockSpec(memory_space=pl.ANY),
            scratch_shapes=[
                pltpu.VMEM((MOE_TM * ns, LANES), jnp.uint32),
                pltpu.VMEM((MOE_TM, d), BF16),
                pltpu.VMEM((d, 2 * tf), BF16),
                pltpu.VMEM((4, 4 * SUB_ROWS, tf), BF16),
                pltpu.VMEM((SUB_ROWS, dff), BF16),
                pltpu.SMEM((4,), jnp.int32),
                pltpu.SemaphoreType.DMA,
                pltpu.SemaphoreType.DMA((4,)),
            ],
        ),
        out_shape=jax.ShapeDtypeStruct((p_alloc, dff), BF16),
        compiler_params=_cparams(("arbitrary", "arbitrary")),
        name="moe_up",
    )(ch_e, ch_start, ch_nsub, tail, xs, w_gate, b_gate.reshape(N_EXPERTS, 1, dff), w_up,
      b_up.reshape(N_EXPERTS, 1, dff))


def _moe_down_kernel(ce_ref, cs_ref, cn_ref, tail_ref, act_hbm, wd_ref, bd_ref, ys_hbm,
                     xa_ref, xb_ref, wdb_ref, yb_ref, zero_ref, pend_ref, sem, osem):
    c = pl.program_id(0)
    f = pl.program_id(1)
    nf = pl.num_programs(1)
    nsub = cn_ref[c]
    start = pl.multiple_of(cs_ref[c], SUB_ROWS)
    tn = wd_ref.shape[2]
    ns = TOK_WORDS // LANES
    nw = tn // 2 // LANES
    copies = [_chunk_dma(cs_ref, cn_ref, act_hbm, ref, 1, sem.at[i]) for i, ref in enumerate((xa_ref, xb_ref))]

    def on_parity(ci, fn):
        for p in range(2):
            @pl.when(ci % 2 == p)
            def _():
                fn(p)

    @pl.when(jnp.logical_and(c == 0, f == 0))
    def _():
        zero_ref[...] = jnp.zeros_like(zero_ref)
        _zero_fill_rows(zero_ref, ys_hbm, tail_ref[0], tail_ref[1], osem)
        for kind in range(3):
            pend_ref[kind] = 0
        copies[0][0](c)

    @pl.when(jnp.logical_and(f == 0, nsub > 0))
    def _():
        on_parity(c, lambda p: copies[p][1](c))
        on_parity(c + 1, lambda p: copies[p][0](c + 1))

    def out_copy(r, rows):
        src = yb_ref.at[pl.ds(pl.multiple_of(r * ns, SUB_ROWS * ns), rows * ns)]
        dst = ys_hbm.at[pl.ds(pl.multiple_of((start + r) * ns, SUB_ROWS * ns), rows * ns)]
        return pltpu.make_async_copy(src, dst, osem)

    def drain():
        for kind, rows in enumerate(SLOT_ROWS[1:]):
            def wait(j, carry):
                out_copy(0, rows).wait()
                return carry

            lax.fori_loop(0, pend_ref[kind], wait, 0)
            pend_ref[kind] = 0

    def compute(x_ref):
        wdb_ref[...] = wd_ref[0].astype(BF16)
        bd = bd_ref[0]
        last = f == nf - 1

        @pl.when(f == 0)
        def _():
            drain()

        def compute(r, rows, slot):
            y = jnp.dot(x_ref[pl.ds(r, rows), :], wdb_ref[...], preferred_element_type=F32) + bd
            lo = pltpu.bitcast(y[:, :tn // 2].astype(BF16).astype(F32), jnp.uint32)
            hi = pltpu.bitcast(y[:, tn // 2:].astype(BF16).astype(F32), jnp.uint32)
            packed = (hi & jnp.uint32(0xFFFF0000)) | (lo >> 16)
            for s in range(nw):
                yb_ref[pl.ds(r * ns + f * nw + s, rows, stride=ns), :] = packed[:, s * LANES:(s + 1) * LANES]

        def commit(tiles):
            @pl.when(last)
            def _():
                for r, rows, slot in tiles:
                    out_copy(r, rows).start()
                    kind = max(slot - 1, 0)
                    pend_ref[kind] = pend_ref[kind] + 1

        _tile_loop(nsub, lambda slots: None, compute, commit)

    @pl.when(nsub > 0)
    def _():
        on_parity(c, lambda p: compute((xa_ref, xb_ref)[p]))

    @pl.when(jnp.logical_and(c == pl.num_programs(0) - 1, f == nf - 1))
    def _():
        drain()


def _moe_down(ch_e, ch_start, ch_nsub, tail, act, w_down, b_down):
    p_alloc, dff = act.shape
    d = w_down.shape[2]
    ns = TOK_WORDS // LANES
    tn = MOE_TN
    nch = ch_e.shape[0]
    nn = d // tn
    wmap = lambda c, f, e, s, n, t: (e[c], 0, jnp.where(n[c] > 0, f, nn - 1))
    return pl.pallas_call(
        _moe_down_kernel,
        grid_spec=pltpu.PrefetchScalarGridSpec(
            num_scalar_prefetch=4,
            grid=(nch, nn),
            in_specs=[
                pl.BlockSpec(memory_space=pl.ANY),
                pl.BlockSpec((1, dff, tn), wmap),
                pl.BlockSpec((1, 1, tn), wmap),
            ],
            out_specs=pl.BlockSpec(memory_space=pl.ANY),
            scratch_shapes=[
                pltpu.VMEM((MOE_TM, dff), BF16),
                pltpu.VMEM((MOE_TM, dff), BF16),
                pltpu.VMEM((dff, tn), BF16),
                pltpu.VMEM((MOE_TM * ns, LANES), jnp.uint32),
                pltpu.VMEM((SUB_ROWS * ns, LANES), jnp.uint32),
                pltpu.SMEM((3,), jnp.int32),
                pltpu.SemaphoreType.DMA((2,)),
                pltpu.SemaphoreType.DMA,
            ],
        ),
        out_shape=jax.ShapeDtypeStruct((p_alloc * ns, LANES), jnp.uint32),
        compiler_params=_cparams(("arbitrary", "arbitrary")),
        name="moe_down",
    )(ch_e, ch_start, ch_nsub, tail, act, w_down, b_down.reshape(N_EXPERTS, 1, d))


def _combine_kernel(pos_hbm, ys_hbm, h_ref, w_ref, g_ref, o_ref, pos_smem, yb_ref, sem, psem):
    i = pl.program_id(0)
    tm, d = h_ref.shape
    n = TOP_K * tm
    ns = TOK_WORDS // LANES

    def fetch(tile, slot):
        pcopy = pltpu.make_async_copy(pos_hbm.at[pl.ds(tile * n, n)], pos_smem.at[pl.ds(slot * n, n)], psem)
        pcopy.start()
        pcopy.wait()

        def issue(jj, c):
            for u in range(ROW_DMA_UNROLL):
                j = jj * ROW_DMA_UNROLL + u
                src = ys_hbm.at[pl.ds(pl.multiple_of(pos_smem[slot * n + j] * ns, ns), ns)]
                dst = yb_ref.at[slot, pl.ds(pl.multiple_of(j * ns, ns), ns)]
                pltpu.make_async_copy(src, dst, sem.at[slot]).start(priority=u % 2)
            return c

        lax.fori_loop(0, n // ROW_DMA_UNROLL, issue, 0)

    @pl.when(i == 0)
    def _():
        fetch(i, 0)

    @pl.when(i + 1 < pl.num_programs(0))
    def _():
        for p in range(2):
            @pl.when((i + 1) % 2 == p)
            def _():
                fetch(i + 1, p)

    def reduce(slot):
        pltpu.make_async_copy(ys_hbm.at[pl.ds(0, n * ns)], yb_ref.at[slot], sem.at[slot]).wait()
        wb = [jnp.broadcast_to(w_ref[:, k:k + 1], (tm, LANES)) for k in range(TOP_K)]
        ssq = jnp.zeros((tm, 1), F32)
        nw = MOE_TN // 2 // LANES
        for s in range(ns):
            c0 = (s // nw) * MOE_TN + (s % nw) * LANES
            lo_cols = slice(c0, c0 + LANES)
            hi_cols = slice(c0 + MOE_TN // 2, c0 + MOE_TN // 2 + LANES)
            acc_lo = h_ref[:, lo_cols]
            acc_hi = h_ref[:, hi_cols]
            for k in range(TOP_K):
                w = yb_ref[slot, pl.ds(k * tm * ns + s, tm, stride=ns), :]
                acc_lo = acc_lo + wb[k] * pltpu.bitcast(w << 16, F32)
                acc_hi = acc_hi + wb[k] * pltpu.bitcast(w & jnp.uint32(0xFFFF0000), F32)
            o_ref[:, lo_cols] = acc_lo
            o_ref[:, hi_cols] = acc_hi
            ssq = ssq + jnp.sum(acc_lo * acc_lo + acc_hi * acc_hi, axis=-1, keepdims=True)
        o_ref[...] = o_ref[...] * lax.rsqrt(ssq * (1.0 / d) + NORM_EPS) * g_ref[...]

    for p in range(2):
        @pl.when(i % 2 == p)
        def _():
            reduce(p)


def _combine(pos_tiles, ys, h1, wts_t, g_final, tm):
    rows, d = h1.shape
    ns = TOK_WORDS // LANES
    return pl.pallas_call(
        _combine_kernel,
        grid=(rows // tm,),
        in_specs=[
            pl.BlockSpec(memory_space=pl.ANY),
            pl.BlockSpec(memory_space=pl.ANY),
            pl.BlockSpec((tm, d), lambda i: (i, 0)),
            pl.BlockSpec((tm, TOP_K), lambda i: (i, 0)),
            pl.BlockSpec((1, d), lambda i: (0, 0)),
        ],
        out_specs=pl.BlockSpec((tm, d), lambda i: (i, 0)),
        out_shape=jax.ShapeDtypeStruct((rows, d), F32),
        scratch_shapes=[
            pltpu.SMEM((2 * TOP_K * tm,), jnp.int32),
            pltpu.VMEM((2, TOP_K * tm * ns, LANES), jnp.uint32),
            pltpu.SemaphoreType.DMA((2,)),
            pltpu.SemaphoreType.DMA,
        ],
        compiler_params=_cparams(("arbitrary",)),
        name="combine",
    )(pos_tiles, ys, h1, wts_t, g_final)


def _rope_tables(n_pos):
    half = HEAD_DIM // 2
    inv = 1.0 / (ROPE_THETA ** (jnp.arange(half, dtype=F32) / half))
    ang = jnp.arange(n_pos, dtype=F32)[:, None] * inv[None, :]
    cos = jnp.tile(jnp.cos(ang), (1, LANES // half))
    sin = jnp.tile(jnp.concatenate([-jnp.sin(ang), jnp.sin(ang)], axis=1), (1, LANES // HEAD_DIM))
    return cos, sin


def _block_diag(w):
    per = LRU_GROUP // LRU_BLOCK
    w4 = w.reshape(D_LRU // LRU_GROUP, per, LRU_BLOCK, LRU_BLOCK)
    eye = jnp.eye(per, dtype=w.dtype)
    bd = jnp.einsum("gpcd,pq->gpcqd", w4, eye)
    return bd.reshape(D_LRU // LRU_GROUP, LRU_GROUP, LRU_GROUP).astype(BF16)


def _tile_positions(pos, tm):
    rows = pos.shape[1]
    return pos.reshape(TOP_K, rows // tm, tm).transpose(1, 0, 2).reshape(-1)


def _chunk_schedule(counts, n_rows):
    aligned = ((counts + SUB_ROWS - 1) // SUB_ROWS) * SUB_ROWS
    offs = jnp.cumsum(aligned) - aligned
    n_ch = (aligned + MOE_TM - 1) // MOE_TM
    cum = jnp.cumsum(n_ch)
    total = cum[-1]
    nch_max = N_EXPERTS + (n_rows + N_EXPERTS * SUB_ROWS) // MOE_TM
    c = jnp.arange(nch_max, dtype=jnp.int32)
    cc = jnp.minimum(c, total - 1)
    e = jnp.searchsorted(cum, cc, side="right").astype(jnp.int32)
    j = cc - (cum[e] - n_ch[e])
    start = offs[e] + j * MOE_TM
    nsub = jnp.where(c < total, jnp.minimum(MOE_TM, aligned[e] - j * MOE_TM) // SUB_ROWS, 0)
    tails = offs + (counts // SUB_ROWS) * SUB_ROWS
    used = jnp.sum(aligned)
    tail = jnp.stack([used, (n_rows + N_EXPERTS * SUB_ROWS - used) // SUB_ROWS])
    tails = jnp.concatenate([tails, tail])
    return (offs, e, start.astype(jnp.int32), nsub.astype(jnp.int32), tails.astype(jnp.int32),
            tail.astype(jnp.int32))


def kernel(x, meta_tokens, norm_mix, w_in, b_in, sinks, conv_w, conv_b, w_a, b_a, w_i, b_i, lru_lambda,
           g_attn_out, g_lru_out, w_out, b_out, norm_ffn, w_router, b_router, w_gate, b_gate, w_up, b_up,
           w_down, b_down, final_norm):
    batch, seq, d = x.shape
    rows = batch * seq
    x2d = x.reshape(rows, d)
    row = lambda v: v.reshape(1, -1)

    w_in_bf = w_in[0].astype(BF16)
    w_out_bf = w_out[0].astype(BF16)
    cos, sin = _rope_tables(N_META + seq)
    wa_bd, wi_bd = _block_diag(w_a[0]), _block_diag(w_i[0])
    lru_args = (conv_w[0], row(conv_b[0]), wa_bd, row(b_a[0]), wi_bd, row(b_i[0]), row(lru_lambda[0]),
                row(g_lru_out[0]))

    _, kvm, xgm = _inproj(meta_tokens, row(norm_mix[0]), w_in_bf, row(b_in[0]), cos[:N_META], sin[:N_META],
                          N_META, 1)
    _, h0, tail0 = _lru(xgm, *lru_args, jnp.zeros((1, D_LRU), F32), jnp.zeros((8, D_LRU), F32), 1, N_META,
                        N_META)

    tm = 512
    q, kv, xg = _inproj(x2d, row(norm_mix[0]), w_in_bf, row(b_in[0]), cos[N_META:], sin[N_META:], tm,
                        seq // tm)
    kvm = jnp.pad(kvm, ((0, BLOCK - N_META), (0, 0)))
    attn_n = _attention(sinks[0], q, kv, kvm, row(g_attn_out[0]), batch, seq)
    lru_n, _, _ = _lru(xg, *lru_args, h0, tail0, batch, seq, 128)

    tri = (jnp.arange(tm)[:, None] <= jnp.arange(tm)[None, :]).astype(BF16)
    h1, xp, ids, wts, rank, cnt = _outproj(
        attn_n, lru_n, w_out_bf, row(b_out[0]), x2d, row(norm_ffn[0]), w_router[0].T.astype(BF16),
        b_router[0].reshape(N_EXPERTS, 1), tri, tm)

    counts = cnt[:, 0].astype(jnp.int32)
    n_rows = rows * TOP_K
    offs, ch_e, ch_start, ch_nsub, tails, tail = _chunk_schedule(counts, n_rows)
    onehot = ids[..., None] == jnp.arange(N_EXPERTS, dtype=jnp.int32)
    pos = jnp.sum(jnp.where(onehot, offs, 0), axis=-1) + rank
    p_alloc = n_rows + N_EXPERTS * SUB_ROWS

    tmd = 512
    xs = _dispatch(tails, _tile_positions(pos, tmd), xp, p_alloc, tmd)
    act = _moe_up(ch_e, ch_start, ch_nsub, tail, xs, w_gate[0], b_gate[0], w_up[0], b_up[0])
    ys = _moe_down(ch_e, ch_start, ch_nsub, tail, act, w_down[0], b_down[0])
    tmc = 256
    out = _combine(_tile_positions(pos, tmc), ys, h1, wts.T, row(final_norm), tmc)
    return out.reshape(batch, seq, d)
```

```python
import jax
import jax.numpy as jnp
from jax import lax
from jax.experimental import pallas as pl
from jax.experimental.pallas import tpu as pltpu

F32 = jnp.float32
BF16 = jnp.bfloat16

N_META = 16
HEAD_DIM = 64
N_Q_HEADS = 16
N_KV_HEADS = 4
GROUP = N_Q_HEADS // N_KV_HEADS
D_ATTN = N_Q_HEADS * HEAD_DIM
D_KV = N_KV_HEADS * HEAD_DIM
KV_COLS = 4 * D_KV
BLOCK = 128
ROPE_THETA = 10000.0
D_LRU = 1024
LRU_BLOCK = 64
LRU_GROUP = 256
CONV_WIDTH = 4
LRU_C = 8.0
N_EXPERTS = 32
TOP_K = 4
SWIGLU_LIMIT = 7.0
SWIGLU_ALPHA = 1.702
NORM_EPS = 1e-5
NEG_INF = -1e30

LANES = 128
D_MODEL = 2048
TOK_WORDS = D_MODEL // 2
SUB_ROWS = 128
MOE_TM = 2304
MOE_TF = 512
MOE_TN = 1024
ROW_DMA_UNROLL = 8
SLOT_ROWS = (4 * SUB_ROWS,) * 5 + (2 * SUB_ROWS, SUB_ROWS)
VMEM_LIMIT = 56 * 1024 * 1024


def _cparams(sem):
    return pltpu.CompilerParams(dimension_semantics=sem, vmem_limit_bytes=VMEM_LIMIT)


def _rms(x, g):
    return x * lax.rsqrt(jnp.mean(x * x, axis=-1, keepdims=True) + NORM_EPS) * g


def _inproj_kernel(x_ref, g_ref, w_ref, b_ref, cos_ref, sin_ref, q_ref, kv_ref, xg_ref):
    xn = _rms(x_ref[...], g_ref[...]).astype(BF16)
    cos = cos_ref[...]
    sin = sin_ref[...]
    lane = lax.broadcasted_iota(jnp.int32, (1, LANES), 1)
    first_half = (lane % HEAD_DIM) < (HEAD_DIM // 2)

    def rope(z):
        partner = jnp.where(first_half, pltpu.roll(z, LANES - HEAD_DIM // 2, 1),
                            pltpu.roll(z, HEAD_DIM // 2, 1))
        return z * cos + partner * sin

    cw = 512
    for c in range(w_ref.shape[1] // cw):
        z = jnp.dot(xn, w_ref[:, c * cw:(c + 1) * cw], preferred_element_type=F32)
        z = z + b_ref[:, c * cw:(c + 1) * cw]
        if c < 2:
            for j in range(cw // LANES):
                zz = rope(z[:, j * LANES:(j + 1) * LANES]) * (HEAD_DIM ** -0.5)
                q_ref[:, c * cw + j * LANES:c * cw + (j + 1) * LANES] = zz.astype(BF16)
        elif c == 2:
            low = lane < HEAD_DIM
            for j in range(2 * D_KV // LANES):
                zz = z[:, j * LANES:(j + 1) * LANES]
                zz = rope(zz) if j < D_KV // LANES else zz
                sw = pltpu.roll(zz, HEAD_DIM, 1)
                kv_ref[:, 2 * j * LANES:(2 * j + 1) * LANES] = jnp.where(low, zz, sw).astype(BF16)
                kv_ref[:, (2 * j + 1) * LANES:(2 * j + 2) * LANES] = jnp.where(low, sw, zz).astype(BF16)
        else:
            xg_ref[:, (c - 3) * cw:(c - 2) * cw] = z


def _inproj(x2d, g, w_bf, b, cos, sin, tm, pos_blocks):
    rows, d = x2d.shape
    dz = w_bf.shape[1]
    return pl.pallas_call(
        _inproj_kernel,
        grid=(rows // tm,),
        in_specs=[
            pl.BlockSpec((tm, d), lambda m: (m, 0)),
            pl.BlockSpec((1, d), lambda m: (0, 0)),
            pl.BlockSpec((d, dz), lambda m: (0, 0)),
            pl.BlockSpec((1, dz), lambda m: (0, 0)),
            pl.BlockSpec((tm, LANES), lambda m: (m % pos_blocks, 0)),
            pl.BlockSpec((tm, LANES), lambda m: (m % pos_blocks, 0)),
        ],
        out_specs=[
            pl.BlockSpec((tm, D_ATTN), lambda m: (m, 0)),
            pl.BlockSpec((tm, KV_COLS), lambda m: (m, 0)),
            pl.BlockSpec((tm, 2 * D_LRU), lambda m: (m, 0)),
        ],
        out_shape=[
            jax.ShapeDtypeStruct((rows, D_ATTN), BF16),
            jax.ShapeDtypeStruct((rows, KV_COLS), BF16),
            jax.ShapeDtypeStruct((rows, 2 * D_LRU), F32),
        ],
        compiler_params=_cparams(("arbitrary",)),
        name="inproj",
    )(x2d, g, w_bf, b, cos, sin)


def _attn_kernel(sink_ref, q_ref, kvc_ref, kvp_ref, kvm_ref, g_ref, o_ref):
    n = pl.program_id(1)
    row = lax.broadcasted_iota(jnp.int32, (GROUP * BLOCK, BLOCK), 0) % BLOCK
    col = lax.broadcasted_iota(jnp.int32, (GROUP * BLOCK, BLOCK), 1)
    in_cur = col <= row
    in_prev = jnp.logical_and(col > row, n > 0)
    is_meta = col < N_META
    low = lax.broadcasted_iota(jnp.int32, (BLOCK, LANES), 1) < HEAD_DIM
    nt = (((1,), (1,)), ((), ()))
    zero = jnp.zeros((), BF16)
    outs = []
    for h in range(N_KV_HEADS):
        ks = slice(h * LANES, (h + 1) * LANES)
        vs = slice(N_KV_HEADS * LANES + h * LANES, N_KV_HEADS * LANES + (h + 1) * LANES)
        parts = []
        for j in range(GROUP // 2):
            qg = q_ref[:, (h * GROUP // 2 + j) * LANES:(h * GROUP // 2 + j + 1) * LANES]
            parts += [jnp.where(low, qg, zero), jnp.where(low, zero, qg)]
        qs = jnp.concatenate(parts, axis=0)
        s_c = lax.dot_general(qs, kvc_ref[:, ks], nt, preferred_element_type=F32)
        s_p = lax.dot_general(qs, kvp_ref[:, ks], nt, preferred_element_type=F32)
        s_m = lax.dot_general(qs, kvm_ref[:, ks], nt, preferred_element_type=F32)
        s_b = jnp.where(in_prev, s_p, jnp.where(in_cur, s_c, NEG_INF))
        s_m = jnp.where(is_meta, s_m, NEG_INF)
        sink = jnp.concatenate(
            [jnp.full((BLOCK, 1), sink_ref[h * GROUP + g], F32) for g in range(GROUP)], axis=0)
        m = jnp.maximum(jnp.max(jnp.maximum(s_b, s_m), axis=-1, keepdims=True), sink)
        p_b = jnp.exp(s_b - m)
        p_m = jnp.exp(s_m - m)
        den = jnp.sum(p_b + p_m, axis=-1, keepdims=True) + jnp.exp(sink - m)
        r = (jnp.dot(jnp.where(in_cur, p_b, 0.0).astype(BF16), kvc_ref[:, vs], preferred_element_type=F32)
             + jnp.dot(jnp.where(in_cur, 0.0, p_b).astype(BF16), kvp_ref[:, vs], preferred_element_type=F32)
             + jnp.dot(p_m.astype(BF16), kvm_ref[:, vs], preferred_element_type=F32))
        r = r / den
        for j in range(GROUP // 2):
            outs.append(jnp.where(low, r[2 * j * BLOCK:(2 * j + 1) * BLOCK], r[(2 * j + 1) * BLOCK:(2 * j + 2) * BLOCK]))
    o_all = jnp.concatenate(outs, axis=1)
    o_ref[...] = _rms(o_all, g_ref[...]).astype(BF16)


def _attention(sinks, q, kv, kvm, g_attn, batch, seq):
    nb = seq // BLOCK
    return pl.pallas_call(
        _attn_kernel,
        grid_spec=pltpu.PrefetchScalarGridSpec(
            num_scalar_prefetch=1,
            grid=(batch, nb),
            in_specs=[
                pl.BlockSpec((BLOCK, D_ATTN), lambda b, n, s: (b * nb + n, 0)),
                pl.BlockSpec((BLOCK, KV_COLS), lambda b, n, s: (b * nb + n, 0)),
                pl.BlockSpec((BLOCK, KV_COLS), lambda b, n, s: (b * nb + jnp.maximum(n - 1, 0), 0)),
                pl.BlockSpec((BLOCK, KV_COLS), lambda b, n, s: (0, 0)),
                pl.BlockSpec((1, D_ATTN), lambda b, n, s: (0, 0)),
            ],
            out_specs=pl.BlockSpec((BLOCK, D_ATTN), lambda b, n, s: (b * nb + n, 0)),
        ),
        out_shape=jax.ShapeDtypeStruct((batch * seq, D_ATTN), BF16),
        compiler_params=_cparams(("arbitrary", "arbitrary")),
        name="attention",
    )(sinks, q, kv, kv, kvm, g_attn)


def _one_minus_sq(a, log_a):
    y = 2.0 * log_a
    p = 1.0 + y * (1.0 / 4.0)
    for k in (3, 2):
        p = 1.0 + y * p * (1.0 / k)
    return jnp.where(y > -1.0 / 64.0, -(y * p), 1.0 - a * a)


def _lru_kernel(xg_ref, cw_ref, cb_ref, wa_ref, ba_ref, wi_ref, bi_ref, lam_ref, g_ref, h0_ref, tail0_ref,
                o_ref, hout_ref, tailout_ref, ext_ref, tail_ref, a_ref, b_ref, h_ref):
    tt = pl.program_id(0)
    batch, rows = xg_ref.shape[0], xg_ref.shape[1]

    @pl.when(tt == 0)
    def _():
        for g in range(D_LRU // LANES):
            h_ref[g] = jnp.broadcast_to(h0_ref[:, g * LANES:(g + 1) * LANES], (batch, LANES))
        for s in range(batch):
            tail_ref[s] = tail0_ref[...]

    sp = jax.nn.softplus(-lam_ref[...])

    def gates(s, carry):
        ext_ref[0:8, :] = tail_ref[s]
        ext_ref[8:, :] = xg_ref[s, :, :D_LRU]
        tail_ref[s] = ext_ref[rows:rows + 8, :]
        xc = cb_ref[...] + sum(cw_ref[j:j + 1, :] * ext_ref[5 + j:5 + j + rows, :] for j in range(CONV_WIDTH))
        xcb = xc.astype(BF16)
        for c in range(D_LRU // LRU_GROUP):
            cs = slice(c * LRU_GROUP, (c + 1) * LRU_GROUP)
            r = jax.nn.sigmoid(jnp.dot(xcb[:, cs], wa_ref[c], preferred_element_type=F32) + ba_ref[:, cs])
            i = jax.nn.sigmoid(jnp.dot(xcb[:, cs], wi_ref[c], preferred_element_type=F32) + bi_ref[:, cs])
            log_a = -LRU_C * r * sp[:, cs]
            a = jnp.exp(log_a)
            b = jnp.sqrt(_one_minus_sq(a, log_a)) * i * xc[:, cs]
            for g in range(LRU_GROUP // LANES):
                lanes = slice(g * LANES, (g + 1) * LANES)
                a_ref[c * (LRU_GROUP // LANES) + g, pl.ds(s, rows, stride=batch), :] = a[:, lanes]
                b_ref[c * (LRU_GROUP // LANES) + g, pl.ds(s, rows, stride=batch), :] = b[:, lanes]
        return carry

    lax.fori_loop(0, batch, gates, 0)

    def step(t, h):
        r0 = pl.multiple_of(t * batch, batch)
        new = []
        for g in range(D_LRU // LANES):
            hg = a_ref[g, pl.ds(r0, batch), :] * h[g] + b_ref[g, pl.ds(r0, batch), :]
            b_ref[g, pl.ds(r0, batch), :] = hg
            new.append(hg)
        return tuple(new)

    h_last = lax.fori_loop(0, rows, step, tuple(h_ref[g] for g in range(D_LRU // LANES)))
    for g in range(D_LRU // LANES):
        h_ref[g] = h_last[g]
    hout_ref[...] = jnp.concatenate([h_last[g][0:1] for g in range(D_LRU // LANES)], axis=1)
    tailout_ref[...] = tail_ref[0]

    def finish(s, carry):
        hs = jnp.concatenate([b_ref[g, pl.ds(s, rows, stride=batch), :] for g in range(D_LRU // LANES)], axis=1)
        y = hs * jax.nn.gelu(xg_ref[s, :, D_LRU:])
        o_ref[s] = _rms(y, g_ref[...]).astype(BF16)
        return carry

    lax.fori_loop(0, batch, finish, 0)


def _lru(xg, conv_w, conv_b, wa_bd, b_a, wi_bd, b_i, lam, g_lru, h0, tail0, batch, seq, tt):
    ntt = seq // tt
    ng = D_LRU // LANES
    vec = pl.BlockSpec((1, D_LRU), lambda t: (0, 0))
    wspec = pl.BlockSpec((D_LRU // LRU_GROUP, LRU_GROUP, LRU_GROUP), lambda t: (0, 0, 0))
    out, h_last, tail = pl.pallas_call(
        _lru_kernel,
        grid=(ntt,),
        in_specs=[
            pl.BlockSpec((batch, tt, 2 * D_LRU), lambda t: (0, t, 0)),
            pl.BlockSpec((CONV_WIDTH, D_LRU), lambda t: (0, 0)),
            vec, wspec, vec, wspec, vec, vec, vec, vec,
            pl.BlockSpec((8, D_LRU), lambda t: (0, 0)),
        ],
        out_specs=[
            pl.BlockSpec((batch, tt, D_LRU), lambda t: (0, t, 0)),
            pl.BlockSpec((1, D_LRU), lambda t: (0, 0)),
            pl.BlockSpec((8, D_LRU), lambda t: (0, 0)),
        ],
        out_shape=[
            jax.ShapeDtypeStruct((batch, seq, D_LRU), BF16),
            jax.ShapeDtypeStruct((1, D_LRU), F32),
            jax.ShapeDtypeStruct((8, D_LRU), F32),
        ],
        scratch_shapes=[
            pltpu.VMEM((8 + tt, D_LRU), F32),
            pltpu.VMEM((batch, 8, D_LRU), F32),
            pltpu.VMEM((ng, tt * batch, LANES), F32),
            pltpu.VMEM((ng, tt * batch, LANES), F32),
            pltpu.VMEM((ng, batch, LANES), F32),
        ],
        compiler_params=_cparams(("arbitrary",)),
        name="rglru",
    )(xg.reshape(batch, seq, 2 * D_LRU), conv_w, conv_b, wa_bd, b_a, wi_bd, b_i, lam, g_lru, h0, tail0)
    return out.reshape(batch * seq, D_LRU), h_last, tail


def _outproj_kernel(a_ref, l_ref, wo_ref, bo_ref, x_ref, gf_ref, wr_ref, br_ref, tri_ref,
                    h_ref, xp_ref, ids_ref, wts_ref, rank_ref, cnt_ref, carry_ref):
    m = pl.program_id(0)
    tm = a_ref.shape[0]

    @pl.when(m == 0)
    def _():
        carry_ref[...] = jnp.zeros_like(carry_ref)

    h = (jnp.dot(a_ref[...], wo_ref[:D_ATTN, :], preferred_element_type=F32)
         + jnp.dot(l_ref[...], wo_ref[D_ATTN:, :], preferred_element_type=F32)
         + bo_ref[...] + x_ref[...])
    h_ref[...] = h
    xn = _rms(h, gf_ref[...]).astype(BF16)
    half = xn.shape[1] // 2
    lo = pltpu.bitcast(xn[:, :half].astype(F32), jnp.uint32)
    hi = pltpu.bitcast(xn[:, half:].astype(F32), jnp.uint32)
    packed = (hi & jnp.uint32(0xFFFF0000)) | (lo >> 16)
    ns = half // LANES
    for s in range(ns):
        xp_ref[pl.ds(s, tm, stride=ns), :] = packed[:, s * LANES:(s + 1) * LANES]

    logits = lax.dot_general(wr_ref[...], xn, (((1,), (1,)), ((), ())), preferred_element_type=F32)
    logits = logits + br_ref[...]
    eidx = lax.broadcasted_iota(jnp.int32, (N_EXPERTS, tm), 0)
    work = logits
    vals, sels = [], []
    for k in range(TOP_K):
        v = jnp.max(work, axis=0, keepdims=True)
        idx = jnp.min(jnp.where(work == v, eidx, N_EXPERTS), axis=0, keepdims=True)
        sel = eidx == idx
        ids_ref[k:k + 1, :] = idx
        vals.append(v)
        sels.append(sel)
        work = jnp.where(sel, -jnp.inf, work)
    es = [jnp.exp(v - vals[0]) for v in vals]
    den = es[0] + es[1] + es[2] + es[3]
    for k in range(TOP_K):
        wts_ref[k:k + 1, :] = es[k] / den
    cnt = sum(s.astype(F32) for s in sels)
    incl = jnp.dot(cnt.astype(BF16), tri_ref[...], preferred_element_type=F32)
    before = incl - cnt + carry_ref[:, 0:1]
    for k in range(TOP_K):
        rk = jnp.sum(jnp.where(sels[k], before, 0.0), axis=0, keepdims=True)
        rank_ref[k:k + 1, :] = rk.astype(jnp.int32)
    carry_ref[...] = carry_ref[...] + incl[:, tm - 1:tm]
    cnt_ref[...] = carry_ref[...]


def _outproj(attn_n, lru_n, wo_bf, b_out, x2d, g_ffn, wr_t, br, tri, tm):
    rows, d = x2d.shape
    const = lambda shape: pl.BlockSpec(shape, lambda m: tuple(0 for _ in shape))
    return pl.pallas_call(
        _outproj_kernel,
        grid=(rows // tm,),
        in_specs=[
            pl.BlockSpec((tm, D_ATTN), lambda m: (m, 0)),
            pl.BlockSpec((tm, D_LRU), lambda m: (m, 0)),
            const((D_ATTN + D_LRU, d)), const((1, d)),
            pl.BlockSpec((tm, d), lambda m: (m, 0)),
            const((1, d)), const((N_EXPERTS, d)), const((N_EXPERTS, 1)), const((tm, tm)),
        ],
        out_specs=[
            pl.BlockSpec((tm, d), lambda m: (m, 0)),
            pl.BlockSpec((tm * (d // 2 // LANES), LANES), lambda m: (m, 0)),
            pl.BlockSpec((TOP_K, tm), lambda m: (0, m)),
            pl.BlockSpec((TOP_K, tm), lambda m: (0, m)),
            pl.BlockSpec((TOP_K, tm), lambda m: (0, m)),
            const((N_EXPERTS, LANES)),
        ],
        out_shape=[
            jax.ShapeDtypeStruct((rows, d), F32),
            jax.ShapeDtypeStruct((rows * (d // 2 // LANES), LANES), jnp.uint32),
            jax.ShapeDtypeStruct((TOP_K, rows), jnp.int32),
            jax.ShapeDtypeStruct((TOP_K, rows), F32),
            jax.ShapeDtypeStruct((TOP_K, rows), jnp.int32),
            jax.ShapeDtypeStruct((N_EXPERTS, LANES), F32),
        ],
        scratch_shapes=[pltpu.VMEM((N_EXPERTS, LANES), F32)],
        compiler_params=_cparams(("arbitrary",)),
        name="outproj_router",
    )(attn_n, lru_n, wo_bf, b_out, x2d, g_ffn, wr_t, br, tri)


def _zero_fill_rows(zero_ref, dst_hbm, first_row, n_blocks, sem):
    blk = zero_ref.shape[0]
    per = blk // SUB_ROWS

    def copy(j):
        row = pl.multiple_of((first_row + j * SUB_ROWS) * per, blk)
        return pltpu.make_async_copy(zero_ref, dst_hbm.at[pl.ds(row, blk)], sem)

    def start(j, c):
        copy(j).start()
        return c

    def wait(j, c):
        copy(j).wait()
        return c

    lax.fori_loop(0, n_blocks, start, 0)
    lax.fori_loop(0, n_blocks, wait, 0)


def _dispatch_kernel(tails_ref, pos_hbm, xp_ref, xs_hbm, pos_smem, zero_ref, sem, psem):
    i = pl.program_id(0)
    ns = TOK_WORDS // LANES
    tm = xp_ref.shape[0] // ns
    n = TOP_K * tm
    pcopy = pltpu.make_async_copy(pos_hbm.at[pl.ds(i * n, n)], pos_smem, psem)
    pcopy.start()

    @pl.when(i == 0)
    def _():
        zero_ref[...] = jnp.zeros_like(zero_ref)
        for e in range(N_EXPERTS):
            _zero_fill_rows(zero_ref, xs_hbm, tails_ref[e], 1, sem)
        _zero_fill_rows(zero_ref, xs_hbm, tails_ref[N_EXPERTS], tails_ref[N_EXPERTS + 1], sem)

    pcopy.wait()

    for k in range(TOP_K):
        def issue(tt, c):
            for u in range(ROW_DMA_UNROLL):
                t = tt * ROW_DMA_UNROLL + u
                src = xp_ref.at[pl.ds(pl.multiple_of(t * ns, ns), ns)]
                dst = xs_hbm.at[pl.ds(pl.multiple_of(pos_smem[k * tm + t] * ns, ns), ns)]
                pltpu.make_async_copy(src, dst, sem).start(priority=u % 2)
            return c

        lax.fori_loop(0, tm // ROW_DMA_UNROLL, issue, 0)
    for _ in range(TOP_K):
        pltpu.make_async_copy(xp_ref, xs_hbm.at[pl.ds(0, tm * ns)], sem).wait()


def _dispatch(tails, pos_tiles, xp, p_alloc, tm):
    ns = TOK_WORDS // LANES
    rows = xp.shape[0] // ns
    return pl.pallas_call(
        _dispatch_kernel,
        grid_spec=pltpu.PrefetchScalarGridSpec(
            num_scalar_prefetch=1,
            grid=(rows // tm,),
            in_specs=[
                pl.BlockSpec(memory_space=pl.ANY),
                pl.BlockSpec((tm * ns, LANES), lambda i, s: (i, 0)),
            ],
            out_specs=pl.BlockSpec(memory_space=pl.ANY),
            scratch_shapes=[
                pltpu.SMEM((TOP_K * tm,), jnp.int32),
                pltpu.VMEM((SUB_ROWS * ns, LANES), jnp.uint32),
                pltpu.SemaphoreType.DMA,
                pltpu.SemaphoreType.DMA,
            ],
        ),
        out_shape=jax.ShapeDtypeStruct((p_alloc * ns, LANES), jnp.uint32),
        compiler_params=_cparams(("arbitrary",)),
        name="dispatch",
    )(tails, pos_tiles, xp)


def _chunk_dma(cs_ref, cn_ref, src_hbm, dst_ref, per, sem):
    blk = SUB_ROWS * per
    nch = cs_ref.shape[0]

    def copy(ci, j):
        src = src_hbm.at[pl.ds(pl.multiple_of((cs_ref[ci] + j * SUB_ROWS) * per, blk), blk)]
        return pltpu.make_async_copy(src, dst_ref.at[pl.ds(pl.multiple_of(j * blk, blk), blk)], sem)

    def run(ci, wait):
        cc = jnp.minimum(ci, nch - 1)
        n = jnp.where(ci < nch, cn_ref[cc], 0)

        def body(j, c):
            if wait:
                copy(cc, j).wait()
            else:
                copy(cc, j).start()
            return c

        lax.fori_loop(0, n, body, 0)

    return (lambda ci: run(ci, False)), (lambda ci: run(ci, True))


def _tile_loop(nsub, prepare, compute, commit):
    def run(tiles):
        prepare(tiles)
        for t in tiles:
            compute(*t)
        commit(tiles)

    n8 = nsub // 8
    rem = nsub % 8
    base = n8 * 8 * SUB_ROWS
    r2 = base + (rem // 4) * 4 * SUB_ROWS
    r1 = r2 + ((rem % 4) // 2) * 2 * SUB_ROWS

    @pl.when(rem % 4 >= 2)
    def _():
        run([(pl.multiple_of(r2, SUB_ROWS), 2 * SUB_ROWS, 5)])

    @pl.when(rem % 2 == 1)
    def _():
        run([(pl.multiple_of(r1, SUB_ROWS), SUB_ROWS, 6)])

    @pl.when(rem >= 4)
    def _():
        run([(pl.multiple_of(base, 4 * SUB_ROWS), 4 * SUB_ROWS, 4)])

    def pair(j, c):
        r = pl.multiple_of(j * 8 * SUB_ROWS, 8 * SUB_ROWS)
        slot = (j % 2) * 2
        run([(r, 4 * SUB_ROWS, slot), (r + 4 * SUB_ROWS, 4 * SUB_ROWS, slot + 1)])
        return c

    lax.fori_loop(0, n8, pair, 0)


def _moe_up_kernel(ce_ref, cs_ref, cn_ref, tail_ref, xs_hbm, wg_ref, bg_ref, wu_ref, bu_ref, act_hbm,
                   land_ref, xb_ref, wgu_ref, ab_ref, zero_ref, pend_ref, sem, osem):
    c = pl.program_id(0)
    f = pl.program_id(1)
    nsub = cn_ref[c]
    start = pl.multiple_of(cs_ref[c], SUB_ROWS)
    tf = wg_ref.shape[2]
    ns = TOK_WORDS // LANES
    start_chunk, wait_chunk = _chunk_dma(cs_ref, cn_ref, xs_hbm, land_ref, ns, sem)

    @pl.when(jnp.logical_and(c == 0, f == 0))
    def _():
        zero_ref[...] = jnp.zeros_like(zero_ref)
        _zero_fill_rows(zero_ref, act_hbm, tail_ref[0], tail_ref[1], osem.at[0])
        for slot in range(len(SLOT_ROWS)):
            pend_ref[slot] = 0
        start_chunk(c)

    @pl.when(jnp.logical_and(f == 0, nsub > 0))
    def _():
        wait_chunk(c)

        def unpack(j, carry):
            r = pl.multiple_of(j * SUB_ROWS, SUB_ROWS)
            for s in range(ns):
                w = land_ref[pl.ds(r * ns + s, SUB_ROWS, stride=ns), :]
                xb_ref[pl.ds(r, SUB_ROWS), s * LANES:(s + 1) * LANES] = pltpu.bitcast(w << 16, F32).astype(BF16)
                xb_ref[pl.ds(r, SUB_ROWS), TOK_WORDS + s * LANES:TOK_WORDS + (s + 1) * LANES] = (
                    pltpu.bitcast(w & jnp.uint32(0xFFFF0000), F32).astype(BF16))
            return carry

        lax.fori_loop(0, nsub, unpack, 0)
        start_chunk(c + 1)

    def out_copy(r, rows, slot):
        return pltpu.make_async_copy(
            ab_ref.at[slot, pl.ds(0, rows)],
            act_hbm.at[pl.ds(pl.multiple_of(start + r, SUB_ROWS), rows), pl.ds(pl.multiple_of(f * tf, tf), tf)],
            osem.at[slot])

    @pl.when(nsub > 0)
    def _():
        wgu_ref[:, :tf] = wg_ref[0].astype(BF16)
        wgu_ref[:, tf:] = wu_ref[0].astype(BF16)
        bg = bg_ref[0]
        bu = bu_ref[0]

        def prepare(tiles):
            for _, rows, slot in tiles:
                @pl.when(pend_ref[slot] == 1)
                def _():
                    out_copy(0, rows, slot).wait()

        def compute(r, rows, slot):
            z = jnp.dot(xb_ref[pl.ds(r, rows), :], wgu_ref[...], preferred_element_type=F32)
            g = z[:, :tf] + bg
            u = z[:, tf:] + bu
            g = jnp.minimum(g, SWIGLU_LIMIT)
            u = jnp.clip(u, -SWIGLU_LIMIT, SWIGLU_LIMIT)
            ab_ref[slot, pl.ds(0, rows), :] = (g * jax.nn.sigmoid(SWIGLU_ALPHA * g) * (u + 1.0)).astype(BF16)

        def commit(tiles):
            for r, rows, slot in tiles:
                out_copy(r, rows, slot).start()
                pend_ref[slot] = 1

        _tile_loop(nsub, prepare, compute, commit)

    @pl.when(jnp.logical_and(c == pl.num_programs(0) - 1, f == pl.num_programs(1) - 1))
    def _():
        for slot, rows in enumerate(SLOT_ROWS):
            @pl.when(pend_ref[slot] == 1)
            def _():
                out_copy(0, rows, slot).wait()


def _moe_up(ch_e, ch_start, ch_nsub, tail, xs, w_gate, b_gate, w_up, b_up):
    ns = TOK_WORDS // LANES
    p_alloc = xs.shape[0] // ns
    d = 2 * TOK_WORDS
    dff = w_gate.shape[2]
    tf = MOE_TF
    nch = ch_e.shape[0]
    nf = dff // tf
    wmap = lambda c, f, e, s, n, t: (e[c], 0, jnp.where(n[c] > 0, f, nf - 1))
    return pl.pallas_call(
        _moe_up_kernel,
        grid_spec=pltpu.PrefetchScalarGridSpec(
            num_scalar_prefetch=4,
            grid=(nch, nf),
            in_specs=[
                pl.BlockSpec(memory_space=pl.ANY),
                pl.BlockSpec((1, d, tf), wmap),
                pl.BlockSpec((1, 1, tf), wmap),
                pl.BlockSpec((1, d, tf), wmap),
                pl.BlockSpec((1, 1, tf), wmap),
            ],
            out_specs=pl.BlockSpec(memory_space=pl.ANY),
            scratch_shapes=[
                pltpu.VMEM((MOE_TM * ns, LANES), jnp.uint32),
                pltpu.VMEM((MOE_TM, d), BF16),
                pltpu.VMEM((d, 2 * tf), BF16),
                pltpu.VMEM((len(SLOT_ROWS), 4 * SUB_ROWS, tf), BF16),
                pltpu.VMEM((SUB_ROWS, dff), BF16),
                pltpu.SMEM((len(SLOT_ROWS),), jnp.int32),
                pltpu.SemaphoreType.DMA,
                pltpu.SemaphoreType.DMA((len(SLOT_ROWS),)),
            ],
        ),
        out_shape=jax.ShapeDtypeStruct((p_alloc, dff), BF16),
        compiler_params=_cparams(("arbitrary", "arbitrary")),
        name="moe_up",
    )(ch_e, ch_start, ch_nsub, tail, xs, w_gate, b_gate.reshape(N_EXPERTS, 1, dff), w_up,
      b_up.reshape(N_EXPERTS, 1, dff))


def _moe_down_kernel(ce_ref, cs_ref, cn_ref, tail_ref, act_hbm, wd_ref, bd_ref, ys_hbm,
                     xa_ref, xb_ref, wdb_ref, yb_ref, zero_ref, pend_ref, sem, osem):
    c = pl.program_id(0)
    f = pl.program_id(1)
    nf = pl.num_programs(1)
    nsub = cn_ref[c]
    start = pl.multiple_of(cs_ref[c], SUB_ROWS)
    tn = wd_ref.shape[2]
    ns = TOK_WORDS // LANES
    nw = tn // 2 // LANES
    copies = [_chunk_dma(cs_ref, cn_ref, act_hbm, ref, 1, sem.at[i]) for i, ref in enumerate((xa_ref, xb_ref))]

    def on_parity(ci, fn):
        for p in range(2):
            @pl.when(ci % 2 == p)
            def _():
                fn(p)

    @pl.when(jnp.logical_and(c == 0, f == 0))
    def _():
        zero_ref[...] = jnp.zeros_like(zero_ref)
        _zero_fill_rows(zero_ref, ys_hbm, tail_ref[0], tail_ref[1], osem)
        for kind in range(3):
            pend_ref[kind] = 0
        copies[0][0](c)

    @pl.when(jnp.logical_and(f == 0, nsub > 0))
    def _():
        on_parity(c, lambda p: copies[p][1](c))
        on_parity(c + 1, lambda p: copies[p][0](c + 1))

    def out_copy(r, rows):
        src = yb_ref.at[pl.ds(pl.multiple_of(r * ns, SUB_ROWS * ns), rows * ns)]
        dst = ys_hbm.at[pl.ds(pl.multiple_of((start + r) * ns, SUB_ROWS * ns), rows * ns)]
        return pltpu.make_async_copy(src, dst, osem)

    def drain():
        for kind, rows in enumerate(SLOT_ROWS[-3:]):
            def wait(j, carry):
                out_copy(0, rows).wait()
                return carry

            lax.fori_loop(0, pend_ref[kind], wait, 0)
            pend_ref[kind] = 0

    def compute(x_ref):
        wdb_ref[...] = wd_ref[0].astype(BF16)
        bd = bd_ref[0]
        last = f == nf - 1

        @pl.when(f == 0)
        def _():
            drain()

        def compute(r, rows, slot):
            y = jnp.dot(x_ref[pl.ds(r, rows), :], wdb_ref[...], preferred_element_type=F32) + bd
            lo = pltpu.bitcast(y[:, :tn // 2].astype(BF16).astype(F32), jnp.uint32)
            hi = pltpu.bitcast(y[:, tn // 2:].astype(BF16).astype(F32), jnp.uint32)
            packed = (hi & jnp.uint32(0xFFFF0000)) | (lo >> 16)
            for s in range(nw):
                yb_ref[pl.ds(r * ns + f * nw + s, rows, stride=ns), :] = packed[:, s * LANES:(s + 1) * LANES]

        def commit(tiles):
            @pl.when(last)
            def _():
                for r, rows, slot in tiles:
                    out_copy(r, rows).start()
                    kind = SLOT_ROWS[-3:].index(rows)
                    pend_ref[kind] = pend_ref[kind] + 1

        _tile_loop(nsub, lambda tiles: None, compute, commit)

    @pl.when(nsub > 0)
    def _():
        on_parity(c, lambda p: compute((xa_ref, xb_ref)[p]))

    @pl.when(jnp.logical_and(c == pl.num_programs(0) - 1, f == nf - 1))
    def _():
        drain()


def _moe_down(ch_e, ch_start, ch_nsub, tail, act, w_down, b_down):
    p_alloc, dff = act.shape
    d = w_down.shape[2]
    ns = TOK_WORDS // LANES
    tn = MOE_TN
    nch = ch_e.shape[0]
    nn = d // tn
    wmap = lambda c, f, e, s, n, t: (e[c], 0, jnp.where(n[c] > 0, f, nn - 1))
    return pl.pallas_call(
        _moe_down_kernel,
        grid_spec=pltpu.PrefetchScalarGridSpec(
            num_scalar_prefetch=4,
            grid=(nch, nn),
            in_specs=[
                pl.BlockSpec(memory_space=pl.ANY),
                pl.BlockSpec((1, dff, tn), wmap),
                pl.BlockSpec((1, 1, tn), wmap),
            ],
            out_specs=pl.BlockSpec(memory_space=pl.ANY),
            scratch_shapes=[
                pltpu.VMEM((MOE_TM, dff), BF16),
                pltpu.VMEM((MOE_TM, dff), BF16),
                pltpu.VMEM((dff, tn), BF16),
                pltpu.VMEM((MOE_TM * ns, LANES), jnp.uint32),
                pltpu.VMEM((SUB_ROWS * ns, LANES), jnp.uint32),
                pltpu.SMEM((3,), jnp.int32),
                pltpu.SemaphoreType.DMA((2,)),
                pltpu.SemaphoreType.DMA,
            ],
        ),
        out_shape=jax.ShapeDtypeStruct((p_alloc * ns, LANES), jnp.uint32),
        compiler_params=_cparams(("arbitrary", "arbitrary")),
        name="moe_down",
    )(ch_e, ch_start, ch_nsub, tail, act, w_down, b_down.reshape(N_EXPERTS, 1, d))


def _combine_kernel(pos_hbm, ys_hbm, h_ref, w_ref, g_ref, o_ref, pos_smem, yb_ref, sem, psem):
    i = pl.program_id(0)
    tm, d = h_ref.shape
    n = TOP_K * tm
    ns = TOK_WORDS // LANES

    def fetch(tile, slot):
        pcopy = pltpu.make_async_copy(pos_hbm.at[pl.ds(tile * n, n)], pos_smem.at[pl.ds(slot * n, n)], psem)
        pcopy.start()
        pcopy.wait()

        def issue(jj, c):
            for u in range(ROW_DMA_UNROLL):
                j = jj * ROW_DMA_UNROLL + u
                src = ys_hbm.at[pl.ds(pl.multiple_of(pos_smem[slot * n + j] * ns, ns), ns)]
                dst = yb_ref.at[slot, pl.ds(pl.multiple_of(j * ns, ns), ns)]
                pltpu.make_async_copy(src, dst, sem.at[slot]).start(priority=u % 2)
            return c

        lax.fori_loop(0, n // ROW_DMA_UNROLL, issue, 0)

    @pl.when(i == 0)
    def _():
        fetch(i, 0)

    @pl.when(i + 1 < pl.num_programs(0))
    def _():
        for p in range(2):
            @pl.when((i + 1) % 2 == p)
            def _():
                fetch(i + 1, p)

    def reduce(slot):
        pltpu.make_async_copy(ys_hbm.at[pl.ds(0, n * ns)], yb_ref.at[slot], sem.at[slot]).wait()
        wb = [jnp.broadcast_to(w_ref[:, k:k + 1], (tm, LANES)) for k in range(TOP_K)]
        ssq = jnp.zeros((tm, 1), F32)
        nw = MOE_TN // 2 // LANES
        for s in range(ns):
            c0 = (s // nw) * MOE_TN + (s % nw) * LANES
            lo_cols = slice(c0, c0 + LANES)
            hi_cols = slice(c0 + MOE_TN // 2, c0 + MOE_TN // 2 + LANES)
            acc_lo = h_ref[:, lo_cols]
            acc_hi = h_ref[:, hi_cols]
            for k in range(TOP_K):
                w = yb_ref[slot, pl.ds(k * tm * ns + s, tm, stride=ns), :]
                acc_lo = acc_lo + wb[k] * pltpu.bitcast(w << 16, F32)
                acc_hi = acc_hi + wb[k] * pltpu.bitcast(w & jnp.uint32(0xFFFF0000), F32)
            o_ref[:, lo_cols] = acc_lo
            o_ref[:, hi_cols] = acc_hi
            ssq = ssq + jnp.sum(acc_lo * acc_lo + acc_hi * acc_hi, axis=-1, keepdims=True)
        o_ref[...] = o_ref[...] * lax.rsqrt(ssq * (1.0 / d) + NORM_EPS) * g_ref[...]

    for p in range(2):
        @pl.when(i % 2 == p)
        def _():
            reduce(p)


def _combine(pos_tiles, ys, h1, wts_t, g_final, tm):
    rows, d = h1.shape
    ns = TOK_WORDS // LANES
    return pl.pallas_call(
        _combine_kernel,
        grid=(rows // tm,),
        in_specs=[
            pl.BlockSpec(memory_space=pl.ANY),
            pl.BlockSpec(memory_space=pl.ANY),
            pl.BlockSpec((tm, d), lambda i: (i, 0)),
            pl.BlockSpec((tm, TOP_K), lambda i: (i, 0)),
            pl.BlockSpec((1, d), lambda i: (0, 0)),
        ],
        out_specs=pl.BlockSpec((tm, d), lambda i: (i, 0)),
        out_shape=jax.ShapeDtypeStruct((rows, d), F32),
        scratch_shapes=[
            pltpu.SMEM((2 * TOP_K * tm,), jnp.int32),
            pltpu.VMEM((2, TOP_K * tm * ns, LANES), jnp.uint32),
            pltpu.SemaphoreType.DMA((2,)),
            pltpu.SemaphoreType.DMA,
        ],
        compiler_params=_cparams(("arbitrary",)),
        name="combine",
    )(pos_tiles, ys, h1, wts_t, g_final)


def _rope_tables(n_pos):
    half = HEAD_DIM // 2
    inv = 1.0 / (ROPE_THETA ** (jnp.arange(half, dtype=F32) / half))
    ang = jnp.arange(n_pos, dtype=F32)[:, None] * inv[None, :]
    cos = jnp.tile(jnp.cos(ang), (1, LANES // half))
    sin = jnp.tile(jnp.concatenate([-jnp.sin(ang), jnp.sin(ang)], axis=1), (1, LANES // HEAD_DIM))
    return cos, sin


def _block_diag(w):
    per = LRU_GROUP // LRU_BLOCK
    w4 = w.reshape(D_LRU // LRU_GROUP, per, LRU_BLOCK, LRU_BLOCK)
    eye = jnp.eye(per, dtype=w.dtype)
    bd = jnp.einsum("gpcd,pq->gpcqd", w4, eye)
    return bd.reshape(D_LRU // LRU_GROUP, LRU_GROUP, LRU_GROUP).astype(BF16)


def _tile_positions(pos, tm):
    rows = pos.shape[1]
    return pos.reshape(TOP_K, rows // tm, tm).transpose(1, 0, 2).reshape(-1)


def _chunk_schedule(counts, n_rows):
    aligned = ((counts + SUB_ROWS - 1) // SUB_ROWS) * SUB_ROWS
    offs = jnp.cumsum(aligned) - aligned
    n_ch = (aligned + MOE_TM - 1) // MOE_TM
    cum = jnp.cumsum(n_ch)
    total = cum[-1]
    nch_max = N_EXPERTS + (n_rows + N_EXPERTS * SUB_ROWS) // MOE_TM
    c = jnp.arange(nch_max, dtype=jnp.int32)
    cc = jnp.minimum(c, total - 1)
    e = jnp.searchsorted(cum, cc, side="right").astype(jnp.int32)
    j = cc - (cum[e] - n_ch[e])
    start = offs[e] + j * MOE_TM
    nsub = jnp.where(c < total, jnp.minimum(MOE_TM, aligned[e] - j * MOE_TM) // SUB_ROWS, 0)
    tails = offs + (counts // SUB_ROWS) * SUB_ROWS
    used = jnp.sum(aligned)
    tail = jnp.stack([used, (n_rows + N_EXPERTS * SUB_ROWS - used) // SUB_ROWS])
    tails = jnp.concatenate([tails, tail])
    return (offs, e, start.astype(jnp.int32), nsub.astype(jnp.int32), tails.astype(jnp.int32),
            tail.astype(jnp.int32))


def kernel(x, meta_tokens, norm_mix, w_in, b_in, sinks, conv_w, conv_b, w_a, b_a, w_i, b_i, lru_lambda,
           g_attn_out, g_lru_out, w_out, b_out, norm_ffn, w_router, b_router, w_gate, b_gate, w_up, b_up,
           w_down, b_down, final_norm):
    batch, seq, d = x.shape
    rows = batch * seq
    x2d = x.reshape(rows, d)
    row = lambda v: v.reshape(1, -1)

    w_in_bf = w_in[0].astype(BF16)
    w_out_bf = w_out[0].astype(BF16)
    cos, sin = _rope_tables(N_META + seq)
    wa_bd, wi_bd = _block_diag(w_a[0]), _block_diag(w_i[0])
    lru_args = (conv_w[0], row(conv_b[0]), wa_bd, row(b_a[0]), wi_bd, row(b_i[0]), row(lru_lambda[0]),
                row(g_lru_out[0]))

    _, kvm, xgm = _inproj(meta_tokens, row(norm_mix[0]), w_in_bf, row(b_in[0]), cos[:N_META], sin[:N_META],
                          N_META, 1)
    _, h0, tail0 = _lru(xgm, *lru_args, jnp.zeros((1, D_LRU), F32), jnp.zeros((8, D_LRU), F32), 1, N_META,
                        N_META)

    tm = 512
    q, kv, xg = _inproj(x2d, row(norm_mix[0]), w_in_bf, row(b_in[0]), cos[N_META:], sin[N_META:], tm,
                        seq // tm)
    kvm = jnp.pad(kvm, ((0, BLOCK - N_META), (0, 0)))
    attn_n = _attention(sinks[0], q, kv, kvm, row(g_attn_out[0]), batch, seq)
    lru_n, _, _ = _lru(xg, *lru_args, h0, tail0, batch, seq, 128)

    tri = (jnp.arange(tm)[:, None] <= jnp.arange(tm)[None, :]).astype(BF16)
    h1, xp, ids, wts, rank, cnt = _outproj(
        attn_n, lru_n, w_out_bf, row(b_out[0]), x2d, row(norm_ffn[0]), w_router[0].T.astype(BF16),
        b_router[0].reshape(N_EXPERTS, 1), tri, tm)

    counts = cnt[:, 0].astype(jnp.int32)
    n_rows = rows * TOP_K
    offs, ch_e, ch_start, ch_nsub, tails, tail = _chunk_schedule(counts, n_rows)
    onehot = ids[..., None] == jnp.arange(N_EXPERTS, dtype=jnp.int32)
    pos = jnp.sum(jnp.where(onehot, offs, 0), axis=-1) + rank
    p_alloc = n_rows + N_EXPERTS * SUB_ROWS

    tmd = 512
    xs = _dispatch(tails, _tile_positions(pos, tmd), xp, p_alloc, tmd)
    act = _moe_up(ch_e, ch_start, ch_nsub, tail, xs, w_gate[0], b_gate[0], w_up[0], b_up[0])
    ys = _moe_down(ch_e, ch_start, ch_nsub, tail, act, w_down[0], b_down[0])
    tmc = 256
    out = _combine(_tile_positions(pos, tmc), ys, h1, wts.T, row(final_norm), tmc)
    return out.reshape(batch, seq, d)
```

```python
import jax
import jax.numpy as jnp
from jax import lax
from jax.experimental import pallas as pl
from jax.experimental.pallas import tpu as pltpu

F32 = jnp.float32
BF16 = jnp.bfloat16

N_META = 16
HEAD_DIM = 64
N_Q_HEADS = 16
N_KV_HEADS = 4
GROUP = N_Q_HEADS // N_KV_HEADS
D_ATTN = N_Q_HEADS * HEAD_DIM
D_KV = N_KV_HEADS * HEAD_DIM
KV_COLS = 4 * D_KV
BLOCK = 128
ROPE_THETA = 10000.0
D_LRU = 1024
LRU_BLOCK = 64
LRU_GROUP = 256
CONV_WIDTH = 4
LRU_C = 8.0
N_EXPERTS = 32
TOP_K = 4
SWIGLU_LIMIT = 7.0
SWIGLU_ALPHA = 1.702
NORM_EPS = 1e-5
NEG_INF = -1e30

LANES = 128
D_MODEL = 2048
TOK_WORDS = D_MODEL // 2
SUB_ROWS = 128
MOE_TM = 2304
MOE_TF = 512
MOE_TN = 1024
ROW_DMA_UNROLL = 8
SLOT_ROWS = (4 * SUB_ROWS,) * 5 + (2 * SUB_ROWS, SUB_ROWS)
VMEM_LIMIT = 56 * 1024 * 1024


def _cparams(sem):
    return pltpu.CompilerParams(dimension_semantics=sem, vmem_limit_bytes=VMEM_LIMIT)


def _mult(x, m):
    return x if isinstance(x, int) else pl.multiple_of(x, m)


def _rms(x, g):
    return x * lax.rsqrt(jnp.mean(x * x, axis=-1, keepdims=True) + NORM_EPS) * g


def _inproj_kernel(x_ref, g_ref, w_ref, b_ref, cos_ref, sin_ref, q_ref, kv_ref, xg_ref):
    xn = _rms(x_ref[...], g_ref[...]).astype(BF16)
    cos = cos_ref[...]
    sin = sin_ref[...]
    lane = lax.broadcasted_iota(jnp.int32, (1, LANES), 1)
    first_half = (lane % HEAD_DIM) < (HEAD_DIM // 2)

    def rope(z):
        partner = jnp.where(first_half, pltpu.roll(z, LANES - HEAD_DIM // 2, 1),
                            pltpu.roll(z, HEAD_DIM // 2, 1))
        return z * cos + partner * sin

    cw = 512
    for c in range(w_ref.shape[1] // cw):
        z = jnp.dot(xn, w_ref[:, c * cw:(c + 1) * cw], preferred_element_type=F32)
        z = z + b_ref[:, c * cw:(c + 1) * cw]
        if c < 2:
            for j in range(cw // LANES):
                zz = rope(z[:, j * LANES:(j + 1) * LANES]) * (HEAD_DIM ** -0.5)
                q_ref[:, c * cw + j * LANES:c * cw + (j + 1) * LANES] = zz.astype(BF16)
        elif c == 2:
            low = lane < HEAD_DIM
            for j in range(2 * D_KV // LANES):
                zz = z[:, j * LANES:(j + 1) * LANES]
                zz = rope(zz) if j < D_KV // LANES else zz
                sw = pltpu.roll(zz, HEAD_DIM, 1)
                kv_ref[:, 2 * j * LANES:(2 * j + 1) * LANES] = jnp.where(low, zz, sw).astype(BF16)
                kv_ref[:, (2 * j + 1) * LANES:(2 * j + 2) * LANES] = jnp.where(low, sw, zz).astype(BF16)
        else:
            xg_ref[:, (c - 3) * cw:(c - 2) * cw] = z


def _inproj(x2d, g, w_bf, b, cos, sin, tm, pos_blocks):
    rows, d = x2d.shape
    dz = w_bf.shape[1]
    return pl.pallas_call(
        _inproj_kernel,
        grid=(rows // tm,),
        in_specs=[
            pl.BlockSpec((tm, d), lambda m: (m, 0)),
            pl.BlockSpec((1, d), lambda m: (0, 0)),
            pl.BlockSpec((d, dz), lambda m: (0, 0)),
            pl.BlockSpec((1, dz), lambda m: (0, 0)),
            pl.BlockSpec((tm, LANES), lambda m: (m % pos_blocks, 0)),
            pl.BlockSpec((tm, LANES), lambda m: (m % pos_blocks, 0)),
        ],
        out_specs=[
            pl.BlockSpec((tm, D_ATTN), lambda m: (m, 0)),
            pl.BlockSpec((tm, KV_COLS), lambda m: (m, 0)),
            pl.BlockSpec((tm, 2 * D_LRU), lambda m: (m, 0)),
        ],
        out_shape=[
            jax.ShapeDtypeStruct((rows, D_ATTN), BF16),
            jax.ShapeDtypeStruct((rows, KV_COLS), BF16),
            jax.ShapeDtypeStruct((rows, 2 * D_LRU), F32),
        ],
        compiler_params=_cparams(("arbitrary",)),
        name="inproj",
    )(x2d, g, w_bf, b, cos, sin)


def _attn_kernel(sink_ref, q_ref, kvc_ref, kvp_ref, kvm_ref, g_ref, o_ref):
    n = pl.program_id(1)
    row = lax.broadcasted_iota(jnp.int32, (GROUP * BLOCK, BLOCK), 0) % BLOCK
    col = lax.broadcasted_iota(jnp.int32, (GROUP * BLOCK, BLOCK), 1)
    in_cur = col <= row
    in_prev = jnp.logical_and(col > row, n > 0)
    is_meta = col < N_META
    low = lax.broadcasted_iota(jnp.int32, (BLOCK, LANES), 1) < HEAD_DIM
    nt = (((1,), (1,)), ((), ()))
    zero = jnp.zeros((), BF16)
    outs = []
    for h in range(N_KV_HEADS):
        ks = slice(h * LANES, (h + 1) * LANES)
        vs = slice(N_KV_HEADS * LANES + h * LANES, N_KV_HEADS * LANES + (h + 1) * LANES)
        parts = []
        for j in range(GROUP // 2):
            qg = q_ref[:, (h * GROUP // 2 + j) * LANES:(h * GROUP // 2 + j + 1) * LANES]
            parts += [jnp.where(low, qg, zero), jnp.where(low, zero, qg)]
        qs = jnp.concatenate(parts, axis=0)
        s_c = lax.dot_general(qs, kvc_ref[:, ks], nt, preferred_element_type=F32)
        s_p = lax.dot_general(qs, kvp_ref[:, ks], nt, preferred_element_type=F32)
        s_m = lax.dot_general(qs, kvm_ref[:, ks], nt, preferred_element_type=F32)
        s_b = jnp.where(in_prev, s_p, jnp.where(in_cur, s_c, NEG_INF))
        s_m = jnp.where(is_meta, s_m, NEG_INF)
        sink = jnp.concatenate(
            [jnp.full((BLOCK, 1), sink_ref[h * GROUP + g], F32) for g in range(GROUP)], axis=0)
        m = jnp.maximum(jnp.max(jnp.maximum(s_b, s_m), axis=-1, keepdims=True), sink)
        p_b = jnp.exp(s_b - m)
        p_m = jnp.exp(s_m - m)
        den = jnp.sum(p_b + p_m, axis=-1, keepdims=True) + jnp.exp(sink - m)
        r = (jnp.dot(jnp.where(in_cur, p_b, 0.0).astype(BF16), kvc_ref[:, vs], preferred_element_type=F32)
             + jnp.dot(jnp.where(in_cur, 0.0, p_b).astype(BF16), kvp_ref[:, vs], preferred_element_type=F32)
             + jnp.dot(p_m.astype(BF16), kvm_ref[:, vs], preferred_element_type=F32))
        r = r / den
        for j in range(GROUP // 2):
            outs.append(jnp.where(low, r[2 * j * BLOCK:(2 * j + 1) * BLOCK], r[(2 * j + 1) * BLOCK:(2 * j + 2) * BLOCK]))
    o_all = jnp.concatenate(outs, axis=1)
    o_ref[...] = _rms(o_all, g_ref[...]).astype(BF16)


def _attention(sinks, q, kv, kvm, g_attn, batch, seq):
    nb = seq // BLOCK
    return pl.pallas_call(
        _attn_kernel,
        grid_spec=pltpu.PrefetchScalarGridSpec(
            num_scalar_prefetch=1,
            grid=(batch, nb),
            in_specs=[
                pl.BlockSpec((BLOCK, D_ATTN), lambda b, n, s: (b * nb + n, 0)),
                pl.BlockSpec((BLOCK, KV_COLS), lambda b, n, s: (b * nb + n, 0)),
                pl.BlockSpec((BLOCK, KV_COLS), lambda b, n, s: (b * nb + jnp.maximum(n - 1, 0), 0)),
                pl.BlockSpec((BLOCK, KV_COLS), lambda b, n, s: (0, 0)),
                pl.BlockSpec((1, D_ATTN), lambda b, n, s: (0, 0)),
            ],
            out_specs=pl.BlockSpec((BLOCK, D_ATTN), lambda b, n, s: (b * nb + n, 0)),
        ),
        out_shape=jax.ShapeDtypeStruct((batch * seq, D_ATTN), BF16),
        compiler_params=_cparams(("arbitrary", "arbitrary")),
        name="attention",
    )(sinks, q, kv, kv, kvm, g_attn)


def _one_minus_sq(a, log_a):
    y = 2.0 * log_a
    p = 1.0 + y * (1.0 / 4.0)
    for k in (3, 2):
        p = 1.0 + y * p * (1.0 / k)
    return jnp.where(y > -1.0 / 64.0, -(y * p), 1.0 - a * a)


def _lru_kernel(xg_ref, cw_ref, cb_ref, wa_ref, ba_ref, wi_ref, bi_ref, lam_ref, g_ref, h0_ref, tail0_ref,
                o_ref, hout_ref, tailout_ref, ext_ref, tail_ref, a_ref, b_ref, h_ref):
    tt = pl.program_id(0)
    batch, rows = xg_ref.shape[0], xg_ref.shape[1]

    @pl.when(tt == 0)
    def _():
        for g in range(D_LRU // LANES):
            h_ref[g] = jnp.broadcast_to(h0_ref[:, g * LANES:(g + 1) * LANES], (batch, LANES))
        for s in range(batch):
            tail_ref[s] = tail0_ref[...]

    sp = jax.nn.softplus(-lam_ref[...])

    def gates(s, carry):
        ext_ref[0:8, :] = tail_ref[s]
        ext_ref[8:, :] = xg_ref[s, :, :D_LRU]
        tail_ref[s] = ext_ref[rows:rows + 8, :]
        xc = cb_ref[...] + sum(cw_ref[j:j + 1, :] * ext_ref[5 + j:5 + j + rows, :] for j in range(CONV_WIDTH))
        xcb = xc.astype(BF16)
        for c in range(D_LRU // LRU_GROUP):
            cs = slice(c * LRU_GROUP, (c + 1) * LRU_GROUP)
            r = jax.nn.sigmoid(jnp.dot(xcb[:, cs], wa_ref[c], preferred_element_type=F32) + ba_ref[:, cs])
            i = jax.nn.sigmoid(jnp.dot(xcb[:, cs], wi_ref[c], preferred_element_type=F32) + bi_ref[:, cs])
            log_a = -LRU_C * r * sp[:, cs]
            a = jnp.exp(log_a)
            b = jnp.sqrt(_one_minus_sq(a, log_a)) * i * xc[:, cs]
            for g in range(LRU_GROUP // LANES):
                lanes = slice(g * LANES, (g + 1) * LANES)
                a_ref[c * (LRU_GROUP // LANES) + g, pl.ds(s, rows, stride=batch), :] = a[:, lanes]
                b_ref[c * (LRU_GROUP // LANES) + g, pl.ds(s, rows, stride=batch), :] = b[:, lanes]
        return carry

    lax.fori_loop(0, batch, gates, 0)

    def step(t, h):
        r0 = pl.multiple_of(t * batch, batch)
        new = []
        for g in range(D_LRU // LANES):
            hg = a_ref[g, pl.ds(r0, batch), :] * h[g] + b_ref[g, pl.ds(r0, batch), :]
            b_ref[g, pl.ds(r0, batch), :] = hg
            new.append(hg)
        return tuple(new)

    h_last = lax.fori_loop(0, rows, step, tuple(h_ref[g] for g in range(D_LRU // LANES)))
    for g in range(D_LRU // LANES):
        h_ref[g] = h_last[g]
    hout_ref[...] = jnp.concatenate([h_last[g][0:1] for g in range(D_LRU // LANES)], axis=1)
    tailout_ref[...] = tail_ref[0]

    def finish(s, carry):
        hs = jnp.concatenate([b_ref[g, pl.ds(s, rows, stride=batch), :] for g in range(D_LRU // LANES)], axis=1)
        y = hs * jax.nn.gelu(xg_ref[s, :, D_LRU:])
        o_ref[s] = _rms(y, g_ref[...]).astype(BF16)
        return carry

    lax.fori_loop(0, batch, finish, 0)


def _lru(xg, conv_w, conv_b, wa_bd, b_a, wi_bd, b_i, lam, g_lru, h0, tail0, batch, seq, tt):
    ntt = seq // tt
    ng = D_LRU // LANES
    vec = pl.BlockSpec((1, D_LRU), lambda t: (0, 0))
    wspec = pl.BlockSpec((D_LRU // LRU_GROUP, LRU_GROUP, LRU_GROUP), lambda t: (0, 0, 0))
    out, h_last, tail = pl.pallas_call(
        _lru_kernel,
        grid=(ntt,),
        in_specs=[
            pl.BlockSpec((batch, tt, 2 * D_LRU), lambda t: (0, t, 0)),
            pl.BlockSpec((CONV_WIDTH, D_LRU), lambda t: (0, 0)),
            vec, wspec, vec, wspec, vec, vec, vec, vec,
            pl.BlockSpec((8, D_LRU), lambda t: (0, 0)),
        ],
        out_specs=[
            pl.BlockSpec((batch, tt, D_LRU), lambda t: (0, t, 0)),
            pl.BlockSpec((1, D_LRU), lambda t: (0, 0)),
            pl.BlockSpec((8, D_LRU), lambda t: (0, 0)),
        ],
        out_shape=[
            jax.ShapeDtypeStruct((batch, seq, D_LRU), BF16),
            jax.ShapeDtypeStruct((1, D_LRU), F32),
            jax.ShapeDtypeStruct((8, D_LRU), F32),
        ],
        scratch_shapes=[
            pltpu.VMEM((8 + tt, D_LRU), F32),
            pltpu.VMEM((batch, 8, D_LRU), F32),
            pltpu.VMEM((ng, tt * batch, LANES), F32),
            pltpu.VMEM((ng, tt * batch, LANES), F32),
            pltpu.VMEM((ng, batch, LANES), F32),
        ],
        compiler_params=_cparams(("arbitrary",)),
        name="rglru",
    )(xg.reshape(batch, seq, 2 * D_LRU), conv_w, conv_b, wa_bd, b_a, wi_bd, b_i, lam, g_lru, h0, tail0)
    return out.reshape(batch * seq, D_LRU), h_last, tail


def _outproj_kernel(a_ref, l_ref, wo_ref, bo_ref, x_ref, gf_ref, wr_ref, br_ref, tri_ref,
                    h_ref, xp_ref, ids_ref, wts_ref, rank_ref, cnt_ref, carry_ref):
    m = pl.program_id(0)
    tm = a_ref.shape[0]

    @pl.when(m == 0)
    def _():
        carry_ref[...] = jnp.zeros_like(carry_ref)

    h = (jnp.dot(a_ref[...], wo_ref[:D_ATTN, :], preferred_element_type=F32)
         + jnp.dot(l_ref[...], wo_ref[D_ATTN:, :], preferred_element_type=F32)
         + bo_ref[...] + x_ref[...])
    h_ref[...] = h
    xn = _rms(h, gf_ref[...]).astype(BF16)
    half = xn.shape[1] // 2
    lo = pltpu.bitcast(xn[:, :half].astype(F32), jnp.uint32)
    hi = pltpu.bitcast(xn[:, half:].astype(F32), jnp.uint32)
    packed = (hi & jnp.uint32(0xFFFF0000)) | (lo >> 16)
    ns = half // LANES
    for s in range(ns):
        xp_ref[pl.ds(s, tm, stride=ns), :] = packed[:, s * LANES:(s + 1) * LANES]

    logits = lax.dot_general(wr_ref[...], xn, (((1,), (1,)), ((), ())), preferred_element_type=F32)
    logits = logits + br_ref[...]
    eidx = lax.broadcasted_iota(jnp.int32, (N_EXPERTS, tm), 0)
    work = logits
    vals, sels = [], []
    for k in range(TOP_K):
        v = jnp.max(work, axis=0, keepdims=True)
        idx = jnp.min(jnp.where(work == v, eidx, N_EXPERTS), axis=0, keepdims=True)
        sel = eidx == idx
        ids_ref[k:k + 1, :] = idx
        vals.append(v)
        sels.append(sel)
        work = jnp.where(sel, -jnp.inf, work)
    es = [jnp.exp(v - vals[0]) for v in vals]
    den = es[0] + es[1] + es[2] + es[3]
    for k in range(TOP_K):
        wts_ref[k:k + 1, :] = es[k] / den
    cnt = sum(s.astype(F32) for s in sels)
    incl = jnp.dot(cnt.astype(BF16), tri_ref[...], preferred_element_type=F32)
    before = incl - cnt + carry_ref[:, 0:1]
    for k in range(TOP_K):
        rk = jnp.sum(jnp.where(sels[k], before, 0.0), axis=0, keepdims=True)
        rank_ref[k:k + 1, :] = rk.astype(jnp.int32)
    carry_ref[...] = carry_ref[...] + incl[:, tm - 1:tm]
    cnt_ref[...] = carry_ref[...]


def _outproj(attn_n, lru_n, wo_bf, b_out, x2d, g_ffn, wr_t, br, tri, tm):
    rows, d = x2d.shape
    const = lambda shape: pl.BlockSpec(shape, lambda m: tuple(0 for _ in shape))
    return pl.pallas_call(
        _outproj_kernel,
        grid=(rows // tm,),
        in_specs=[
            pl.BlockSpec((tm, D_ATTN), lambda m: (m, 0)),
            pl.BlockSpec((tm, D_LRU), lambda m: (m, 0)),
            const((D_ATTN + D_LRU, d)), const((1, d)),
            pl.BlockSpec((tm, d), lambda m: (m, 0)),
            const((1, d)), const((N_EXPERTS, d)), const((N_EXPERTS, 1)), const((tm, tm)),
        ],
        out_specs=[
            pl.BlockSpec((tm, d), lambda m: (m, 0)),
            pl.BlockSpec((tm * (d // 2 // LANES), LANES), lambda m: (m, 0)),
            pl.BlockSpec((TOP_K, tm), lambda m: (0, m)),
            pl.BlockSpec((TOP_K, tm), lambda m: (0, m)),
            pl.BlockSpec((TOP_K, tm), lambda m: (0, m)),
            const((N_EXPERTS, LANES)),
        ],
        out_shape=[
            jax.ShapeDtypeStruct((rows, d), F32),
            jax.ShapeDtypeStruct((rows * (d // 2 // LANES), LANES), jnp.uint32),
            jax.ShapeDtypeStruct((TOP_K, rows), jnp.int32),
            jax.ShapeDtypeStruct((TOP_K, rows), F32),
            jax.ShapeDtypeStruct((TOP_K, rows), jnp.int32),
            jax.ShapeDtypeStruct((N_EXPERTS, LANES), F32),
        ],
        scratch_shapes=[pltpu.VMEM((N_EXPERTS, LANES), F32)],
        compiler_params=_cparams(("arbitrary",)),
        name="outproj_router",
    )(attn_n, lru_n, wo_bf, b_out, x2d, g_ffn, wr_t, br, tri)


def _zero_fill_rows(zero_ref, dst_hbm, first_row, n_blocks, sem):
    blk = zero_ref.shape[0]
    per = blk // SUB_ROWS

    def copy(j):
        row = pl.multiple_of((first_row + j * SUB_ROWS) * per, blk)
        return pltpu.make_async_copy(zero_ref, dst_hbm.at[pl.ds(row, blk)], sem)

    def start(j, c):
        copy(j).start()
        return c

    def wait(j, c):
        copy(j).wait()
        return c

    lax.fori_loop(0, n_blocks, start, 0)
    lax.fori_loop(0, n_blocks, wait, 0)


def _dispatch_kernel(tails_ref, pos_hbm, xp_ref, xs_hbm, pos_smem, zero_ref, sem, psem):
    i = pl.program_id(0)
    ns = TOK_WORDS // LANES
    tm = xp_ref.shape[0] // ns
    n = TOP_K * tm
    pcopy = pltpu.make_async_copy(pos_hbm.at[pl.ds(i * n, n)], pos_smem, psem)
    pcopy.start()

    @pl.when(i == 0)
    def _():
        zero_ref[...] = jnp.zeros_like(zero_ref)
        for e in range(N_EXPERTS):
            _zero_fill_rows(zero_ref, xs_hbm, tails_ref[e], 1, sem)
        _zero_fill_rows(zero_ref, xs_hbm, tails_ref[N_EXPERTS], tails_ref[N_EXPERTS + 1], sem)

    pcopy.wait()

    for k in range(TOP_K):
        def issue(tt, c):
            for u in range(ROW_DMA_UNROLL):
                t = tt * ROW_DMA_UNROLL + u
                src = xp_ref.at[pl.ds(pl.multiple_of(t * ns, ns), ns)]
                dst = xs_hbm.at[pl.ds(pl.multiple_of(pos_smem[k * tm + t] * ns, ns), ns)]
                pltpu.make_async_copy(src, dst, sem).start(priority=u % 2)
            return c

        lax.fori_loop(0, tm // ROW_DMA_UNROLL, issue, 0)
    for _ in range(TOP_K):
        pltpu.make_async_copy(xp_ref, xs_hbm.at[pl.ds(0, tm * ns)], sem).wait()


def _dispatch(tails, pos_tiles, xp, p_alloc, tm):
    ns = TOK_WORDS // LANES
    rows = xp.shape[0] // ns
    return pl.pallas_call(
        _dispatch_kernel,
        grid_spec=pltpu.PrefetchScalarGridSpec(
            num_scalar_prefetch=1,
            grid=(rows // tm,),
            in_specs=[
                pl.BlockSpec(memory_space=pl.ANY),
                pl.BlockSpec((tm * ns, LANES), lambda i, s: (i, 0)),
            ],
            out_specs=pl.BlockSpec(memory_space=pl.ANY),
            scratch_shapes=[
                pltpu.SMEM((TOP_K * tm,), jnp.int32),
                pltpu.VMEM((SUB_ROWS * ns, LANES), jnp.uint32),
                pltpu.SemaphoreType.DMA,
                pltpu.SemaphoreType.DMA,
            ],
        ),
        out_shape=jax.ShapeDtypeStruct((p_alloc * ns, LANES), jnp.uint32),
        compiler_params=_cparams(("arbitrary",)),
        name="dispatch",
    )(tails, pos_tiles, xp)


def _chunk_dma(cs_ref, cn_ref, src_hbm, dst_ref, per, sem):
    blk = SUB_ROWS * per
    nch = cs_ref.shape[0]

    def copy(ci, j):
        src = src_hbm.at[pl.ds(pl.multiple_of((cs_ref[ci] + j * SUB_ROWS) * per, blk), blk)]
        return pltpu.make_async_copy(src, dst_ref.at[pl.ds(pl.multiple_of(j * blk, blk), blk)], sem)

    def run(ci, wait):
        cc = jnp.minimum(ci, nch - 1)
        n = jnp.where(ci < nch, cn_ref[cc], 0)

        def body(j, c):
            if wait:
                copy(cc, j).wait()
            else:
                copy(cc, j).start()
            return c

        lax.fori_loop(0, n, body, 0)

    return (lambda ci: run(ci, False)), (lambda ci: run(ci, True))


def _tile_loop(nsub, prepare, compute, commit):
    def run(tiles):
        prepare(tiles)
        for t in tiles:
            compute(*t)
        commit(tiles)

    n8 = nsub // 8
    rem = nsub % 8
    base = n8 * 8 * SUB_ROWS
    r2 = base + (rem // 4) * 4 * SUB_ROWS
    r1 = r2 + ((rem % 4) // 2) * 2 * SUB_ROWS

    @pl.when(rem % 4 >= 2)
    def _():
        run([(pl.multiple_of(r2, SUB_ROWS), 2 * SUB_ROWS, 5)])

    @pl.when(rem % 2 == 1)
    def _():
        run([(pl.multiple_of(r1, SUB_ROWS), SUB_ROWS, 6)])

    @pl.when(rem >= 4)
    def _():
        run([(pl.multiple_of(base, 4 * SUB_ROWS), 4 * SUB_ROWS, 4)])

    def pair(j, c):
        r = pl.multiple_of(j * 8 * SUB_ROWS, 8 * SUB_ROWS)
        slot = (j % 2) * 2
        run([(r, 4 * SUB_ROWS, slot), (r + 4 * SUB_ROWS, 4 * SUB_ROWS, slot + 1)])
        return c

    lax.fori_loop(0, n8, pair, 0)


def _weight_stream(ce_ref, cn_ref, nf, w_hbms, tn, wf_ref, wb_ref, wsem):
    def copies(cc, f):
        cols = pl.ds(_mult(f * tn, tn), tn)
        return [pltpu.make_async_copy(w.at[ce_ref[cc], :, cols], wf_ref.at[f % 2, i], wsem.at[f % 2])
                for i, w in enumerate(w_hbms)]

    def first():
        for cp in copies(0, 0):
            cp.start()

    def advance(c, f):
        q = c * nf + f + 1
        cc = jnp.minimum(q // nf, cn_ref.shape[0] - 1)

        @pl.when(jnp.logical_and(q // nf < cn_ref.shape[0], cn_ref[cc] > 0))
        def _():
            for cp in copies(cc, q % nf):
                cp.start()

        for cp in copies(c, f):
            cp.wait()
        for i in range(len(w_hbms)):
            wb_ref[:, i * tn:(i + 1) * tn] = wf_ref[f % 2, i].astype(BF16)

    return first, advance


def _moe_up_kernel(ce_ref, cs_ref, cn_ref, tail_ref, xs_hbm, wg_hbm, bg_ref, wu_hbm, bu_ref, act_hbm,
                   land_ref, xb_ref, wf_ref, wgu_ref, ab_ref, zero_ref, pend_ref, sem, wsem, osem):
    c = pl.program_id(0)
    nsub = cn_ref[c]
    start = pl.multiple_of(cs_ref[c], SUB_ROWS)
    tf = wf_ref.shape[3]
    nf = bg_ref.shape[2] // tf
    ns = TOK_WORDS // LANES
    start_chunk, wait_chunk = _chunk_dma(cs_ref, cn_ref, xs_hbm, land_ref, ns, sem)
    first_weights, next_weights = _weight_stream(ce_ref, cn_ref, nf, (wg_hbm, wu_hbm), tf, wf_ref, wgu_ref, wsem)

    def out_copy(r, rows, slot, f):
        return pltpu.make_async_copy(
            ab_ref.at[slot, pl.ds(0, rows)],
            act_hbm.at[pl.ds(pl.multiple_of(start + r, SUB_ROWS), rows), pl.ds(_mult(f * tf, tf), tf)],
            osem.at[slot])

    @pl.when(c == 0)
    def _():
        zero_ref[...] = jnp.zeros_like(zero_ref)
        _zero_fill_rows(zero_ref, act_hbm, tail_ref[0], tail_ref[1], osem.at[0])
        for slot in range(len(SLOT_ROWS)):
            pend_ref[slot] = 0
        start_chunk(c)
        first_weights()

    @pl.when(nsub > 0)
    def _():
        wait_chunk(c)

        def unpack(j, carry):
            r = pl.multiple_of(j * SUB_ROWS, SUB_ROWS)
            for s in range(ns):
                w = land_ref[pl.ds(r * ns + s, SUB_ROWS, stride=ns), :]
                xb_ref[pl.ds(r, SUB_ROWS), s * LANES:(s + 1) * LANES] = pltpu.bitcast(w << 16, F32).astype(BF16)
                xb_ref[pl.ds(r, SUB_ROWS), TOK_WORDS + s * LANES:TOK_WORDS + (s + 1) * LANES] = (
                    pltpu.bitcast(w & jnp.uint32(0xFFFF0000), F32).astype(BF16))
            return carry

        lax.fori_loop(0, nsub, unpack, 0)
        start_chunk(c + 1)

        def column_block(f, carry):
            next_weights(c, f)
            cols = pl.ds(pl.multiple_of(f * tf, tf), tf)
            bg = bg_ref[0, :, cols]
            bu = bu_ref[0, :, cols]

            def prepare(tiles):
                for _, rows, slot in tiles:
                    @pl.when(pend_ref[slot] == 1)
                    def _():
                        out_copy(0, rows, slot, f).wait()

            def compute(r, rows, slot):
                z = jnp.dot(xb_ref[pl.ds(r, rows), :], wgu_ref[...], preferred_element_type=F32)
                g = z[:, :tf] + bg
                u = z[:, tf:] + bu
                g = jnp.minimum(g, SWIGLU_LIMIT)
                u = jnp.clip(u, -SWIGLU_LIMIT, SWIGLU_LIMIT)
                ab_ref[slot, pl.ds(0, rows), :] = (g * jax.nn.sigmoid(SWIGLU_ALPHA * g) * (u + 1.0)).astype(BF16)

            def commit(tiles):
                for r, rows, slot in tiles:
                    out_copy(r, rows, slot, f).start()
                    pend_ref[slot] = 1

            _tile_loop(nsub, prepare, compute, commit)
            return carry

        lax.fori_loop(0, nf, column_block, 0)

    @pl.when(c == pl.num_programs(0) - 1)
    def _():
        for slot, rows in enumerate(SLOT_ROWS):
            @pl.when(pend_ref[slot] == 1)
            def _():
                out_copy(0, rows, slot, 0).wait()


def _moe_up(ch_e, ch_start, ch_nsub, tail, xs, w_gate, b_gate, w_up, b_up):
    ns = TOK_WORDS // LANES
    p_alloc = xs.shape[0] // ns
    d = 2 * TOK_WORDS
    dff = w_gate.shape[2]
    tf = MOE_TF
    assert (dff // tf) % 2 == 0
    nch = ch_e.shape[0]
    bmap = lambda c, e, s, n, t: (e[c], 0, 0)
    return pl.pallas_call(
        _moe_up_kernel,
        grid_spec=pltpu.PrefetchScalarGridSpec(
            num_scalar_prefetch=4,
            grid=(nch,),
            in_specs=[
                pl.BlockSpec(memory_space=pl.ANY),
                pl.BlockSpec(memory_space=pl.ANY),
                pl.BlockSpec((1, 1, dff), bmap),
                pl.BlockSpec(memory_space=pl.ANY),
                pl.BlockSpec((1, 1, dff), bmap),
            ],
            out_specs=pl.BlockSpec(memory_space=pl.ANY),
            scratch_shapes=[
                pltpu.VMEM((MOE_TM * ns, LANES), jnp.uint32),
                pltpu.VMEM((MOE_TM, d), BF16),
                pltpu.VMEM((2, 2, d, tf), F32),
                pltpu.VMEM((d, 2 * tf), BF16),
                pltpu.VMEM((len(SLOT_ROWS), 4 * SUB_ROWS, tf), BF16),
                pltpu.VMEM((SUB_ROWS, dff), BF16),
                pltpu.SMEM((len(SLOT_ROWS),), jnp.int32),
                pltpu.SemaphoreType.DMA,
                pltpu.SemaphoreType.DMA((2,)),
                pltpu.SemaphoreType.DMA((len(SLOT_ROWS),)),
            ],
        ),
        out_shape=jax.ShapeDtypeStruct((p_alloc, dff), BF16),
        compiler_params=_cparams(("arbitrary",)),
        name="moe_up",
    )(ch_e, ch_start, ch_nsub, tail, xs, w_gate, b_gate.reshape(N_EXPERTS, 1, dff), w_up,
      b_up.reshape(N_EXPERTS, 1, dff))


def _moe_down_kernel(ce_ref, cs_ref, cn_ref, tail_ref, act_hbm, wd_hbm, bd_ref, ys_hbm,
                     x_ref, wf_ref, wdb_ref, yb_ref, zero_ref, pend_ref, sem, wsem, osem):
    c = pl.program_id(0)
    nsub = cn_ref[c]
    start = pl.multiple_of(cs_ref[c], SUB_ROWS)
    tn = wf_ref.shape[3]
    nf = bd_ref.shape[2] // tn
    ns = TOK_WORDS // LANES
    nw = tn // 2 // LANES
    chunk = [_chunk_dma(cs_ref, cn_ref, act_hbm, x_ref.at[i], 1, sem.at[i]) for i in range(2)]
    first_weights, next_weights = _weight_stream(ce_ref, cn_ref, nf, (wd_hbm,), tn, wf_ref, wdb_ref, wsem)

    def on_parity(ci, fn):
        for p in range(2):
            @pl.when(ci % 2 == p)
            def _():
                fn(p)

    def out_copy(r, rows):
        src = yb_ref.at[pl.ds(pl.multiple_of(r * ns, SUB_ROWS * ns), rows * ns)]
        dst = ys_hbm.at[pl.ds(pl.multiple_of((start + r) * ns, SUB_ROWS * ns), rows * ns)]
        return pltpu.make_async_copy(src, dst, osem)

    def drain():
        for kind, rows in enumerate(SLOT_ROWS[-3:]):
            def wait(j, carry):
                out_copy(0, rows).wait()
                return carry

            lax.fori_loop(0, pend_ref[kind], wait, 0)
            pend_ref[kind] = 0

    @pl.when(c == 0)
    def _():
        zero_ref[...] = jnp.zeros_like(zero_ref)
        _zero_fill_rows(zero_ref, ys_hbm, tail_ref[0], tail_ref[1], osem)
        for kind in range(3):
            pend_ref[kind] = 0
        chunk[0][0](c)
        first_weights()

    def run_chunk():
        for f in range(nf):
            next_weights(c, f)
            bd = bd_ref[0, :, f * tn:(f + 1) * tn]
            last = f == nf - 1
            if f == 0:
                drain()

            def compute(r, rows, slot, f=f, bd=bd):
                y = jnp.dot(x_ref[c % 2, pl.ds(r, rows), :], wdb_ref[...], preferred_element_type=F32) + bd
                lo = pltpu.bitcast(y[:, :tn // 2].astype(BF16).astype(F32), jnp.uint32)
                hi = pltpu.bitcast(y[:, tn // 2:].astype(BF16).astype(F32), jnp.uint32)
                packed = (hi & jnp.uint32(0xFFFF0000)) | (lo >> 16)
                for s in range(nw):
                    yb_ref[pl.ds(r * ns + f * nw + s, rows, stride=ns), :] = packed[:, s * LANES:(s + 1) * LANES]

            def commit(tiles, last=last):
                if last:
                    for r, rows, slot in tiles:
                        out_copy(r, rows).start()
                        kind = SLOT_ROWS[-3:].index(rows)
                        pend_ref[kind] = pend_ref[kind] + 1

            _tile_loop(nsub, lambda tiles: None, compute, commit)

    @pl.when(nsub > 0)
    def _():
        on_parity(c, lambda p: chunk[p][1](c))
        on_parity(c + 1, lambda p: chunk[p][0](c + 1))
        run_chunk()

    @pl.when(c == pl.num_programs(0) - 1)
    def _():
        drain()


def _moe_down(ch_e, ch_start, ch_nsub, tail, act, w_down, b_down):
    p_alloc, dff = act.shape
    d = w_down.shape[2]
    ns = TOK_WORDS // LANES
    tn = MOE_TN
    assert (d // tn) % 2 == 0
    nch = ch_e.shape[0]
    return pl.pallas_call(
        _moe_down_kernel,
        grid_spec=pltpu.PrefetchScalarGridSpec(
            num_scalar_prefetch=4,
            grid=(nch,),
            in_specs=[
                pl.BlockSpec(memory_space=pl.ANY),
                pl.BlockSpec(memory_space=pl.ANY),
                pl.BlockSpec((1, 1, d), lambda c, e, s, n, t: (e[c], 0, 0)),
            ],
            out_specs=pl.BlockSpec(memory_space=pl.ANY),
            scratch_shapes=[
                pltpu.VMEM((2, MOE_TM, dff), BF16),
                pltpu.VMEM((2, 1, dff, tn), F32),
                pltpu.VMEM((dff, tn), BF16),
                pltpu.VMEM((MOE_TM * ns, LANES), jnp.uint32),
                pltpu.VMEM((SUB_ROWS * ns, LANES), jnp.uint32),
                pltpu.SMEM((3,), jnp.int32),
                pltpu.SemaphoreType.DMA((2,)),
                pltpu.SemaphoreType.DMA((2,)),
                pltpu.SemaphoreType.DMA,
            ],
        ),
        out_shape=jax.ShapeDtypeStruct((p_alloc * ns, LANES), jnp.uint32),
        compiler_params=_cparams(("arbitrary",)),
        name="moe_down",
    )(ch_e, ch_start, ch_nsub, tail, act, w_down, b_down.reshape(N_EXPERTS, 1, d))


def _combine_kernel(pos_hbm, ys_hbm, h_ref, w_ref, g_ref, o_ref, pos_smem, yb_ref, sem, psem):
    i = pl.program_id(0)
    tm, d = h_ref.shape
    n = TOP_K * tm
    ns = TOK_WORDS // LANES

    def fetch(tile, slot):
        pcopy = pltpu.make_async_copy(pos_hbm.at[pl.ds(tile * n, n)], pos_smem.at[pl.ds(slot * n, n)], psem)
        pcopy.start()
        pcopy.wait()

        def issue(jj, c):
            for u in range(ROW_DMA_UNROLL):
                j = jj * ROW_DMA_UNROLL + u
                src = ys_hbm.at[pl.ds(pl.multiple_of(pos_smem[slot * n + j] * ns, ns), ns)]
                dst = yb_ref.at[slot, pl.ds(pl.multiple_of(j * ns, ns), ns)]
                pltpu.make_async_copy(src, dst, sem.at[slot]).start(priority=u % 2)
            return c

        lax.fori_loop(0, n // ROW_DMA_UNROLL, issue, 0)

    @pl.when(i == 0)
    def _():
        fetch(i, 0)

    @pl.when(i + 1 < pl.num_programs(0))
    def _():
        for p in range(2):
            @pl.when((i + 1) % 2 == p)
            def _():
                fetch(i + 1, p)

    def reduce(slot):
        pltpu.make_async_copy(ys_hbm.at[pl.ds(0, n * ns)], yb_ref.at[slot], sem.at[slot]).wait()
        wb = [jnp.broadcast_to(w_ref[:, k:k + 1], (tm, LANES)) for k in range(TOP_K)]
        ssq = jnp.zeros((tm, 1), F32)
        nw = MOE_TN // 2 // LANES
        for s in range(ns):
            c0 = (s // nw) * MOE_TN + (s % nw) * LANES
            lo_cols = slice(c0, c0 + LANES)
            hi_cols = slice(c0 + MOE_TN // 2, c0 + MOE_TN // 2 + LANES)
            acc_lo = h_ref[:, lo_cols]
            acc_hi = h_ref[:, hi_cols]
            for k in range(TOP_K):
                w = yb_ref[slot, pl.ds(k * tm * ns + s, tm, stride=ns), :]
                acc_lo = acc_lo + wb[k] * pltpu.bitcast(w << 16, F32)
                acc_hi = acc_hi + wb[k] * pltpu.bitcast(w & jnp.uint32(0xFFFF0000), F32)
            o_ref[:, lo_cols] = acc_lo
            o_ref[:, hi_cols] = acc_hi
            ssq = ssq + jnp.sum(acc_lo * acc_lo + acc_hi * acc_hi, axis=-1, keepdims=True)
        o_ref[...] = o_ref[...] * lax.rsqrt(ssq * (1.0 / d) + NORM_EPS) * g_ref[...]

    for p in range(2):
        @pl.when(i % 2 == p)
        def _():
            reduce(p)


def _combine(pos_tiles, ys, h1, wts_t, g_final, tm):
    rows, d = h1.shape
    ns = TOK_WORDS // LANES
    return pl.pallas_call(
        _combine_kernel,
        grid=(rows // tm,),
        in_specs=[
            pl.BlockSpec(memory_space=pl.ANY),
            pl.BlockSpec(memory_space=pl.ANY),
            pl.BlockSpec((tm, d), lambda i: (i, 0)),
            pl.BlockSpec((tm, TOP_K), lambda i: (i, 0)),
            pl.BlockSpec((1, d), lambda i: (0, 0)),
        ],
        out_specs=pl.BlockSpec((tm, d), lambda i: (i, 0)),
        out_shape=jax.ShapeDtypeStruct((rows, d), F32),
        scratch_shapes=[
            pltpu.SMEM((2 * TOP_K * tm,), jnp.int32),
            pltpu.VMEM((2, TOP_K * tm * ns, LANES), jnp.uint32),
            pltpu.SemaphoreType.DMA((2,)),
            pltpu.SemaphoreType.DMA,
        ],
        compiler_params=_cparams(("arbitrary",)),
        name="combine",
    )(pos_tiles, ys, h1, wts_t, g_final)


def _rope_tables(n_pos):
    half = HEAD_DIM // 2
    inv = 1.0 / (ROPE_THETA ** (jnp.arange(half, dtype=F32) / half))
    ang = jnp.arange(n_pos, dtype=F32)[:, None] * inv[None, :]
    cos = jnp.tile(jnp.cos(ang), (1, LANES // half))
    sin = jnp.tile(jnp.concatenate([-jnp.sin(ang), jnp.sin(ang)], axis=1), (1, LANES // HEAD_DIM))
    return cos, sin


def _block_diag(w):
    per = LRU_GROUP // LRU_BLOCK
    w4 = w.reshape(D_LRU // LRU_GROUP, per, LRU_BLOCK, LRU_BLOCK)
    eye = jnp.eye(per, dtype=w.dtype)
    bd = jnp.einsum("gpcd,pq->gpcqd", w4, eye)
    return bd.reshape(D_LRU // LRU_GROUP, LRU_GROUP, LRU_GROUP).astype(BF16)


def _tile_positions(pos, tm):
    rows = pos.shape[1]
    return pos.reshape(TOP_K, rows // tm, tm).transpose(1, 0, 2).reshape(-1)


def _chunk_schedule(counts, n_rows):
    aligned = ((counts + SUB_ROWS - 1) // SUB_ROWS) * SUB_ROWS
    offs = jnp.cumsum(aligned) - aligned
    n_ch = (aligned + MOE_TM - 1) // MOE_TM
    cum = jnp.cumsum(n_ch)
    total = cum[-1]
    nch_max = N_EXPERTS + (n_rows + N_EXPERTS * SUB_ROWS) // MOE_TM
    c = jnp.arange(nch_max, dtype=jnp.int32)
    cc = jnp.minimum(c, total - 1)
    e = jnp.searchsorted(cum, cc, side="right").astype(jnp.int32)
    j = cc - (cum[e] - n_ch[e])
    start = offs[e] + j * MOE_TM
    nsub = jnp.where(c < total, jnp.minimum(MOE_TM, aligned[e] - j * MOE_TM) // SUB_ROWS, 0)
    tails = offs + (counts // SUB_ROWS) * SUB_ROWS
    used = jnp.sum(aligned)
    tail = jnp.stack([used, (n_rows + N_EXPERTS * SUB_ROWS - used) // SUB_ROWS])
    tails = jnp.concatenate([tails, tail])
    return (offs, e, start.astype(jnp.int32), nsub.astype(jnp.int32), tails.astype(jnp.int32),
            tail.astype(jnp.int32))


def kernel(x, meta_tokens, norm_mix, w_in, b_in, sinks, conv_w, conv_b, w_a, b_a, w_i, b_i, lru_lambda,
           g_attn_out, g_lru_out, w_out, b_out, norm_ffn, w_router, b_router, w_gate, b_gate, w_up, b_up,
           w_down, b_down, final_norm):
    batch, seq, d = x.shape
    rows = batch * seq
    x2d = x.reshape(rows, d)
    row = lambda v: v.reshape(1, -1)

    w_in_bf = w_in[0].astype(BF16)
    w_out_bf = w_out[0].astype(BF16)
    cos, sin = _rope_tables(N_META + seq)
    wa_bd, wi_bd = _block_diag(w_a[0]), _block_diag(w_i[0])
    lru_args = (conv_w[0], row(conv_b[0]), wa_bd, row(b_a[0]), wi_bd, row(b_i[0]), row(lru_lambda[0]),
                row(g_lru_out[0]))

    _, kvm, xgm = _inproj(meta_tokens, row(norm_mix[0]), w_in_bf, row(b_in[0]), cos[:N_META], sin[:N_META],
                          N_META, 1)
    _, h0, tail0 = _lru(xgm, *lru_args, jnp.zeros((1, D_LRU), F32), jnp.zeros((8, D_LRU), F32), 1, N_META,
                        N_META)

    tm = 512
    q, kv, xg = _inproj(x2d, row(norm_mix[0]), w_in_bf, row(b_in[0]), cos[N_META:], sin[N_META:], tm,
                        seq // tm)
    kvm = jnp.pad(kvm, ((0, BLOCK - N_META), (0, 0)))
    attn_n = _attention(sinks[0], q, kv, kvm, row(g_attn_out[0]), batch, seq)
    lru_n, _, _ = _lru(xg, *lru_args, h0, tail0, batch, seq, 128)

    tri = (jnp.arange(tm)[:, None] <= jnp.arange(tm)[None, :]).astype(BF16)
    h1, xp, ids, wts, rank, cnt = _outproj(
        attn_n, lru_n, w_out_bf, row(b_out[0]), x2d, row(norm_ffn[0]), w_router[0].T.astype(BF16),
        b_router[0].reshape(N_EXPERTS, 1), tri, tm)

    counts = cnt[:, 0].astype(jnp.int32)
    n_rows = rows * TOP_K
    offs, ch_e, ch_start, ch_nsub, tails, tail = _chunk_schedule(counts, n_rows)
    onehot = ids[..., None] == jnp.arange(N_EXPERTS, dtype=jnp.int32)
    pos = jnp.sum(jnp.where(onehot, offs, 0), axis=-1) + rank
    p_alloc = n_rows + N_EXPERTS * SUB_ROWS

    tmd = 512
    xs = _dispatch(tails, _tile_positions(pos, tmd), xp, p_alloc, tmd)
    act = _moe_up(ch_e, ch_start, ch_nsub, tail, xs, w_gate[0], b_gate[0], w_up[0], b_up[0])
    ys = _moe_down(ch_e, ch_start, ch_nsub, tail, act, w_down[0], b_down[0])
    tmc = 256
    out = _combine(_tile_positions(pos, tmc), ys, h1, wts.T, row(final_norm), tmc)
    return out.reshape(batch, seq, d)
```

```python
import jax
import jax.numpy as jnp
from jax import lax
from jax.experimental import pallas as pl
from jax.experimental.pallas import tpu as pltpu

F32 = jnp.float32
BF16 = jnp.bfloat16

N_META = 16
HEAD_DIM = 64
N_Q_HEADS = 16
N_KV_HEADS = 4
GROUP = N_Q_HEADS // N_KV_HEADS
D_ATTN = N_Q_HEADS * HEAD_DIM
D_KV = N_KV_HEADS * HEAD_DIM
KV_COLS = 4 * D_KV
BLOCK = 128
ROPE_THETA = 10000.0
D_LRU = 1024
LRU_BLOCK = 64
LRU_GROUP = 256
CONV_WIDTH = 4
LRU_C = 8.0
N_EXPERTS = 32
TOP_K = 4
SWIGLU_LIMIT = 7.0
SWIGLU_ALPHA = 1.702
NORM_EPS = 1e-5
NEG_INF = -1e30

LANES = 128
D_MODEL = 2048
TOK_WORDS = D_MODEL // 2
SUB_ROWS = 128
MOE_TM = 2304
MOE_TF = 512
MOE_TN = 1024
ROW_DMA_UNROLL = 8
SLOT_ROWS = (4 * SUB_ROWS,) * 5 + (2 * SUB_ROWS, SUB_ROWS)
VMEM_LIMIT = 56 * 1024 * 1024


def _cparams(sem):
    return pltpu.CompilerParams(dimension_semantics=sem, vmem_limit_bytes=VMEM_LIMIT)


def _mult(x, m):
    return x if isinstance(x, int) else pl.multiple_of(x, m)


def _rms(x, g):
    return x * lax.rsqrt(jnp.mean(x * x, axis=-1, keepdims=True) + NORM_EPS) * g


def _inproj_kernel(x_ref, g_ref, w_ref, b_ref, cos_ref, sin_ref, q_ref, kv_ref, xg_ref):
    xn = _rms(x_ref[...], g_ref[...]).astype(BF16)
    cos = cos_ref[...]
    sin = sin_ref[...]
    lane = lax.broadcasted_iota(jnp.int32, (1, LANES), 1)
    first_half = (lane % HEAD_DIM) < (HEAD_DIM // 2)

    def rope(z):
        partner = jnp.where(first_half, pltpu.roll(z, LANES - HEAD_DIM // 2, 1),
                            pltpu.roll(z, HEAD_DIM // 2, 1))
        return z * cos + partner * sin

    cw = 512
    for c in range(w_ref.shape[1] // cw):
        z = jnp.dot(xn, w_ref[:, c * cw:(c + 1) * cw], preferred_element_type=F32)
        z = z + b_ref[:, c * cw:(c + 1) * cw]
        if c < 2:
            for j in range(cw // LANES):
                zz = rope(z[:, j * LANES:(j + 1) * LANES]) * (HEAD_DIM ** -0.5)
                q_ref[:, c * cw + j * LANES:c * cw + (j + 1) * LANES] = zz.astype(BF16)
        elif c == 2:
            low = lane < HEAD_DIM
            for j in range(2 * D_KV // LANES):
                zz = z[:, j * LANES:(j + 1) * LANES]
                zz = rope(zz) if j < D_KV // LANES else zz
                sw = pltpu.roll(zz, HEAD_DIM, 1)
                kv_ref[:, 2 * j * LANES:(2 * j + 1) * LANES] = jnp.where(low, zz, sw).astype(BF16)
                kv_ref[:, (2 * j + 1) * LANES:(2 * j + 2) * LANES] = jnp.where(low, sw, zz).astype(BF16)
        else:
            xg_ref[:, (c - 3) * cw:(c - 2) * cw] = z


def _inproj(x2d, g, w_bf, b, cos, sin, tm, pos_blocks):
    rows, d = x2d.shape
    dz = w_bf.shape[1]
    return pl.pallas_call(
        _inproj_kernel,
        grid=(rows // tm,),
        in_specs=[
            pl.BlockSpec((tm, d), lambda m: (m, 0)),
            pl.BlockSpec((1, d), lambda m: (0, 0)),
            pl.BlockSpec((d, dz), lambda m: (0, 0)),
            pl.BlockSpec((1, dz), lambda m: (0, 0)),
            pl.BlockSpec((tm, LANES), lambda m: (m % pos_blocks, 0)),
            pl.BlockSpec((tm, LANES), lambda m: (m % pos_blocks, 0)),
        ],
        out_specs=[
            pl.BlockSpec((tm, D_ATTN), lambda m: (m, 0)),
            pl.BlockSpec((tm, KV_COLS), lambda m: (m, 0)),
            pl.BlockSpec((tm, 2 * D_LRU), lambda m: (m, 0)),
        ],
        out_shape=[
            jax.ShapeDtypeStruct((rows, D_ATTN), BF16),
            jax.ShapeDtypeStruct((rows, KV_COLS), BF16),
            jax.ShapeDtypeStruct((rows, 2 * D_LRU), F32),
        ],
        compiler_params=_cparams(("arbitrary",)),
        name="inproj",
    )(x2d, g, w_bf, b, cos, sin)


def _attn_kernel(sink_ref, q_ref, kvc_ref, kvp_ref, kvm_ref, g_ref, o_ref):
    n = pl.program_id(1)
    row = lax.broadcasted_iota(jnp.int32, (GROUP * BLOCK, BLOCK), 0) % BLOCK
    col = lax.broadcasted_iota(jnp.int32, (GROUP * BLOCK, BLOCK), 1)
    in_cur = col <= row
    in_prev = jnp.logical_and(col > row, n > 0)
    is_meta = col < N_META
    low = lax.broadcasted_iota(jnp.int32, (BLOCK, LANES), 1) < HEAD_DIM
    nt = (((1,), (1,)), ((), ()))
    zero = jnp.zeros((), BF16)
    outs = []
    for h in range(N_KV_HEADS):
        ks = slice(h * LANES, (h + 1) * LANES)
        vs = slice(N_KV_HEADS * LANES + h * LANES, N_KV_HEADS * LANES + (h + 1) * LANES)
        parts = []
        for j in range(GROUP // 2):
            qg = q_ref[:, (h * GROUP // 2 + j) * LANES:(h * GROUP // 2 + j + 1) * LANES]
            parts += [jnp.where(low, qg, zero), jnp.where(low, zero, qg)]
        qs = jnp.concatenate(parts, axis=0)
        s_c = lax.dot_general(qs, kvc_ref[:, ks], nt, preferred_element_type=F32)
        s_p = lax.dot_general(qs, kvp_ref[:, ks], nt, preferred_element_type=F32)
        s_m = lax.dot_general(qs, kvm_ref[:, ks], nt, preferred_element_type=F32)
        s_b = jnp.where(in_prev, s_p, jnp.where(in_cur, s_c, NEG_INF))
        s_m = jnp.where(is_meta, s_m, NEG_INF)
        sink = jnp.concatenate(
            [jnp.full((BLOCK, 1), sink_ref[h * GROUP + g], F32) for g in range(GROUP)], axis=0)
        m = jnp.maximum(jnp.max(jnp.maximum(s_b, s_m), axis=-1, keepdims=True), sink)
        p_b = jnp.exp(s_b - m)
        p_m = jnp.exp(s_m - m)
        den = jnp.sum(p_b + p_m, axis=-1, keepdims=True) + jnp.exp(sink - m)
        r = (jnp.dot(jnp.where(in_cur, p_b, 0.0).astype(BF16), kvc_ref[:, vs], preferred_element_type=F32)
             + jnp.dot(jnp.where(in_cur, 0.0, p_b).astype(BF16), kvp_ref[:, vs], preferred_element_type=F32)
             + jnp.dot(p_m.astype(BF16), kvm_ref[:, vs], preferred_element_type=F32))
        r = r / den
        for j in range(GROUP // 2):
            outs.append(jnp.where(low, r[2 * j * BLOCK:(2 * j + 1) * BLOCK], r[(2 * j + 1) * BLOCK:(2 * j + 2) * BLOCK]))
    o_all = jnp.concatenate(outs, axis=1)
    o_ref[...] = _rms(o_all, g_ref[...]).astype(BF16)


def _attention(sinks, q, kv, kvm, g_attn, batch, seq):
    nb = seq // BLOCK
    return pl.pallas_call(
        _attn_kernel,
        grid_spec=pltpu.PrefetchScalarGridSpec(
            num_scalar_prefetch=1,
            grid=(batch, nb),
            in_specs=[
                pl.BlockSpec((BLOCK, D_ATTN), lambda b, n, s: (b * nb + n, 0)),
                pl.BlockSpec((BLOCK, KV_COLS), lambda b, n, s: (b * nb + n, 0)),
                pl.BlockSpec((BLOCK, KV_COLS), lambda b, n, s: (b * nb + jnp.maximum(n - 1, 0), 0)),
                pl.BlockSpec((BLOCK, KV_COLS), lambda b, n, s: (0, 0)),
                pl.BlockSpec((1, D_ATTN), lambda b, n, s: (0, 0)),
            ],
            out_specs=pl.BlockSpec((BLOCK, D_ATTN), lambda b, n, s: (b * nb + n, 0)),
        ),
        out_shape=jax.ShapeDtypeStruct((batch * seq, D_ATTN), BF16),
        compiler_params=_cparams(("arbitrary", "arbitrary")),
        name="attention",
    )(sinks, q, kv, kv, kvm, g_attn)


def _one_minus_sq(a, log_a):
    y = 2.0 * log_a
    p = 1.0 + y * (1.0 / 4.0)
    for k in (3, 2):
        p = 1.0 + y * p * (1.0 / k)
    return jnp.where(y > -1.0 / 64.0, -(y * p), 1.0 - a * a)


def _lru_kernel(xg_ref, cw_ref, cb_ref, wa_ref, ba_ref, wi_ref, bi_ref, lam_ref, g_ref, h0_ref, tail0_ref,
                o_ref, hout_ref, tailout_ref, ext_ref, tail_ref, a_ref, b_ref, h_ref):
    tt = pl.program_id(0)
    batch, rows = xg_ref.shape[0], xg_ref.shape[1]

    @pl.when(tt == 0)
    def _():
        for g in range(D_LRU // LANES):
            h_ref[g] = jnp.broadcast_to(h0_ref[:, g * LANES:(g + 1) * LANES], (batch, LANES))
        for s in range(batch):
            tail_ref[s] = tail0_ref[...]

    sp = jax.nn.softplus(-lam_ref[...])

    def gates(s, carry):
        ext_ref[0:8, :] = tail_ref[s]
        ext_ref[8:, :] = xg_ref[s, :, :D_LRU]
        tail_ref[s] = ext_ref[rows:rows + 8, :]
        xc = cb_ref[...] + sum(cw_ref[j:j + 1, :] * ext_ref[5 + j:5 + j + rows, :] for j in range(CONV_WIDTH))
        xcb = xc.astype(BF16)
        for c in range(D_LRU // LRU_GROUP):
            cs = slice(c * LRU_GROUP, (c + 1) * LRU_GROUP)
            r = jax.nn.sigmoid(jnp.dot(xcb[:, cs], wa_ref[c], preferred_element_type=F32) + ba_ref[:, cs])
            i = jax.nn.sigmoid(jnp.dot(xcb[:, cs], wi_ref[c], preferred_element_type=F32) + bi_ref[:, cs])
            log_a = -LRU_C * r * sp[:, cs]
            a = jnp.exp(log_a)
            b = jnp.sqrt(_one_minus_sq(a, log_a)) * i * xc[:, cs]
            for g in range(LRU_GROUP // LANES):
                lanes = slice(g * LANES, (g + 1) * LANES)
                a_ref[c * (LRU_GROUP // LANES) + g, pl.ds(s, rows, stride=batch), :] = a[:, lanes]
                b_ref[c * (LRU_GROUP // LANES) + g, pl.ds(s, rows, stride=batch), :] = b[:, lanes]
        return carry

    lax.fori_loop(0, batch, gates, 0)

    def step(t, h):
        r0 = pl.multiple_of(t * batch, batch)
        new = []
        for g in range(D_LRU // LANES):
            hg = a_ref[g, pl.ds(r0, batch), :] * h[g] + b_ref[g, pl.ds(r0, batch), :]
            b_ref[g, pl.ds(r0, batch), :] = hg
            new.append(hg)
        return tuple(new)

    h_last = lax.fori_loop(0, rows, step, tuple(h_ref[g] for g in range(D_LRU // LANES)))
    for g in range(D_LRU // LANES):
        h_ref[g] = h_last[g]
    hout_ref[...] = jnp.concatenate([h_last[g][0:1] for g in range(D_LRU // LANES)], axis=1)
    tailout_ref[...] = tail_ref[0]

    def finish(s, carry):
        hs = jnp.concatenate([b_ref[g, pl.ds(s, rows, stride=batch), :] for g in range(D_LRU // LANES)], axis=1)
        y = hs * jax.nn.gelu(xg_ref[s, :, D_LRU:])
        o_ref[s] = _rms(y, g_ref[...]).astype(BF16)
        return carry

    lax.fori_loop(0, batch, finish, 0)


def _lru(xg, conv_w, conv_b, wa_bd, b_a, wi_bd, b_i, lam, g_lru, h0, tail0, batch, seq, tt):
    ntt = seq // tt
    ng = D_LRU // LANES
    vec = pl.BlockSpec((1, D_LRU), lambda t: (0, 0))
    wspec = pl.BlockSpec((D_LRU // LRU_GROUP, LRU_GROUP, LRU_GROUP), lambda t: (0, 0, 0))
    out, h_last, tail = pl.pallas_call(
        _lru_kernel,
        grid=(ntt,),
        in_specs=[
            pl.BlockSpec((batch, tt, 2 * D_LRU), lambda t: (0, t, 0)),
            pl.BlockSpec((CONV_WIDTH, D_LRU), lambda t: (0, 0)),
            vec, wspec, vec, wspec, vec, vec, vec, vec,
            pl.BlockSpec((8, D_LRU), lambda t: (0, 0)),
        ],
        out_specs=[
            pl.BlockSpec((batch, tt, D_LRU), lambda t: (0, t, 0)),
            pl.BlockSpec((1, D_LRU), lambda t: (0, 0)),
            pl.BlockSpec((8, D_LRU), lambda t: (0, 0)),
        ],
        out_shape=[
            jax.ShapeDtypeStruct((batch, seq, D_LRU), BF16),
            jax.ShapeDtypeStruct((1, D_LRU), F32),
            jax.ShapeDtypeStruct((8, D_LRU), F32),
        ],
        scratch_shapes=[
            pltpu.VMEM((8 + tt, D_LRU), F32),
            pltpu.VMEM((batch, 8, D_LRU), F32),
            pltpu.VMEM((ng, tt * batch, LANES), F32),
            pltpu.VMEM((ng, tt * batch, LANES), F32),
            pltpu.VMEM((ng, batch, LANES), F32),
        ],
        compiler_params=_cparams(("arbitrary",)),
        name="rglru",
    )(xg.reshape(batch, seq, 2 * D_LRU), conv_w, conv_b, wa_bd, b_a, wi_bd, b_i, lam, g_lru, h0, tail0)
    return out.reshape(batch * seq, D_LRU), h_last, tail


def _outproj_kernel(a_ref, l_ref, wo_ref, bo_ref, x_ref, gf_ref, wr_ref, br_ref, tri_ref,
                    h_ref, xp_ref, ids_ref, wts_ref, rank_ref, cnt_ref, carry_ref):
    m = pl.program_id(0)
    tm = a_ref.shape[0]

    @pl.when(m == 0)
    def _():
        carry_ref[...] = jnp.zeros_like(carry_ref)

    h = (jnp.dot(a_ref[...], wo_ref[:D_ATTN, :], preferred_element_type=F32)
         + jnp.dot(l_ref[...], wo_ref[D_ATTN:, :], preferred_element_type=F32)
         + bo_ref[...] + x_ref[...])
    h_ref[...] = h
    xn = _rms(h, gf_ref[...]).astype(BF16)
    half = xn.shape[1] // 2
    lo = pltpu.bitcast(xn[:, :half].astype(F32), jnp.uint32)
    hi = pltpu.bitcast(xn[:, half:].astype(F32), jnp.uint32)
    packed = (hi & jnp.uint32(0xFFFF0000)) | (lo >> 16)
    ns = half // LANES
    for s in range(ns):
        xp_ref[pl.ds(s, tm, stride=ns), :] = packed[:, s * LANES:(s + 1) * LANES]

    logits = lax.dot_general(wr_ref[...], xn, (((1,), (1,)), ((), ())), preferred_element_type=F32)
    logits = logits + br_ref[...]
    eidx = lax.broadcasted_iota(jnp.int32, (N_EXPERTS, tm), 0)
    work = logits
    vals, sels = [], []
    for k in range(TOP_K):
        v = jnp.max(work, axis=0, keepdims=True)
        idx = jnp.min(jnp.where(work == v, eidx, N_EXPERTS), axis=0, keepdims=True)
        sel = eidx == idx
        ids_ref[k:k + 1, :] = idx
        vals.append(v)
        sels.append(sel)
        work = jnp.where(sel, -jnp.inf, work)
    es = [jnp.exp(v - vals[0]) for v in vals]
    den = es[0] + es[1] + es[2] + es[3]
    for k in range(TOP_K):
        wts_ref[k:k + 1, :] = es[k] / den
    cnt = sum(s.astype(F32) for s in sels)
    incl = jnp.dot(cnt.astype(BF16), tri_ref[...], preferred_element_type=F32)
    before = incl - cnt + carry_ref[:, 0:1]
    for k in range(TOP_K):
        rk = jnp.sum(jnp.where(sels[k], before, 0.0), axis=0, keepdims=True)
        rank_ref[k:k + 1, :] = rk.astype(jnp.int32)
    carry_ref[...] = carry_ref[...] + incl[:, tm - 1:tm]
    cnt_ref[...] = carry_ref[...]


def _outproj(attn_n, lru_n, wo_bf, b_out, x2d, g_ffn, wr_t, br, tri, tm):
    rows, d = x2d.shape
    const = lambda shape: pl.BlockSpec(shape, lambda m: tuple(0 for _ in shape))
    return pl.pallas_call(
        _outproj_kernel,
        grid=(rows // tm,),
        in_specs=[
            pl.BlockSpec((tm, D_ATTN), lambda m: (m, 0)),
            pl.BlockSpec((tm, D_LRU), lambda m: (m, 0)),
            const((D_ATTN + D_LRU, d)), const((1, d)),
            pl.BlockSpec((tm, d), lambda m: (m, 0)),
            const((1, d)), const((N_EXPERTS, d)), const((N_EXPERTS, 1)), const((tm, tm)),
        ],
        out_specs=[
            pl.BlockSpec((tm, d), lambda m: (m, 0)),
            pl.BlockSpec((tm * (d // 2 // LANES), LANES), lambda m: (m, 0)),
            pl.BlockSpec((TOP_K, tm), lambda m: (0, m)),
            pl.BlockSpec((TOP_K, tm), lambda m: (0, m)),
            pl.BlockSpec((TOP_K, tm), lambda m: (0, m)),
            const((N_EXPERTS, LANES)),
        ],
        out_shape=[
            jax.ShapeDtypeStruct((rows, d), F32),
            jax.ShapeDtypeStruct((rows * (d // 2 // LANES), LANES), jnp.uint32),
            jax.ShapeDtypeStruct((TOP_K, rows), jnp.int32),
            jax.ShapeDtypeStruct((TOP_K, rows), F32),
            jax.ShapeDtypeStruct((TOP_K, rows), jnp.int32),
            jax.ShapeDtypeStruct((N_EXPERTS, LANES), F32),
        ],
        scratch_shapes=[pltpu.VMEM((N_EXPERTS, LANES), F32)],
        compiler_params=_cparams(("arbitrary",)),
        name="outproj_router",
    )(attn_n, lru_n, wo_bf, b_out, x2d, g_ffn, wr_t, br, tri)


def _zero_fill_rows(zero_ref, dst_hbm, first_row, n_blocks, sem):
    blk = zero_ref.shape[0]
    per = blk // SUB_ROWS

    def copy(j):
        row = pl.multiple_of((first_row + j * SUB_ROWS) * per, blk)
        return pltpu.make_async_copy(zero_ref, dst_hbm.at[pl.ds(row, blk)], sem)

    def start(j, c):
        copy(j).start()
        return c

    def wait(j, c):
        copy(j).wait()
        return c

    lax.fori_loop(0, n_blocks, start, 0)
    lax.fori_loop(0, n_blocks, wait, 0)


def _dispatch_kernel(tails_ref, pos_hbm, xp_ref, xs_hbm, pos_smem, zero_ref, sem, psem):
    i = pl.program_id(0)
    ns = TOK_WORDS // LANES
    tm = xp_ref.shape[0] // ns
    n = TOP_K * tm

    def pos_copy(step):
        half = pl.multiple_of((step % 2) * n, n)
        return pltpu.make_async_copy(pos_hbm.at[pl.ds(pl.multiple_of(step * n, n), n)],
                                     pos_smem.at[pl.ds(half, n)], psem.at[step % 2])

    @pl.when(i == 0)
    def _():
        pos_copy(i).start()
        zero_ref[...] = jnp.zeros_like(zero_ref)
        for e in range(N_EXPERTS):
            _zero_fill_rows(zero_ref, xs_hbm, tails_ref[e], 1, sem)
        _zero_fill_rows(zero_ref, xs_hbm, tails_ref[N_EXPERTS], tails_ref[N_EXPERTS + 1], sem)

    pos_copy(i).wait()

    @pl.when(i + 1 < pl.num_programs(0))
    def _():
        pos_copy(i + 1).start()

    base = (i % 2) * n
    for k in range(TOP_K):
        def issue(tt, c):
            for u in range(ROW_DMA_UNROLL):
                t = tt * ROW_DMA_UNROLL + u
                src = xp_ref.at[pl.ds(pl.multiple_of(t * ns, ns), ns)]
                dst = xs_hbm.at[pl.ds(pl.multiple_of(pos_smem[base + k * tm + t] * ns, ns), ns)]
                pltpu.make_async_copy(src, dst, sem).start(priority=u % 2)
            return c

        lax.fori_loop(0, tm // ROW_DMA_UNROLL, issue, 0)
    for _ in range(TOP_K):
        pltpu.make_async_copy(xp_ref, xs_hbm.at[pl.ds(0, tm * ns)], sem).wait()


def _dispatch(tails, pos_tiles, xp, p_alloc, tm):
    ns = TOK_WORDS // LANES
    rows = xp.shape[0] // ns
    return pl.pallas_call(
        _dispatch_kernel,
        grid_spec=pltpu.PrefetchScalarGridSpec(
            num_scalar_prefetch=1,
            grid=(rows // tm,),
            in_specs=[
                pl.BlockSpec(memory_space=pl.ANY),
                pl.BlockSpec((tm * ns, LANES), lambda i, s: (i, 0)),
            ],
            out_specs=pl.BlockSpec(memory_space=pl.ANY),
            scratch_shapes=[
                pltpu.SMEM((2 * TOP_K * tm,), jnp.int32),
                pltpu.VMEM((SUB_ROWS * ns, LANES), jnp.uint32),
                pltpu.SemaphoreType.DMA,
                pltpu.SemaphoreType.DMA((2,)),
            ],
        ),
        out_shape=jax.ShapeDtypeStruct((p_alloc * ns, LANES), jnp.uint32),
        compiler_params=_cparams(("arbitrary",)),
        name="dispatch",
    )(tails, pos_tiles, xp)


def _chunk_dma(cs_ref, cn_ref, src_hbm, dst_ref, per, sem):
    blk = SUB_ROWS * per
    nch = cs_ref.shape[0]

    def copy(ci, j):
        src = src_hbm.at[pl.ds(pl.multiple_of((cs_ref[ci] + j * SUB_ROWS) * per, blk), blk)]
        return pltpu.make_async_copy(src, dst_ref.at[pl.ds(pl.multiple_of(j * blk, blk), blk)], sem)

    def run(ci, wait):
        cc = jnp.minimum(ci, nch - 1)
        n = jnp.where(ci < nch, cn_ref[cc], 0)

        def body(j, c):
            if wait:
                copy(cc, j).wait()
            else:
                copy(cc, j).start()
            return c

        lax.fori_loop(0, n, body, 0)

    return (lambda ci: run(ci, False)), (lambda ci: run(ci, True))


def _tile_loop(nsub, prepare, compute, commit):
    def run(tiles):
        prepare(tiles)
        for t in tiles:
            compute(*t)
        commit(tiles)

    n8 = nsub // 8
    rem = nsub % 8
    base = n8 * 8 * SUB_ROWS
    r2 = base + (rem // 4) * 4 * SUB_ROWS
    r1 = r2 + ((rem % 4) // 2) * 2 * SUB_ROWS

    @pl.when(rem % 4 >= 2)
    def _():
        run([(pl.multiple_of(r2, SUB_ROWS), 2 * SUB_ROWS, 5)])

    @pl.when(rem % 2 == 1)
    def _():
        run([(pl.multiple_of(r1, SUB_ROWS), SUB_ROWS, 6)])

    @pl.when(rem >= 4)
    def _():
        run([(pl.multiple_of(base, 4 * SUB_ROWS), 4 * SUB_ROWS, 4)])

    def pair(j, c):
        r = pl.multiple_of(j * 8 * SUB_ROWS, 8 * SUB_ROWS)
        slot = (j % 2) * 2
        run([(r, 4 * SUB_ROWS, slot), (r + 4 * SUB_ROWS, 4 * SUB_ROWS, slot + 1)])
        return c

    lax.fori_loop(0, n8, pair, 0)


def _weight_stream(ce_ref, cn_ref, nf, w_hbms, tn, wf_ref, wb_ref, wsem):
    def copies(cc, f):
        cols = pl.ds(_mult(f * tn, tn), tn)
        return [pltpu.make_async_copy(w.at[ce_ref[cc], :, cols], wf_ref.at[f % 2, i], wsem.at[f % 2])
                for i, w in enumerate(w_hbms)]

    def first():
        for cp in copies(0, 0):
            cp.start()

    def advance(c, f):
        q = c * nf + f + 1
        cc = jnp.minimum(q // nf, cn_ref.shape[0] - 1)

        @pl.when(jnp.logical_and(q // nf < cn_ref.shape[0], cn_ref[cc] > 0))
        def _():
            for cp in copies(cc, q % nf):
                cp.start()

        for cp in copies(c, f):
            cp.wait()
        for i in range(len(w_hbms)):
            wb_ref[:, i * tn:(i + 1) * tn] = wf_ref[f % 2, i].astype(BF16)

    return first, advance


def _moe_up_kernel(ce_ref, cs_ref, cn_ref, tail_ref, xs_hbm, wg_hbm, bg_ref, wu_hbm, bu_ref, act_hbm,
                   land_ref, xb_ref, wf_ref, wgu_ref, ab_ref, zero_ref, pend_ref, sem, wsem, osem):
    c = pl.program_id(0)
    nsub = cn_ref[c]
    start = pl.multiple_of(cs_ref[c], SUB_ROWS)
    tf = wf_ref.shape[3]
    nf = bg_ref.shape[2] // tf
    ns = TOK_WORDS // LANES
    start_chunk, wait_chunk = _chunk_dma(cs_ref, cn_ref, xs_hbm, land_ref, ns, sem)
    first_weights, next_weights = _weight_stream(ce_ref, cn_ref, nf, (wg_hbm, wu_hbm), tf, wf_ref, wgu_ref, wsem)

    def out_copy(r, rows, slot, f):
        return pltpu.make_async_copy(
            ab_ref.at[slot, pl.ds(0, rows)],
            act_hbm.at[pl.ds(pl.multiple_of(start + r, SUB_ROWS), rows), pl.ds(_mult(f * tf, tf), tf)],
            osem.at[slot])

    @pl.when(c == 0)
    def _():
        zero_ref[...] = jnp.zeros_like(zero_ref)
        _zero_fill_rows(zero_ref, act_hbm, tail_ref[0], tail_ref[1], osem.at[0])
        for slot in range(len(SLOT_ROWS)):
            pend_ref[slot] = 0
        start_chunk(c)
        first_weights()

    @pl.when(nsub > 0)
    def _():
        wait_chunk(c)

        def unpack(j, carry):
            r = pl.multiple_of(j * SUB_ROWS, SUB_ROWS)
            for s in range(ns):
                w = land_ref[pl.ds(r * ns + s, SUB_ROWS, stride=ns), :]
                xb_ref[pl.ds(r, SUB_ROWS), s * LANES:(s + 1) * LANES] = pltpu.bitcast(w << 16, F32).astype(BF16)
                xb_ref[pl.ds(r, SUB_ROWS), TOK_WORDS + s * LANES:TOK_WORDS + (s + 1) * LANES] = (
                    pltpu.bitcast(w & jnp.uint32(0xFFFF0000), F32).astype(BF16))
            return carry

        lax.fori_loop(0, nsub, unpack, 0)
        start_chunk(c + 1)

        def column_block(f, carry):
            next_weights(c, f)
            cols = pl.ds(pl.multiple_of(f * tf, tf), tf)
            bg = bg_ref[0, :, cols]
            bu = bu_ref[0, :, cols]

            def prepare(tiles):
                for _, rows, slot in tiles:
                    @pl.when(pend_ref[slot] == 1)
                    def _():
                        out_copy(0, rows, slot, f).wait()

            def compute(r, rows, slot):
                z = jnp.dot(xb_ref[pl.ds(r, rows), :], wgu_ref[...], preferred_element_type=F32)
                g = z[:, :tf] + bg
                u = z[:, tf:] + bu
                g = jnp.minimum(g, SWIGLU_LIMIT)
                u = jnp.clip(u, -SWIGLU_LIMIT, SWIGLU_LIMIT)
                ab_ref[slot, pl.ds(0, rows), :] = (g * jax.nn.sigmoid(SWIGLU_ALPHA * g) * (u + 1.0)).astype(BF16)

            def commit(tiles):
                for r, rows, slot in tiles:
                    out_copy(r, rows, slot, f).start()
                    pend_ref[slot] = 1

            _tile_loop(nsub, prepare, compute, commit)
            return carry

        lax.fori_loop(0, nf, column_block, 0)

    @pl.when(c == pl.num_programs(0) - 1)
    def _():
        for slot, rows in enumerate(SLOT_ROWS):
            @pl.when(pend_ref[slot] == 1)
            def _():
                out_copy(0, rows, slot, 0).wait()


def _moe_up(ch_e, ch_start, ch_nsub, tail, xs, w_gate, b_gate, w_up, b_up):
    ns = TOK_WORDS // LANES
    p_alloc = xs.shape[0] // ns
    d = 2 * TOK_WORDS
    dff = w_gate.shape[2]
    tf = MOE_TF
    assert (dff // tf) % 2 == 0
    nch = ch_e.shape[0]
    bmap = lambda c, e, s, n, t: (e[c], 0, 0)
    return pl.pallas_call(
        _moe_up_kernel,
        grid_spec=pltpu.PrefetchScalarGridSpec(
            num_scalar_prefetch=4,
            grid=(nch,),
            in_specs=[
                pl.BlockSpec(memory_space=pl.ANY),
                pl.BlockSpec(memory_space=pl.ANY),
                pl.BlockSpec((1, 1, dff), bmap),
                pl.BlockSpec(memory_space=pl.ANY),
                pl.BlockSpec((1, 1, dff), bmap),
            ],
            out_specs=pl.BlockSpec(memory_space=pl.ANY),
            scratch_shapes=[
                pltpu.VMEM((MOE_TM * ns, LANES), jnp.uint32),
                pltpu.VMEM((MOE_TM, d), BF16),
                pltpu.VMEM((2, 2, d, tf), F32),
                pltpu.VMEM((d, 2 * tf), BF16),
                pltpu.VMEM((len(SLOT_ROWS), 4 * SUB_ROWS, tf), BF16),
                pltpu.VMEM((SUB_ROWS, dff), BF16),
                pltpu.SMEM((len(SLOT_ROWS),), jnp.int32),
                pltpu.SemaphoreType.DMA,
                pltpu.SemaphoreType.DMA((2,)),
                pltpu.SemaphoreType.DMA((len(SLOT_ROWS),)),
            ],
        ),
        out_shape=jax.ShapeDtypeStruct((p_alloc, dff), BF16),
        compiler_params=_cparams(("arbitrary",)),
        name="moe_up",
    )(ch_e, ch_start, ch_nsub, tail, xs, w_gate, b_gate.reshape(N_EXPERTS, 1, dff), w_up,
      b_up.reshape(N_EXPERTS, 1, dff))


def _moe_down_kernel(ce_ref, cs_ref, cn_ref, tail_ref, act_hbm, wd_hbm, bd_ref, ys_hbm,
                     x_ref, wf_ref, wdb_ref, yb_ref, zero_ref, pend_ref, sem, wsem, osem):
    c = pl.program_id(0)
    nsub = cn_ref[c]
    start = pl.multiple_of(cs_ref[c], SUB_ROWS)
    tn = wf_ref.shape[3]
    nf = bd_ref.shape[2] // tn
    ns = TOK_WORDS // LANES
    nw = tn // 2 // LANES
    chunk = [_chunk_dma(cs_ref, cn_ref, act_hbm, x_ref.at[i], 1, sem.at[i]) for i in range(2)]
    first_weights, next_weights = _weight_stream(ce_ref, cn_ref, nf, (wd_hbm,), tn, wf_ref, wdb_ref, wsem)

    def on_parity(ci, fn):
        for p in range(2):
            @pl.when(ci % 2 == p)
            def _():
                fn(p)

    def out_copy(r, rows):
        src = yb_ref.at[pl.ds(pl.multiple_of(r * ns, SUB_ROWS * ns), rows * ns)]
        dst = ys_hbm.at[pl.ds(pl.multiple_of((start + r) * ns, SUB_ROWS * ns), rows * ns)]
        return pltpu.make_async_copy(src, dst, osem)

    def drain():
        for kind, rows in enumerate(SLOT_ROWS[-3:]):
            def wait(j, carry):
                out_copy(0, rows).wait()
                return carry

            lax.fori_loop(0, pend_ref[kind], wait, 0)
            pend_ref[kind] = 0

    @pl.when(c == 0)
    def _():
        zero_ref[...] = jnp.zeros_like(zero_ref)
        _zero_fill_rows(zero_ref, ys_hbm, tail_ref[0], tail_ref[1], osem)
        for kind in range(3):
            pend_ref[kind] = 0
        chunk[0][0](c)
        first_weights()

    def run_chunk():
        for f in range(nf):
            next_weights(c, f)
            bd = bd_ref[0, :, f * tn:(f + 1) * tn]
            last = f == nf - 1
            if f == 0:
                drain()

            def compute(r, rows, slot, f=f, bd=bd):
                y = jnp.dot(x_ref[c % 2, pl.ds(r, rows), :], wdb_ref[...], preferred_element_type=F32) + bd
                lo = pltpu.bitcast(y[:, :tn // 2].astype(BF16).astype(F32), jnp.uint32)
                hi = pltpu.bitcast(y[:, tn // 2:].astype(BF16).astype(F32), jnp.uint32)
                packed = (hi & jnp.uint32(0xFFFF0000)) | (lo >> 16)
                for s in range(nw):
                    yb_ref[pl.ds(r * ns + f * nw + s, rows, stride=ns), :] = packed[:, s * LANES:(s + 1) * LANES]

            def commit(tiles, last=last):
                if last:
                    for r, rows, slot in tiles:
                        out_copy(r, rows).start()
                        kind = SLOT_ROWS[-3:].index(rows)
                        pend_ref[kind] = pend_ref[kind] + 1

            _tile_loop(nsub, lambda tiles: None, compute, commit)

    @pl.when(nsub > 0)
    def _():
        on_parity(c, lambda p: chunk[p][1](c))
        on_parity(c + 1, lambda p: chunk[p][0](c + 1))
        run_chunk()

    @pl.when(c == pl.num_programs(0) - 1)
    def _():
        drain()


def _moe_down(ch_e, ch_start, ch_nsub, tail, act, w_down, b_down):
    p_alloc, dff = act.shape
    d = w_down.shape[2]
    ns = TOK_WORDS // LANES
    tn = MOE_TN
    assert (d // tn) % 2 == 0
    nch = ch_e.shape[0]
    return pl.pallas_call(
        _moe_down_kernel,
        grid_spec=pltpu.PrefetchScalarGridSpec(
            num_scalar_prefetch=4,
            grid=(nch,),
            in_specs=[
                pl.BlockSpec(memory_space=pl.ANY),
                pl.BlockSpec(memory_space=pl.ANY),
                pl.BlockSpec((1, 1, d), lambda c, e, s, n, t: (e[c], 0, 0)),
            ],
            out_specs=pl.BlockSpec(memory_space=pl.ANY),
            scratch_shapes=[
                pltpu.VMEM((2, MOE_TM, dff), BF16),
                pltpu.VMEM((2, 1, dff, tn), F32),
                pltpu.VMEM((dff, tn), BF16),
                pltpu.VMEM((MOE_TM * ns, LANES), jnp.uint32),
                pltpu.VMEM((SUB_ROWS * ns, LANES), jnp.uint32),
                pltpu.SMEM((3,), jnp.int32),
                pltpu.SemaphoreType.DMA((2,)),
                pltpu.SemaphoreType.DMA((2,)),
                pltpu.SemaphoreType.DMA,
            ],
        ),
        out_shape=jax.ShapeDtypeStruct((p_alloc * ns, LANES), jnp.uint32),
        compiler_params=_cparams(("arbitrary",)),
        name="moe_down",
    )(ch_e, ch_start, ch_nsub, tail, act, w_down, b_down.reshape(N_EXPERTS, 1, d))


def _combine_kernel(pos_hbm, ys_hbm, h_ref, w_ref, g_ref, o_ref, pos_smem, yb_ref, sem, psem):
    i = pl.program_id(0)
    tm, d = h_ref.shape
    n = TOP_K * tm
    ns = TOK_WORDS // LANES

    def pos_copy(tile):
        third = pl.multiple_of((tile % 3) * n, n)
        return pltpu.make_async_copy(pos_hbm.at[pl.ds(pl.multiple_of(tile * n, n), n)],
                                     pos_smem.at[pl.ds(third, n)], psem.at[tile % 3])

    def fetch(tile, slot):
        pos_copy(tile).wait()

        @pl.when(tile + 1 < pl.num_programs(0))
        def _():
            pos_copy(tile + 1).start()

        base = (tile % 3) * n

        def issue(jj, c):
            for u in range(ROW_DMA_UNROLL):
                j = jj * ROW_DMA_UNROLL + u
                src = ys_hbm.at[pl.ds(pl.multiple_of(pos_smem[base + j] * ns, ns), ns)]
                dst = yb_ref.at[slot, pl.ds(pl.multiple_of(j * ns, ns), ns)]
                pltpu.make_async_copy(src, dst, sem.at[slot]).start(priority=u % 2)
            return c

        lax.fori_loop(0, n // ROW_DMA_UNROLL, issue, 0)

    @pl.when(i == 0)
    def _():
        pos_copy(i).start()
        fetch(i, 0)

    @pl.when(i + 1 < pl.num_programs(0))
    def _():
        for p in range(2):
            @pl.when((i + 1) % 2 == p)
            def _():
                fetch(i + 1, p)

    def reduce(slot):
        pltpu.make_async_copy(ys_hbm.at[pl.ds(0, n * ns)], yb_ref.at[slot], sem.at[slot]).wait()
        wb = [jnp.broadcast_to(w_ref[:, k:k + 1], (tm, LANES)) for k in range(TOP_K)]
        ssq = jnp.zeros((tm, 1), F32)
        nw = MOE_TN // 2 // LANES
        for s in range(ns):
            c0 = (s // nw) * MOE_TN + (s % nw) * LANES
            lo_cols = slice(c0, c0 + LANES)
            hi_cols = slice(c0 + MOE_TN // 2, c0 + MOE_TN // 2 + LANES)
            acc_lo = h_ref[:, lo_cols]
            acc_hi = h_ref[:, hi_cols]
            for k in range(TOP_K):
                w = yb_ref[slot, pl.ds(k * tm * ns + s, tm, stride=ns), :]
                acc_lo = acc_lo + wb[k] * pltpu.bitcast(w << 16, F32)
                acc_hi = acc_hi + wb[k] * pltpu.bitcast(w & jnp.uint32(0xFFFF0000), F32)
            o_ref[:, lo_cols] = acc_lo
            o_ref[:, hi_cols] = acc_hi
            ssq = ssq + jnp.sum(acc_lo * acc_lo + acc_hi * acc_hi, axis=-1, keepdims=True)
        o_ref[...] = o_ref[...] * lax.rsqrt(ssq * (1.0 / d) + NORM_EPS) * g_ref[...]

    for p in range(2):
        @pl.when(i % 2 == p)
        def _():
            reduce(p)


def _combine(pos_tiles, ys, h1, wts_t, g_final, tm):
    rows, d = h1.shape
    ns = TOK_WORDS // LANES
    return pl.pallas_call(
        _combine_kernel,
        grid=(rows // tm,),
        in_specs=[
            pl.BlockSpec(memory_space=pl.ANY),
            pl.BlockSpec(memory_space=pl.ANY),
            pl.BlockSpec((tm, d), lambda i: (i, 0)),
            pl.BlockSpec((tm, TOP_K), lambda i: (i, 0)),
            pl.BlockSpec((1, d), lambda i: (0, 0)),
        ],
        out_specs=pl.BlockSpec((tm, d), lambda i: (i, 0)),
        out_shape=jax.ShapeDtypeStruct((rows, d), F32),
        scratch_shapes=[
            pltpu.SMEM((3 * TOP_K * tm,), jnp.int32),
            pltpu.VMEM((2, TOP_K * tm * ns, LANES), jnp.uint32),
            pltpu.SemaphoreType.DMA((2,)),
            pltpu.SemaphoreType.DMA((3,)),
        ],
        compiler_params=_cparams(("arbitrary",)),
        name="combine",
    )(pos_tiles, ys, h1, wts_t, g_final)


def _rope_tables(n_pos):
    half = HEAD_DIM // 2
    inv = 1.0 / (ROPE_THETA ** (jnp.arange(half, dtype=F32) / half))
    ang = jnp.arange(n_pos, dtype=F32)[:, None] * inv[None, :]
    cos = jnp.tile(jnp.cos(ang), (1, LANES // half))
    sin = jnp.tile(jnp.concatenate([-jnp.sin(ang), jnp.sin(ang)], axis=1), (1, LANES // HEAD_DIM))
    return cos, sin


def _block_diag(w):
    per = LRU_GROUP // LRU_BLOCK
    w4 = w.reshape(D_LRU // LRU_GROUP, per, LRU_BLOCK, LRU_BLOCK)
    eye = jnp.eye(per, dtype=w.dtype)
    bd = jnp.einsum("gpcd,pq->gpcqd", w4, eye)
    return bd.reshape(D_LRU // LRU_GROUP, LRU_GROUP, LRU_GROUP).astype(BF16)


def _tile_positions(pos, tm):
    rows = pos.shape[1]
    return pos.reshape(TOP_K, rows // tm, tm).transpose(1, 0, 2).reshape(-1)


def _chunk_schedule(counts, n_rows):
    aligned = ((counts + SUB_ROWS - 1) // SUB_ROWS) * SUB_ROWS
    offs = jnp.cumsum(aligned) - aligned
    n_ch = (aligned + MOE_TM - 1) // MOE_TM
    cum = jnp.cumsum(n_ch)
    total = cum[-1]
    nch_max = N_EXPERTS + (n_rows + N_EXPERTS * SUB_ROWS) // MOE_TM
    c = jnp.arange(nch_max, dtype=jnp.int32)
    cc = jnp.minimum(c, total - 1)
    e = jnp.searchsorted(cum, cc, side="right").astype(jnp.int32)
    j = cc - (cum[e] - n_ch[e])
    start = offs[e] + j * MOE_TM
    nsub = jnp.where(c < total, jnp.minimum(MOE_TM, aligned[e] - j * MOE_TM) // SUB_ROWS, 0)
    tails = offs + (counts // SUB_ROWS) * SUB_ROWS
    used = jnp.sum(aligned)
    tail = jnp.stack([used, (n_rows + N_EXPERTS * SUB_ROWS - used) // SUB_ROWS])
    tails = jnp.concatenate([tails, tail])
    return (offs, e, start.astype(jnp.int32), nsub.astype(jnp.int32), tails.astype(jnp.int32),
            tail.astype(jnp.int32))


def kernel(x, meta_tokens, norm_mix, w_in, b_in, sinks, conv_w, conv_b, w_a, b_a, w_i, b_i, lru_lambda,
           g_attn_out, g_lru_out, w_out, b_out, norm_ffn, w_router, b_router, w_gate, b_gate, w_up, b_up,
           w_down, b_down, final_norm):
    batch, seq, d = x.shape
    rows = batch * seq
    x2d = x.reshape(rows, d)
    row = lambda v: v.reshape(1, -1)

    w_in_bf = w_in[0].astype(BF16)
    w_out_bf = w_out[0].astype(BF16)
    cos, sin = _rope_tables(N_META + seq)
    wa_bd, wi_bd = _block_diag(w_a[0]), _block_diag(w_i[0])
    lru_args = (conv_w[0], row(conv_b[0]), wa_bd, row(b_a[0]), wi_bd, row(b_i[0]), row(lru_lambda[0]),
                row(g_lru_out[0]))

    _, kvm, xgm = _inproj(meta_tokens, row(norm_mix[0]), w_in_bf, row(b_in[0]), cos[:N_META], sin[:N_META],
                          N_META, 1)
    _, h0, tail0 = _lru(xgm, *lru_args, jnp.zeros((1, D_LRU), F32), jnp.zeros((8, D_LRU), F32), 1, N_META,
                        N_META)

    tm = 512
    q, kv, xg = _inproj(x2d, row(norm_mix[0]), w_in_bf, row(b_in[0]), cos[N_META:], sin[N_META:], tm,
                        seq // tm)
    kvm = jnp.pad(kvm, ((0, BLOCK - N_META), (0, 0)))
    attn_n = _attention(sinks[0], q, kv, kvm, row(g_attn_out[0]), batch, seq)
    lru_n, _, _ = _lru(xg, *lru_args, h0, tail0, batch, seq, 128)

    tri = (jnp.arange(tm)[:, None] <= jnp.arange(tm)[None, :]).astype(BF16)
    h1, xp, ids, wts, rank, cnt = _outproj(
        attn_n, lru_n, w_out_bf, row(b_out[0]), x2d, row(norm_ffn[0]), w_router[0].T.astype(BF16),
        b_router[0].reshape(N_EXPERTS, 1), tri, tm)

    counts = cnt[:, 0].astype(jnp.int32)
    n_rows = rows * TOP_K
    offs, ch_e, ch_start, ch_nsub, tails, tail = _chunk_schedule(counts, n_rows)
    onehot = ids[..., None] == jnp.arange(N_EXPERTS, dtype=jnp.int32)
    pos = jnp.sum(jnp.where(onehot, offs, 0), axis=-1) + rank
    p_alloc = n_rows + N_EXPERTS * SUB_ROWS

    tmd = 512
    xs = _dispatch(tails, _tile_positions(pos, tmd), xp, p_alloc, tmd)
    act = _moe_up(ch_e, ch_start, ch_nsub, tail, xs, w_gate[0], b_gate[0], w_up[0], b_up[0])
    ys = _moe_down(ch_e, ch_start, ch_nsub, tail, act, w_down[0], b_down[0])
    tmc = 256
    out = _combine(_tile_positions(pos, tmc), ys, h1, wts.T, row(final_norm), tmc)
    return out.reshape(batch, seq, d)
```

```python
import jax
import jax.numpy as jnp
from jax import lax
from jax.experimental import pallas as pl
from jax.experimental.pallas import tpu as pltpu

F32 = jnp.float32
BF16 = jnp.bfloat16

N_META = 16
HEAD_DIM = 64
N_Q_HEADS = 16
N_KV_HEADS = 4
GROUP = N_Q_HEADS // N_KV_HEADS
D_ATTN = N_Q_HEADS * HEAD_DIM
D_KV = N_KV_HEADS * HEAD_DIM
KV_COLS = 4 * D_KV
BLOCK = 128
ROPE_THETA = 10000.0
D_LRU = 1024
LRU_BLOCK = 64
LRU_GROUP = 256
CONV_WIDTH = 4
LRU_C = 8.0
N_EXPERTS = 32
TOP_K = 4
SWIGLU_LIMIT = 7.0
SWIGLU_ALPHA = 1.702
NORM_EPS = 1e-5
NEG_INF = -1e30

LANES = 128
D_MODEL = 2048
TOK_WORDS = D_MODEL // 2
SUB_ROWS = 128
MOE_TM = 2304
MOE_TF = 512
MOE_TN = 1024
ROW_DMA_UNROLL = 8
SLOT_ROWS = (4 * SUB_ROWS,) * 5 + (2 * SUB_ROWS, SUB_ROWS)
VMEM_LIMIT = 56 * 1024 * 1024
VMEM_LIMIT_MAX = 62 * 1024 * 1024


def _cparams(sem):
    return pltpu.CompilerParams(dimension_semantics=sem, vmem_limit_bytes=VMEM_LIMIT)


def _mult(x, m):
    return x if isinstance(x, int) else pl.multiple_of(x, m)


def _rms(x, g):
    return x * lax.rsqrt(jnp.mean(x * x, axis=-1, keepdims=True) + NORM_EPS) * g


def _inproj_kernel(x_ref, g_ref, w_ref, b_ref, cos_ref, sin_ref, q_ref, kv_ref, xg_ref):
    xn = _rms(x_ref[...], g_ref[...]).astype(BF16)
    cos = cos_ref[...]
    sin = sin_ref[...]
    lane = lax.broadcasted_iota(jnp.int32, (1, LANES), 1)
    first_half = (lane % HEAD_DIM) < (HEAD_DIM // 2)

    def rope(z):
        partner = jnp.where(first_half, pltpu.roll(z, LANES - HEAD_DIM // 2, 1),
                            pltpu.roll(z, HEAD_DIM // 2, 1))
        return z * cos + partner * sin

    cw = 512
    for c in range(w_ref.shape[1] // cw):
        z = jnp.dot(xn, w_ref[:, c * cw:(c + 1) * cw], preferred_element_type=F32)
        z = z + b_ref[:, c * cw:(c + 1) * cw]
        if c < 2:
            for j in range(cw // LANES):
                zz = rope(z[:, j * LANES:(j + 1) * LANES]) * (HEAD_DIM ** -0.5)
                q_ref[:, c * cw + j * LANES:c * cw + (j + 1) * LANES] = zz.astype(BF16)
        elif c == 2:
            low = lane < HEAD_DIM
            for j in range(2 * D_KV // LANES):
                zz = z[:, j * LANES:(j + 1) * LANES]
                zz = rope(zz) if j < D_KV // LANES else zz
                sw = pltpu.roll(zz, HEAD_DIM, 1)
                kv_ref[:, 2 * j * LANES:(2 * j + 1) * LANES] = jnp.where(low, zz, sw).astype(BF16)
                kv_ref[:, (2 * j + 1) * LANES:(2 * j + 2) * LANES] = jnp.where(low, sw, zz).astype(BF16)
        else:
            xg_ref[:, (c - 3) * cw:(c - 2) * cw] = z


def _inproj(x2d, g, w_bf, b, cos, sin, tm, pos_blocks):
    rows, d = x2d.shape
    dz = w_bf.shape[1]
    return pl.pallas_call(
        _inproj_kernel,
        grid=(rows // tm,),
        in_specs=[
            pl.BlockSpec((tm, d), lambda m: (m, 0)),
            pl.BlockSpec((1, d), lambda m: (0, 0)),
            pl.BlockSpec((d, dz), lambda m: (0, 0)),
            pl.BlockSpec((1, dz), lambda m: (0, 0)),
            pl.BlockSpec((tm, LANES), lambda m: (m % pos_blocks, 0)),
            pl.BlockSpec((tm, LANES), lambda m: (m % pos_blocks, 0)),
        ],
        out_specs=[
            pl.BlockSpec((tm, D_ATTN), lambda m: (m, 0)),
            pl.BlockSpec((tm, KV_COLS), lambda m: (m, 0)),
            pl.BlockSpec((tm, 2 * D_LRU), lambda m: (m, 0)),
        ],
        out_shape=[
            jax.ShapeDtypeStruct((rows, D_ATTN), BF16),
            jax.ShapeDtypeStruct((rows, KV_COLS), BF16),
            jax.ShapeDtypeStruct((rows, 2 * D_LRU), F32),
        ],
        compiler_params=_cparams(("arbitrary",)),
        name="inproj",
    )(x2d, g, w_bf, b, cos, sin)


def _attn_kernel(sink_ref, q_ref, kvc_ref, kvp_ref, kvm_ref, g_ref, o_ref):
    n = pl.program_id(1)
    row = lax.broadcasted_iota(jnp.int32, (GROUP * BLOCK, BLOCK), 0) % BLOCK
    col = lax.broadcasted_iota(jnp.int32, (GROUP * BLOCK, BLOCK), 1)
    in_cur = col <= row
    in_prev = jnp.logical_and(col > row, n > 0)
    is_meta = col < N_META
    low = lax.broadcasted_iota(jnp.int32, (BLOCK, LANES), 1) < HEAD_DIM
    nt = (((1,), (1,)), ((), ()))
    zero = jnp.zeros((), BF16)
    outs = []
    for h in range(N_KV_HEADS):
        ks = slice(h * LANES, (h + 1) * LANES)
        vs = slice(N_KV_HEADS * LANES + h * LANES, N_KV_HEADS * LANES + (h + 1) * LANES)
        parts = []
        for j in range(GROUP // 2):
            qg = q_ref[:, (h * GROUP // 2 + j) * LANES:(h * GROUP // 2 + j + 1) * LANES]
            parts += [jnp.where(low, qg, zero), jnp.where(low, zero, qg)]
        qs = jnp.concatenate(parts, axis=0)
        s_c = lax.dot_general(qs, kvc_ref[:, ks], nt, preferred_element_type=F32)
        s_p = lax.dot_general(qs, kvp_ref[:, ks], nt, preferred_element_type=F32)
        s_m = lax.dot_general(qs, kvm_ref[:, ks], nt, preferred_element_type=F32)
        s_b = jnp.where(in_prev, s_p, jnp.where(in_cur, s_c, NEG_INF))
        s_m = jnp.where(is_meta, s_m, NEG_INF)
        sink = jnp.concatenate(
            [jnp.full((BLOCK, 1), sink_ref[h * GROUP + g], F32) for g in range(GROUP)], axis=0)
        m = jnp.maximum(jnp.max(jnp.maximum(s_b, s_m), axis=-1, keepdims=True), sink)
        p_b = jnp.exp(s_b - m)
        p_m = jnp.exp(s_m - m)
        den = jnp.sum(p_b + p_m, axis=-1, keepdims=True) + jnp.exp(sink - m)
        r = (jnp.dot(jnp.where(in_cur, p_b, 0.0).astype(BF16), kvc_ref[:, vs], preferred_element_type=F32)
             + jnp.dot(jnp.where(in_cur, 0.0, p_b).astype(BF16), kvp_ref[:, vs], preferred_element_type=F32)
             + jnp.dot(p_m.astype(BF16), kvm_ref[:, vs], preferred_element_type=F32))
        r = r / den
        for j in range(GROUP // 2):
            outs.append(jnp.where(low, r[2 * j * BLOCK:(2 * j + 1) * BLOCK], r[(2 * j + 1) * BLOCK:(2 * j + 2) * BLOCK]))
    o_all = jnp.concatenate(outs, axis=1)
    o_ref[...] = _rms(o_all, g_ref[...]).astype(BF16)


def _attention(sinks, q, kv, kvm, g_attn, batch, seq):
    nb = seq // BLOCK
    return pl.pallas_call(
        _attn_kernel,
        grid_spec=pltpu.PrefetchScalarGridSpec(
            num_scalar_prefetch=1,
            grid=(batch, nb),
            in_specs=[
                pl.BlockSpec((BLOCK, D_ATTN), lambda b, n, s: (b * nb + n, 0)),
                pl.BlockSpec((BLOCK, KV_COLS), lambda b, n, s: (b * nb + n, 0)),
                pl.BlockSpec((BLOCK, KV_COLS), lambda b, n, s: (b * nb + jnp.maximum(n - 1, 0), 0)),
                pl.BlockSpec((BLOCK, KV_COLS), lambda b, n, s: (0, 0)),
                pl.BlockSpec((1, D_ATTN), lambda b, n, s: (0, 0)),
            ],
            out_specs=pl.BlockSpec((BLOCK, D_ATTN), lambda b, n, s: (b * nb + n, 0)),
        ),
        out_shape=jax.ShapeDtypeStruct((batch * seq, D_ATTN), BF16),
        compiler_params=_cparams(("arbitrary", "arbitrary")),
        name="attention",
    )(sinks, q, kv, kv, kvm, g_attn)


def _one_minus_sq(a, log_a):
    y = 2.0 * log_a
    p = 1.0 + y * (1.0 / 4.0)
    for k in (3, 2):
        p = 1.0 + y * p * (1.0 / k)
    return jnp.where(y > -1.0 / 64.0, -(y * p), 1.0 - a * a)


def _lru_kernel(xg_ref, cw_ref, cb_ref, wa_ref, ba_ref, wi_ref, bi_ref, lam_ref, g_ref, h0_ref, tail0_ref,
                o_ref, hout_ref, tailout_ref, ext_ref, tail_ref, a_ref, b_ref, h_ref):
    tt = pl.program_id(0)
    batch, rows = xg_ref.shape[0], xg_ref.shape[1]

    @pl.when(tt == 0)
    def _():
        for g in range(D_LRU // LANES):
            h_ref[g] = jnp.broadcast_to(h0_ref[:, g * LANES:(g + 1) * LANES], (batch, LANES))
        for s in range(batch):
            tail_ref[s] = tail0_ref[...]

    sp = jax.nn.softplus(-lam_ref[...])

    def gates(s, carry):
        ext_ref[0:8, :] = tail_ref[s]
        ext_ref[8:, :] = xg_ref[s, :, :D_LRU]
        tail_ref[s] = ext_ref[rows:rows + 8, :]
        xc = cb_ref[...] + sum(cw_ref[j:j + 1, :] * ext_ref[5 + j:5 + j + rows, :] for j in range(CONV_WIDTH))
        xcb = xc.astype(BF16)
        for c in range(D_LRU // LRU_GROUP):
            cs = slice(c * LRU_GROUP, (c + 1) * LRU_GROUP)
            r = jax.nn.sigmoid(jnp.dot(xcb[:, cs], wa_ref[c], preferred_element_type=F32) + ba_ref[:, cs])
            i = jax.nn.sigmoid(jnp.dot(xcb[:, cs], wi_ref[c], preferred_element_type=F32) + bi_ref[:, cs])
            log_a = -LRU_C * r * sp[:, cs]
            a = jnp.exp(log_a)
            b = jnp.sqrt(_one_minus_sq(a, log_a)) * i * xc[:, cs]
            for g in range(LRU_GROUP // LANES):
                lanes = slice(g * LANES, (g + 1) * LANES)
                a_ref[c * (LRU_GROUP // LANES) + g, pl.ds(s, rows, stride=batch), :] = a[:, lanes]
                b_ref[c * (LRU_GROUP // LANES) + g, pl.ds(s, rows, stride=batch), :] = b[:, lanes]
        return carry

    lax.fori_loop(0, batch, gates, 0)

    def step(t, h):
        r0 = pl.multiple_of(t * batch, batch)
        new = []
        for g in range(D_LRU // LANES):
            hg = a_ref[g, pl.ds(r0, batch), :] * h[g] + b_ref[g, pl.ds(r0, batch), :]
            b_ref[g, pl.ds(r0, batch), :] = hg
            new.append(hg)
        return tuple(new)

    h_last = lax.fori_loop(0, rows, step, tuple(h_ref[g] for g in range(D_LRU // LANES)))
    for g in range(D_LRU // LANES):
        h_ref[g] = h_last[g]
    hout_ref[...] = jnp.concatenate([h_last[g][0:1] for g in range(D_LRU // LANES)], axis=1)
    tailout_ref[...] = tail_ref[0]

    def finish(s, carry):
        hs = jnp.concatenate([b_ref[g, pl.ds(s, rows, stride=batch), :] for g in range(D_LRU // LANES)], axis=1)
        y = hs * jax.nn.gelu(xg_ref[s, :, D_LRU:])
        o_ref[s] = _rms(y, g_ref[...]).astype(BF16)
        return carry

    lax.fori_loop(0, batch, finish, 0)


def _lru(xg, conv_w, conv_b, wa_bd, b_a, wi_bd, b_i, lam, g_lru, h0, tail0, batch, seq, tt):
    ntt = seq // tt
    ng = D_LRU // LANES
    vec = pl.BlockSpec((1, D_LRU), lambda t: (0, 0))
    wspec = pl.BlockSpec((D_LRU // LRU_GROUP, LRU_GROUP, LRU_GROUP), lambda t: (0, 0, 0))
    out, h_last, tail = pl.pallas_call(
        _lru_kernel,
        grid=(ntt,),
        in_specs=[
            pl.BlockSpec((batch, tt, 2 * D_LRU), lambda t: (0, t, 0)),
            pl.BlockSpec((CONV_WIDTH, D_LRU), lambda t: (0, 0)),
            vec, wspec, vec, wspec, vec, vec, vec, vec,
            pl.BlockSpec((8, D_LRU), lambda t: (0, 0)),
        ],
        out_specs=[
            pl.BlockSpec((batch, tt, D_LRU), lambda t: (0, t, 0)),
            pl.BlockSpec((1, D_LRU), lambda t: (0, 0)),
            pl.BlockSpec((8, D_LRU), lambda t: (0, 0)),
        ],
        out_shape=[
            jax.ShapeDtypeStruct((batch, seq, D_LRU), BF16),
            jax.ShapeDtypeStruct((1, D_LRU), F32),
            jax.ShapeDtypeStruct((8, D_LRU), F32),
        ],
        scratch_shapes=[
            pltpu.VMEM((8 + tt, D_LRU), F32),
            pltpu.VMEM((batch, 8, D_LRU), F32),
            pltpu.VMEM((ng, tt * batch, LANES), F32),
            pltpu.VMEM((ng, tt * batch, LANES), F32),
            pltpu.VMEM((ng, batch, LANES), F32),
        ],
        compiler_params=_cparams(("arbitrary",)),
        name="rglru",
    )(xg.reshape(batch, seq, 2 * D_LRU), conv_w, conv_b, wa_bd, b_a, wi_bd, b_i, lam, g_lru, h0, tail0)
    return out.reshape(batch * seq, D_LRU), h_last, tail


def _outproj_kernel(a_ref, l_ref, wo_ref, bo_ref, x_ref, gf_ref, wr_ref, br_ref, tri_ref,
                    h_ref, xp_ref, ids_ref, wts_ref, rank_ref, cnt_ref, carry_ref):
    m = pl.program_id(0)
    tm = a_ref.shape[0]

    @pl.when(m == 0)
    def _():
        carry_ref[...] = jnp.zeros_like(carry_ref)

    h = (jnp.dot(a_ref[...], wo_ref[:D_ATTN, :], preferred_element_type=F32)
         + jnp.dot(l_ref[...], wo_ref[D_ATTN:, :], preferred_element_type=F32)
         + bo_ref[...] + x_ref[...])
    h_ref[...] = h
    xn = _rms(h, gf_ref[...]).astype(BF16)
    half = xn.shape[1] // 2
    lo = pltpu.bitcast(xn[:, :half].astype(F32), jnp.uint32)
    hi = pltpu.bitcast(xn[:, half:].astype(F32), jnp.uint32)
    packed = (hi & jnp.uint32(0xFFFF0000)) | (lo >> 16)
    ns = half // LANES
    for s in range(ns):
        xp_ref[pl.ds(s, tm, stride=ns), :] = packed[:, s * LANES:(s + 1) * LANES]

    logits = lax.dot_general(wr_ref[...], xn, (((1,), (1,)), ((), ())), preferred_element_type=F32)
    logits = logits + br_ref[...]
    eidx = lax.broadcasted_iota(jnp.int32, (N_EXPERTS, tm), 0)
    work = logits
    vals, sels = [], []
    for k in range(TOP_K):
        v = jnp.max(work, axis=0, keepdims=True)
        idx = jnp.min(jnp.where(work == v, eidx, N_EXPERTS), axis=0, keepdims=True)
        sel = eidx == idx
        ids_ref[k:k + 1, :] = idx
        vals.append(v)
        sels.append(sel)
        work = jnp.where(sel, -jnp.inf, work)
    es = [jnp.exp(v - vals[0]) for v in vals]
    den = es[0] + es[1] + es[2] + es[3]
    for k in range(TOP_K):
        wts_ref[k:k + 1, :] = es[k] / den
    cnt = sum(s.astype(F32) for s in sels)
    incl = jnp.dot(cnt.astype(BF16), tri_ref[...], preferred_element_type=F32)
    before = incl - cnt + carry_ref[:, 0:1]
    for k in range(TOP_K):
        rk = jnp.sum(jnp.where(sels[k], before, 0.0), axis=0, keepdims=True)
        rank_ref[k:k + 1, :] = rk.astype(jnp.int32)
    carry_ref[...] = carry_ref[...] + incl[:, tm - 1:tm]
    cnt_ref[...] = carry_ref[...]


def _outproj(attn_n, lru_n, wo_bf, b_out, x2d, g_ffn, wr_t, br, tri, tm):
    rows, d = x2d.shape
    const = lambda shape: pl.BlockSpec(shape, lambda m: tuple(0 for _ in shape))
    return pl.pallas_call(
        _outproj_kernel,
        grid=(rows // tm,),
        in_specs=[
            pl.BlockSpec((tm, D_ATTN), lambda m: (m, 0)),
            pl.BlockSpec((tm, D_LRU), lambda m: (m, 0)),
            const((D_ATTN + D_LRU, d)), const((1, d)),
            pl.BlockSpec((tm, d), lambda m: (m, 0)),
            const((1, d)), const((N_EXPERTS, d)), const((N_EXPERTS, 1)), const((tm, tm)),
        ],
        out_specs=[
            pl.BlockSpec((tm, d), lambda m: (m, 0)),
            pl.BlockSpec((tm * (d // 2 // LANES), LANES), lambda m: (m, 0)),
            pl.BlockSpec((TOP_K, tm), lambda m: (0, m)),
            pl.BlockSpec((TOP_K, tm), lambda m: (0, m)),
            pl.BlockSpec((TOP_K, tm), lambda m: (0, m)),
            const((N_EXPERTS, LANES)),
        ],
        out_shape=[
            jax.ShapeDtypeStruct((rows, d), F32),
            jax.ShapeDtypeStruct((rows * (d // 2 // LANES), LANES), jnp.uint32),
            jax.ShapeDtypeStruct((TOP_K, rows), jnp.int32),
            jax.ShapeDtypeStruct((TOP_K, rows), F32),
            jax.ShapeDtypeStruct((TOP_K, rows), jnp.int32),
            jax.ShapeDtypeStruct((N_EXPERTS, LANES), F32),
        ],
        scratch_shapes=[pltpu.VMEM((N_EXPERTS, LANES), F32)],
        compiler_params=_cparams(("arbitrary",)),
        name="outproj_router",
    )(attn_n, lru_n, wo_bf, b_out, x2d, g_ffn, wr_t, br, tri)


def _zero_fill_rows(zero_ref, dst_hbm, first_row, n_blocks, sem):
    blk = zero_ref.shape[0]
    per = blk // SUB_ROWS

    def copy(j):
        row = pl.multiple_of((first_row + j * SUB_ROWS) * per, blk)
        return pltpu.make_async_copy(zero_ref, dst_hbm.at[pl.ds(row, blk)], sem)

    def start(j, c):
        copy(j).start()
        return c

    def wait(j, c):
        copy(j).wait()
        return c

    lax.fori_loop(0, n_blocks, start, 0)
    lax.fori_loop(0, n_blocks, wait, 0)


def _dispatch_kernel(tails_ref, pos_hbm, xp_ref, xs_hbm, pos_smem, zero_ref, sem, psem):
    i = pl.program_id(0)
    ns = TOK_WORDS // LANES
    tm = xp_ref.shape[0] // ns
    n = TOP_K * tm

    def pos_copy(step):
        half = pl.multiple_of((step % 2) * n, n)
        return pltpu.make_async_copy(pos_hbm.at[pl.ds(pl.multiple_of(step * n, n), n)],
                                     pos_smem.at[pl.ds(half, n)], psem.at[step % 2])

    @pl.when(i == 0)
    def _():
        pos_copy(i).start()
        zero_ref[...] = jnp.zeros_like(zero_ref)
        for e in range(N_EXPERTS):
            _zero_fill_rows(zero_ref, xs_hbm, tails_ref[e], 1, sem)
        _zero_fill_rows(zero_ref, xs_hbm, tails_ref[N_EXPERTS], tails_ref[N_EXPERTS + 1], sem)

    pos_copy(i).wait()

    @pl.when(i + 1 < pl.num_programs(0))
    def _():
        pos_copy(i + 1).start()

    base = (i % 2) * n
    for k in range(TOP_K):
        def issue(tt, c):
            for u in range(ROW_DMA_UNROLL):
                t = tt * ROW_DMA_UNROLL + u
                src = xp_ref.at[pl.ds(pl.multiple_of(t * ns, ns), ns)]
                dst = xs_hbm.at[pl.ds(pl.multiple_of(pos_smem[base + k * tm + t] * ns, ns), ns)]
                pltpu.make_async_copy(src, dst, sem).start(priority=u % 2)
            return c

        lax.fori_loop(0, tm // ROW_DMA_UNROLL, issue, 0)
    for _ in range(TOP_K):
        pltpu.make_async_copy(xp_ref, xs_hbm.at[pl.ds(0, tm * ns)], sem).wait()


def _dispatch(tails, pos_tiles, xp, p_alloc, tm):
    ns = TOK_WORDS // LANES
    rows = xp.shape[0] // ns
    return pl.pallas_call(
        _dispatch_kernel,
        grid_spec=pltpu.PrefetchScalarGridSpec(
            num_scalar_prefetch=1,
            grid=(rows // tm,),
            in_specs=[
                pl.BlockSpec(memory_space=pl.ANY),
                pl.BlockSpec((tm * ns, LANES), lambda i, s: (i, 0)),
            ],
            out_specs=pl.BlockSpec(memory_space=pl.ANY),
            scratch_shapes=[
                pltpu.SMEM((2 * TOP_K * tm,), jnp.int32),
                pltpu.VMEM((SUB_ROWS * ns, LANES), jnp.uint32),
                pltpu.SemaphoreType.DMA,
                pltpu.SemaphoreType.DMA((2,)),
            ],
        ),
        out_shape=jax.ShapeDtypeStruct((p_alloc * ns, LANES), jnp.uint32),
        compiler_params=_cparams(("arbitrary",)),
        name="dispatch",
    )(tails, pos_tiles, xp)


def _chunk_dma(cs_ref, cn_ref, src_hbm, dst_ref, per, sem):
    blk = SUB_ROWS * per
    nch = cs_ref.shape[0]

    def copy(ci, j):
        src = src_hbm.at[pl.ds(pl.multiple_of((cs_ref[ci] + j * SUB_ROWS) * per, blk), blk)]
        return pltpu.make_async_copy(src, dst_ref.at[pl.ds(pl.multiple_of(j * blk, blk), blk)], sem)

    def run(ci, wait):
        cc = jnp.minimum(ci, nch - 1)
        n = jnp.where(ci < nch, cn_ref[cc], 0)

        def body(j, c):
            if wait:
                copy(cc, j).wait()
            else:
                copy(cc, j).start()
            return c

        lax.fori_loop(0, n, body, 0)

    return (lambda ci: run(ci, False)), (lambda ci: run(ci, True))


def _tile_loop(nsub, prepare, compute, commit):
    def run(tiles):
        prepare(tiles)
        for t in tiles:
            compute(*t)
        commit(tiles)

    n8 = nsub // 8
    rem = nsub % 8
    base = n8 * 8 * SUB_ROWS
    r2 = base + (rem // 4) * 4 * SUB_ROWS
    r1 = r2 + ((rem % 4) // 2) * 2 * SUB_ROWS

    @pl.when(rem % 4 >= 2)
    def _():
        run([(pl.multiple_of(r2, SUB_ROWS), 2 * SUB_ROWS, 5)])

    @pl.when(rem % 2 == 1)
    def _():
        run([(pl.multiple_of(r1, SUB_ROWS), SUB_ROWS, 6)])

    @pl.when(rem >= 4)
    def _():
        run([(pl.multiple_of(base, 4 * SUB_ROWS), 4 * SUB_ROWS, 4)])

    def pair(j, c):
        r = pl.multiple_of(j * 8 * SUB_ROWS, 8 * SUB_ROWS)
        slot = (j % 2) * 2
        run([(r, 4 * SUB_ROWS, slot), (r + 4 * SUB_ROWS, 4 * SUB_ROWS, slot + 1)])
        return c

    lax.fori_loop(0, n8, pair, 0)


def _weight_stream(ce_ref, cn_ref, nf, w_hbms, tn, wf_ref, wb_ref, wsem):
    def copies(cc, f):
        cols = pl.ds(_mult(f * tn, tn), tn)
        return [pltpu.make_async_copy(w.at[ce_ref[cc], :, cols], wf_ref.at[f % 2, i], wsem.at[f % 2])
                for i, w in enumerate(w_hbms)]

    def first():
        for cp in copies(0, 0):
            cp.start()

    def advance(c, f):
        q = c * nf + f + 1
        cc = jnp.minimum(q // nf, cn_ref.shape[0] - 1)

        @pl.when(jnp.logical_and(q // nf < cn_ref.shape[0], cn_ref[cc] > 0))
        def _():
            for cp in copies(cc, q % nf):
                cp.start()

        for cp in copies(c, f):
            cp.wait()
        for i in range(len(w_hbms)):
            wb_ref[:, i * tn:(i + 1) * tn] = wf_ref[f % 2, i].astype(BF16)

    return first, advance


def _moe_up_kernel(ce_ref, cs_ref, cn_ref, tail_ref, xs_hbm, wg_hbm, bg_ref, wu_hbm, bu_ref, act_hbm,
                   land_ref, xb_ref, wf_ref, wgu_ref, ab_ref, zero_ref, pend_ref, sem, wsem, osem):
    c = pl.program_id(0)
    nsub = cn_ref[c]
    start = pl.multiple_of(cs_ref[c], SUB_ROWS)
    tf = wf_ref.shape[3]
    nf = bg_ref.shape[2] // tf
    ns = TOK_WORDS // LANES
    start_chunk, wait_chunk = _chunk_dma(cs_ref, cn_ref, xs_hbm, land_ref, ns, sem)
    first_weights, next_weights = _weight_stream(ce_ref, cn_ref, nf, (wg_hbm, wu_hbm), tf, wf_ref, wgu_ref, wsem)

    def out_copy(r, rows, slot, f):
        return pltpu.make_async_copy(
            ab_ref.at[slot, pl.ds(0, rows)],
            act_hbm.at[pl.ds(pl.multiple_of(start + r, SUB_ROWS), rows), pl.ds(_mult(f * tf, tf), tf)],
            osem.at[slot])

    @pl.when(c == 0)
    def _():
        zero_ref[...] = jnp.zeros_like(zero_ref)
        _zero_fill_rows(zero_ref, act_hbm, tail_ref[0], tail_ref[1], osem.at[0])
        for slot in range(len(SLOT_ROWS)):
            pend_ref[slot] = 0
        start_chunk(c)
        first_weights()

    @pl.when(nsub > 0)
    def _():
        wait_chunk(c)

        def unpack(j, carry):
            r = pl.multiple_of(j * SUB_ROWS, SUB_ROWS)
            for s in range(ns):
                w = land_ref[pl.ds(r * ns + s, SUB_ROWS, stride=ns), :]
                xb_ref[pl.ds(r, SUB_ROWS), s * LANES:(s + 1) * LANES] = pltpu.bitcast(w << 16, F32).astype(BF16)
                xb_ref[pl.ds(r, SUB_ROWS), TOK_WORDS + s * LANES:TOK_WORDS + (s + 1) * LANES] = (
                    pltpu.bitcast(w & jnp.uint32(0xFFFF0000), F32).astype(BF16))
            return carry

        lax.fori_loop(0, nsub, unpack, 0)
        start_chunk(c + 1)

        def column_block(f, carry):
            next_weights(c, f)
            cols = pl.ds(pl.multiple_of(f * tf, tf), tf)
            bg = bg_ref[0, :, cols]
            bu = bu_ref[0, :, cols]

            def prepare(tiles):
                for _, rows, slot in tiles:
                    @pl.when(pend_ref[slot] == 1)
                    def _():
                        out_copy(0, rows, slot, f).wait()

            def compute(r, rows, slot):
                z = jnp.dot(xb_ref[pl.ds(r, rows), :], wgu_ref[...], preferred_element_type=F32)
                g = z[:, :tf] + bg
                u = z[:, tf:] + bu
                g = jnp.minimum(g, SWIGLU_LIMIT)
                u = jnp.clip(u, -SWIGLU_LIMIT, SWIGLU_LIMIT)
                ab_ref[slot, pl.ds(0, rows), :] = (g * jax.nn.sigmoid(SWIGLU_ALPHA * g) * (u + 1.0)).astype(BF16)

            def commit(tiles):
                for r, rows, slot in tiles:
                    out_copy(r, rows, slot, f).start()
                    pend_ref[slot] = 1

            _tile_loop(nsub, prepare, compute, commit)
            return carry

        lax.fori_loop(0, nf, column_block, 0)

    @pl.when(c == pl.num_programs(0) - 1)
    def _():
        for slot, rows in enumerate(SLOT_ROWS):
            @pl.when(pend_ref[slot] == 1)
            def _():
                out_copy(0, rows, slot, 0).wait()


def _moe_up(ch_e, ch_start, ch_nsub, tail, xs, w_gate, b_gate, w_up, b_up):
    ns = TOK_WORDS // LANES
    p_alloc = xs.shape[0] // ns
    d = 2 * TOK_WORDS
    dff = w_gate.shape[2]
    tf = MOE_TF
    assert (dff // tf) % 2 == 0
    nch = ch_e.shape[0]
    bmap = lambda c, e, s, n, t: (e[c], 0, 0)
    return pl.pallas_call(
        _moe_up_kernel,
        grid_spec=pltpu.PrefetchScalarGridSpec(
            num_scalar_prefetch=4,
            grid=(nch,),
            in_specs=[
                pl.BlockSpec(memory_space=pl.ANY),
                pl.BlockSpec(memory_space=pl.ANY),
                pl.BlockSpec((1, 1, dff), bmap),
                pl.BlockSpec(memory_space=pl.ANY),
                pl.BlockSpec((1, 1, dff), bmap),
            ],
            out_specs=pl.BlockSpec(memory_space=pl.ANY),
            scratch_shapes=[
                pltpu.VMEM((MOE_TM * ns, LANES), jnp.uint32),
                pltpu.VMEM((MOE_TM, d), BF16),
                pltpu.VMEM((2, 2, d, tf), F32),
                pltpu.VMEM((d, 2 * tf), BF16),
                pltpu.VMEM((len(SLOT_ROWS), 4 * SUB_ROWS, tf), BF16),
                pltpu.VMEM((SUB_ROWS, dff), BF16),
                pltpu.SMEM((len(SLOT_ROWS),), jnp.int32),
                pltpu.SemaphoreType.DMA,
                pltpu.SemaphoreType.DMA((2,)),
                pltpu.SemaphoreType.DMA((len(SLOT_ROWS),)),
            ],
        ),
        out_shape=jax.ShapeDtypeStruct((p_alloc, dff), BF16),
        compiler_params=_cparams(("arbitrary",)),
        name="moe_up",
    )(ch_e, ch_start, ch_nsub, tail, xs, w_gate, b_gate.reshape(N_EXPERTS, 1, dff), w_up,
      b_up.reshape(N_EXPERTS, 1, dff))


def _moe_down_kernel(ce_ref, cs_ref, cn_ref, tail_ref, act_hbm, wd_hbm, bd_ref, ys_hbm,
                     x_ref, wf_ref, wdb_ref, yb_ref, zero_ref, pend_ref, sem, wsem, osem):
    c = pl.program_id(0)
    nsub = cn_ref[c]
    start = pl.multiple_of(cs_ref[c], SUB_ROWS)
    tn = wf_ref.shape[3]
    nf = bd_ref.shape[2] // tn
    ns = TOK_WORDS // LANES
    nw = tn // 2 // LANES
    chunk = [_chunk_dma(cs_ref, cn_ref, act_hbm, x_ref.at[i], 1, sem.at[i]) for i in range(2)]
    first_weights, next_weights = _weight_stream(ce_ref, cn_ref, nf, (wd_hbm,), tn, wf_ref, wdb_ref, wsem)

    def on_parity(ci, fn):
        for p in range(2):
            @pl.when(ci % 2 == p)
            def _():
                fn(p)

    def out_copy(r, rows, par):
        src = yb_ref.at[par, pl.ds(pl.multiple_of(r * ns, SUB_ROWS * ns), rows * ns)]
        dst = ys_hbm.at[pl.ds(pl.multiple_of((start + r) * ns, SUB_ROWS * ns), rows * ns)]
        return pltpu.make_async_copy(src, dst, osem.at[par])

    def drain(par):
        for kind, rows in enumerate(SLOT_ROWS[-3:]):
            def wait(j, carry):
                out_copy(0, rows, par).wait()
                return carry

            lax.fori_loop(0, pend_ref[par * 3 + kind], wait, 0)
            pend_ref[par * 3 + kind] = 0

    @pl.when(c == 0)
    def _():
        zero_ref[...] = jnp.zeros_like(zero_ref)
        _zero_fill_rows(zero_ref, ys_hbm, tail_ref[0], tail_ref[1], osem.at[0])
        for kind in range(6):
            pend_ref[kind] = 0
        chunk[0][0](c)
        first_weights()

    def run_chunk():
        for f in range(nf):
            next_weights(c, f)
            bd = bd_ref[0, :, f * tn:(f + 1) * tn]
            last = f == nf - 1
            if f == 0:
                drain(c % 2)

            def compute(r, rows, slot, f=f, bd=bd):
                y = jnp.dot(x_ref[c % 2, pl.ds(r, rows), :], wdb_ref[...], preferred_element_type=F32) + bd
                lo = pltpu.bitcast(y[:, :tn // 2].astype(BF16).astype(F32), jnp.uint32)
                hi = pltpu.bitcast(y[:, tn // 2:].astype(BF16).astype(F32), jnp.uint32)
                packed = (hi & jnp.uint32(0xFFFF0000)) | (lo >> 16)
                for s in range(nw):
                    yb_ref[c % 2, pl.ds(r * ns + f * nw + s, rows, stride=ns), :] = (
                        packed[:, s * LANES:(s + 1) * LANES])

            def commit(tiles, last=last):
                if last:
                    for r, rows, slot in tiles:
                        out_copy(r, rows, c % 2).start()
                        kind = (c % 2) * 3 + SLOT_ROWS[-3:].index(rows)
                        pend_ref[kind] = pend_ref[kind] + 1

            _tile_loop(nsub, lambda tiles: None, compute, commit)

    @pl.when(nsub > 0)
    def _():
        on_parity(c, lambda p: chunk[p][1](c))
        on_parity(c + 1, lambda p: chunk[p][0](c + 1))
        run_chunk()

    @pl.when(c == pl.num_programs(0) - 1)
    def _():
        drain(0)
        drain(1)


def _moe_down(ch_e, ch_start, ch_nsub, tail, act, w_down, b_down):
    p_alloc, dff = act.shape
    d = w_down.shape[2]
    ns = TOK_WORDS // LANES
    tn = MOE_TN
    assert (d // tn) % 2 == 0
    nch = ch_e.shape[0]
    return pl.pallas_call(
        _moe_down_kernel,
        grid_spec=pltpu.PrefetchScalarGridSpec(
            num_scalar_prefetch=4,
            grid=(nch,),
            in_specs=[
                pl.BlockSpec(memory_space=pl.ANY),
                pl.BlockSpec(memory_space=pl.ANY),
                pl.BlockSpec((1, 1, d), lambda c, e, s, n, t: (e[c], 0, 0)),
            ],
            out_specs=pl.BlockSpec(memory_space=pl.ANY),
            scratch_shapes=[
                pltpu.VMEM((2, MOE_TM, dff), BF16),
                pltpu.VMEM((2, 1, dff, tn), F32),
                pltpu.VMEM((dff, tn), BF16),
                pltpu.VMEM((2, MOE_TM * ns, LANES), jnp.uint32),
                pltpu.VMEM((SUB_ROWS * ns, LANES), jnp.uint32),
                pltpu.SMEM((6,), jnp.int32),
                pltpu.SemaphoreType.DMA((2,)),
                pltpu.SemaphoreType.DMA((2,)),
                pltpu.SemaphoreType.DMA((2,)),
            ],
        ),
        out_shape=jax.ShapeDtypeStruct((p_alloc * ns, LANES), jnp.uint32),
        compiler_params=pltpu.CompilerParams(dimension_semantics=("arbitrary",), vmem_limit_bytes=VMEM_LIMIT_MAX),
        name="moe_down",
    )(ch_e, ch_start, ch_nsub, tail, act, w_down, b_down.reshape(N_EXPERTS, 1, d))


def _combine_kernel(pos_hbm, ys_hbm, h_ref, w_ref, g_ref, o_ref, pos_smem, yb_ref, sem, psem):
    i = pl.program_id(0)
    tm, d = h_ref.shape
    n = TOP_K * tm
    ns = TOK_WORDS // LANES

    def pos_copy(tile):
        third = pl.multiple_of((tile % 3) * n, n)
        return pltpu.make_async_copy(pos_hbm.at[pl.ds(pl.multiple_of(tile * n, n), n)],
                                     pos_smem.at[pl.ds(third, n)], psem.at[tile % 3])

    def fetch(tile, slot):
        pos_copy(tile).wait()

        @pl.when(tile + 1 < pl.num_programs(0))
        def _():
            pos_copy(tile + 1).start()

        base = (tile % 3) * n

        def issue(jj, c):
            for u in range(ROW_DMA_UNROLL):
                j = jj * ROW_DMA_UNROLL + u
                src = ys_hbm.at[pl.ds(pl.multiple_of(pos_smem[base + j] * ns, ns), ns)]
                dst = yb_ref.at[slot, pl.ds(pl.multiple_of(j * ns, ns), ns)]
                pltpu.make_async_copy(src, dst, sem.at[slot]).start(priority=u % 2)
            return c

        lax.fori_loop(0, n // ROW_DMA_UNROLL, issue, 0)

    @pl.when(i == 0)
    def _():
        pos_copy(i).start()
        fetch(i, 0)

    @pl.when(i + 1 < pl.num_programs(0))
    def _():
        for p in range(2):
            @pl.when((i + 1) % 2 == p)
            def _():
                fetch(i + 1, p)

    def reduce(slot):
        pltpu.make_async_copy(ys_hbm.at[pl.ds(0, n * ns)], yb_ref.at[slot], sem.at[slot]).wait()
        wb = [jnp.broadcast_to(w_ref[:, k:k + 1], (tm, LANES)) for k in range(TOP_K)]
        ssq = jnp.zeros((tm, 1), F32)
        nw = MOE_TN // 2 // LANES
        for s in range(ns):
            c0 = (s // nw) * MOE_TN + (s % nw) * LANES
            lo_cols = slice(c0, c0 + LANES)
            hi_cols = slice(c0 + MOE_TN // 2, c0 + MOE_TN // 2 + LANES)
            acc_lo = h_ref[:, lo_cols]
            acc_hi = h_ref[:, hi_cols]
            for k in range(TOP_K):
                w = yb_ref[slot, pl.ds(k * tm * ns + s, tm, stride=ns), :]
                acc_lo = acc_lo + wb[k] * pltpu.bitcast(w << 16, F32)
                acc_hi = acc_hi + wb[k] * pltpu.bitcast(w & jnp.uint32(0xFFFF0000), F32)
            o_ref[:, lo_cols] = acc_lo
            o_ref[:, hi_cols] = acc_hi
            ssq = ssq + jnp.sum(acc_lo * acc_lo + acc_hi * acc_hi, axis=-1, keepdims=True)
        o_ref[...] = o_ref[...] * lax.rsqrt(ssq * (1.0 / d) + NORM_EPS) * g_ref[...]

    for p in range(2):
        @pl.when(i % 2 == p)
        def _():
            reduce(p)


def _combine(pos_tiles, ys, h1, wts_t, g_final, tm):
    rows, d = h1.shape
    ns = TOK_WORDS // LANES
    return pl.pallas_call(
        _combine_kernel,
        grid=(rows // tm,),
        in_specs=[
            pl.BlockSpec(memory_space=pl.ANY),
            pl.BlockSpec(memory_space=pl.ANY),
            pl.BlockSpec((tm, d), lambda i: (i, 0)),
            pl.BlockSpec((tm, TOP_K), lambda i: (i, 0)),
            pl.BlockSpec((1, d), lambda i: (0, 0)),
        ],
        out_specs=pl.BlockSpec((tm, d), lambda i: (i, 0)),
        out_shape=jax.ShapeDtypeStruct((rows, d), F32),
        scratch_shapes=[
            pltpu.SMEM((3 * TOP_K * tm,), jnp.int32),
            pltpu.VMEM((2, TOP_K * tm * ns, LANES), jnp.uint32),
            pltpu.SemaphoreType.DMA((2,)),
            pltpu.SemaphoreType.DMA((3,)),
        ],
        compiler_params=_cparams(("arbitrary",)),
        name="combine",
    )(pos_tiles, ys, h1, wts_t, g_final)


def _rope_tables(n_pos):
    half = HEAD_DIM // 2
    inv = 1.0 / (ROPE_THETA ** (jnp.arange(half, dtype=F32) / half))
    ang = jnp.arange(n_pos, dtype=F32)[:, None] * inv[None, :]
    cos = jnp.tile(jnp.cos(ang), (1, LANES // half))
    sin = jnp.tile(jnp.concatenate([-jnp.sin(ang), jnp.sin(ang)], axis=1), (1, LANES // HEAD_DIM))
    return cos, sin


def _block_diag(w):
    per = LRU_GROUP // LRU_BLOCK
    w4 = w.reshape(D_LRU // LRU_GROUP, per, LRU_BLOCK, LRU_BLOCK)
    eye = jnp.eye(per, dtype=w.dtype)
    bd = jnp.einsum("gpcd,pq->gpcqd", w4, eye)
    return bd.reshape(D_LRU // LRU_GROUP, LRU_GROUP, LRU_GROUP).astype(BF16)


def _tile_positions(pos, tm):
    rows = pos.shape[1]
    return pos.reshape(TOP_K, rows // tm, tm).transpose(1, 0, 2).reshape(-1)


def _chunk_schedule(counts, n_rows):
    aligned = ((counts + SUB_ROWS - 1) // SUB_ROWS) * SUB_ROWS
    offs = jnp.cumsum(aligned) - aligned
    n_ch = (aligned + MOE_TM - 1) // MOE_TM
    cum = jnp.cumsum(n_ch)
    total = cum[-1]
    nch_max = N_EXPERTS + (n_rows + N_EXPERTS * SUB_ROWS) // MOE_TM
    c = jnp.arange(nch_max, dtype=jnp.int32)
    cc = jnp.minimum(c, total - 1)
    e = jnp.searchsorted(cum, cc, side="right").astype(jnp.int32)
    j = cc - (cum[e] - n_ch[e])
    start = offs[e] + j * MOE_TM
    nsub = jnp.where(c < total, jnp.minimum(MOE_TM, aligned[e] - j * MOE_TM) // SUB_ROWS, 0)
    tails = offs + (counts // SUB_ROWS) * SUB_ROWS
    used = jnp.sum(aligned)
    tail = jnp.stack([used, (n_rows + N_EXPERTS * SUB_ROWS - used) // SUB_ROWS])
    tails = jnp.concatenate([tails, tail])
    return (offs, e, start.astype(jnp.int32), nsub.astype(jnp.int32), tails.astype(jnp.int32),
            tail.astype(jnp.int32))


def kernel(x, meta_tokens, norm_mix, w_in, b_in, sinks, conv_w, conv_b, w_a, b_a, w_i, b_i, lru_lambda,
           g_attn_out, g_lru_out, w_out, b_out, norm_ffn, w_router, b_router, w_gate, b_gate, w_up, b_up,
           w_down, b_down, final_norm):
    batch, seq, d = x.shape
    rows = batch * seq
    x2d = x.reshape(rows, d)
    row = lambda v: v.reshape(1, -1)

    w_in_bf = w_in[0].astype(BF16)
    w_out_bf = w_out[0].astype(BF16)
    cos, sin = _rope_tables(N_META + seq)
    wa_bd, wi_bd = _block_diag(w_a[0]), _block_diag(w_i[0])
    lru_args = (conv_w[0], row(conv_b[0]), wa_bd, row(b_a[0]), wi_bd, row(b_i[0]), row(lru_lambda[0]),
                row(g_lru_out[0]))

    _, kvm, xgm = _inproj(meta_tokens, row(norm_mix[0]), w_in_bf, row(b_in[0]), cos[:N_META], sin[:N_META],
                          N_META, 1)
    _, h0, tail0 = _lru(xgm, *lru_args, jnp.zeros((1, D_LRU), F32), jnp.zeros((8, D_LRU), F32), 1, N_META,
                        N_META)

    tm = 512
    q, kv, xg = _inproj(x2d, row(norm_mix[0]), w_in_bf, row(b_in[0]), cos[N_META:], sin[N_META:], tm,
                        seq // tm)
    kvm = jnp.pad(kvm, ((0, BLOCK - N_META), (0, 0)))
    attn_n = _attention(sinks[0], q, kv, kvm, row(g_attn_out[0]), batch, seq)
    lru_n, _, _ = _lru(xg, *lru_args, h0, tail0, batch, seq, 128)

    tri = (jnp.arange(tm)[:, None] <= jnp.arange(tm)[None, :]).astype(BF16)
    h1, xp, ids, wts, rank, cnt = _outproj(
        attn_n, lru_n, w_out_bf, row(b_out[0]), x2d, row(norm_ffn[0]), w_router[0].T.astype(BF16),
        b_router[0].reshape(N_EXPERTS, 1), tri, tm)

    counts = cnt[:, 0].astype(jnp.int32)
    n_rows = rows * TOP_K
    offs, ch_e, ch_start, ch_nsub, tails, tail = _chunk_schedule(counts, n_rows)
    onehot = ids[..., None] == jnp.arange(N_EXPERTS, dtype=jnp.int32)
    pos = jnp.sum(jnp.where(onehot, offs, 0), axis=-1) + rank
    p_alloc = n_rows + N_EXPERTS * SUB_ROWS

    tmd = 512
    xs = _dispatch(tails, _tile_positions(pos, tmd), xp, p_alloc, tmd)
    act = _moe_up(ch_e, ch_start, ch_nsub, tail, xs, w_gate[0], b_gate[0], w_up[0], b_up[0])
    ys = _moe_down(ch_e, ch_start, ch_nsub, tail, act, w_down[0], b_down[0])
    tmc = 256
    out = _combine(_tile_positions(pos, tmc), ys, h1, wts.T, row(final_norm), tmc)
    return out.reshape(batch, seq, d)
```

```python
import jax
import jax.numpy as jnp
from jax import lax
from jax.experimental import pallas as pl
from jax.experimental.pallas import tpu as pltpu

F32 = jnp.float32
BF16 = jnp.bfloat16

N_META = 16
HEAD_DIM = 64
N_Q_HEADS = 16
N_KV_HEADS = 4
GROUP = N_Q_HEADS // N_KV_HEADS
D_ATTN = N_Q_HEADS * HEAD_DIM
D_KV = N_KV_HEADS * HEAD_DIM
KV_COLS = 4 * D_KV
BLOCK = 128
ROPE_THETA = 10000.0
D_LRU = 1024
LRU_BLOCK = 64
LRU_GROUP = 256
CONV_WIDTH = 4
LRU_C = 8.0
N_EXPERTS = 32
TOP_K = 4
SWIGLU_LIMIT = 7.0
SWIGLU_ALPHA = 1.702
NORM_EPS = 1e-5
NEG_INF = -1e30

LANES = 128
D_MODEL = 2048
TOK_WORDS = D_MODEL // 2
SUB_ROWS = 128
MOE_TM = 2304
MOE_TF = 512
MOE_TN = 1024
ROW_TILE = 512
INPROJ_COLS = 512
LRU_TIME_TILE = 128
COMBINE_TILE = 256
ROW_DMA_UNROLL = 16
SLOT_ROWS = (4 * SUB_ROWS,) * 5 + (2 * SUB_ROWS, SUB_ROWS)
VMEM_LIMIT = 56 * 1024 * 1024
VMEM_LIMIT_MAX = 62 * 1024 * 1024


def _cparams(sem):
    return pltpu.CompilerParams(dimension_semantics=sem, vmem_limit_bytes=VMEM_LIMIT)


def _mult(x, m):
    return x if isinstance(x, int) else pl.multiple_of(x, m)


def _rms(x, g):
    return x * lax.rsqrt(jnp.mean(x * x, axis=-1, keepdims=True) + NORM_EPS) * g


def _inproj_kernel(x_ref, g_ref, w_ref, b_ref, cos_ref, sin_ref, q_ref, kv_ref, xg_ref):
    xn = _rms(x_ref[...], g_ref[...]).astype(BF16)
    cos = cos_ref[...]
    sin = sin_ref[...]
    lane = lax.broadcasted_iota(jnp.int32, (1, LANES), 1)
    first_half = (lane % HEAD_DIM) < (HEAD_DIM // 2)

    def rope(z):
        partner = jnp.where(first_half, pltpu.roll(z, LANES - HEAD_DIM // 2, 1),
                            pltpu.roll(z, HEAD_DIM // 2, 1))
        return z * cos + partner * sin

    cw = INPROJ_COLS
    for c in range(w_ref.shape[1] // cw):
        z = jnp.dot(xn, w_ref[:, c * cw:(c + 1) * cw], preferred_element_type=F32)
        z = z + b_ref[:, c * cw:(c + 1) * cw]
        if c < 2:
            for j in range(cw // LANES):
                zz = rope(z[:, j * LANES:(j + 1) * LANES]) * (HEAD_DIM ** -0.5)
                q_ref[:, c * cw + j * LANES:c * cw + (j + 1) * LANES] = zz.astype(BF16)
        elif c == 2:
            low = lane < HEAD_DIM
            for j in range(2 * D_KV // LANES):
                zz = z[:, j * LANES:(j + 1) * LANES]
                zz = rope(zz) if j < D_KV // LANES else zz
                sw = pltpu.roll(zz, HEAD_DIM, 1)
                kv_ref[:, 2 * j * LANES:(2 * j + 1) * LANES] = jnp.where(low, zz, sw).astype(BF16)
                kv_ref[:, (2 * j + 1) * LANES:(2 * j + 2) * LANES] = jnp.where(low, sw, zz).astype(BF16)
        else:
            xg_ref[:, (c - 3) * cw:(c - 2) * cw] = z


def _inproj(x2d, g, w_bf, b, cos, sin, tm, pos_blocks):
    rows, d = x2d.shape
    dz = w_bf.shape[1]
    return pl.pallas_call(
        _inproj_kernel,
        grid=(rows // tm,),
        in_specs=[
            pl.BlockSpec((tm, d), lambda m: (m, 0)),
            pl.BlockSpec((1, d), lambda m: (0, 0)),
            pl.BlockSpec((d, dz), lambda m: (0, 0)),
            pl.BlockSpec((1, dz), lambda m: (0, 0)),
            pl.BlockSpec((tm, LANES), lambda m: (m % pos_blocks, 0)),
            pl.BlockSpec((tm, LANES), lambda m: (m % pos_blocks, 0)),
        ],
        out_specs=[
            pl.BlockSpec((tm, D_ATTN), lambda m: (m, 0)),
            pl.BlockSpec((tm, KV_COLS), lambda m: (m, 0)),
            pl.BlockSpec((tm, 2 * D_LRU), lambda m: (m, 0)),
        ],
        out_shape=[
            jax.ShapeDtypeStruct((rows, D_ATTN), BF16),
            jax.ShapeDtypeStruct((rows, KV_COLS), BF16),
            jax.ShapeDtypeStruct((rows, 2 * D_LRU), F32),
        ],
        compiler_params=_cparams(("arbitrary",)),
        name="inproj",
    )(x2d, g, w_bf, b, cos, sin)


def _attn_kernel(sink_ref, q_ref, kvc_ref, kvp_ref, kvm_ref, g_ref, o_ref):
    n = pl.program_id(1)
    row = lax.broadcasted_iota(jnp.int32, (GROUP * BLOCK, BLOCK), 0) % BLOCK
    col = lax.broadcasted_iota(jnp.int32, (GROUP * BLOCK, BLOCK), 1)
    in_cur = col <= row
    in_prev = jnp.logical_and(col > row, n > 0)
    is_meta = col < N_META
    low = lax.broadcasted_iota(jnp.int32, (BLOCK, LANES), 1) < HEAD_DIM
    nt = (((1,), (1,)), ((), ()))
    zero = jnp.zeros((), BF16)
    outs = []
    for h in range(N_KV_HEADS):
        ks = slice(h * LANES, (h + 1) * LANES)
        vs = slice(N_KV_HEADS * LANES + h * LANES, N_KV_HEADS * LANES + (h + 1) * LANES)
        parts = []
        for j in range(GROUP // 2):
            qg = q_ref[:, (h * GROUP // 2 + j) * LANES:(h * GROUP // 2 + j + 1) * LANES]
            parts += [jnp.where(low, qg, zero), jnp.where(low, zero, qg)]
        qs = jnp.concatenate(parts, axis=0)
        s_c = lax.dot_general(qs, kvc_ref[:, ks], nt, preferred_element_type=F32)
        s_p = lax.dot_general(qs, kvp_ref[:, ks], nt, preferred_element_type=F32)
        s_m = lax.dot_general(qs, kvm_ref[:, ks], nt, preferred_element_type=F32)
        s_b = jnp.where(in_prev, s_p, jnp.where(in_cur, s_c, NEG_INF))
        s_m = jnp.where(is_meta, s_m, NEG_INF)
        sink = jnp.concatenate(
            [jnp.full((BLOCK, 1), sink_ref[h * GROUP + g], F32) for g in range(GROUP)], axis=0)
        m = jnp.maximum(jnp.max(jnp.maximum(s_b, s_m), axis=-1, keepdims=True), sink)
        p_b = jnp.exp(s_b - m)
        p_m = jnp.exp(s_m - m)
        den = jnp.sum(p_b + p_m, axis=-1, keepdims=True) + jnp.exp(sink - m)
        r = (jnp.dot(jnp.where(in_cur, p_b, 0.0).astype(BF16), kvc_ref[:, vs], preferred_element_type=F32)
             + jnp.dot(jnp.where(in_cur, 0.0, p_b).astype(BF16), kvp_ref[:, vs], preferred_element_type=F32)
             + jnp.dot(p_m.astype(BF16), kvm_ref[:, vs], preferred_element_type=F32))
        r = r / den
        for j in range(GROUP // 2):
            outs.append(jnp.where(low, r[2 * j * BLOCK:(2 * j + 1) * BLOCK], r[(2 * j + 1) * BLOCK:(2 * j + 2) * BLOCK]))
    o_all = jnp.concatenate(outs, axis=1)
    o_ref[...] = _rms(o_all, g_ref[...]).astype(BF16)


def _attention(sinks, q, kv, kvm, g_attn, batch, seq):
    nb = seq // BLOCK
    return pl.pallas_call(
        _attn_kernel,
        grid_spec=pltpu.PrefetchScalarGridSpec(
            num_scalar_prefetch=1,
            grid=(batch, nb),
            in_specs=[
                pl.BlockSpec((BLOCK, D_ATTN), lambda b, n, s: (b * nb + n, 0)),
                pl.BlockSpec((BLOCK, KV_COLS), lambda b, n, s: (b * nb + n, 0)),
                pl.BlockSpec((BLOCK, KV_COLS), lambda b, n, s: (b * nb + jnp.maximum(n - 1, 0), 0)),
                pl.BlockSpec((BLOCK, KV_COLS), lambda b, n, s: (0, 0)),
                pl.BlockSpec((1, D_ATTN), lambda b, n, s: (0, 0)),
            ],
            out_specs=pl.BlockSpec((BLOCK, D_ATTN), lambda b, n, s: (b * nb + n, 0)),
        ),
        out_shape=jax.ShapeDtypeStruct((batch * seq, D_ATTN), BF16),
        compiler_params=_cparams(("arbitrary", "arbitrary")),
        name="attention",
    )(sinks, q, kv, kv, kvm, g_attn)


def _one_minus_sq(a, log_a):
    y = 2.0 * log_a
    p = 1.0 + y * (1.0 / 4.0)
    for k in (3, 2):
        p = 1.0 + y * p * (1.0 / k)
    return jnp.where(y > -1.0 / 64.0, -(y * p), 1.0 - a * a)


def _lru_kernel(xg_ref, cw_ref, cb_ref, wa_ref, ba_ref, wi_ref, bi_ref, lam_ref, g_ref, h0_ref, tail0_ref,
                o_ref, hout_ref, tailout_ref, ext_ref, tail_ref, a_ref, b_ref, h_ref):
    tt = pl.program_id(0)
    batch, rows = xg_ref.shape[0], xg_ref.shape[1]

    @pl.when(tt == 0)
    def _():
        for g in range(D_LRU // LANES):
            h_ref[g] = jnp.broadcast_to(h0_ref[:, g * LANES:(g + 1) * LANES], (batch, LANES))
        for s in range(batch):
            tail_ref[s] = tail0_ref[...]

    sp = jax.nn.softplus(-lam_ref[...])

    def gates(s, carry):
        ext_ref[0:8, :] = tail_ref[s]
        ext_ref[8:, :] = xg_ref[s, :, :D_LRU]
        tail_ref[s] = ext_ref[rows:rows + 8, :]
        xc = cb_ref[...] + sum(cw_ref[j:j + 1, :] * ext_ref[5 + j:5 + j + rows, :] for j in range(CONV_WIDTH))
        xcb = xc.astype(BF16)
        for c in range(D_LRU // LRU_GROUP):
            cs = slice(c * LRU_GROUP, (c + 1) * LRU_GROUP)
            r = jax.nn.sigmoid(jnp.dot(xcb[:, cs], wa_ref[c], preferred_element_type=F32) + ba_ref[:, cs])
            i = jax.nn.sigmoid(jnp.dot(xcb[:, cs], wi_ref[c], preferred_element_type=F32) + bi_ref[:, cs])
            log_a = -LRU_C * r * sp[:, cs]
            a = jnp.exp(log_a)
            b = jnp.sqrt(_one_minus_sq(a, log_a)) * i * xc[:, cs]
            for g in range(LRU_GROUP // LANES):
                lanes = slice(g * LANES, (g + 1) * LANES)
                a_ref[c * (LRU_GROUP // LANES) + g, pl.ds(s, rows, stride=batch), :] = a[:, lanes]
                b_ref[c * (LRU_GROUP // LANES) + g, pl.ds(s, rows, stride=batch), :] = b[:, lanes]
        return carry

    lax.fori_loop(0, batch, gates, 0)

    def step(t, h):
        r0 = pl.multiple_of(t * batch, batch)
        new = []
        for g in range(D_LRU // LANES):
            hg = a_ref[g, pl.ds(r0, batch), :] * h[g] + b_ref[g, pl.ds(r0, batch), :]
            b_ref[g, pl.ds(r0, batch), :] = hg
            new.append(hg)
        return tuple(new)

    h_last = lax.fori_loop(0, rows, step, tuple(h_ref[g] for g in range(D_LRU // LANES)))
    for g in range(D_LRU // LANES):
        h_ref[g] = h_last[g]
    hout_ref[...] = jnp.concatenate([h_last[g][0:1] for g in range(D_LRU // LANES)], axis=1)
    tailout_ref[...] = tail_ref[0]

    def finish(s, carry):
        hs = jnp.concatenate([b_ref[g, pl.ds(s, rows, stride=batch), :] for g in range(D_LRU // LANES)], axis=1)
        y = hs * jax.nn.gelu(xg_ref[s, :, D_LRU:])
        o_ref[s] = _rms(y, g_ref[...]).astype(BF16)
        return carry

    lax.fori_loop(0, batch, finish, 0)


def _lru(xg, conv_w, conv_b, wa_bd, b_a, wi_bd, b_i, lam, g_lru, h0, tail0, batch, seq, tt):
    ntt = seq // tt
    ng = D_LRU // LANES
    vec = pl.BlockSpec((1, D_LRU), lambda t: (0, 0))
    wspec = pl.BlockSpec((D_LRU // LRU_GROUP, LRU_GROUP, LRU_GROUP), lambda t: (0, 0, 0))
    out, h_last, tail = pl.pallas_call(
        _lru_kernel,
        grid=(ntt,),
        in_specs=[
            pl.BlockSpec((batch, tt, 2 * D_LRU), lambda t: (0, t, 0)),
            pl.BlockSpec((CONV_WIDTH, D_LRU), lambda t: (0, 0)),
            vec, wspec, vec, wspec, vec, vec, vec, vec,
            pl.BlockSpec((8, D_LRU), lambda t: (0, 0)),
        ],
        out_specs=[
            pl.BlockSpec((batch, tt, D_LRU), lambda t: (0, t, 0)),
            pl.BlockSpec((1, D_LRU), lambda t: (0, 0)),
            pl.BlockSpec((8, D_LRU), lambda t: (0, 0)),
        ],
        out_shape=[
            jax.ShapeDtypeStruct((batch, seq, D_LRU), BF16),
            jax.ShapeDtypeStruct((1, D_LRU), F32),
            jax.ShapeDtypeStruct((8, D_LRU), F32),
        ],
        scratch_shapes=[
            pltpu.VMEM((8 + tt, D_LRU), F32),
            pltpu.VMEM((batch, 8, D_LRU), F32),
            pltpu.VMEM((ng, tt * batch, LANES), F32),
            pltpu.VMEM((ng, tt * batch, LANES), F32),
            pltpu.VMEM((ng, batch, LANES), F32),
        ],
        compiler_params=_cparams(("arbitrary",)),
        name="rglru",
    )(xg.reshape(batch, seq, 2 * D_LRU), conv_w, conv_b, wa_bd, b_a, wi_bd, b_i, lam, g_lru, h0, tail0)
    return out.reshape(batch * seq, D_LRU), h_last, tail


def _outproj_kernel(a_ref, l_ref, wo_ref, bo_ref, x_ref, gf_ref, wr_ref, br_ref, tri_ref,
                    h_ref, xp_ref, ids_ref, wts_ref, rank_ref, cnt_ref, carry_ref):
    m = pl.program_id(0)
    tm = a_ref.shape[0]

    @pl.when(m == 0)
    def _():
        carry_ref[...] = jnp.zeros_like(carry_ref)

    mix = jnp.concatenate([a_ref[...], l_ref[...]], axis=1)
    h = jnp.dot(mix, wo_ref[...], preferred_element_type=F32) + bo_ref[...] + x_ref[...]
    h_ref[...] = h
    xn = _rms(h, gf_ref[...]).astype(BF16)
    half = xn.shape[1] // 2
    lo = pltpu.bitcast(xn[:, :half].astype(F32), jnp.uint32)
    hi = pltpu.bitcast(xn[:, half:].astype(F32), jnp.uint32)
    packed = (hi & jnp.uint32(0xFFFF0000)) | (lo >> 16)
    ns = half // LANES
    for s in range(ns):
        xp_ref[pl.ds(s, tm, stride=ns), :] = packed[:, s * LANES:(s + 1) * LANES]

    logits = lax.dot_general(wr_ref[...], xn, (((1,), (1,)), ((), ())), preferred_element_type=F32)
    logits = logits + br_ref[...]
    eidx = lax.broadcasted_iota(jnp.int32, (N_EXPERTS, tm), 0)
    work = logits
    vals, sels = [], []
    for k in range(TOP_K):
        v = jnp.max(work, axis=0, keepdims=True)
        idx = jnp.min(jnp.where(work == v, eidx, N_EXPERTS), axis=0, keepdims=True)
        sel = eidx == idx
        ids_ref[k:k + 1, :] = idx
        vals.append(v)
        sels.append(sel)
        work = jnp.where(sel, -jnp.inf, work)
    es = [jnp.exp(v - vals[0]) for v in vals]
    den = es[0] + es[1] + es[2] + es[3]
    for k in range(TOP_K):
        wts_ref[k:k + 1, :] = es[k] / den
    cnt = sum(s.astype(F32) for s in sels)
    incl = jnp.dot(cnt.astype(BF16), tri_ref[...], preferred_element_type=F32)
    before = incl - cnt + carry_ref[:, 0:1]
    for k in range(TOP_K):
        rk = jnp.sum(jnp.where(sels[k], before, 0.0), axis=0, keepdims=True)
        rank_ref[k:k + 1, :] = rk.astype(jnp.int32)
    carry_ref[...] = carry_ref[...] + incl[:, tm - 1:tm]
    cnt_ref[...] = carry_ref[...]


def _outproj(attn_n, lru_n, wo_bf, b_out, x2d, g_ffn, wr_t, br, tri, tm):
    rows, d = x2d.shape
    const = lambda shape: pl.BlockSpec(shape, lambda m: tuple(0 for _ in shape))
    return pl.pallas_call(
        _outproj_kernel,
        grid=(rows // tm,),
        in_specs=[
            pl.BlockSpec((tm, D_ATTN), lambda m: (m, 0)),
            pl.BlockSpec((tm, D_LRU), lambda m: (m, 0)),
            const((D_ATTN + D_LRU, d)), const((1, d)),
            pl.BlockSpec((tm, d), lambda m: (m, 0)),
            const((1, d)), const((N_EXPERTS, d)), const((N_EXPERTS, 1)), const((tm, tm)),
        ],
        out_specs=[
            pl.BlockSpec((tm, d), lambda m: (m, 0)),
            pl.BlockSpec((tm * (d // 2 // LANES), LANES), lambda m: (m, 0)),
            pl.BlockSpec((TOP_K, tm), lambda m: (0, m)),
            pl.BlockSpec((TOP_K, tm), lambda m: (0, m)),
            pl.BlockSpec((TOP_K, tm), lambda m: (0, m)),
            const((N_EXPERTS, LANES)),
        ],
        out_shape=[
            jax.ShapeDtypeStruct((rows, d), F32),
            jax.ShapeDtypeStruct((rows * (d // 2 // LANES), LANES), jnp.uint32),
            jax.ShapeDtypeStruct((TOP_K, rows), jnp.int32),
            jax.ShapeDtypeStruct((TOP_K, rows), F32),
            jax.ShapeDtypeStruct((TOP_K, rows), jnp.int32),
            jax.ShapeDtypeStruct((N_EXPERTS, LANES), F32),
        ],
        scratch_shapes=[pltpu.VMEM((N_EXPERTS, LANES), F32)],
        compiler_params=_cparams(("arbitrary",)),
        name="outproj_router",
    )(attn_n, lru_n, wo_bf, b_out, x2d, g_ffn, wr_t, br, tri)


def _zero_fill_rows(zero_ref, dst_hbm, first_row, n_blocks, sem):
    blk = zero_ref.shape[0]
    per = blk // SUB_ROWS

    def copy(j):
        row = pl.multiple_of((first_row + j * SUB_ROWS) * per, blk)
        return pltpu.make_async_copy(zero_ref, dst_hbm.at[pl.ds(row, blk)], sem)

    def start(j, c):
        copy(j).start()
        return c

    def wait(j, c):
        copy(j).wait()
        return c

    lax.fori_loop(0, n_blocks, start, 0)
    lax.fori_loop(0, n_blocks, wait, 0)


def _dispatch_kernel(tails_ref, pos_hbm, xp_ref, xs_hbm, pos_smem, zero_ref, sem, psem):
    i = pl.program_id(0)
    ns = TOK_WORDS // LANES
    tm = xp_ref.shape[0] // ns
    n = TOP_K * tm

    def pos_copy(step):
        half = pl.multiple_of((step % 2) * n, n)
        return pltpu.make_async_copy(pos_hbm.at[pl.ds(pl.multiple_of(step * n, n), n)],
                                     pos_smem.at[pl.ds(half, n)], psem.at[step % 2])

    @pl.when(i == 0)
    def _():
        pos_copy(i).start()
        zero_ref[...] = jnp.zeros_like(zero_ref)
        for e in range(N_EXPERTS):
            _zero_fill_rows(zero_ref, xs_hbm, tails_ref[e], 1, sem)
        _zero_fill_rows(zero_ref, xs_hbm, tails_ref[N_EXPERTS], tails_ref[N_EXPERTS + 1], sem)

    pos_copy(i).wait()

    @pl.when(i + 1 < pl.num_programs(0))
    def _():
        pos_copy(i + 1).start()

    base = (i % 2) * n
    for k in range(TOP_K):
        def issue(tt, c):
            for u in range(ROW_DMA_UNROLL):
                t = tt * ROW_DMA_UNROLL + u
                src = xp_ref.at[pl.ds(pl.multiple_of(t * ns, ns), ns)]
                dst = xs_hbm.at[pl.ds(pl.multiple_of(pos_smem[base + k * tm + t] * ns, ns), ns)]
                pltpu.make_async_copy(src, dst, sem).start(priority=u % 2)
            return c

        lax.fori_loop(0, tm // ROW_DMA_UNROLL, issue, 0)
    for _ in range(TOP_K):
        pltpu.make_async_copy(xp_ref, xs_hbm.at[pl.ds(0, tm * ns)], sem).wait()


def _dispatch(tails, pos_tiles, xp, p_alloc, tm):
    ns = TOK_WORDS // LANES
    rows = xp.shape[0] // ns
    return pl.pallas_call(
        _dispatch_kernel,
        grid_spec=pltpu.PrefetchScalarGridSpec(
            num_scalar_prefetch=1,
            grid=(rows // tm,),
            in_specs=[
                pl.BlockSpec(memory_space=pl.ANY),
                pl.BlockSpec((tm * ns, LANES), lambda i, s: (i, 0)),
            ],
            out_specs=pl.BlockSpec(memory_space=pl.ANY),
            scratch_shapes=[
                pltpu.SMEM((2 * TOP_K * tm,), jnp.int32),
                pltpu.VMEM((SUB_ROWS * ns, LANES), jnp.uint32),
                pltpu.SemaphoreType.DMA,
                pltpu.SemaphoreType.DMA((2,)),
            ],
        ),
        out_shape=jax.ShapeDtypeStruct((p_alloc * ns, LANES), jnp.uint32),
        compiler_params=_cparams(("arbitrary",)),
        name="dispatch",
    )(tails, pos_tiles, xp)


def _chunk_dma(cs_ref, cn_ref, src_hbm, dst_ref, per, sem):
    blk = SUB_ROWS * per
    nch = cs_ref.shape[0]

    def copy(ci, j):
        src = src_hbm.at[pl.ds(pl.multiple_of((cs_ref[ci] + j * SUB_ROWS) * per, blk), blk)]
        return pltpu.make_async_copy(src, dst_ref.at[pl.ds(pl.multiple_of(j * blk, blk), blk)], sem)

    def run(ci, wait):
        cc = jnp.minimum(ci, nch - 1)
        n = jnp.where(ci < nch, cn_ref[cc], 0)

        def body(j, c):
            if wait:
                copy(cc, j).wait()
            else:
                copy(cc, j).start()
            return c

        lax.fori_loop(0, n, body, 0)

    return (lambda ci: run(ci, False)), (lambda ci: run(ci, True))


def _tile_loop(nsub, prepare, compute, commit):
    def run(tiles):
        prepare(tiles)
        for t in tiles:
            compute(*t)
        commit(tiles)

    n8 = nsub // 8
    rem = nsub % 8
    base = n8 * 8 * SUB_ROWS
    r2 = base + (rem // 4) * 4 * SUB_ROWS
    r1 = r2 + ((rem % 4) // 2) * 2 * SUB_ROWS

    @pl.when(rem % 4 >= 2)
    def _():
        run([(pl.multiple_of(r2, SUB_ROWS), 2 * SUB_ROWS, 5)])

    @pl.when(rem % 2 == 1)
    def _():
        run([(pl.multiple_of(r1, SUB_ROWS), SUB_ROWS, 6)])

    @pl.when(rem >= 4)
    def _():
        run([(pl.multiple_of(base, 4 * SUB_ROWS), 4 * SUB_ROWS, 4)])

    def pair(j, c):
        r = pl.multiple_of(j * 8 * SUB_ROWS, 8 * SUB_ROWS)
        slot = (j % 2) * 2
        run([(r, 4 * SUB_ROWS, slot), (r + 4 * SUB_ROWS, 4 * SUB_ROWS, slot + 1)])
        return c

    lax.fori_loop(0, n8, pair, 0)


def _weight_stream(ce_ref, cn_ref, nf, w_hbms, tn, wf_ref, wb_ref, wsem):
    def copies(cc, f):
        cols = pl.ds(_mult(f * tn, tn), tn)
        return [pltpu.make_async_copy(w.at[ce_ref[cc], :, cols], wf_ref.at[f % 2, i], wsem.at[f % 2])
                for i, w in enumerate(w_hbms)]

    def first():
        for cp in copies(0, 0):
            cp.start()

    def advance(c, f):
        q = c * nf + f + 1
        cc = jnp.minimum(q // nf, cn_ref.shape[0] - 1)

        @pl.when(jnp.logical_and(q // nf < cn_ref.shape[0], cn_ref[cc] > 0))
        def _():
            for cp in copies(cc, q % nf):
                cp.start()

        for cp in copies(c, f):
            cp.wait()
        for i in range(len(w_hbms)):
            wb_ref[:, i * tn:(i + 1) * tn] = wf_ref[f % 2, i].astype(BF16)

    return first, advance


def _moe_up_kernel(ce_ref, cs_ref, cn_ref, tail_ref, xs_hbm, wg_hbm, bg_ref, wu_hbm, bu_ref, act_hbm,
                   land_ref, xb_ref, wf_ref, wgu_ref, ab_ref, zero_ref, pend_ref, sem, wsem, osem):
    c = pl.program_id(0)
    nsub = cn_ref[c]
    start = pl.multiple_of(cs_ref[c], SUB_ROWS)
    tf = wf_ref.shape[3]
    nf = bg_ref.shape[2] // tf
    ns = TOK_WORDS // LANES
    start_chunk, wait_chunk = _chunk_dma(cs_ref, cn_ref, xs_hbm, land_ref, ns, sem)
    first_weights, next_weights = _weight_stream(ce_ref, cn_ref, nf, (wg_hbm, wu_hbm), tf, wf_ref, wgu_ref, wsem)

    def out_copy(r, rows, slot, f):
        return pltpu.make_async_copy(
            ab_ref.at[slot, pl.ds(0, rows)],
            act_hbm.at[pl.ds(pl.multiple_of(start + r, SUB_ROWS), rows), pl.ds(_mult(f * tf, tf), tf)],
            osem.at[slot])

    @pl.when(c == 0)
    def _():
        zero_ref[...] = jnp.zeros_like(zero_ref)
        _zero_fill_rows(zero_ref, act_hbm, tail_ref[0], tail_ref[1], osem.at[0])
        for slot in range(len(SLOT_ROWS)):
            pend_ref[slot] = 0
        start_chunk(c)
        first_weights()

    @pl.when(nsub > 0)
    def _():
        wait_chunk(c)

        def unpack(j, carry):
            r = pl.multiple_of(j * SUB_ROWS, SUB_ROWS)
            for s in range(ns):
                w = land_ref[pl.ds(r * ns + s, SUB_ROWS, stride=ns), :]
                xb_ref[pl.ds(r, SUB_ROWS), s * LANES:(s + 1) * LANES] = pltpu.bitcast(w << 16, F32).astype(BF16)
                xb_ref[pl.ds(r, SUB_ROWS), TOK_WORDS + s * LANES:TOK_WORDS + (s + 1) * LANES] = (
                    pltpu.bitcast(w & jnp.uint32(0xFFFF0000), F32).astype(BF16))
            return carry

        lax.fori_loop(0, nsub, unpack, 0)
        start_chunk(c + 1)

        def column_block(f, carry):
            next_weights(c, f)
            cols = pl.ds(pl.multiple_of(f * tf, tf), tf)
            bg = bg_ref[0, :, cols]
            bu = bu_ref[0, :, cols]

            def prepare(tiles):
                for _, rows, slot in tiles:
                    @pl.when(pend_ref[slot] == 1)
                    def _():
                        out_copy(0, rows, slot, f).wait()

            def compute(r, rows, slot):
                z = jnp.dot(xb_ref[pl.ds(r, rows), :], wgu_ref[...], preferred_element_type=F32)
                g = z[:, :tf] + bg
                u = z[:, tf:] + bu
                g = jnp.minimum(g, SWIGLU_LIMIT)
                u = jnp.clip(u, -SWIGLU_LIMIT, SWIGLU_LIMIT)
                ab_ref[slot, pl.ds(0, rows), :] = (g * jax.nn.sigmoid(SWIGLU_ALPHA * g) * (u + 1.0)).astype(BF16)

            def commit(tiles):
                for r, rows, slot in tiles:
                    out_copy(r, rows, slot, f).start()
                    pend_ref[slot] = 1

            _tile_loop(nsub, prepare, compute, commit)
            return carry

        lax.fori_loop(0, nf, column_block, 0)

    @pl.when(c == pl.num_programs(0) - 1)
    def _():
        for slot, rows in enumerate(SLOT_ROWS):
            @pl.when(pend_ref[slot] == 1)
            def _():
                out_copy(0, rows, slot, 0).wait()


def _moe_up(ch_e, ch_start, ch_nsub, tail, xs, w_gate, b_gate, w_up, b_up):
    ns = TOK_WORDS // LANES
    p_alloc = xs.shape[0] // ns
    d = 2 * TOK_WORDS
    dff = w_gate.shape[2]
    tf = MOE_TF
    assert (dff // tf) % 2 == 0
    nch = ch_e.shape[0]
    bmap = lambda c, e, s, n, t: (e[c], 0, 0)
    return pl.pallas_call(
        _moe_up_kernel,
        grid_spec=pltpu.PrefetchScalarGridSpec(
            num_scalar_prefetch=4,
            grid=(nch,),
            in_specs=[
                pl.BlockSpec(memory_space=pl.ANY),
                pl.BlockSpec(memory_space=pl.ANY),
                pl.BlockSpec((1, 1, dff), bmap),
                pl.BlockSpec(memory_space=pl.ANY),
                pl.BlockSpec((1, 1, dff), bmap),
            ],
            out_specs=pl.BlockSpec(memory_space=pl.ANY),
            scratch_shapes=[
                pltpu.VMEM((MOE_TM * ns, LANES), jnp.uint32),
                pltpu.VMEM((MOE_TM, d), BF16),
                pltpu.VMEM((2, 2, d, tf), F32),
                pltpu.VMEM((d, 2 * tf), BF16),
                pltpu.VMEM((len(SLOT_ROWS), 4 * SUB_ROWS, tf), BF16),
                pltpu.VMEM((SUB_ROWS, dff), BF16),
                pltpu.SMEM((len(SLOT_ROWS),), jnp.int32),
                pltpu.SemaphoreType.DMA,
                pltpu.SemaphoreType.DMA((2,)),
                pltpu.SemaphoreType.DMA((len(SLOT_ROWS),)),
            ],
        ),
        out_shape=jax.ShapeDtypeStruct((p_alloc, dff), BF16),
        compiler_params=_cparams(("arbitrary",)),
        name="moe_up",
    )(ch_e, ch_start, ch_nsub, tail, xs, w_gate, b_gate.reshape(N_EXPERTS, 1, dff), w_up,
      b_up.reshape(N_EXPERTS, 1, dff))


def _moe_down_kernel(ce_ref, cs_ref, cn_ref, tail_ref, act_hbm, wd_hbm, bd_ref, ys_hbm,
                     x_ref, wf_ref, wdb_ref, yb_ref, zero_ref, pend_ref, sem, wsem, osem):
    c = pl.program_id(0)
    nsub = cn_ref[c]
    start = pl.multiple_of(cs_ref[c], SUB_ROWS)
    tn = wf_ref.shape[3]
    nf = bd_ref.shape[2] // tn
    ns = TOK_WORDS // LANES
    nw = tn // 2 // LANES
    chunk = [_chunk_dma(cs_ref, cn_ref, act_hbm, x_ref.at[i], 1, sem.at[i]) for i in range(2)]
    first_weights, next_weights = _weight_stream(ce_ref, cn_ref, nf, (wd_hbm,), tn, wf_ref, wdb_ref, wsem)

    def on_parity(ci, fn):
        for p in range(2):
            @pl.when(ci % 2 == p)
            def _():
                fn(p)

    def out_copy(r, rows, par):
        src = yb_ref.at[par, pl.ds(pl.multiple_of(r * ns, SUB_ROWS * ns), rows * ns)]
        dst = ys_hbm.at[pl.ds(pl.multiple_of((start + r) * ns, SUB_ROWS * ns), rows * ns)]
        return pltpu.make_async_copy(src, dst, osem.at[par])

    def drain(par):
        for kind, rows in enumerate(SLOT_ROWS[-3:]):
            def wait(j, carry):
                out_copy(0, rows, par).wait()
                return carry

            lax.fori_loop(0, pend_ref[par * 3 + kind], wait, 0)
            pend_ref[par * 3 + kind] = 0

    @pl.when(c == 0)
    def _():
        zero_ref[...] = jnp.zeros_like(zero_ref)
        _zero_fill_rows(zero_ref, ys_hbm, tail_ref[0], tail_ref[1], osem.at[0])
        for kind in range(6):
            pend_ref[kind] = 0
        chunk[0][0](c)
        first_weights()

    def run_chunk():
        for f in range(nf):
            next_weights(c, f)
            bd = bd_ref[0, :, f * tn:(f + 1) * tn]
            last = f == nf - 1
            if f == 0:
                drain(c % 2)

            def compute(r, rows, slot, f=f, bd=bd):
                y = jnp.dot(x_ref[c % 2, pl.ds(r, rows), :], wdb_ref[...], preferred_element_type=F32) + bd
                lo = pltpu.bitcast(y[:, :tn // 2].astype(BF16).astype(F32), jnp.uint32)
                hi = pltpu.bitcast(y[:, tn // 2:].astype(BF16).astype(F32), jnp.uint32)
                packed = (hi & jnp.uint32(0xFFFF0000)) | (lo >> 16)
                for s in range(nw):
                    yb_ref[c % 2, pl.ds(r * ns + f * nw + s, rows, stride=ns), :] = (
                        packed[:, s * LANES:(s + 1) * LANES])

            def commit(tiles, last=last):
                if last:
                    for r, rows, slot in tiles:
                        out_copy(r, rows, c % 2).start()
                        kind = (c % 2) * 3 + SLOT_ROWS[-3:].index(rows)
                        pend_ref[kind] = pend_ref[kind] + 1

            _tile_loop(nsub, lambda tiles: None, compute, commit)

    @pl.when(nsub > 0)
    def _():
        on_parity(c, lambda p: chunk[p][1](c))
        on_parity(c + 1, lambda p: chunk[p][0](c + 1))
        run_chunk()

    @pl.when(c == pl.num_programs(0) - 1)
    def _():
        drain(0)
        drain(1)


def _moe_down(ch_e, ch_start, ch_nsub, tail, act, w_down, b_down):
    p_alloc, dff = act.shape
    d = w_down.shape[2]
    ns = TOK_WORDS // LANES
    tn = MOE_TN
    assert (d // tn) % 2 == 0
    nch = ch_e.shape[0]
    return pl.pallas_call(
        _moe_down_kernel,
        grid_spec=pltpu.PrefetchScalarGridSpec(
            num_scalar_prefetch=4,
            grid=(nch,),
            in_specs=[
                pl.BlockSpec(memory_space=pl.ANY),
                pl.BlockSpec(memory_space=pl.ANY),
                pl.BlockSpec((1, 1, d), lambda c, e, s, n, t: (e[c], 0, 0)),
            ],
            out_specs=pl.BlockSpec(memory_space=pl.ANY),
            scratch_shapes=[
                pltpu.VMEM((2, MOE_TM, dff), BF16),
                pltpu.VMEM((2, 1, dff, tn), F32),
                pltpu.VMEM((dff, tn), BF16),
                pltpu.VMEM((2, MOE_TM * ns, LANES), jnp.uint32),
                pltpu.VMEM((SUB_ROWS * ns, LANES), jnp.uint32),
                pltpu.SMEM((6,), jnp.int32),
                pltpu.SemaphoreType.DMA((2,)),
                pltpu.SemaphoreType.DMA((2,)),
                pltpu.SemaphoreType.DMA((2,)),
            ],
        ),
        out_shape=jax.ShapeDtypeStruct((p_alloc * ns, LANES), jnp.uint32),
        compiler_params=pltpu.CompilerParams(dimension_semantics=("arbitrary",), vmem_limit_bytes=VMEM_LIMIT_MAX),
        name="moe_down",
    )(ch_e, ch_start, ch_nsub, tail, act, w_down, b_down.reshape(N_EXPERTS, 1, d))


def _combine_kernel(pos_hbm, ys_hbm, h_ref, w_ref, g_ref, o_ref, pos_smem, yb_ref, sem, psem):
    i = pl.program_id(0)
    tm, d = h_ref.shape
    n = TOP_K * tm
    ns = TOK_WORDS // LANES

    def pos_copy(tile):
        third = pl.multiple_of((tile % 3) * n, n)
        return pltpu.make_async_copy(pos_hbm.at[pl.ds(pl.multiple_of(tile * n, n), n)],
                                     pos_smem.at[pl.ds(third, n)], psem.at[tile % 3])

    def fetch(tile, slot):
        pos_copy(tile).wait()

        @pl.when(tile + 1 < pl.num_programs(0))
        def _():
            pos_copy(tile + 1).start()

        base = (tile % 3) * n

        def issue(jj, c):
            for u in range(ROW_DMA_UNROLL):
                j = jj * ROW_DMA_UNROLL + u
                src = ys_hbm.at[pl.ds(pl.multiple_of(pos_smem[base + j] * ns, ns), ns)]
                dst = yb_ref.at[slot, pl.ds(pl.multiple_of(j * ns, ns), ns)]
                pltpu.make_async_copy(src, dst, sem.at[slot]).start(priority=u % 2)
            return c

        lax.fori_loop(0, n // ROW_DMA_UNROLL, issue, 0)

    @pl.when(i == 0)
    def _():
        pos_copy(i).start()
        fetch(i, 0)

    @pl.when(i + 1 < pl.num_programs(0))
    def _():
        for p in range(2):
            @pl.when((i + 1) % 2 == p)
            def _():
                fetch(i + 1, p)

    def reduce(slot):
        pltpu.make_async_copy(ys_hbm.at[pl.ds(0, n * ns)], yb_ref.at[slot], sem.at[slot]).wait()
        wb = [jnp.broadcast_to(w_ref[:, k:k + 1], (tm, LANES)) for k in range(TOP_K)]
        ssq = jnp.zeros((tm, 1), F32)
        nw = MOE_TN // 2 // LANES
        for s in range(ns):
            c0 = (s // nw) * MOE_TN + (s % nw) * LANES
            lo_cols = slice(c0, c0 + LANES)
            hi_cols = slice(c0 + MOE_TN // 2, c0 + MOE_TN // 2 + LANES)
            acc_lo = h_ref[:, lo_cols]
            acc_hi = h_ref[:, hi_cols]
            for k in range(TOP_K):
                w = yb_ref[slot, pl.ds(k * tm * ns + s, tm, stride=ns), :]
                acc_lo = acc_lo + wb[k] * pltpu.bitcast(w << 16, F32)
                acc_hi = acc_hi + wb[k] * pltpu.bitcast(w & jnp.uint32(0xFFFF0000), F32)
            o_ref[:, lo_cols] = acc_lo
            o_ref[:, hi_cols] = acc_hi
            ssq = ssq + jnp.sum(acc_lo * acc_lo + acc_hi * acc_hi, axis=-1, keepdims=True)
        o_ref[...] = o_ref[...] * lax.rsqrt(ssq * (1.0 / d) + NORM_EPS) * g_ref[...]

    for p in range(2):
        @pl.when(i % 2 == p)
        def _():
            reduce(p)


def _combine(pos_tiles, ys, h1, wts_t, g_final, tm):
    rows, d = h1.shape
    ns = TOK_WORDS // LANES
    return pl.pallas_call(
        _combine_kernel,
        grid=(rows // tm,),
        in_specs=[
            pl.BlockSpec(memory_space=pl.ANY),
            pl.BlockSpec(memory_space=pl.ANY),
            pl.BlockSpec((tm, d), lambda i: (i, 0)),
            pl.BlockSpec((tm, TOP_K), lambda i: (i, 0)),
            pl.BlockSpec((1, d), lambda i: (0, 0)),
        ],
        out_specs=pl.BlockSpec((tm, d), lambda i: (i, 0)),
        out_shape=jax.ShapeDtypeStruct((rows, d), F32),
        scratch_shapes=[
            pltpu.SMEM((3 * TOP_K * tm,), jnp.int32),
            pltpu.VMEM((2, TOP_K * tm * ns, LANES), jnp.uint32),
            pltpu.SemaphoreType.DMA((2,)),
            pltpu.SemaphoreType.DMA((3,)),
        ],
        compiler_params=_cparams(("arbitrary",)),
        name="combine",
    )(pos_tiles, ys, h1, wts_t, g_final)


def _rope_tables(n_pos):
    half = HEAD_DIM // 2
    inv = 1.0 / (ROPE_THETA ** (jnp.arange(half, dtype=F32) / half))
    ang = jnp.arange(n_pos, dtype=F32)[:, None] * inv[None, :]
    cos = jnp.tile(jnp.cos(ang), (1, LANES // half))
    sin = jnp.tile(jnp.concatenate([-jnp.sin(ang), jnp.sin(ang)], axis=1), (1, LANES // HEAD_DIM))
    return cos, sin


def _block_diag(w):
    per = LRU_GROUP // LRU_BLOCK
    w4 = w.reshape(D_LRU // LRU_GROUP, per, LRU_BLOCK, LRU_BLOCK)
    eye = jnp.eye(per, dtype=w.dtype)
    bd = jnp.einsum("gpcd,pq->gpcqd", w4, eye)
    return bd.reshape(D_LRU // LRU_GROUP, LRU_GROUP, LRU_GROUP).astype(BF16)


def _tile_positions(pos, tm):
    rows = pos.shape[1]
    return pos.reshape(TOP_K, rows // tm, tm).transpose(1, 0, 2).reshape(-1)


def _chunk_schedule(counts, n_rows):
    aligned = ((counts + SUB_ROWS - 1) // SUB_ROWS) * SUB_ROWS
    offs = jnp.cumsum(aligned) - aligned
    n_ch = (aligned + MOE_TM - 1) // MOE_TM
    cum = jnp.cumsum(n_ch)
    total = cum[-1]
    nch_max = N_EXPERTS + (n_rows + N_EXPERTS * SUB_ROWS) // MOE_TM
    c = jnp.arange(nch_max, dtype=jnp.int32)
    cc = jnp.minimum(c, total - 1)
    e = jnp.searchsorted(cum, cc, side="right").astype(jnp.int32)
    j = cc - (cum[e] - n_ch[e])
    start = offs[e] + j * MOE_TM
    nsub = jnp.where(c < total, jnp.minimum(MOE_TM, aligned[e] - j * MOE_TM) // SUB_ROWS, 0)
    tails = offs + (counts // SUB_ROWS) * SUB_ROWS
    used = jnp.sum(aligned)
    tail = jnp.stack([used, (n_rows + N_EXPERTS * SUB_ROWS - used) // SUB_ROWS])
    tails = jnp.concatenate([tails, tail])
    return (offs, e, start.astype(jnp.int32), nsub.astype(jnp.int32), tails.astype(jnp.int32),
            tail.astype(jnp.int32))


def kernel(x, meta_tokens, norm_mix, w_in, b_in, sinks, conv_w, conv_b, w_a, b_a, w_i, b_i, lru_lambda,
           g_attn_out, g_lru_out, w_out, b_out, norm_ffn, w_router, b_router, w_gate, b_gate, w_up, b_up,
           w_down, b_down, final_norm):
    batch, seq, d = x.shape
    rows = batch * seq
    x2d = x.reshape(rows, d)
    row = lambda v: v.reshape(1, -1)

    w_in_bf = w_in[0].astype(BF16)
    w_out_bf = w_out[0].astype(BF16)
    cos, sin = _rope_tables(N_META + seq)
    wa_bd, wi_bd = _block_diag(w_a[0]), _block_diag(w_i[0])
    lru_args = (conv_w[0], row(conv_b[0]), wa_bd, row(b_a[0]), wi_bd, row(b_i[0]), row(lru_lambda[0]),
                row(g_lru_out[0]))

    _, kvm, xgm = _inproj(meta_tokens, row(norm_mix[0]), w_in_bf, row(b_in[0]), cos[:N_META], sin[:N_META],
                          N_META, 1)
    _, h0, tail0 = _lru(xgm, *lru_args, jnp.zeros((1, D_LRU), F32), jnp.zeros((8, D_LRU), F32), 1, N_META,
                        N_META)

    tm = ROW_TILE
    q, kv, xg = _inproj(x2d, row(norm_mix[0]), w_in_bf, row(b_in[0]), cos[N_META:], sin[N_META:], tm,
                        seq // tm)
    kvm = jnp.pad(kvm, ((0, BLOCK - N_META), (0, 0)))
    attn_n = _attention(sinks[0], q, kv, kvm, row(g_attn_out[0]), batch, seq)
    lru_n, _, _ = _lru(xg, *lru_args, h0, tail0, batch, seq, LRU_TIME_TILE)

    tri = (jnp.arange(tm)[:, None] <= jnp.arange(tm)[None, :]).astype(BF16)
    h1, xp, ids, wts, rank, cnt = _outproj(
        attn_n, lru_n, w_out_bf, row(b_out[0]), x2d, row(norm_ffn[0]), w_router[0].T.astype(BF16),
        b_router[0].reshape(N_EXPERTS, 1), tri, tm)

    counts = cnt[:, 0].astype(jnp.int32)
    n_rows = rows * TOP_K
    offs, ch_e, ch_start, ch_nsub, tails, tail = _chunk_schedule(counts, n_rows)
    onehot = ids[..., None] == jnp.arange(N_EXPERTS, dtype=jnp.int32)
    pos = jnp.sum(jnp.where(onehot, offs, 0), axis=-1) + rank
    p_alloc = n_rows + N_EXPERTS * SUB_ROWS

    xs = _dispatch(tails, _tile_positions(pos, ROW_TILE), xp, p_alloc, ROW_TILE)
    act = _moe_up(ch_e, ch_start, ch_nsub, tail, xs, w_gate[0], b_gate[0], w_up[0], b_up[0])
    ys = _moe_down(ch_e, ch_start, ch_nsub, tail, act, w_down[0], b_down[0])
    out = _combine(_tile_positions(pos, COMBINE_TILE), ys, h1, wts.T, row(final_norm), COMBINE_TILE)
    return out.reshape(batch, seq, d)
```

```python
import jax
import jax.numpy as jnp
from jax import lax
from jax.experimental import pallas as pl
from jax.experimental.pallas import tpu as pltpu

F32 = jnp.float32
BF16 = jnp.bfloat16

N_META = 16
HEAD_DIM = 64
N_Q_HEADS = 16
N_KV_HEADS = 4
GROUP = N_Q_HEADS // N_KV_HEADS
D_ATTN = N_Q_HEADS * HEAD_DIM
D_KV = N_KV_HEADS * HEAD_DIM
KV_COLS = 4 * D_KV
BLOCK = 128
ATTN_QBLOCKS = 4
ROPE_THETA = 10000.0
D_LRU = 1024
LRU_BLOCK = 64
LRU_GROUP = 256
CONV_WIDTH = 4
LRU_C = 8.0
N_EXPERTS = 32
TOP_K = 4
SWIGLU_LIMIT = 7.0
SWIGLU_ALPHA = 1.702
NORM_EPS = 1e-5
NEG_INF = -1e30

LANES = 128
D_MODEL = 2048
TOK_WORDS = D_MODEL // 2
SUB_ROWS = 128
MOE_TM = 2304
MOE_TF = 512
MOE_TN = 1024
ROW_TILE = 512
INPROJ_COLS = 512
LRU_TIME_TILE = 128
COMBINE_TILE = 256
ROW_DMA_UNROLL = 16
SLOT_ROWS = (4 * SUB_ROWS,) * 5 + (2 * SUB_ROWS, SUB_ROWS)
VMEM_LIMIT = 56 * 1024 * 1024
VMEM_LIMIT_MAX = 62 * 1024 * 1024


def _cparams(sem):
    return pltpu.CompilerParams(dimension_semantics=sem, vmem_limit_bytes=VMEM_LIMIT)


def _mult(x, m):
    return x if isinstance(x, int) else pl.multiple_of(x, m)


def _rms(x, g):
    return x * lax.rsqrt(jnp.mean(x * x, axis=-1, keepdims=True) + NORM_EPS) * g


def _inproj_kernel(x_ref, g_ref, w_ref, b_ref, cos_ref, sin_ref, q_ref, kv_ref, xg_ref):
    xn = _rms(x_ref[...], g_ref[...]).astype(BF16)
    cos = cos_ref[...]
    sin = sin_ref[...]
    lane = lax.broadcasted_iota(jnp.int32, (1, LANES), 1)
    first_half = (lane % HEAD_DIM) < (HEAD_DIM // 2)

    def rope(z):
        partner = jnp.where(first_half, pltpu.roll(z, LANES - HEAD_DIM // 2, 1),
                            pltpu.roll(z, HEAD_DIM // 2, 1))
        return z * cos + partner * sin

    cw = INPROJ_COLS
    for c in range(w_ref.shape[1] // cw):
        z = jnp.dot(xn, w_ref[:, c * cw:(c + 1) * cw], preferred_element_type=F32)
        z = z + b_ref[:, c * cw:(c + 1) * cw]
        if c < 2:
            for j in range(cw // LANES):
                zz = rope(z[:, j * LANES:(j + 1) * LANES]) * (HEAD_DIM ** -0.5)
                q_ref[:, c * cw + j * LANES:c * cw + (j + 1) * LANES] = zz.astype(BF16)
        elif c == 2:
            low = lane < HEAD_DIM
            for j in range(2 * D_KV // LANES):
                zz = z[:, j * LANES:(j + 1) * LANES]
                zz = rope(zz) if j < D_KV // LANES else zz
                sw = pltpu.roll(zz, HEAD_DIM, 1)
                kv_ref[:, 2 * j * LANES:(2 * j + 1) * LANES] = jnp.where(low, zz, sw).astype(BF16)
                kv_ref[:, (2 * j + 1) * LANES:(2 * j + 2) * LANES] = jnp.where(low, sw, zz).astype(BF16)
        else:
            xg_ref[:, (c - 3) * cw:(c - 2) * cw] = z


def _inproj(x2d, g, w_bf, b, cos, sin, tm, pos_blocks):
    rows, d = x2d.shape
    dz = w_bf.shape[1]
    return pl.pallas_call(
        _inproj_kernel,
        grid=(rows // tm,),
        in_specs=[
            pl.BlockSpec((tm, d), lambda m: (m, 0)),
            pl.BlockSpec((1, d), lambda m: (0, 0)),
            pl.BlockSpec((d, dz), lambda m: (0, 0)),
            pl.BlockSpec((1, dz), lambda m: (0, 0)),
            pl.BlockSpec((tm, LANES), lambda m: (m % pos_blocks, 0)),
            pl.BlockSpec((tm, LANES), lambda m: (m % pos_blocks, 0)),
        ],
        out_specs=[
            pl.BlockSpec((tm, D_ATTN), lambda m: (m, 0)),
            pl.BlockSpec((tm, KV_COLS), lambda m: (m, 0)),
            pl.BlockSpec((tm, 2 * D_LRU), lambda m: (m, 0)),
        ],
        out_shape=[
            jax.ShapeDtypeStruct((rows, D_ATTN), BF16),
            jax.ShapeDtypeStruct((rows, KV_COLS), BF16),
            jax.ShapeDtypeStruct((rows, 2 * D_LRU), F32),
        ],
        compiler_params=_cparams(("arbitrary",)),
        name="inproj",
    )(x2d, g, w_bf, b, cos, sin)


def _attn_block(sink_ref, q_ref, kvc_ref, kvp_ref, kvm_ref, g_ref, o_ref, n, rs):
    row = lax.broadcasted_iota(jnp.int32, (GROUP * BLOCK, BLOCK), 0) % BLOCK
    col = lax.broadcasted_iota(jnp.int32, (GROUP * BLOCK, BLOCK), 1)
    in_cur = col <= row
    in_prev = jnp.logical_and(col > row, n > 0)
    is_meta = col < N_META
    low = lax.broadcasted_iota(jnp.int32, (BLOCK, LANES), 1) < HEAD_DIM
    nt = (((1,), (1,)), ((), ()))
    zero = jnp.zeros((), BF16)
    outs = []
    for h in range(N_KV_HEADS):
        ks = slice(h * LANES, (h + 1) * LANES)
        vs = slice(N_KV_HEADS * LANES + h * LANES, N_KV_HEADS * LANES + (h + 1) * LANES)
        parts = []
        for j in range(GROUP // 2):
            qg = q_ref[rs, (h * GROUP // 2 + j) * LANES:(h * GROUP // 2 + j + 1) * LANES]
            parts += [jnp.where(low, qg, zero), jnp.where(low, zero, qg)]
        qs = jnp.concatenate(parts, axis=0)
        s_c = lax.dot_general(qs, kvc_ref[:, ks], nt, preferred_element_type=F32)
        s_p = lax.dot_general(qs, kvp_ref[:, ks], nt, preferred_element_type=F32)
        s_m = lax.dot_general(qs, kvm_ref[:, ks], nt, preferred_element_type=F32)
        s_b = jnp.where(in_prev, s_p, jnp.where(in_cur, s_c, NEG_INF))
        s_m = jnp.where(is_meta, s_m, NEG_INF)
        sink = jnp.concatenate(
            [jnp.full((BLOCK, 1), sink_ref[h * GROUP + g], F32) for g in range(GROUP)], axis=0)
        m = jnp.maximum(jnp.max(jnp.maximum(s_b, s_m), axis=-1, keepdims=True), sink)
        p_b = jnp.exp(s_b - m)
        p_m = jnp.exp(s_m - m)
        den = jnp.sum(p_b + p_m, axis=-1, keepdims=True) + jnp.exp(sink - m)
        r = (jnp.dot(jnp.where(in_cur, p_b, 0.0).astype(BF16), kvc_ref[:, vs], preferred_element_type=F32)
             + jnp.dot(jnp.where(in_cur, 0.0, p_b).astype(BF16), kvp_ref[:, vs], preferred_element_type=F32)
             + jnp.dot(p_m.astype(BF16), kvm_ref[:, vs], preferred_element_type=F32))
        r = r / den
        for j in range(GROUP // 2):
            outs.append(jnp.where(low, r[2 * j * BLOCK:(2 * j + 1) * BLOCK], r[(2 * j + 1) * BLOCK:(2 * j + 2) * BLOCK]))
    o_all = jnp.concatenate(outs, axis=1)
    o_ref[rs, :] = _rms(o_all, g_ref[...]).astype(BF16)


def _attn_kernel(sink_ref, q_ref, *rest):
    kv_refs, (kvm_ref, g_ref, o_ref) = rest[:ATTN_QBLOCKS + 1], rest[ATTN_QBLOCKS + 1:]
    n0 = pl.program_id(1) * ATTN_QBLOCKS
    for sub in range(ATTN_QBLOCKS):
        _attn_block(sink_ref, q_ref, kv_refs[sub + 1], kv_refs[sub], kvm_ref, g_ref, o_ref, n0 + sub,
                    slice(sub * BLOCK, (sub + 1) * BLOCK))


def _attention(sinks, q, kv, kvm, g_attn, batch, seq):
    nb = seq // BLOCK
    nq = ATTN_QBLOCKS
    kv_spec = lambda i: pl.BlockSpec((BLOCK, KV_COLS), lambda b, n, s: (b * nb + jnp.maximum(nq * n + i - 1, 0), 0))
    return pl.pallas_call(
        _attn_kernel,
        grid_spec=pltpu.PrefetchScalarGridSpec(
            num_scalar_prefetch=1,
            grid=(batch, nb // nq),
            in_specs=[
                pl.BlockSpec((nq * BLOCK, D_ATTN), lambda b, n, s: (b * (nb // nq) + n, 0)),
                *[kv_spec(i) for i in range(nq + 1)],
                pl.BlockSpec((BLOCK, KV_COLS), lambda b, n, s: (0, 0)),
                pl.BlockSpec((1, D_ATTN), lambda b, n, s: (0, 0)),
            ],
            out_specs=pl.BlockSpec((nq * BLOCK, D_ATTN), lambda b, n, s: (b * (nb // nq) + n, 0)),
        ),
        out_shape=jax.ShapeDtypeStruct((batch * seq, D_ATTN), BF16),
        compiler_params=_cparams(("arbitrary", "arbitrary")),
        name="attention",
    )(sinks, q, *([kv] * (ATTN_QBLOCKS + 1)), kvm, g_attn)


def _one_minus_sq(a, log_a):
    y = 2.0 * log_a
    p = 1.0 + y * (1.0 / 4.0)
    for k in (3, 2):
        p = 1.0 + y * p * (1.0 / k)
    return jnp.where(y > -1.0 / 64.0, -(y * p), 1.0 - a * a)


def _lru_kernel(xg_ref, cw_ref, cb_ref, wa_ref, ba_ref, wi_ref, bi_ref, lam_ref, g_ref, h0_ref, tail0_ref,
                o_ref, hout_ref, tailout_ref, ext_ref, tail_ref, a_ref, b_ref, h_ref):
    tt = pl.program_id(0)
    batch, rows = xg_ref.shape[0], xg_ref.shape[1]

    @pl.when(tt == 0)
    def _():
        for g in range(D_LRU // LANES):
            h_ref[g] = jnp.broadcast_to(h0_ref[:, g * LANES:(g + 1) * LANES], (batch, LANES))
        for s in range(batch):
            tail_ref[s] = tail0_ref[...]

    sp = jax.nn.softplus(-lam_ref[...])

    def gates(s, carry):
        ext_ref[0:8, :] = tail_ref[s]
        ext_ref[8:, :] = xg_ref[s, :, :D_LRU]
        tail_ref[s] = ext_ref[rows:rows + 8, :]
        xc = cb_ref[...] + sum(cw_ref[j:j + 1, :] * ext_ref[5 + j:5 + j + rows, :] for j in range(CONV_WIDTH))
        xcb = xc.astype(BF16)
        for c in range(D_LRU // LRU_GROUP):
            cs = slice(c * LRU_GROUP, (c + 1) * LRU_GROUP)
            r = jax.nn.sigmoid(jnp.dot(xcb[:, cs], wa_ref[c], preferred_element_type=F32) + ba_ref[:, cs])
            i = jax.nn.sigmoid(jnp.dot(xcb[:, cs], wi_ref[c], preferred_element_type=F32) + bi_ref[:, cs])
            log_a = -LRU_C * r * sp[:, cs]
            a = jnp.exp(log_a)
            b = jnp.sqrt(_one_minus_sq(a, log_a)) * i * xc[:, cs]
            for g in range(LRU_GROUP // LANES):
                lanes = slice(g * LANES, (g + 1) * LANES)
                a_ref[c * (LRU_GROUP // LANES) + g, pl.ds(s, rows, stride=batch), :] = a[:, lanes]
                b_ref[c * (LRU_GROUP // LANES) + g, pl.ds(s, rows, stride=batch), :] = b[:, lanes]
        return carry

    lax.fori_loop(0, batch, gates, 0)

    def step(t, h):
        r0 = pl.multiple_of(t * batch, batch)
        new = []
        for g in range(D_LRU // LANES):
            hg = a_ref[g, pl.ds(r0, batch), :] * h[g] + b_ref[g, pl.ds(r0, batch), :]
            b_ref[g, pl.ds(r0, batch), :] = hg
            new.append(hg)
        return tuple(new)

    h_last = lax.fori_loop(0, rows, step, tuple(h_ref[g] for g in range(D_LRU // LANES)))
    for g in range(D_LRU // LANES):
        h_ref[g] = h_last[g]
    hout_ref[...] = jnp.concatenate([h_last[g][0:1] for g in range(D_LRU // LANES)], axis=1)
    tailout_ref[...] = tail_ref[0]

    def finish(s, carry):
        hs = jnp.concatenate([b_ref[g, pl.ds(s, rows, stride=batch), :] for g in range(D_LRU // LANES)], axis=1)
        y = hs * jax.nn.gelu(xg_ref[s, :, D_LRU:])
        o_ref[s] = _rms(y, g_ref[...]).astype(BF16)
        return carry

    lax.fori_loop(0, batch, finish, 0)


def _lru(xg, conv_w, conv_b, wa_bd, b_a, wi_bd, b_i, lam, g_lru, h0, tail0, batch, seq, tt):
    ntt = seq // tt
    ng = D_LRU // LANES
    vec = pl.BlockSpec((1, D_LRU), lambda t: (0, 0))
    wspec = pl.BlockSpec((D_LRU // LRU_GROUP, LRU_GROUP, LRU_GROUP), lambda t: (0, 0, 0))
    out, h_last, tail = pl.pallas_call(
        _lru_kernel,
        grid=(ntt,),
        in_specs=[
            pl.BlockSpec((batch, tt, 2 * D_LRU), lambda t: (0, t, 0)),
            pl.BlockSpec((CONV_WIDTH, D_LRU), lambda t: (0, 0)),
            vec, wspec, vec, wspec, vec, vec, vec, vec,
            pl.BlockSpec((8, D_LRU), lambda t: (0, 0)),
        ],
        out_specs=[
            pl.BlockSpec((batch, tt, D_LRU), lambda t: (0, t, 0)),
            pl.BlockSpec((1, D_LRU), lambda t: (0, 0)),
            pl.BlockSpec((8, D_LRU), lambda t: (0, 0)),
        ],
        out_shape=[
            jax.ShapeDtypeStruct((batch, seq, D_LRU), BF16),
            jax.ShapeDtypeStruct((1, D_LRU), F32),
            jax.ShapeDtypeStruct((8, D_LRU), F32),
        ],
        scratch_shapes=[
            pltpu.VMEM((8 + tt, D_LRU), F32),
            pltpu.VMEM((batch, 8, D_LRU), F32),
            pltpu.VMEM((ng, tt * batch, LANES), F32),
            pltpu.VMEM((ng, tt * batch, LANES), F32),
            pltpu.VMEM((ng, batch, LANES), F32),
        ],
        compiler_params=_cparams(("arbitrary",)),
        name="rglru",
    )(xg.reshape(batch, seq, 2 * D_LRU), conv_w, conv_b, wa_bd, b_a, wi_bd, b_i, lam, g_lru, h0, tail0)
    return out.reshape(batch * seq, D_LRU), h_last, tail


def _outproj_kernel(a_ref, l_ref, wo_ref, bo_ref, x_ref, gf_ref, wr_ref, br_ref, tri_ref,
                    h_ref, xp_ref, ids_ref, wts_ref, rank_ref, cnt_ref, carry_ref):
    m = pl.program_id(0)
    tm = a_ref.shape[0]

    @pl.when(m == 0)
    def _():
        carry_ref[...] = jnp.zeros_like(carry_ref)

    mix = jnp.concatenate([a_ref[...], l_ref[...]], axis=1)
    h = jnp.dot(mix, wo_ref[...], preferred_element_type=F32) + bo_ref[...] + x_ref[...]
    h_ref[...] = h
    xn = _rms(h, gf_ref[...]).astype(BF16)
    half = xn.shape[1] // 2
    lo = pltpu.bitcast(xn[:, :half].astype(F32), jnp.uint32)
    hi = pltpu.bitcast(xn[:, half:].astype(F32), jnp.uint32)
    packed = (hi & jnp.uint32(0xFFFF0000)) | (lo >> 16)
    ns = half // LANES
    for s in range(ns):
        xp_ref[pl.ds(s, tm, stride=ns), :] = packed[:, s * LANES:(s + 1) * LANES]

    logits = lax.dot_general(wr_ref[...], xn, (((1,), (1,)), ((), ())), preferred_element_type=F32)
    logits = logits + br_ref[...]
    eidx = lax.broadcasted_iota(jnp.int32, (N_EXPERTS, tm), 0)
    work = logits
    vals, sels = [], []
    for k in range(TOP_K):
        v = jnp.max(work, axis=0, keepdims=True)
        idx = jnp.min(jnp.where(work == v, eidx, N_EXPERTS), axis=0, keepdims=True)
        sel = eidx == idx
        ids_ref[k:k + 1, :] = idx
        vals.append(v)
        sels.append(sel)
        work = jnp.where(sel, -jnp.inf, work)
    es = [jnp.exp(v - vals[0]) for v in vals]
    den = es[0] + es[1] + es[2] + es[3]
    for k in range(TOP_K):
        wts_ref[k:k + 1, :] = es[k] / den
    cnt = sum(s.astype(F32) for s in sels)
    incl = jnp.dot(cnt.astype(BF16), tri_ref[...], preferred_element_type=F32)
    before = incl - cnt + carry_ref[:, 0:1]
    for k in range(TOP_K):
        rk = jnp.sum(jnp.where(sels[k], before, 0.0), axis=0, keepdims=True)
        rank_ref[k:k + 1, :] = rk.astype(jnp.int32)
    carry_ref[...] = carry_ref[...] + incl[:, tm - 1:tm]
    cnt_ref[...] = carry_ref[...]


def _outproj(attn_n, lru_n, wo_bf, b_out, x2d, g_ffn, wr_t, br, tri, tm):
    rows, d = x2d.shape
    const = lambda shape: pl.BlockSpec(shape, lambda m: tuple(0 for _ in shape))
    return pl.pallas_call(
        _outproj_kernel,
        grid=(rows // tm,),
        in_specs=[
            pl.BlockSpec((tm, D_ATTN), lambda m: (m, 0)),
            pl.BlockSpec((tm, D_LRU), lambda m: (m, 0)),
            const((D_ATTN + D_LRU, d)), const((1, d)),
            pl.BlockSpec((tm, d), lambda m: (m, 0)),
            const((1, d)), const((N_EXPERTS, d)), const((N_EXPERTS, 1)), const((tm, tm)),
        ],
        out_specs=[
            pl.BlockSpec((tm, d), lambda m: (m, 0)),
            pl.BlockSpec((tm * (d // 2 // LANES), LANES), lambda m: (m, 0)),
            pl.BlockSpec((TOP_K, tm), lambda m: (0, m)),
            pl.BlockSpec((TOP_K, tm), lambda m: (0, m)),
            pl.BlockSpec((TOP_K, tm), lambda m: (0, m)),
            const((N_EXPERTS, LANES)),
        ],
        out_shape=[
            jax.ShapeDtypeStruct((rows, d), F32),
            jax.ShapeDtypeStruct((rows * (d // 2 // LANES), LANES), jnp.uint32),
            jax.ShapeDtypeStruct((TOP_K, rows), jnp.int32),
            jax.ShapeDtypeStruct((TOP_K, rows), F32),
            jax.ShapeDtypeStruct((TOP_K, rows), jnp.int32),
            jax.ShapeDtypeStruct((N_EXPERTS, LANES), F32),
        ],
        scratch_shapes=[pltpu.VMEM((N_EXPERTS, LANES), F32)],
        compiler_params=_cparams(("arbitrary",)),
        name="outproj_router",
    )(attn_n, lru_n, wo_bf, b_out, x2d, g_ffn, wr_t, br, tri)


def _zero_fill_rows(zero_ref, dst_hbm, first_row, n_blocks, sem):
    blk = zero_ref.shape[0]
    per = blk // SUB_ROWS

    def copy(j):
        row = pl.multiple_of((first_row + j * SUB_ROWS) * per, blk)
        return pltpu.make_async_copy(zero_ref, dst_hbm.at[pl.ds(row, blk)], sem)

    def start(j, c):
        copy(j).start()
        return c

    def wait(j, c):
        copy(j).wait()
        return c

    lax.fori_loop(0, n_blocks, start, 0)
    lax.fori_loop(0, n_blocks, wait, 0)


def _dispatch_kernel(tails_ref, pos_hbm, xp_ref, xs_hbm, pos_smem, zero_ref, sem, psem):
    i = pl.program_id(0)
    ns = TOK_WORDS // LANES
    tm = xp_ref.shape[0] // ns
    n = TOP_K * tm

    def pos_copy(step):
        half = pl.multiple_of((step % 2) * n, n)
        return pltpu.make_async_copy(pos_hbm.at[pl.ds(pl.multiple_of(step * n, n), n)],
                                     pos_smem.at[pl.ds(half, n)], psem.at[step % 2])

    @pl.when(i == 0)
    def _():
        pos_copy(i).start()
        zero_ref[...] = jnp.zeros_like(zero_ref)
        for e in range(N_EXPERTS):
            _zero_fill_rows(zero_ref, xs_hbm, tails_ref[e], 1, sem)
        _zero_fill_rows(zero_ref, xs_hbm, tails_ref[N_EXPERTS], tails_ref[N_EXPERTS + 1], sem)

    pos_copy(i).wait()

    @pl.when(i + 1 < pl.num_programs(0))
    def _():
        pos_copy(i + 1).start()

    base = (i % 2) * n
    for k in range(TOP_K):
        def issue(tt, c):
            for u in range(ROW_DMA_UNROLL):
                t = tt * ROW_DMA_UNROLL + u
                src = xp_ref.at[pl.ds(pl.multiple_of(t * ns, ns), ns)]
                dst = xs_hbm.at[pl.ds(pl.multiple_of(pos_smem[base + k * tm + t] * ns, ns), ns)]
                pltpu.make_async_copy(src, dst, sem).start(priority=u % 2)
            return c

        lax.fori_loop(0, tm // ROW_DMA_UNROLL, issue, 0)
    for _ in range(TOP_K):
        pltpu.make_async_copy(xp_ref, xs_hbm.at[pl.ds(0, tm * ns)], sem).wait()


def _dispatch(tails, pos_tiles, xp, p_alloc, tm):
    ns = TOK_WORDS // LANES
    rows = xp.shape[0] // ns
    return pl.pallas_call(
        _dispatch_kernel,
        grid_spec=pltpu.PrefetchScalarGridSpec(
            num_scalar_prefetch=1,
            grid=(rows // tm,),
            in_specs=[
                pl.BlockSpec(memory_space=pl.ANY),
                pl.BlockSpec((tm * ns, LANES), lambda i, s: (i, 0)),
            ],
            out_specs=pl.BlockSpec(memory_space=pl.ANY),
            scratch_shapes=[
                pltpu.SMEM((2 * TOP_K * tm,), jnp.int32),
                pltpu.VMEM((SUB_ROWS * ns, LANES), jnp.uint32),
                pltpu.SemaphoreType.DMA,
                pltpu.SemaphoreType.DMA((2,)),
            ],
        ),
        out_shape=jax.ShapeDtypeStruct((p_alloc * ns, LANES), jnp.uint32),
        compiler_params=_cparams(("arbitrary",)),
        name="dispatch",
    )(tails, pos_tiles, xp)


def _chunk_dma(cs_ref, cn_ref, src_hbm, dst_ref, per, sem):
    blk = SUB_ROWS * per
    nch = cs_ref.shape[0]

    def copy(ci, j):
        src = src_hbm.at[pl.ds(pl.multiple_of((cs_ref[ci] + j * SUB_ROWS) * per, blk), blk)]
        return pltpu.make_async_copy(src, dst_ref.at[pl.ds(pl.multiple_of(j * blk, blk), blk)], sem)

    def run(ci, wait):
        cc = jnp.minimum(ci, nch - 1)
        n = jnp.where(ci < nch, cn_ref[cc], 0)

        def body(j, c):
            if wait:
                copy(cc, j).wait()
            else:
                copy(cc, j).start()
            return c

        lax.fori_loop(0, n, body, 0)

    return (lambda ci: run(ci, False)), (lambda ci: run(ci, True))


def _tile_loop(nsub, prepare, compute, commit):
    def run(tiles):
        prepare(tiles)
        for t in tiles:
            compute(*t)
        commit(tiles)

    n8 = nsub // 8
    rem = nsub % 8
    base = n8 * 8 * SUB_ROWS
    r2 = base + (rem // 4) * 4 * SUB_ROWS
    r1 = r2 + ((rem % 4) // 2) * 2 * SUB_ROWS

    @pl.when(rem % 4 >= 2)
    def _():
        run([(pl.multiple_of(r2, SUB_ROWS), 2 * SUB_ROWS, 5)])

    @pl.when(rem % 2 == 1)
    def _():
        run([(pl.multiple_of(r1, SUB_ROWS), SUB_ROWS, 6)])

    @pl.when(rem >= 4)
    def _():
        run([(pl.multiple_of(base, 4 * SUB_ROWS), 4 * SUB_ROWS, 4)])

    def pair(j, c):
        r = pl.multiple_of(j * 8 * SUB_ROWS, 8 * SUB_ROWS)
        slot = (j % 2) * 2
        run([(r, 4 * SUB_ROWS, slot), (r + 4 * SUB_ROWS, 4 * SUB_ROWS, slot + 1)])
        return c

    lax.fori_loop(0, n8, pair, 0)


def _weight_stream(ce_ref, cn_ref, nf, w_hbms, tn, wf_ref, wb_ref, wsem):
    def copies(cc, f):
        cols = pl.ds(_mult(f * tn, tn), tn)
        return [pltpu.make_async_copy(w.at[ce_ref[cc], :, cols], wf_ref.at[f % 2, i], wsem.at[f % 2])
                for i, w in enumerate(w_hbms)]

    def first():
        for cp in copies(0, 0):
            cp.start()

    def advance(c, f):
        q = c * nf + f + 1
        cc = jnp.minimum(q // nf, cn_ref.shape[0] - 1)

        @pl.when(jnp.logical_and(q // nf < cn_ref.shape[0], cn_ref[cc] > 0))
        def _():
            for cp in copies(cc, q % nf):
                cp.start()

        for cp in copies(c, f):
            cp.wait()
        for i in range(len(w_hbms)):
            wb_ref[:, i * tn:(i + 1) * tn] = wf_ref[f % 2, i].astype(BF16)

    return first, advance


def _moe_up_kernel(ce_ref, cs_ref, cn_ref, tail_ref, xs_hbm, wg_hbm, bg_ref, wu_hbm, bu_ref, act_hbm,
                   land_ref, xb_ref, wf_ref, wgu_ref, ab_ref, zero_ref, pend_ref, sem, wsem, osem):
    c = pl.program_id(0)
    nsub = cn_ref[c]
    start = pl.multiple_of(cs_ref[c], SUB_ROWS)
    tf = wf_ref.shape[3]
    nf = bg_ref.shape[2] // tf
    ns = TOK_WORDS // LANES
    start_chunk, wait_chunk = _chunk_dma(cs_ref, cn_ref, xs_hbm, land_ref, ns, sem)
    first_weights, next_weights = _weight_stream(ce_ref, cn_ref, nf, (wg_hbm, wu_hbm), tf, wf_ref, wgu_ref, wsem)

    def out_copy(r, rows, slot, f):
        return pltpu.make_async_copy(
            ab_ref.at[slot, pl.ds(0, rows)],
            act_hbm.at[pl.ds(pl.multiple_of(start + r, SUB_ROWS), rows), pl.ds(_mult(f * tf, tf), tf)],
            osem.at[slot])

    @pl.when(c == 0)
    def _():
        zero_ref[...] = jnp.zeros_like(zero_ref)
        _zero_fill_rows(zero_ref, act_hbm, tail_ref[0], tail_ref[1], osem.at[0])
        for slot in range(len(SLOT_ROWS)):
            pend_ref[slot] = 0
        start_chunk(c)
        first_weights()

    @pl.when(nsub > 0)
    def _():
        wait_chunk(c)

        def unpack(j, carry):
            r = pl.multiple_of(j * SUB_ROWS, SUB_ROWS)
            for s in range(ns):
                w = land_ref[pl.ds(r * ns + s, SUB_ROWS, stride=ns), :]
                xb_ref[pl.ds(r, SUB_ROWS), s * LANES:(s + 1) * LANES] = pltpu.bitcast(w << 16, F32).astype(BF16)
                xb_ref[pl.ds(r, SUB_ROWS), TOK_WORDS + s * LANES:TOK_WORDS + (s + 1) * LANES] = (
                    pltpu.bitcast(w & jnp.uint32(0xFFFF0000), F32).astype(BF16))
            return carry

        lax.fori_loop(0, nsub, unpack, 0)
        start_chunk(c + 1)

        def column_block(f, carry):
            next_weights(c, f)
            cols = pl.ds(pl.multiple_of(f * tf, tf), tf)
            bg = bg_ref[0, :, cols]
            bu = bu_ref[0, :, cols]

            def prepare(tiles):
                for _, rows, slot in tiles:
                    @pl.when(pend_ref[slot] == 1)
                    def _():
                        out_copy(0, rows, slot, f).wait()

            def compute(r, rows, slot):
                z = jnp.dot(xb_ref[pl.ds(r, rows), :], wgu_ref[...], preferred_element_type=F32)
                g = z[:, :tf] + bg
                u = z[:, tf:] + bu
                g = jnp.minimum(g, SWIGLU_LIMIT)
                u = jnp.clip(u, -SWIGLU_LIMIT, SWIGLU_LIMIT)
                ab_ref[slot, pl.ds(0, rows), :] = (g * jax.nn.sigmoid(SWIGLU_ALPHA * g) * (u + 1.0)).astype(BF16)

            def commit(tiles):
                for r, rows, slot in tiles:
                    out_copy(r, rows, slot, f).start()
                    pend_ref[slot] = 1

            _tile_loop(nsub, prepare, compute, commit)
            return carry

        lax.fori_loop(0, nf, column_block, 0)

    @pl.when(c == pl.num_programs(0) - 1)
    def _():
        for slot, rows in enumerate(SLOT_ROWS):
            @pl.when(pend_ref[slot] == 1)
            def _():
                out_copy(0, rows, slot, 0).wait()


def _moe_up(ch_e, ch_start, ch_nsub, tail, xs, w_gate, b_gate, w_up, b_up):
    ns = TOK_WORDS // LANES
    p_alloc = xs.shape[0] // ns
    d = 2 * TOK_WORDS
    dff = w_gate.shape[2]
    tf = MOE_TF
    assert (dff // tf) % 2 == 0
    nch = ch_e.shape[0]
    bmap = lambda c, e, s, n, t: (e[c], 0, 0)
    return pl.pallas_call(
        _moe_up_kernel,
        grid_spec=pltpu.PrefetchScalarGridSpec(
            num_scalar_prefetch=4,
            grid=(nch,),
            in_specs=[
                pl.BlockSpec(memory_space=pl.ANY),
                pl.BlockSpec(memory_space=pl.ANY),
                pl.BlockSpec((1, 1, dff), bmap),
                pl.BlockSpec(memory_space=pl.ANY),
                pl.BlockSpec((1, 1, dff), bmap),
            ],
            out_specs=pl.BlockSpec(memory_space=pl.ANY),
            scratch_shapes=[
                pltpu.VMEM((MOE_TM * ns, LANES), jnp.uint32),
                pltpu.VMEM((MOE_TM, d), BF16),
                pltpu.VMEM((2, 2, d, tf), F32),
                pltpu.VMEM((d, 2 * tf), BF16),
                pltpu.VMEM((len(SLOT_ROWS), 4 * SUB_ROWS, tf), BF16),
                pltpu.VMEM((SUB_ROWS, dff), BF16),
                pltpu.SMEM((len(SLOT_ROWS),), jnp.int32),
                pltpu.SemaphoreType.DMA,
                pltpu.SemaphoreType.DMA((2,)),
                pltpu.SemaphoreType.DMA((len(SLOT_ROWS),)),
            ],
        ),
        out_shape=jax.ShapeDtypeStruct((p_alloc, dff), BF16),
        compiler_params=_cparams(("arbitrary",)),
        name="moe_up",
    )(ch_e, ch_start, ch_nsub, tail, xs, w_gate, b_gate.reshape(N_EXPERTS, 1, dff), w_up,
      b_up.reshape(N_EXPERTS, 1, dff))


def _moe_down_kernel(ce_ref, cs_ref, cn_ref, tail_ref, act_hbm, wd_hbm, bd_ref, ys_hbm,
                     x_ref, wf_ref, wdb_ref, yb_ref, zero_ref, pend_ref, sem, wsem, osem):
    c = pl.program_id(0)
    nsub = cn_ref[c]
    start = pl.multiple_of(cs_ref[c], SUB_ROWS)
    tn = wf_ref.shape[3]
    nf = bd_ref.shape[2] // tn
    ns = TOK_WORDS // LANES
    nw = tn // 2 // LANES
    chunk = [_chunk_dma(cs_ref, cn_ref, act_hbm, x_ref.at[i], 1, sem.at[i]) for i in range(2)]
    first_weights, next_weights = _weight_stream(ce_ref, cn_ref, nf, (wd_hbm,), tn, wf_ref, wdb_ref, wsem)

    def on_parity(ci, fn):
        for p in range(2):
            @pl.when(ci % 2 == p)
            def _():
                fn(p)

    def out_copy(r, rows, par):
        src = yb_ref.at[par, pl.ds(pl.multiple_of(r * ns, SUB_ROWS * ns), rows * ns)]
        dst = ys_hbm.at[pl.ds(pl.multiple_of((start + r) * ns, SUB_ROWS * ns), rows * ns)]
        return pltpu.make_async_copy(src, dst, osem.at[par])

    def drain(par):
        for kind, rows in enumerate(SLOT_ROWS[-3:]):
            def wait(j, carry):
                out_copy(0, rows, par).wait()
                return carry

            lax.fori_loop(0, pend_ref[par * 3 + kind], wait, 0)
            pend_ref[par * 3 + kind] = 0

    @pl.when(c == 0)
    def _():
        zero_ref[...] = jnp.zeros_like(zero_ref)
        _zero_fill_rows(zero_ref, ys_hbm, tail_ref[0], tail_ref[1], osem.at[0])
        for kind in range(6):
            pend_ref[kind] = 0
        chunk[0][0](c)
        first_weights()

    def run_chunk():
        for f in range(nf):
            next_weights(c, f)
            bd = bd_ref[0, :, f * tn:(f + 1) * tn]
            last = f == nf - 1
            if f == 0:
                drain(c % 2)

            def compute(r, rows, slot, f=f, bd=bd):
                y = jnp.dot(x_ref[c % 2, pl.ds(r, rows), :], wdb_ref[...], preferred_element_type=F32) + bd
                lo = pltpu.bitcast(y[:, :tn // 2].astype(BF16).astype(F32), jnp.uint32)
                hi = pltpu.bitcast(y[:, tn // 2:].astype(BF16).astype(F32), jnp.uint32)
                packed = (hi & jnp.uint32(0xFFFF0000)) | (lo >> 16)
                for s in range(nw):
                    yb_ref[c % 2, pl.ds(r * ns + f * nw + s, rows, stride=ns), :] = (
                        packed[:, s * LANES:(s + 1) * LANES])

            def commit(tiles, last=last):
                if last:
                    for r, rows, slot in tiles:
                        out_copy(r, rows, c % 2).start()
                        kind = (c % 2) * 3 + SLOT_ROWS[-3:].index(rows)
                        pend_ref[kind] = pend_ref[kind] + 1

            _tile_loop(nsub, lambda tiles: None, compute, commit)

    @pl.when(nsub > 0)
    def _():
        on_parity(c, lambda p: chunk[p][1](c))
        on_parity(c + 1, lambda p: chunk[p][0](c + 1))
        run_chunk()

    @pl.when(c == pl.num_programs(0) - 1)
    def _():
        drain(0)
        drain(1)


def _moe_down(ch_e, ch_start, ch_nsub, tail, act, w_down, b_down):
    p_alloc, dff = act.shape
    d = w_down.shape[2]
    ns = TOK_WORDS // LANES
    tn = MOE_TN
    assert (d // tn) % 2 == 0
    nch = ch_e.shape[0]
    return pl.pallas_call(
        _moe_down_kernel,
        grid_spec=pltpu.PrefetchScalarGridSpec(
            num_scalar_prefetch=4,
            grid=(nch,),
            in_specs=[
                pl.BlockSpec(memory_space=pl.ANY),
                pl.BlockSpec(memory_space=pl.ANY),
                pl.BlockSpec((1, 1, d), lambda c, e, s, n, t: (e[c], 0, 0)),
            ],
            out_specs=pl.BlockSpec(memory_space=pl.ANY),
            scratch_shapes=[
                pltpu.VMEM((2, MOE_TM, dff), BF16),
                pltpu.VMEM((2, 1, dff, tn), F32),
                pltpu.VMEM((dff, tn), BF16),
                pltpu.VMEM((2, MOE_TM * ns, LANES), jnp.uint32),
                pltpu.VMEM((SUB_ROWS * ns, LANES), jnp.uint32),
                pltpu.SMEM((6,), jnp.int32),
                pltpu.SemaphoreType.DMA((2,)),
                pltpu.SemaphoreType.DMA((2,)),
                pltpu.SemaphoreType.DMA((2,)),
            ],
        ),
        out_shape=jax.ShapeDtypeStruct((p_alloc * ns, LANES), jnp.uint32),
        compiler_params=pltpu.CompilerParams(dimension_semantics=("arbitrary",), vmem_limit_bytes=VMEM_LIMIT_MAX),
        name="moe_down",
    )(ch_e, ch_start, ch_nsub, tail, act, w_down, b_down.reshape(N_EXPERTS, 1, d))


def _combine_kernel(pos_hbm, ys_hbm, h_ref, w_ref, g_ref, o_ref, pos_smem, yb_ref, sem, psem):
    i = pl.program_id(0)
    tm, d = h_ref.shape
    n = TOP_K * tm
    ns = TOK_WORDS // LANES

    def pos_copy(tile):
        third = pl.multiple_of((tile % 3) * n, n)
        return pltpu.make_async_copy(pos_hbm.at[pl.ds(pl.multiple_of(tile * n, n), n)],
                                     pos_smem.at[pl.ds(third, n)], psem.at[tile % 3])

    def fetch(tile, slot):
        pos_copy(tile).wait()

        @pl.when(tile + 1 < pl.num_programs(0))
        def _():
            pos_copy(tile + 1).start()

        base = (tile % 3) * n

        def issue(jj, c):
            for u in range(ROW_DMA_UNROLL):
                j = jj * ROW_DMA_UNROLL + u
                src = ys_hbm.at[pl.ds(pl.multiple_of(pos_smem[base + j] * ns, ns), ns)]
                dst = yb_ref.at[slot, pl.ds(pl.multiple_of(j * ns, ns), ns)]
                pltpu.make_async_copy(src, dst, sem.at[slot]).start(priority=u % 2)
            return c

        lax.fori_loop(0, n // ROW_DMA_UNROLL, issue, 0)

    @pl.when(i == 0)
    def _():
        pos_copy(i).start()
        fetch(i, 0)

    @pl.when(i + 1 < pl.num_programs(0))
    def _():
        for p in range(2):
            @pl.when((i + 1) % 2 == p)
            def _():
                fetch(i + 1, p)

    def reduce(slot):
        pltpu.make_async_copy(ys_hbm.at[pl.ds(0, n * ns)], yb_ref.at[slot], sem.at[slot]).wait()
        wb = [jnp.broadcast_to(w_ref[:, k:k + 1], (tm, LANES)) for k in range(TOP_K)]
        ssq = jnp.zeros((tm, 1), F32)
        nw = MOE_TN // 2 // LANES
        for s in range(ns):
            c0 = (s // nw) * MOE_TN + (s % nw) * LANES
            lo_cols = slice(c0, c0 + LANES)
            hi_cols = slice(c0 + MOE_TN // 2, c0 + MOE_TN // 2 + LANES)
            acc_lo = h_ref[:, lo_cols]
            acc_hi = h_ref[:, hi_cols]
            for k in range(TOP_K):
                w = yb_ref[slot, pl.ds(k * tm * ns + s, tm, stride=ns), :]
                acc_lo = acc_lo + wb[k] * pltpu.bitcast(w << 16, F32)
                acc_hi = acc_hi + wb[k] * pltpu.bitcast(w & jnp.uint32(0xFFFF0000), F32)
            o_ref[:, lo_cols] = acc_lo
            o_ref[:, hi_cols] = acc_hi
            ssq = ssq + jnp.sum(acc_lo * acc_lo + acc_hi * acc_hi, axis=-1, keepdims=True)
        o_ref[...] = o_ref[...] * lax.rsqrt(ssq * (1.0 / d) + NORM_EPS) * g_ref[...]

    for p in range(2):
        @pl.when(i % 2 == p)
        def _():
            reduce(p)


def _combine(pos_tiles, ys, h1, wts_t, g_final, tm):
    rows, d = h1.shape
    ns = TOK_WORDS // LANES
    return pl.pallas_call(
        _combine_kernel,
        grid=(rows // tm,),
        in_specs=[
            pl.BlockSpec(memory_space=pl.ANY),
            pl.BlockSpec(memory_space=pl.ANY),
            pl.BlockSpec((tm, d), lambda i: (i, 0)),
            pl.BlockSpec((tm, TOP_K), lambda i: (i, 0)),
            pl.BlockSpec((1, d), lambda i: (0, 0)),
        ],
        out_specs=pl.BlockSpec((tm, d), lambda i: (i, 0)),
        out_shape=jax.ShapeDtypeStruct((rows, d), F32),
        scratch_shapes=[
            pltpu.SMEM((3 * TOP_K * tm,), jnp.int32),
            pltpu.VMEM((2, TOP_K * tm * ns, LANES), jnp.uint32),
            pltpu.SemaphoreType.DMA((2,)),
            pltpu.SemaphoreType.DMA((3,)),
        ],
        compiler_params=_cparams(("arbitrary",)),
        name="combine",
    )(pos_tiles, ys, h1, wts_t, g_final)


def _rope_tables(n_pos):
    half = HEAD_DIM // 2
    inv = 1.0 / (ROPE_THETA ** (jnp.arange(half, dtype=F32) / half))
    ang = jnp.arange(n_pos, dtype=F32)[:, None] * inv[None, :]
    cos = jnp.tile(jnp.cos(ang), (1, LANES // half))
    sin = jnp.tile(jnp.concatenate([-jnp.sin(ang), jnp.sin(ang)], axis=1), (1, LANES // HEAD_DIM))
    return cos, sin


def _block_diag(w):
    per = LRU_GROUP // LRU_BLOCK
    w4 = w.reshape(D_LRU // LRU_GROUP, per, LRU_BLOCK, LRU_BLOCK)
    eye = jnp.eye(per, dtype=w.dtype)
    bd = jnp.einsum("gpcd,pq->gpcqd", w4, eye)
    return bd.reshape(D_LRU // LRU_GROUP, LRU_GROUP, LRU_GROUP).astype(BF16)


def _tile_positions(pos, tm):
    rows = pos.shape[1]
    return pos.reshape(TOP_K, rows // tm, tm).transpose(1, 0, 2).reshape(-1)


def _chunk_schedule(counts, n_rows):
    aligned = ((counts + SUB_ROWS - 1) // SUB_ROWS) * SUB_ROWS
    offs = jnp.cumsum(aligned) - aligned
    n_ch = (aligned + MOE_TM - 1) // MOE_TM
    cum = jnp.cumsum(n_ch)
    total = cum[-1]
    nch_max = N_EXPERTS + (n_rows + N_EXPERTS * SUB_ROWS) // MOE_TM
    c = jnp.arange(nch_max, dtype=jnp.int32)
    cc = jnp.minimum(c, total - 1)
    e = jnp.searchsorted(cum, cc, side="right").astype(jnp.int32)
    j = cc - (cum[e] - n_ch[e])
    start = offs[e] + j * MOE_TM
    nsub = jnp.where(c < total, jnp.minimum(MOE_TM, aligned[e] - j * MOE_TM) // SUB_ROWS, 0)
    tails = offs + (counts // SUB_ROWS) * SUB_ROWS
    used = jnp.sum(aligned)
    tail = jnp.stack([used, (n_rows + N_EXPERTS * SUB_ROWS - used) // SUB_ROWS])
    tails = jnp.concatenate([tails, tail])
    return (offs, e, start.astype(jnp.int32), nsub.astype(jnp.int32), tails.astype(jnp.int32),
            tail.astype(jnp.int32))


def kernel(x, meta_tokens, norm_mix, w_in, b_in, sinks, conv_w, conv_b, w_a, b_a, w_i, b_i, lru_lambda,
           g_attn_out, g_lru_out, w_out, b_out, norm_ffn, w_router, b_router, w_gate, b_gate, w_up, b_up,
           w_down, b_down, final_norm):
    batch, seq, d = x.shape
    rows = batch * seq
    x2d = x.reshape(rows, d)
    row = lambda v: v.reshape(1, -1)

    w_in_bf = w_in[0].astype(BF16)
    w_out_bf = w_out[0].astype(BF16)
    cos, sin = _rope_tables(N_META + seq)
    wa_bd, wi_bd = _block_diag(w_a[0]), _block_diag(w_i[0])
    lru_args = (conv_w[0], row(conv_b[0]), wa_bd, row(b_a[0]), wi_bd, row(b_i[0]), row(lru_lambda[0]),
                row(g_lru_out[0]))

    _, kvm, xgm = _inproj(meta_tokens, row(norm_mix[0]), w_in_bf, row(b_in[0]), cos[:N_META], sin[:N_META],
                          N_META, 1)
    _, h0, tail0 = _lru(xgm, *lru_args, jnp.zeros((1, D_LRU), F32), jnp.zeros((8, D_LRU), F32), 1, N_META,
                        N_META)

    tm = ROW_TILE
    q, kv, xg = _inproj(x2d, row(norm_mix[0]), w_in_bf, row(b_in[0]), cos[N_META:], sin[N_META:], tm,
                        seq // tm)
    kvm = jnp.pad(kvm, ((0, BLOCK - N_META), (0, 0)))
    attn_n = _attention(sinks[0], q, kv, kvm, row(g_attn_out[0]), batch, seq)
    lru_n, _, _ = _lru(xg, *lru_args, h0, tail0, batch, seq, LRU_TIME_TILE)

    tri = (jnp.arange(tm)[:, None] <= jnp.arange(tm)[None, :]).astype(BF16)
    h1, xp, ids, wts, rank, cnt = _outproj(
        attn_n, lru_n, w_out_bf, row(b_out[0]), x2d, row(norm_ffn[0]), w_router[0].T.astype(BF16),
        b_router[0].reshape(N_EXPERTS, 1), tri, tm)

    counts = cnt[:, 0].astype(jnp.int32)
    n_rows = rows * TOP_K
    offs, ch_e, ch_start, ch_nsub, tails, tail = _chunk_schedule(counts, n_rows)
    onehot = ids[..., None] == jnp.arange(N_EXPERTS, dtype=jnp.int32)
    pos = jnp.sum(jnp.where(onehot, offs, 0), axis=-1) + rank
    p_alloc = n_rows + N_EXPERTS * SUB_ROWS

    xs = _dispatch(tails, _tile_positions(pos, ROW_TILE), xp, p_alloc, ROW_TILE)
    act = _moe_up(ch_e, ch_start, ch_nsub, tail, xs, w_gate[0], b_gate[0], w_up[0], b_up[0])
    ys = _moe_down(ch_e, ch_start, ch_nsub, tail, act, w_down[0], b_down[0])
    out = _combine(_tile_positions(pos, COMBINE_TILE), ys, h1, wts.T, row(final_norm), COMBINE_TILE)
    return out.reshape(batch, seq, d)
```

```python
import jax
import jax.numpy as jnp
from jax import lax
from jax.experimental import pallas as pl
from jax.experimental.pallas import tpu as pltpu

F32 = jnp.float32
BF16 = jnp.bfloat16

N_META = 16
HEAD_DIM = 64
N_Q_HEADS = 16
N_KV_HEADS = 4
GROUP = N_Q_HEADS // N_KV_HEADS
D_ATTN = N_Q_HEADS * HEAD_DIM
D_KV = N_KV_HEADS * HEAD_DIM
KV_COLS = 4 * D_KV
BLOCK = 128
ATTN_QBLOCKS = 8
ROPE_THETA = 10000.0
D_LRU = 1024
LRU_BLOCK = 64
LRU_GROUP = 256
CONV_WIDTH = 4
LRU_C = 8.0
N_EXPERTS = 32
TOP_K = 4
SWIGLU_LIMIT = 7.0
SWIGLU_ALPHA = 1.702
NORM_EPS = 1e-5
NEG_INF = -1e30

LANES = 128
D_MODEL = 2048
TOK_WORDS = D_MODEL // 2
SUB_ROWS = 128
MOE_TM = 2304
MOE_TF = 512
MOE_TN = 1024
ROW_TILE = 512
INPROJ_COLS = 512
LRU_TIME_TILE = 128
COMBINE_TILE = 256
ROW_DMA_UNROLL = 16
SLOT_ROWS = (4 * SUB_ROWS,) * 5 + (2 * SUB_ROWS, SUB_ROWS)
VMEM_LIMIT = 56 * 1024 * 1024
VMEM_LIMIT_MAX = 62 * 1024 * 1024


def _cparams(sem):
    return pltpu.CompilerParams(dimension_semantics=sem, vmem_limit_bytes=VMEM_LIMIT)


def _mult(x, m):
    return x if isinstance(x, int) else pl.multiple_of(x, m)


def _rms(x, g):
    return x * lax.rsqrt(jnp.mean(x * x, axis=-1, keepdims=True) + NORM_EPS) * g


def _inproj_kernel(x_ref, g_ref, w_ref, b_ref, cos_ref, sin_ref, q_ref, kv_ref, xg_ref):
    xn = _rms(x_ref[...], g_ref[...]).astype(BF16)
    cos = cos_ref[...]
    sin = sin_ref[...]
    lane = lax.broadcasted_iota(jnp.int32, (1, LANES), 1)
    first_half = (lane % HEAD_DIM) < (HEAD_DIM // 2)

    def rope(z):
        partner = jnp.where(first_half, pltpu.roll(z, LANES - HEAD_DIM // 2, 1),
                            pltpu.roll(z, HEAD_DIM // 2, 1))
        return z * cos + partner * sin

    cw = INPROJ_COLS
    for c in range(w_ref.shape[1] // cw):
        z = jnp.dot(xn, w_ref[:, c * cw:(c + 1) * cw], preferred_element_type=F32)
        z = z + b_ref[:, c * cw:(c + 1) * cw]
        if c < 2:
            for j in range(cw // LANES):
                zz = rope(z[:, j * LANES:(j + 1) * LANES]) * (HEAD_DIM ** -0.5)
                q_ref[:, c * cw + j * LANES:c * cw + (j + 1) * LANES] = zz.astype(BF16)
        elif c == 2:
            low = lane < HEAD_DIM
            for j in range(2 * D_KV // LANES):
                zz = z[:, j * LANES:(j + 1) * LANES]
                zz = rope(zz) if j < D_KV // LANES else zz
                sw = pltpu.roll(zz, HEAD_DIM, 1)
                kv_ref[:, 2 * j * LANES:(2 * j + 1) * LANES] = jnp.where(low, zz, sw).astype(BF16)
                kv_ref[:, (2 * j + 1) * LANES:(2 * j + 2) * LANES] = jnp.where(low, sw, zz).astype(BF16)
        else:
            xg_ref[:, (c - 3) * cw:(c - 2) * cw] = z


def _inproj(x2d, g, w_bf, b, cos, sin, tm, pos_blocks):
    rows, d = x2d.shape
    dz = w_bf.shape[1]
    return pl.pallas_call(
        _inproj_kernel,
        grid=(rows // tm,),
        in_specs=[
            pl.BlockSpec((tm, d), lambda m: (m, 0)),
            pl.BlockSpec((1, d), lambda m: (0, 0)),
            pl.BlockSpec((d, dz), lambda m: (0, 0)),
            pl.BlockSpec((1, dz), lambda m: (0, 0)),
            pl.BlockSpec((tm, LANES), lambda m: (m % pos_blocks, 0)),
            pl.BlockSpec((tm, LANES), lambda m: (m % pos_blocks, 0)),
        ],
        out_specs=[
            pl.BlockSpec((tm, D_ATTN), lambda m: (m, 0)),
            pl.BlockSpec((tm, KV_COLS), lambda m: (m, 0)),
            pl.BlockSpec((tm, 2 * D_LRU), lambda m: (m, 0)),
        ],
        out_shape=[
            jax.ShapeDtypeStruct((rows, D_ATTN), BF16),
            jax.ShapeDtypeStruct((rows, KV_COLS), BF16),
            jax.ShapeDtypeStruct((rows, 2 * D_LRU), F32),
        ],
        compiler_params=_cparams(("arbitrary",)),
        name="inproj",
    )(x2d, g, w_bf, b, cos, sin)


def _attn_block(sink_ref, q_ref, kvc_ref, kvp_ref, kvm_ref, g_ref, o_ref, n, rs):
    row = lax.broadcasted_iota(jnp.int32, (GROUP * BLOCK, BLOCK), 0) % BLOCK
    col = lax.broadcasted_iota(jnp.int32, (GROUP * BLOCK, BLOCK), 1)
    in_cur = col <= row
    in_prev = jnp.logical_and(col > row, n > 0)
    is_meta = col < N_META
    low = lax.broadcasted_iota(jnp.int32, (BLOCK, LANES), 1) < HEAD_DIM
    nt = (((1,), (1,)), ((), ()))
    zero = jnp.zeros((), BF16)
    outs = []
    for h in range(N_KV_HEADS):
        ks = slice(h * LANES, (h + 1) * LANES)
        vs = slice(N_KV_HEADS * LANES + h * LANES, N_KV_HEADS * LANES + (h + 1) * LANES)
        parts = []
        for j in range(GROUP // 2):
            qg = q_ref[rs, (h * GROUP // 2 + j) * LANES:(h * GROUP // 2 + j + 1) * LANES]
            parts += [jnp.where(low, qg, zero), jnp.where(low, zero, qg)]
        qs = jnp.concatenate(parts, axis=0)
        s_c = lax.dot_general(qs, kvc_ref[:, ks], nt, preferred_element_type=F32)
        s_p = lax.dot_general(qs, kvp_ref[:, ks], nt, preferred_element_type=F32)
        s_m = lax.dot_general(qs, kvm_ref[:, ks], nt, preferred_element_type=F32)
        s_b = jnp.where(in_prev, s_p, jnp.where(in_cur, s_c, NEG_INF))
        s_m = jnp.where(is_meta, s_m, NEG_INF)
        sink = jnp.concatenate(
            [jnp.full((BLOCK, 1), sink_ref[h * GROUP + g], F32) for g in range(GROUP)], axis=0)
        m = jnp.maximum(jnp.max(jnp.maximum(s_b, s_m), axis=-1, keepdims=True), sink)
        p_b = jnp.exp(s_b - m)
        p_m = jnp.exp(s_m - m)
        den = jnp.sum(p_b + p_m, axis=-1, keepdims=True) + jnp.exp(sink - m)
        r = (jnp.dot(jnp.where(in_cur, p_b, 0.0).astype(BF16), kvc_ref[:, vs], preferred_element_type=F32)
             + jnp.dot(jnp.where(in_cur, 0.0, p_b).astype(BF16), kvp_ref[:, vs], preferred_element_type=F32)
             + jnp.dot(p_m.astype(BF16), kvm_ref[:, vs], preferred_element_type=F32))
        r = r / den
        for j in range(GROUP // 2):
            outs.append(jnp.where(low, r[2 * j * BLOCK:(2 * j + 1) * BLOCK], r[(2 * j + 1) * BLOCK:(2 * j + 2) * BLOCK]))
    o_all = jnp.concatenate(outs, axis=1)
    o_ref[rs, :] = _rms(o_all, g_ref[...]).astype(BF16)


def _attn_kernel(sink_ref, q_ref, *rest):
    kv_refs, (kvm_ref, g_ref, o_ref) = rest[:ATTN_QBLOCKS + 1], rest[ATTN_QBLOCKS + 1:]
    n0 = pl.program_id(1) * ATTN_QBLOCKS
    for sub in range(ATTN_QBLOCKS):
        _attn_block(sink_ref, q_ref, kv_refs[sub + 1], kv_refs[sub], kvm_ref, g_ref, o_ref, n0 + sub,
                    slice(sub * BLOCK, (sub + 1) * BLOCK))


def _attention(sinks, q, kv, kvm, g_attn, batch, seq):
    nb = seq // BLOCK
    nq = ATTN_QBLOCKS
    kv_spec = lambda i: pl.BlockSpec((BLOCK, KV_COLS), lambda b, n, s: (b * nb + jnp.maximum(nq * n + i - 1, 0), 0))
    return pl.pallas_call(
        _attn_kernel,
        grid_spec=pltpu.PrefetchScalarGridSpec(
            num_scalar_prefetch=1,
            grid=(batch, nb // nq),
            in_specs=[
                pl.BlockSpec((nq * BLOCK, D_ATTN), lambda b, n, s: (b * (nb // nq) + n, 0)),
                *[kv_spec(i) for i in range(nq + 1)],
                pl.BlockSpec((BLOCK, KV_COLS), lambda b, n, s: (0, 0)),
                pl.BlockSpec((1, D_ATTN), lambda b, n, s: (0, 0)),
            ],
            out_specs=pl.BlockSpec((nq * BLOCK, D_ATTN), lambda b, n, s: (b * (nb // nq) + n, 0)),
        ),
        out_shape=jax.ShapeDtypeStruct((batch * seq, D_ATTN), BF16),
        compiler_params=_cparams(("arbitrary", "arbitrary")),
        name="attention",
    )(sinks, q, *([kv] * (ATTN_QBLOCKS + 1)), kvm, g_attn)


def _one_minus_sq(a, log_a):
    y = 2.0 * log_a
    p = 1.0 + y * (1.0 / 4.0)
    for k in (3, 2):
        p = 1.0 + y * p * (1.0 / k)
    return jnp.where(y > -1.0 / 64.0, -(y * p), 1.0 - a * a)


def _lru_kernel(xg_ref, cw_ref, cb_ref, wa_ref, ba_ref, wi_ref, bi_ref, lam_ref, g_ref, h0_ref, tail0_ref,
                o_ref, hout_ref, tailout_ref, ext_ref, tail_ref, a_ref, b_ref, h_ref):
    tt = pl.program_id(0)
    batch, rows = xg_ref.shape[0], xg_ref.shape[1]

    @pl.when(tt == 0)
    def _():
        for g in range(D_LRU // LANES):
            h_ref[g] = jnp.broadcast_to(h0_ref[:, g * LANES:(g + 1) * LANES], (batch, LANES))
        for s in range(batch):
            tail_ref[s] = tail0_ref[...]

    sp = jax.nn.softplus(-lam_ref[...])

    def gates(s, carry):
        ext_ref[0:8, :] = tail_ref[s]
        ext_ref[8:, :] = xg_ref[s, :, :D_LRU]
        tail_ref[s] = ext_ref[rows:rows + 8, :]
        xc = cb_ref[...] + sum(cw_ref[j:j + 1, :] * ext_ref[5 + j:5 + j + rows, :] for j in range(CONV_WIDTH))
        xcb = xc.astype(BF16)
        for c in range(D_LRU // LRU_GROUP):
            cs = slice(c * LRU_GROUP, (c + 1) * LRU_GROUP)
            r = jax.nn.sigmoid(jnp.dot(xcb[:, cs], wa_ref[c], preferred_element_type=F32) + ba_ref[:, cs])
            i = jax.nn.sigmoid(jnp.dot(xcb[:, cs], wi_ref[c], preferred_element_type=F32) + bi_ref[:, cs])
            log_a = -LRU_C * r * sp[:, cs]
            a = jnp.exp(log_a)
            b = jnp.sqrt(_one_minus_sq(a, log_a)) * i * xc[:, cs]
            for g in range(LRU_GROUP // LANES):
                lanes = slice(g * LANES, (g + 1) * LANES)
                a_ref[c * (LRU_GROUP // LANES) + g, pl.ds(s, rows, stride=batch), :] = a[:, lanes]
                b_ref[c * (LRU_GROUP // LANES) + g, pl.ds(s, rows, stride=batch), :] = b[:, lanes]
        return carry

    lax.fori_loop(0, batch, gates, 0)

    def step(t, h):
        r0 = pl.multiple_of(t * batch, batch)
        new = []
        for g in range(D_LRU // LANES):
            hg = a_ref[g, pl.ds(r0, batch), :] * h[g] + b_ref[g, pl.ds(r0, batch), :]
            b_ref[g, pl.ds(r0, batch), :] = hg
            new.append(hg)
        return tuple(new)

    h_last = lax.fori_loop(0, rows, step, tuple(h_ref[g] for g in range(D_LRU // LANES)))
    for g in range(D_LRU // LANES):
        h_ref[g] = h_last[g]
    hout_ref[...] = jnp.concatenate([h_last[g][0:1] for g in range(D_LRU // LANES)], axis=1)
    tailout_ref[...] = tail_ref[0]

    def finish(s, carry):
        hs = jnp.concatenate([b_ref[g, pl.ds(s, rows, stride=batch), :] for g in range(D_LRU // LANES)], axis=1)
        y = hs * jax.nn.gelu(xg_ref[s, :, D_LRU:])
        o_ref[s] = _rms(y, g_ref[...]).astype(BF16)
        return carry

    lax.fori_loop(0, batch, finish, 0)


def _lru(xg, conv_w, conv_b, wa_bd, b_a, wi_bd, b_i, lam, g_lru, h0, tail0, batch, seq, tt):
    ntt = seq // tt
    ng = D_LRU // LANES
    vec = pl.BlockSpec((1, D_LRU), lambda t: (0, 0))
    wspec = pl.BlockSpec((D_LRU // LRU_GROUP, LRU_GROUP, LRU_GROUP), lambda t: (0, 0, 0))
    out, h_last, tail = pl.pallas_call(
        _lru_kernel,
        grid=(ntt,),
        in_specs=[
            pl.BlockSpec((batch, tt, 2 * D_LRU), lambda t: (0, t, 0)),
            pl.BlockSpec((CONV_WIDTH, D_LRU), lambda t: (0, 0)),
            vec, wspec, vec, wspec, vec, vec, vec, vec,
            pl.BlockSpec((8, D_LRU), lambda t: (0, 0)),
        ],
        out_specs=[
            pl.BlockSpec((batch, tt, D_LRU), lambda t: (0, t, 0)),
            pl.BlockSpec((1, D_LRU), lambda t: (0, 0)),
            pl.BlockSpec((8, D_LRU), lambda t: (0, 0)),
        ],
        out_shape=[
            jax.ShapeDtypeStruct((batch, seq, D_LRU), BF16),
            jax.ShapeDtypeStruct((1, D_LRU), F32),
            jax.ShapeDtypeStruct((8, D_LRU), F32),
        ],
        scratch_shapes=[
            pltpu.VMEM((8 + tt, D_LRU), F32),
            pltpu.VMEM((batch, 8, D_LRU), F32),
            pltpu.VMEM((ng, tt * batch, LANES), F32),
            pltpu.VMEM((ng, tt * batch, LANES), F32),
            pltpu.VMEM((ng, batch, LANES), F32),
        ],
        compiler_params=_cparams(("arbitrary",)),
        name="rglru",
    )(xg.reshape(batch, seq, 2 * D_LRU), conv_w, conv_b, wa_bd, b_a, wi_bd, b_i, lam, g_lru, h0, tail0)
    return out.reshape(batch * seq, D_LRU), h_last, tail


def _outproj_kernel(a_ref, l_ref, wo_ref, bo_ref, x_ref, gf_ref, wr_ref, br_ref, tri_ref,
                    h_ref, xp_ref, ids_ref, wts_ref, rank_ref, cnt_ref, carry_ref):
    m = pl.program_id(0)
    tm = a_ref.shape[0]

    @pl.when(m == 0)
    def _():
        carry_ref[...] = jnp.zeros_like(carry_ref)

    mix = jnp.concatenate([a_ref[...], l_ref[...]], axis=1)
    h = jnp.dot(mix, wo_ref[...], preferred_element_type=F32) + bo_ref[...] + x_ref[...]
    h_ref[...] = h
    xn = _rms(h, gf_ref[...]).astype(BF16)
    half = xn.shape[1] // 2
    lo = pltpu.bitcast(xn[:, :half].astype(F32), jnp.uint32)
    hi = pltpu.bitcast(xn[:, half:].astype(F32), jnp.uint32)
    packed = (hi & jnp.uint32(0xFFFF0000)) | (lo >> 16)
    ns = half // LANES
    for s in range(ns):
        xp_ref[pl.ds(s, tm, stride=ns), :] = packed[:, s * LANES:(s + 1) * LANES]

    logits = lax.dot_general(wr_ref[...], xn, (((1,), (1,)), ((), ())), preferred_element_type=F32)
    logits = logits + br_ref[...]
    eidx = lax.broadcasted_iota(jnp.int32, (N_EXPERTS, tm), 0)
    work = logits
    vals, sels = [], []
    for k in range(TOP_K):
        v = jnp.max(work, axis=0, keepdims=True)
        idx = jnp.min(jnp.where(work == v, eidx, N_EXPERTS), axis=0, keepdims=True)
        sel = eidx == idx
        ids_ref[k:k + 1, :] = idx
        vals.append(v)
        sels.append(sel)
        work = jnp.where(sel, -jnp.inf, work)
    es = [jnp.exp(v - vals[0]) for v in vals]
    den = es[0] + es[1] + es[2] + es[3]
    for k in range(TOP_K):
        wts_ref[k:k + 1, :] = es[k] / den
    cnt = sum(s.astype(F32) for s in sels)
    incl = jnp.dot(cnt.astype(BF16), tri_ref[...], preferred_element_type=F32)
    before = incl - cnt + carry_ref[:, 0:1]
    for k in range(TOP_K):
        rk = jnp.sum(jnp.where(sels[k], before, 0.0), axis=0, keepdims=True)
        rank_ref[k:k + 1, :] = rk.astype(jnp.int32)
    carry_ref[...] = carry_ref[...] + incl[:, tm - 1:tm]
    cnt_ref[...] = carry_ref[...]


def _outproj(attn_n, lru_n, wo_bf, b_out, x2d, g_ffn, wr_t, br, tri, tm):
    rows, d = x2d.shape
    const = lambda shape: pl.BlockSpec(shape, lambda m: tuple(0 for _ in shape))
    return pl.pallas_call(
        _outproj_kernel,
        grid=(rows // tm,),
        in_specs=[
            pl.BlockSpec((tm, D_ATTN), lambda m: (m, 0)),
            pl.BlockSpec((tm, D_LRU), lambda m: (m, 0)),
            const((D_ATTN + D_LRU, d)), const((1, d)),
            pl.BlockSpec((tm, d), lambda m: (m, 0)),
            const((1, d)), const((N_EXPERTS, d)), const((N_EXPERTS, 1)), const((tm, tm)),
        ],
        out_specs=[
            pl.BlockSpec((tm, d), lambda m: (m, 0)),
            pl.BlockSpec((tm * (d // 2 // LANES), LANES), lambda m: (m, 0)),
            pl.BlockSpec((TOP_K, tm), lambda m: (0, m)),
            pl.BlockSpec((TOP_K, tm), lambda m: (0, m)),
            pl.BlockSpec((TOP_K, tm), lambda m: (0, m)),
            const((N_EXPERTS, LANES)),
        ],
        out_shape=[
            jax.ShapeDtypeStruct((rows, d), F32),
            jax.ShapeDtypeStruct((rows * (d // 2 // LANES), LANES), jnp.uint32),
            jax.ShapeDtypeStruct((TOP_K, rows), jnp.int32),
            jax.ShapeDtypeStruct((TOP_K, rows), F32),
            jax.ShapeDtypeStruct((TOP_K, rows), jnp.int32),
            jax.ShapeDtypeStruct((N_EXPERTS, LANES), F32),
        ],
        scratch_shapes=[pltpu.VMEM((N_EXPERTS, LANES), F32)],
        compiler_params=_cparams(("arbitrary",)),
        name="outproj_router",
    )(attn_n, lru_n, wo_bf, b_out, x2d, g_ffn, wr_t, br, tri)


def _zero_fill_rows(zero_ref, dst_hbm, first_row, n_blocks, sem):
    blk = zero_ref.shape[0]
    per = blk // SUB_ROWS

    def copy(j):
        row = pl.multiple_of((first_row + j * SUB_ROWS) * per, blk)
        return pltpu.make_async_copy(zero_ref, dst_hbm.at[pl.ds(row, blk)], sem)

    def start(j, c):
        copy(j).start()
        return c

    def wait(j, c):
        copy(j).wait()
        return c

    lax.fori_loop(0, n_blocks, start, 0)
    lax.fori_loop(0, n_blocks, wait, 0)


def _dispatch_kernel(tails_ref, pos_hbm, xp_ref, xs_hbm, pos_smem, zero_ref, sem, psem):
    i = pl.program_id(0)
    ns = TOK_WORDS // LANES
    tm = xp_ref.shape[0] // ns
    n = TOP_K * tm

    def pos_copy(step):
        half = pl.multiple_of((step % 2) * n, n)
        return pltpu.make_async_copy(pos_hbm.at[pl.ds(pl.multiple_of(step * n, n), n)],
                                     pos_smem.at[pl.ds(half, n)], psem.at[step % 2])

    @pl.when(i == 0)
    def _():
        pos_copy(i).start()
        zero_ref[...] = jnp.zeros_like(zero_ref)
        for e in range(N_EXPERTS):
            _zero_fill_rows(zero_ref, xs_hbm, tails_ref[e], 1, sem)
        _zero_fill_rows(zero_ref, xs_hbm, tails_ref[N_EXPERTS], tails_ref[N_EXPERTS + 1], sem)

    pos_copy(i).wait()

    @pl.when(i + 1 < pl.num_programs(0))
    def _():
        pos_copy(i + 1).start()

    base = (i % 2) * n
    for k in range(TOP_K):
        def issue(tt, c):
            for u in range(ROW_DMA_UNROLL):
                t = tt * ROW_DMA_UNROLL + u
                src = xp_ref.at[pl.ds(pl.multiple_of(t * ns, ns), ns)]
                dst = xs_hbm.at[pl.ds(pl.multiple_of(pos_smem[base + k * tm + t] * ns, ns), ns)]
                pltpu.make_async_copy(src, dst, sem).start(priority=u % 2)
            return c

        lax.fori_loop(0, tm // ROW_DMA_UNROLL, issue, 0)
    for _ in range(TOP_K):
        pltpu.make_async_copy(xp_ref, xs_hbm.at[pl.ds(0, tm * ns)], sem).wait()


def _dispatch(tails, pos_tiles, xp, p_alloc, tm):
    ns = TOK_WORDS // LANES
    rows = xp.shape[0] // ns
    return pl.pallas_call(
        _dispatch_kernel,
        grid_spec=pltpu.PrefetchScalarGridSpec(
            num_scalar_prefetch=1,
            grid=(rows // tm,),
            in_specs=[
                pl.BlockSpec(memory_space=pl.ANY),
                pl.BlockSpec((tm * ns, LANES), lambda i, s: (i, 0)),
            ],
            out_specs=pl.BlockSpec(memory_space=pl.ANY),
            scratch_shapes=[
                pltpu.SMEM((2 * TOP_K * tm,), jnp.int32),
                pltpu.VMEM((SUB_ROWS * ns, LANES), jnp.uint32),
                pltpu.SemaphoreType.DMA,
                pltpu.SemaphoreType.DMA((2,)),
            ],
        ),
        out_shape=jax.ShapeDtypeStruct((p_alloc * ns, LANES), jnp.uint32),
        compiler_params=_cparams(("arbitrary",)),
        name="dispatch",
    )(tails, pos_tiles, xp)


def _chunk_dma(cs_ref, cn_ref, src_hbm, dst_ref, per, sem):
    blk = SUB_ROWS * per
    nch = cs_ref.shape[0]

    def copy(ci, j):
        src = src_hbm.at[pl.ds(pl.multiple_of((cs_ref[ci] + j * SUB_ROWS) * per, blk), blk)]
        return pltpu.make_async_copy(src, dst_ref.at[pl.ds(pl.multiple_of(j * blk, blk), blk)], sem)

    def run(ci, wait):
        cc = jnp.minimum(ci, nch - 1)
        n = jnp.where(ci < nch, cn_ref[cc], 0)

        def body(j, c):
            if wait:
                copy(cc, j).wait()
            else:
                copy(cc, j).start()
            return c

        lax.fori_loop(0, n, body, 0)

    return (lambda ci: run(ci, False)), (lambda ci: run(ci, True))


def _tile_loop(nsub, prepare, compute, commit):
    def run(tiles):
        prepare(tiles)
        for t in tiles:
            compute(*t)
        commit(tiles)

    n8 = nsub // 8
    rem = nsub % 8
    base = n8 * 8 * SUB_ROWS
    r2 = base + (rem // 4) * 4 * SUB_ROWS
    r1 = r2 + ((rem % 4) // 2) * 2 * SUB_ROWS

    @pl.when(rem % 4 >= 2)
    def _():
        run([(pl.multiple_of(r2, SUB_ROWS), 2 * SUB_ROWS, 5)])

    @pl.when(rem % 2 == 1)
    def _():
        run([(pl.multiple_of(r1, SUB_ROWS), SUB_ROWS, 6)])

    @pl.when(rem >= 4)
    def _():
        run([(pl.multiple_of(base, 4 * SUB_ROWS), 4 * SUB_ROWS, 4)])

    def pair(j, c):
        r = pl.multiple_of(j * 8 * SUB_ROWS, 8 * SUB_ROWS)
        slot = (j % 2) * 2
        run([(r, 4 * SUB_ROWS, slot), (r + 4 * SUB_ROWS, 4 * SUB_ROWS, slot + 1)])
        return c

    lax.fori_loop(0, n8, pair, 0)


def _weight_stream(ce_ref, cn_ref, nf, w_hbms, tn, wf_ref, wb_ref, wsem):
    def copies(cc, f):
        cols = pl.ds(_mult(f * tn, tn), tn)
        return [pltpu.make_async_copy(w.at[ce_ref[cc], :, cols], wf_ref.at[f % 2, i], wsem.at[f % 2])
                for i, w in enumerate(w_hbms)]

    def first():
        for cp in copies(0, 0):
            cp.start()

    def advance(c, f):
        q = c * nf + f + 1
        cc = jnp.minimum(q // nf, cn_ref.shape[0] - 1)

        @pl.when(jnp.logical_and(q // nf < cn_ref.shape[0], cn_ref[cc] > 0))
        def _():
            for cp in copies(cc, q % nf):
                cp.start()

        for cp in copies(c, f):
            cp.wait()
        for i in range(len(w_hbms)):
            wb_ref[:, i * tn:(i + 1) * tn] = wf_ref[f % 2, i].astype(BF16)

    return first, advance


def _moe_up_kernel(ce_ref, cs_ref, cn_ref, tail_ref, xs_hbm, wg_hbm, bg_ref, wu_hbm, bu_ref, act_hbm,
                   land_ref, xb_ref, wf_ref, wgu_ref, ab_ref, zero_ref, pend_ref, sem, wsem, osem):
    c = pl.program_id(0)
    nsub = cn_ref[c]
    start = pl.multiple_of(cs_ref[c], SUB_ROWS)
    tf = wf_ref.shape[3]
    nf = bg_ref.shape[2] // tf
    ns = TOK_WORDS // LANES
    start_chunk, wait_chunk = _chunk_dma(cs_ref, cn_ref, xs_hbm, land_ref, ns, sem)
    first_weights, next_weights = _weight_stream(ce_ref, cn_ref, nf, (wg_hbm, wu_hbm), tf, wf_ref, wgu_ref, wsem)

    def out_copy(r, rows, slot, f):
        return pltpu.make_async_copy(
            ab_ref.at[slot, pl.ds(0, rows)],
            act_hbm.at[pl.ds(pl.multiple_of(start + r, SUB_ROWS), rows), pl.ds(_mult(f * tf, tf), tf)],
            osem.at[slot])

    @pl.when(c == 0)
    def _():
        zero_ref[...] = jnp.zeros_like(zero_ref)
        _zero_fill_rows(zero_ref, act_hbm, tail_ref[0], tail_ref[1], osem.at[0])
        for slot in range(len(SLOT_ROWS)):
            pend_ref[slot] = 0
        start_chunk(c)
        first_weights()

    @pl.when(nsub > 0)
    def _():
        wait_chunk(c)

        def unpack(j, carry):
            r = pl.multiple_of(j * SUB_ROWS, SUB_ROWS)
            for s in range(ns):
                w = land_ref[pl.ds(r * ns + s, SUB_ROWS, stride=ns), :]
                xb_ref[pl.ds(r, SUB_ROWS), s * LANES:(s + 1) * LANES] = pltpu.bitcast(w << 16, F32).astype(BF16)
                xb_ref[pl.ds(r, SUB_ROWS), TOK_WORDS + s * LANES:TOK_WORDS + (s + 1) * LANES] = (
                    pltpu.bitcast(w & jnp.uint32(0xFFFF0000), F32).astype(BF16))
            return carry

        lax.fori_loop(0, nsub, unpack, 0)
        start_chunk(c + 1)

        def column_block(f, carry):
            next_weights(c, f)
            cols = pl.ds(pl.multiple_of(f * tf, tf), tf)
            bg = bg_ref[0, :, cols]
            bu = bu_ref[0, :, cols]

            def prepare(tiles):
                for _, rows, slot in tiles:
                    @pl.when(pend_ref[slot] == 1)
                    def _():
                        out_copy(0, rows, slot, f).wait()

            def compute(r, rows, slot):
                z = jnp.dot(xb_ref[pl.ds(r, rows), :], wgu_ref[...], preferred_element_type=F32)
                g = z[:, :tf] + bg
                u = z[:, tf:] + bu
                g = jnp.minimum(g, SWIGLU_LIMIT)
                u = jnp.clip(u, -SWIGLU_LIMIT, SWIGLU_LIMIT)
                ab_ref[slot, pl.ds(0, rows), :] = (g * jax.nn.sigmoid(SWIGLU_ALPHA * g) * (u + 1.0)).astype(BF16)

            def commit(tiles):
                for r, rows, slot in tiles:
                    out_copy(r, rows, slot, f).start()
                    pend_ref[slot] = 1

            _tile_loop(nsub, prepare, compute, commit)
            return carry

        lax.fori_loop(0, nf, column_block, 0)

    @pl.when(c == pl.num_programs(0) - 1)
    def _():
        for slot, rows in enumerate(SLOT_ROWS):
            @pl.when(pend_ref[slot] == 1)
            def _():
                out_copy(0, rows, slot, 0).wait()


def _moe_up(ch_e, ch_start, ch_nsub, tail, xs, w_gate, b_gate, w_up, b_up):
    ns = TOK_WORDS // LANES
    p_alloc = xs.shape[0] // ns
    d = 2 * TOK_WORDS
    dff = w_gate.shape[2]
    tf = MOE_TF
    assert (dff // tf) % 2 == 0
    nch = ch_e.shape[0]
    bmap = lambda c, e, s, n, t: (e[c], 0, 0)
    return pl.pallas_call(
        _moe_up_kernel,
        grid_spec=pltpu.PrefetchScalarGridSpec(
            num_scalar_prefetch=4,
            grid=(nch,),
            in_specs=[
                pl.BlockSpec(memory_space=pl.ANY),
                pl.BlockSpec(memory_space=pl.ANY),
                pl.BlockSpec((1, 1, dff), bmap),
                pl.BlockSpec(memory_space=pl.ANY),
                pl.BlockSpec((1, 1, dff), bmap),
            ],
            out_specs=pl.BlockSpec(memory_space=pl.ANY),
            scratch_shapes=[
                pltpu.VMEM((MOE_TM * ns, LANES), jnp.uint32),
                pltpu.VMEM((MOE_TM, d), BF16),
                pltpu.VMEM((2, 2, d, tf), F32),
                pltpu.VMEM((d, 2 * tf), BF16),
                pltpu.VMEM((len(SLOT_ROWS), 4 * SUB_ROWS, tf), BF16),
                pltpu.VMEM((SUB_ROWS, dff), BF16),
                pltpu.SMEM((len(SLOT_ROWS),), jnp.int32),
                pltpu.SemaphoreType.DMA,
                pltpu.SemaphoreType.DMA((2,)),
                pltpu.SemaphoreType.DMA((len(SLOT_ROWS),)),
            ],
        ),
        out_shape=jax.ShapeDtypeStruct((p_alloc, dff), BF16),
        compiler_params=_cparams(("arbitrary",)),
        name="moe_up",
    )(ch_e, ch_start, ch_nsub, tail, xs, w_gate, b_gate.reshape(N_EXPERTS, 1, dff), w_up,
      b_up.reshape(N_EXPERTS, 1, dff))


def _moe_down_kernel(ce_ref, cs_ref, cn_ref, tail_ref, act_hbm, wd_hbm, bd_ref, ys_hbm,
                     x_ref, wf_ref, wdb_ref, yb_ref, zero_ref, pend_ref, sem, wsem, osem):
    c = pl.program_id(0)
    nsub = cn_ref[c]
    start = pl.multiple_of(cs_ref[c], SUB_ROWS)
    tn = wf_ref.shape[3]
    nf = bd_ref.shape[2] // tn
    ns = TOK_WORDS // LANES
    nw = tn // 2 // LANES
    chunk = [_chunk_dma(cs_ref, cn_ref, act_hbm, x_ref.at[i], 1, sem.at[i]) for i in range(2)]
    first_weights, next_weights = _weight_stream(ce_ref, cn_ref, nf, (wd_hbm,), tn, wf_ref, wdb_ref, wsem)

    def on_parity(ci, fn):
        for p in range(2):
            @pl.when(ci % 2 == p)
            def _():
                fn(p)

    def out_copy(r, rows, par):
        src = yb_ref.at[par, pl.ds(pl.multiple_of(r * ns, SUB_ROWS * ns), rows * ns)]
        dst = ys_hbm.at[pl.ds(pl.multiple_of((start + r) * ns, SUB_ROWS * ns), rows * ns)]
        return pltpu.make_async_copy(src, dst, osem.at[par])

    def drain(par):
        for kind, rows in enumerate(SLOT_ROWS[-3:]):
            def wait(j, carry):
                out_copy(0, rows, par).wait()
                return carry

            lax.fori_loop(0, pend_ref[par * 3 + kind], wait, 0)
            pend_ref[par * 3 + kind] = 0

    @pl.when(c == 0)
    def _():
        zero_ref[...] = jnp.zeros_like(zero_ref)
        _zero_fill_rows(zero_ref, ys_hbm, tail_ref[0], tail_ref[1], osem.at[0])
        for kind in range(6):
            pend_ref[kind] = 0
        chunk[0][0](c)
        first_weights()

    def run_chunk():
        for f in range(nf):
            next_weights(c, f)
            bd = bd_ref[0, :, f * tn:(f + 1) * tn]
            last = f == nf - 1
            if f == 0:
                drain(c % 2)

            def compute(r, rows, slot, f=f, bd=bd):
                y = jnp.dot(x_ref[c % 2, pl.ds(r, rows), :], wdb_ref[...], preferred_element_type=F32) + bd
                lo = pltpu.bitcast(y[:, :tn // 2].astype(BF16).astype(F32), jnp.uint32)
                hi = pltpu.bitcast(y[:, tn // 2:].astype(BF16).astype(F32), jnp.uint32)
                packed = (hi & jnp.uint32(0xFFFF0000)) | (lo >> 16)
                for s in range(nw):
                    yb_ref[c % 2, pl.ds(r * ns + f * nw + s, rows, stride=ns), :] = (
                        packed[:, s * LANES:(s + 1) * LANES])

            def commit(tiles, last=last):
                if last:
                    for r, rows, slot in tiles:
                        out_copy(r, rows, c % 2).start()
                        kind = (c % 2) * 3 + SLOT_ROWS[-3:].index(rows)
                        pend_ref[kind] = pend_ref[kind] + 1

            _tile_loop(nsub, lambda tiles: None, compute, commit)

    @pl.when(nsub > 0)
    def _():
        on_parity(c, lambda p: chunk[p][1](c))
        on_parity(c + 1, lambda p: chunk[p][0](c + 1))
        run_chunk()

    @pl.when(c == pl.num_programs(0) - 1)
    def _():
        drain(0)
        drain(1)


def _moe_down(ch_e, ch_start, ch_nsub, tail, act, w_down, b_down):
    p_alloc, dff = act.shape
    d = w_down.shape[2]
    ns = TOK_WORDS // LANES
    tn = MOE_TN
    assert (d // tn) % 2 == 0
    nch = ch_e.shape[0]
    return pl.pallas_call(
        _moe_down_kernel,
        grid_spec=pltpu.PrefetchScalarGridSpec(
            num_scalar_prefetch=4,
            grid=(nch,),
            in_specs=[
                pl.BlockSpec(memory_space=pl.ANY),
                pl.BlockSpec(memory_space=pl.ANY),
                pl.BlockSpec((1, 1, d), lambda c, e, s, n, t: (e[c], 0, 0)),
            ],
            out_specs=pl.BlockSpec(memory_space=pl.ANY),
            scratch_shapes=[
                pltpu.VMEM((2, MOE_TM, dff), BF16),
                pltpu.VMEM((2, 1, dff, tn), F32),
                pltpu.VMEM((dff, tn), BF16),
                pltpu.VMEM((2, MOE_TM * ns, LANES), jnp.uint32),
                pltpu.VMEM((SUB_ROWS * ns, LANES), jnp.uint32),
                pltpu.SMEM((6,), jnp.int32),
                pltpu.SemaphoreType.DMA((2,)),
                pltpu.SemaphoreType.DMA((2,)),
                pltpu.SemaphoreType.DMA((2,)),
            ],
        ),
        out_shape=jax.ShapeDtypeStruct((p_alloc * ns, LANES), jnp.uint32),
        compiler_params=pltpu.CompilerParams(dimension_semantics=("arbitrary",), vmem_limit_bytes=VMEM_LIMIT_MAX),
        name="moe_down",
    )(ch_e, ch_start, ch_nsub, tail, act, w_down, b_down.reshape(N_EXPERTS, 1, d))


def _combine_kernel(pos_hbm, ys_hbm, h_ref, w_ref, g_ref, o_ref, pos_smem, yb_ref, sem, psem):
    i = pl.program_id(0)
    tm, d = h_ref.shape
    n = TOP_K * tm
    ns = TOK_WORDS // LANES

    def pos_copy(tile):
        third = pl.multiple_of((tile % 3) * n, n)
        return pltpu.make_async_copy(pos_hbm.at[pl.ds(pl.multiple_of(tile * n, n), n)],
                                     pos_smem.at[pl.ds(third, n)], psem.at[tile % 3])

    def fetch(tile, slot):
        pos_copy(tile).wait()

        @pl.when(tile + 1 < pl.num_programs(0))
        def _():
            pos_copy(tile + 1).start()

        base = (tile % 3) * n

        def issue(jj, c):
            for u in range(ROW_DMA_UNROLL):
                j = jj * ROW_DMA_UNROLL + u
                src = ys_hbm.at[pl.ds(pl.multiple_of(pos_smem[base + j] * ns, ns), ns)]
                dst = yb_ref.at[slot, pl.ds(pl.multiple_of(j * ns, ns), ns)]
                pltpu.make_async_copy(src, dst, sem.at[slot]).start(priority=u % 2)
            return c

        lax.fori_loop(0, n // ROW_DMA_UNROLL, issue, 0)

    @pl.when(i == 0)
    def _():
        pos_copy(i).start()
        fetch(i, 0)

    @pl.when(i + 1 < pl.num_programs(0))
    def _():
        for p in range(2):
            @pl.when((i + 1) % 2 == p)
            def _():
                fetch(i + 1, p)

    def reduce(slot):
        pltpu.make_async_copy(ys_hbm.at[pl.ds(0, n * ns)], yb_ref.at[slot], sem.at[slot]).wait()
        wb = [jnp.broadcast_to(w_ref[:, k:k + 1], (tm, LANES)) for k in range(TOP_K)]
        ssq = jnp.zeros((tm, 1), F32)
        nw = MOE_TN // 2 // LANES
        for s in range(ns):
            c0 = (s // nw) * MOE_TN + (s % nw) * LANES
            lo_cols = slice(c0, c0 + LANES)
            hi_cols = slice(c0 + MOE_TN // 2, c0 + MOE_TN // 2 + LANES)
            acc_lo = h_ref[:, lo_cols]
            acc_hi = h_ref[:, hi_cols]
            for k in range(TOP_K):
                w = yb_ref[slot, pl.ds(k * tm * ns + s, tm, stride=ns), :]
                acc_lo = acc_lo + wb[k] * pltpu.bitcast(w << 16, F32)
                acc_hi = acc_hi + wb[k] * pltpu.bitcast(w & jnp.uint32(0xFFFF0000), F32)
            o_ref[:, lo_cols] = acc_lo
            o_ref[:, hi_cols] = acc_hi
            ssq = ssq + jnp.sum(acc_lo * acc_lo + acc_hi * acc_hi, axis=-1, keepdims=True)
        o_ref[...] = o_ref[...] * lax.rsqrt(ssq * (1.0 / d) + NORM_EPS) * g_ref[...]

    for p in range(2):
        @pl.when(i % 2 == p)
        def _():
            reduce(p)


def _combine(pos_tiles, ys, h1, wts_t, g_final, tm):
    rows, d = h1.shape
    ns = TOK_WORDS // LANES
    return pl.pallas_call(
        _combine_kernel,
        grid=(rows // tm,),
        in_specs=[
            pl.BlockSpec(memory_space=pl.ANY),
            pl.BlockSpec(memory_space=pl.ANY),
            pl.BlockSpec((tm, d), lambda i: (i, 0)),
            pl.BlockSpec((tm, TOP_K), lambda i: (i, 0)),
            pl.BlockSpec((1, d), lambda i: (0, 0)),
        ],
        out_specs=pl.BlockSpec((tm, d), lambda i: (i, 0)),
        out_shape=jax.ShapeDtypeStruct((rows, d), F32),
        scratch_shapes=[
            pltpu.SMEM((3 * TOP_K * tm,), jnp.int32),
            pltpu.VMEM((2, TOP_K * tm * ns, LANES), jnp.uint32),
            pltpu.SemaphoreType.DMA((2,)),
            pltpu.SemaphoreType.DMA((3,)),
        ],
        compiler_params=_cparams(("arbitrary",)),
        name="combine",
    )(pos_tiles, ys, h1, wts_t, g_final)


def _rope_tables(n_pos):
    half = HEAD_DIM // 2
    inv = 1.0 / (ROPE_THETA ** (jnp.arange(half, dtype=F32) / half))
    ang = jnp.arange(n_pos, dtype=F32)[:, None] * inv[None, :]
    cos = jnp.tile(jnp.cos(ang), (1, LANES // half))
    sin = jnp.tile(jnp.concatenate([-jnp.sin(ang), jnp.sin(ang)], axis=1), (1, LANES // HEAD_DIM))
    return cos, sin


def _block_diag(w):
    per = LRU_GROUP // LRU_BLOCK
    w4 = w.reshape(D_LRU // LRU_GROUP, per, LRU_BLOCK, LRU_BLOCK)
    eye = jnp.eye(per, dtype=w.dtype)
    bd = jnp.einsum("gpcd,pq->gpcqd", w4, eye)
    return bd.reshape(D_LRU // LRU_GROUP, LRU_GROUP, LRU_GROUP).astype(BF16)


def _tile_positions(pos, tm):
    rows = pos.shape[1]
    return pos.reshape(TOP_K, rows // tm, tm).transpose(1, 0, 2).reshape(-1)


def _chunk_schedule(counts, n_rows):
    aligned = ((counts + SUB_ROWS - 1) // SUB_ROWS) * SUB_ROWS
    offs = jnp.cumsum(aligned) - aligned
    n_ch = (aligned + MOE_TM - 1) // MOE_TM
    cum = jnp.cumsum(n_ch)
    total = cum[-1]
    nch_max = N_EXPERTS + (n_rows + N_EXPERTS * SUB_ROWS) // MOE_TM
    c = jnp.arange(nch_max, dtype=jnp.int32)
    cc = jnp.minimum(c, total - 1)
    e = jnp.sum(cc[:, None] >= cum[None, :], axis=1).astype(jnp.int32)
    is_e = e[:, None] == jnp.arange(N_EXPERTS, dtype=jnp.int32)[None, :]
    of_e = lambda v: jnp.sum(jnp.where(is_e, v[None, :], 0), axis=1)
    j = cc - (of_e(cum) - of_e(n_ch))
    start = of_e(offs) + j * MOE_TM
    nsub = jnp.where(c < total, jnp.minimum(MOE_TM, of_e(aligned) - j * MOE_TM) // SUB_ROWS, 0)
    tails = offs + (counts // SUB_ROWS) * SUB_ROWS
    used = jnp.sum(aligned)
    tail = jnp.stack([used, (n_rows + N_EXPERTS * SUB_ROWS - used) // SUB_ROWS])
    tails = jnp.concatenate([tails, tail])
    return (offs, e, start.astype(jnp.int32), nsub.astype(jnp.int32), tails.astype(jnp.int32),
            tail.astype(jnp.int32))


def kernel(x, meta_tokens, norm_mix, w_in, b_in, sinks, conv_w, conv_b, w_a, b_a, w_i, b_i, lru_lambda,
           g_attn_out, g_lru_out, w_out, b_out, norm_ffn, w_router, b_router, w_gate, b_gate, w_up, b_up,
           w_down, b_down, final_norm):
    batch, seq, d = x.shape
    rows = batch * seq
    x2d = x.reshape(rows, d)
    row = lambda v: v.reshape(1, -1)

    w_in_bf = w_in[0].astype(BF16)
    w_out_bf = w_out[0].astype(BF16)
    cos, sin = _rope_tables(N_META + seq)
    wa_bd, wi_bd = _block_diag(w_a[0]), _block_diag(w_i[0])
    lru_args = (conv_w[0], row(conv_b[0]), wa_bd, row(b_a[0]), wi_bd, row(b_i[0]), row(lru_lambda[0]),
                row(g_lru_out[0]))

    _, kvm, xgm = _inproj(meta_tokens, row(norm_mix[0]), w_in_bf, row(b_in[0]), cos[:N_META], sin[:N_META],
                          N_META, 1)
    _, h0, tail0 = _lru(xgm, *lru_args, jnp.zeros((1, D_LRU), F32), jnp.zeros((8, D_LRU), F32), 1, N_META,
                        N_META)

    tm = ROW_TILE
    q, kv, xg = _inproj(x2d, row(norm_mix[0]), w_in_bf, row(b_in[0]), cos[N_META:], sin[N_META:], tm,
                        seq // tm)
    kvm = jnp.pad(kvm, ((0, BLOCK - N_META), (0, 0)))
    attn_n = _attention(sinks[0], q, kv, kvm, row(g_attn_out[0]), batch, seq)
    lru_n, _, _ = _lru(xg, *lru_args, h0, tail0, batch, seq, LRU_TIME_TILE)

    tri = (jnp.arange(tm)[:, None] <= jnp.arange(tm)[None, :]).astype(BF16)
    h1, xp, ids, wts, rank, cnt = _outproj(
        attn_n, lru_n, w_out_bf, row(b_out[0]), x2d, row(norm_ffn[0]), w_router[0].T.astype(BF16),
        b_router[0].reshape(N_EXPERTS, 1), tri, tm)

    counts = cnt[:, 0].astype(jnp.int32)
    n_rows = rows * TOP_K
    offs, ch_e, ch_start, ch_nsub, tails, tail = _chunk_schedule(counts, n_rows)
    onehot = ids[..., None] == jnp.arange(N_EXPERTS, dtype=jnp.int32)
    pos = jnp.sum(jnp.where(onehot, offs, 0), axis=-1) + rank
    p_alloc = n_rows + N_EXPERTS * SUB_ROWS

    xs = _dispatch(tails, _tile_positions(pos, ROW_TILE), xp, p_alloc, ROW_TILE)
    act = _moe_up(ch_e, ch_start, ch_nsub, tail, xs, w_gate[0], b_gate[0], w_up[0], b_up[0])
    ys = _moe_down(ch_e, ch_start, ch_nsub, tail, act, w_down[0], b_down[0])
    out = _combine(_tile_positions(pos, COMBINE_TILE), ys, h1, wts.T, row(final_norm), COMBINE_TILE)
    return out.reshape(batch, seq, d)
```

```python
import jax
import jax.numpy as jnp
from jax import lax
from jax.experimental import pallas as pl
from jax.experimental.pallas import tpu as pltpu

F32 = jnp.float32
BF16 = jnp.bfloat16

N_META = 16
HEAD_DIM = 64
N_Q_HEADS = 16
N_KV_HEADS = 4
GROUP = N_Q_HEADS // N_KV_HEADS
D_ATTN = N_Q_HEADS * HEAD_DIM
D_KV = N_KV_HEADS * HEAD_DIM
KV_COLS = 4 * D_KV
BLOCK = 128
ATTN_QBLOCKS = 8
ROPE_THETA = 10000.0
D_LRU = 1024
LRU_BLOCK = 64
LRU_GROUP = 256
CONV_WIDTH = 4
LRU_C = 8.0
N_EXPERTS = 32
TOP_K = 4
SWIGLU_LIMIT = 7.0
SWIGLU_ALPHA = 1.702
NORM_EPS = 1e-5
NEG_INF = -1e30

LANES = 128
D_MODEL = 2048
TOK_WORDS = D_MODEL // 2
SUB_ROWS = 128
MOE_TM = 2304
MOE_TF = 512
MOE_TN = 1024
ROW_TILE = 512
INPROJ_COLS = 512
LRU_TIME_TILE = 128
COMBINE_TILE = 256
ROW_DMA_UNROLL = 16
SLOT_ROWS = (4 * SUB_ROWS,) * 5 + (2 * SUB_ROWS, SUB_ROWS)
VMEM_LIMIT = 56 * 1024 * 1024
VMEM_LIMIT_MAX = 62 * 1024 * 1024


def _cparams(sem):
    return pltpu.CompilerParams(dimension_semantics=sem, vmem_limit_bytes=VMEM_LIMIT)


def _mult(x, m):
    return x if isinstance(x, int) else pl.multiple_of(x, m)


def _rms(x, g):
    return x * lax.rsqrt(jnp.mean(x * x, axis=-1, keepdims=True) + NORM_EPS) * g


def _inproj_kernel(x_ref, g_ref, w_ref, b_ref, cos_ref, sin_ref, q_ref, kv_ref, xg_ref):
    xn = _rms(x_ref[...], g_ref[...]).astype(BF16)
    cos = cos_ref[...]
    sin = sin_ref[...]
    lane = lax.broadcasted_iota(jnp.int32, (1, LANES), 1)
    first_half = (lane % HEAD_DIM) < (HEAD_DIM // 2)

    def rope(z):
        partner = jnp.where(first_half, pltpu.roll(z, LANES - HEAD_DIM // 2, 1),
                            pltpu.roll(z, HEAD_DIM // 2, 1))
        return z * cos + partner * sin

    cw = INPROJ_COLS
    for c in range(w_ref.shape[1] // cw):
        z = jnp.dot(xn, w_ref[:, c * cw:(c + 1) * cw], preferred_element_type=F32)
        z = z + b_ref[:, c * cw:(c + 1) * cw]
        if c < 2:
            for j in range(cw // LANES):
                zz = rope(z[:, j * LANES:(j + 1) * LANES]) * (HEAD_DIM ** -0.5)
                q_ref[:, c * cw + j * LANES:c * cw + (j + 1) * LANES] = zz.astype(BF16)
        elif c == 2:
            low = lane < HEAD_DIM
            for j in range(2 * D_KV // LANES):
                zz = z[:, j * LANES:(j + 1) * LANES]
                zz = rope(zz) if j < D_KV // LANES else zz
                sw = pltpu.roll(zz, HEAD_DIM, 1)
                kv_ref[:, 2 * j * LANES:(2 * j + 1) * LANES] = jnp.where(low, zz, sw).astype(BF16)
                kv_ref[:, (2 * j + 1) * LANES:(2 * j + 2) * LANES] = jnp.where(low, sw, zz).astype(BF16)
        else:
            xg_ref[:, (c - 3) * cw:(c - 2) * cw] = z


def _inproj(x2d, g, w_bf, b, cos, sin, tm, pos_blocks):
    rows, d = x2d.shape
    dz = w_bf.shape[1]
    return pl.pallas_call(
        _inproj_kernel,
        grid=(rows // tm,),
        in_specs=[
            pl.BlockSpec((tm, d), lambda m: (m, 0)),
            pl.BlockSpec((1, d), lambda m: (0, 0)),
            pl.BlockSpec((d, dz), lambda m: (0, 0)),
            pl.BlockSpec((1, dz), lambda m: (0, 0)),
            pl.BlockSpec((tm, LANES), lambda m: (m % pos_blocks, 0)),
            pl.BlockSpec((tm, LANES), lambda m: (m % pos_blocks, 0)),
        ],
        out_specs=[
            pl.BlockSpec((tm, D_ATTN), lambda m: (m, 0)),
            pl.BlockSpec((tm, KV_COLS), lambda m: (m, 0)),
            pl.BlockSpec((tm, 2 * D_LRU), lambda m: (m, 0)),
        ],
        out_shape=[
            jax.ShapeDtypeStruct((rows, D_ATTN), BF16),
            jax.ShapeDtypeStruct((rows, KV_COLS), BF16),
            jax.ShapeDtypeStruct((rows, 2 * D_LRU), F32),
        ],
        compiler_params=_cparams(("arbitrary",)),
        name="inproj",
    )(x2d, g, w_bf, b, cos, sin)


def _attn_block(sink_ref, q_ref, kvc_ref, kvp_ref, kvm_ref, g_ref, o_ref, n, rs):
    row = lax.broadcasted_iota(jnp.int32, (GROUP * BLOCK, BLOCK), 0) % BLOCK
    col = lax.broadcasted_iota(jnp.int32, (GROUP * BLOCK, BLOCK), 1)
    in_cur = col <= row
    in_prev = jnp.logical_and(col > row, n > 0)
    is_meta = col < N_META
    low = lax.broadcasted_iota(jnp.int32, (BLOCK, LANES), 1) < HEAD_DIM
    nt = (((1,), (1,)), ((), ()))
    zero = jnp.zeros((), BF16)
    outs = []
    for h in range(N_KV_HEADS):
        ks = slice(h * LANES, (h + 1) * LANES)
        vs = slice(N_KV_HEADS * LANES + h * LANES, N_KV_HEADS * LANES + (h + 1) * LANES)
        parts = []
        for j in range(GROUP // 2):
            qg = q_ref[rs, (h * GROUP // 2 + j) * LANES:(h * GROUP // 2 + j + 1) * LANES]
            parts += [jnp.where(low, qg, zero), jnp.where(low, zero, qg)]
        qs = jnp.concatenate(parts, axis=0)
        s_c = lax.dot_general(qs, kvc_ref[:, ks], nt, preferred_element_type=F32)
        s_p = lax.dot_general(qs, kvp_ref[:, ks], nt, preferred_element_type=F32)
        s_m = lax.dot_general(qs, kvm_ref[:, ks], nt, preferred_element_type=F32)
        s_b = jnp.where(in_prev, s_p, jnp.where(in_cur, s_c, NEG_INF))
        s_m = jnp.where(is_meta, s_m, NEG_INF)
        sink = jnp.concatenate(
            [jnp.full((BLOCK, 1), sink_ref[h * GROUP + g], F32) for g in range(GROUP)], axis=0)
        m = jnp.maximum(jnp.max(jnp.maximum(s_b, s_m), axis=-1, keepdims=True), sink)
        p_b = jnp.exp(s_b - m)
        p_m = jnp.exp(s_m - m)
        den = jnp.sum(p_b + p_m, axis=-1, keepdims=True) + jnp.exp(sink - m)
        r = (jnp.dot(jnp.where(in_cur, p_b, 0.0).astype(BF16), kvc_ref[:, vs], preferred_element_type=F32)
             + jnp.dot(jnp.where(in_cur, 0.0, p_b).astype(BF16), kvp_ref[:, vs], preferred_element_type=F32)
             + jnp.dot(p_m.astype(BF16), kvm_ref[:, vs], preferred_element_type=F32))
        r = r / den
        for j in range(GROUP // 2):
            outs.append(jnp.where(low, r[2 * j * BLOCK:(2 * j + 1) * BLOCK], r[(2 * j + 1) * BLOCK:(2 * j + 2) * BLOCK]))
    o_all = jnp.concatenate(outs, axis=1)
    o_ref[rs, :] = _rms(o_all, g_ref[...]).astype(BF16)


def _attn_kernel(sink_ref, q_ref, *rest):
    kv_refs, (kvm_ref, g_ref, o_ref) = rest[:ATTN_QBLOCKS + 1], rest[ATTN_QBLOCKS + 1:]
    n0 = pl.program_id(1) * ATTN_QBLOCKS
    for sub in range(ATTN_QBLOCKS):
        _attn_block(sink_ref, q_ref, kv_refs[sub + 1], kv_refs[sub], kvm_ref, g_ref, o_ref, n0 + sub,
                    slice(sub * BLOCK, (sub + 1) * BLOCK))


def _attention(sinks, q, kv, kvm, g_attn, batch, seq):
    nb = seq // BLOCK
    nq = ATTN_QBLOCKS
    kv_spec = lambda i: pl.BlockSpec((BLOCK, KV_COLS), lambda b, n, s: (b * nb + jnp.maximum(nq * n + i - 1, 0), 0))
    return pl.pallas_call(
        _attn_kernel,
        grid_spec=pltpu.PrefetchScalarGridSpec(
            num_scalar_prefetch=1,
            grid=(batch, nb // nq),
            in_specs=[
                pl.BlockSpec((nq * BLOCK, D_ATTN), lambda b, n, s: (b * (nb // nq) + n, 0)),
                *[kv_spec(i) for i in range(nq + 1)],
                pl.BlockSpec((BLOCK, KV_COLS), lambda b, n, s: (0, 0)),
                pl.BlockSpec((1, D_ATTN), lambda b, n, s: (0, 0)),
            ],
            out_specs=pl.BlockSpec((nq * BLOCK, D_ATTN), lambda b, n, s: (b * (nb // nq) + n, 0)),
        ),
        out_shape=jax.ShapeDtypeStruct((batch * seq, D_ATTN), BF16),
        compiler_params=_cparams(("arbitrary", "arbitrary")),
        name="attention",
    )(sinks, q, *([kv] * (ATTN_QBLOCKS + 1)), kvm, g_attn)


def _one_minus_sq(a, log_a):
    y = 2.0 * log_a
    p = 1.0 + y * (1.0 / 4.0)
    for k in (3, 2):
        p = 1.0 + y * p * (1.0 / k)
    return jnp.where(y > -1.0 / 64.0, -(y * p), 1.0 - a * a)


def _lru_kernel(xg_ref, cw_ref, cb_ref, wa_ref, ba_ref, wi_ref, bi_ref, lam_ref, g_ref, h0_ref, tail0_ref,
                o_ref, hout_ref, tailout_ref, ext_ref, tail_ref, a_ref, b_ref, h_ref):
    tt = pl.program_id(0)
    batch, rows = xg_ref.shape[0], xg_ref.shape[1]

    @pl.when(tt == 0)
    def _():
        for g in range(D_LRU // LANES):
            h_ref[g] = jnp.broadcast_to(h0_ref[:, g * LANES:(g + 1) * LANES], (batch, LANES))
        for s in range(batch):
            tail_ref[s] = tail0_ref[...]

    sp = jax.nn.softplus(-lam_ref[...])

    def gates(s, carry):
        ext_ref[0:8, :] = tail_ref[s]
        ext_ref[8:, :] = xg_ref[s, :, :D_LRU]
        tail_ref[s] = ext_ref[rows:rows + 8, :]
        xc = cb_ref[...] + sum(cw_ref[j:j + 1, :] * ext_ref[5 + j:5 + j + rows, :] for j in range(CONV_WIDTH))
        xcb = xc.astype(BF16)
        for c in range(D_LRU // LRU_GROUP):
            cs = slice(c * LRU_GROUP, (c + 1) * LRU_GROUP)
            r = jax.nn.sigmoid(jnp.dot(xcb[:, cs], wa_ref[c], preferred_element_type=F32) + ba_ref[:, cs])
            i = jax.nn.sigmoid(jnp.dot(xcb[:, cs], wi_ref[c], preferred_element_type=F32) + bi_ref[:, cs])
            log_a = -LRU_C * r * sp[:, cs]
            a = jnp.exp(log_a)
            b = jnp.sqrt(_one_minus_sq(a, log_a)) * i * xc[:, cs]
            for g in range(LRU_GROUP // LANES):
                lanes = slice(g * LANES, (g + 1) * LANES)
                a_ref[c * (LRU_GROUP // LANES) + g, pl.ds(s, rows, stride=batch), :] = a[:, lanes]
                b_ref[c * (LRU_GROUP // LANES) + g, pl.ds(s, rows, stride=batch), :] = b[:, lanes]
        return carry

    lax.fori_loop(0, batch, gates, 0)

    def step(t, h):
        r0 = pl.multiple_of(t * batch, batch)
        new = []
        for g in range(D_LRU // LANES):
            hg = a_ref[g, pl.ds(r0, batch), :] * h[g] + b_ref[g, pl.ds(r0, batch), :]
            b_ref[g, pl.ds(r0, batch), :] = hg
            new.append(hg)
        return tuple(new)

    h_last = lax.fori_loop(0, rows, step, tuple(h_ref[g] for g in range(D_LRU // LANES)))
    for g in range(D_LRU // LANES):
        h_ref[g] = h_last[g]
    hout_ref[...] = jnp.concatenate([h_last[g][0:1] for g in range(D_LRU // LANES)], axis=1)
    tailout_ref[...] = tail_ref[0]

    def finish(s, carry):
        hs = jnp.concatenate([b_ref[g, pl.ds(s, rows, stride=batch), :] for g in range(D_LRU // LANES)], axis=1)
        y = hs * jax.nn.gelu(xg_ref[s, :, D_LRU:])
        o_ref[s] = _rms(y, g_ref[...]).astype(BF16)
        return carry

    lax.fori_loop(0, batch, finish, 0)


def _lru(xg, conv_w, conv_b, wa_bd, b_a, wi_bd, b_i, lam, g_lru, h0, tail0, batch, seq, tt):
    ntt = seq // tt
    ng = D_LRU // LANES
    vec = pl.BlockSpec((1, D_LRU), lambda t: (0, 0))
    wspec = pl.BlockSpec((D_LRU // LRU_GROUP, LRU_GROUP, LRU_GROUP), lambda t: (0, 0, 0))
    out, h_last, tail = pl.pallas_call(
        _lru_kernel,
        grid=(ntt,),
        in_specs=[
            pl.BlockSpec((batch, tt, 2 * D_LRU), lambda t: (0, t, 0)),
            pl.BlockSpec((CONV_WIDTH, D_LRU), lambda t: (0, 0)),
            vec, wspec, vec, wspec, vec, vec, vec, vec,
            pl.BlockSpec((8, D_LRU), lambda t: (0, 0)),
        ],
        out_specs=[
            pl.BlockSpec((batch, tt, D_LRU), lambda t: (0, t, 0)),
            pl.BlockSpec((1, D_LRU), lambda t: (0, 0)),
            pl.BlockSpec((8, D_LRU), lambda t: (0, 0)),
        ],
        out_shape=[
            jax.ShapeDtypeStruct((batch, seq, D_LRU), BF16),
            jax.ShapeDtypeStruct((1, D_LRU), F32),
            jax.ShapeDtypeStruct((8, D_LRU), F32),
        ],
        scratch_shapes=[
            pltpu.VMEM((8 + tt, D_LRU), F32),
            pltpu.VMEM((batch, 8, D_LRU), F32),
            pltpu.VMEM((ng, tt * batch, LANES), F32),
            pltpu.VMEM((ng, tt * batch, LANES), F32),
            pltpu.VMEM((ng, batch, LANES), F32),
        ],
        compiler_params=_cparams(("arbitrary",)),
        name="rglru",
    )(xg.reshape(batch, seq, 2 * D_LRU), conv_w, conv_b, wa_bd, b_a, wi_bd, b_i, lam, g_lru, h0, tail0)
    return out.reshape(batch * seq, D_LRU), h_last, tail


def _outproj_kernel(a_ref, l_ref, wo_ref, bo_ref, x_ref, gf_ref, wr_ref, br_ref, tri_ref,
                    h_ref, xp_ref, ids_ref, wts_ref, rank_ref, cnt_ref, carry_ref):
    m = pl.program_id(0)
    tm = a_ref.shape[0]

    @pl.when(m == 0)
    def _():
        carry_ref[...] = jnp.zeros_like(carry_ref)

    mix = jnp.concatenate([a_ref[...], l_ref[...]], axis=1)
    h = jnp.dot(mix, wo_ref[...], preferred_element_type=F32) + bo_ref[...] + x_ref[...]
    h_ref[...] = h
    xn = _rms(h, gf_ref[...]).astype(BF16)
    half = xn.shape[1] // 2
    lo = pltpu.bitcast(xn[:, :half].astype(F32), jnp.uint32)
    hi = pltpu.bitcast(xn[:, half:].astype(F32), jnp.uint32)
    packed = (hi & jnp.uint32(0xFFFF0000)) | (lo >> 16)
    ns = half // LANES
    for s in range(ns):
        xp_ref[pl.ds(s, tm, stride=ns), :] = packed[:, s * LANES:(s + 1) * LANES]

    logits = lax.dot_general(wr_ref[...], xn, (((1,), (1,)), ((), ())), preferred_element_type=F32)
    logits = logits + br_ref[...]
    eidx = lax.broadcasted_iota(jnp.int32, (N_EXPERTS, tm), 0)
    work = logits
    vals, sels = [], []
    for k in range(TOP_K):
        v = jnp.max(work, axis=0, keepdims=True)
        idx = jnp.min(jnp.where(work == v, eidx, N_EXPERTS), axis=0, keepdims=True)
        sel = eidx == idx
        ids_ref[k:k + 1, :] = idx
        vals.append(v)
        sels.append(sel)
        work = jnp.where(sel, -jnp.inf, work)
    es = [jnp.exp(v - vals[0]) for v in vals]
    den = es[0] + es[1] + es[2] + es[3]
    for k in range(TOP_K):
        wts_ref[k:k + 1, :] = es[k] / den
    cnt = sum(s.astype(F32) for s in sels)
    incl = jnp.dot(cnt.astype(BF16), tri_ref[...], preferred_element_type=F32)
    before = incl - cnt + carry_ref[:, 0:1]
    for k in range(TOP_K):
        rk = jnp.sum(jnp.where(sels[k], before, 0.0), axis=0, keepdims=True)
        rank_ref[k:k + 1, :] = rk.astype(jnp.int32)
    carry_ref[...] = carry_ref[...] + incl[:, tm - 1:tm]
    cnt_ref[...] = carry_ref[...]


def _outproj(attn_n, lru_n, wo_bf, b_out, x2d, g_ffn, wr_t, br, tri, tm):
    rows, d = x2d.shape
    const = lambda shape: pl.BlockSpec(shape, lambda m: tuple(0 for _ in shape))
    return pl.pallas_call(
        _outproj_kernel,
        grid=(rows // tm,),
        in_specs=[
            pl.BlockSpec((tm, D_ATTN), lambda m: (m, 0)),
            pl.BlockSpec((tm, D_LRU), lambda m: (m, 0)),
            const((D_ATTN + D_LRU, d)), const((1, d)),
            pl.BlockSpec((tm, d), lambda m: (m, 0)),
            const((1, d)), const((N_EXPERTS, d)), const((N_EXPERTS, 1)), const((tm, tm)),
        ],
        out_specs=[
            pl.BlockSpec((tm, d), lambda m: (m, 0)),
            pl.BlockSpec((tm * (d // 2 // LANES), LANES), lambda m: (m, 0)),
            pl.BlockSpec((TOP_K, tm), lambda m: (0, m)),
            pl.BlockSpec((TOP_K, tm), lambda m: (0, m)),
            pl.BlockSpec((TOP_K, tm), lambda m: (0, m)),
            const((N_EXPERTS, LANES)),
        ],
        out_shape=[
            jax.ShapeDtypeStruct((rows, d), F32),
            jax.ShapeDtypeStruct((rows * (d // 2 // LANES), LANES), jnp.uint32),
            jax.ShapeDtypeStruct((TOP_K, rows), jnp.int32),
            jax.ShapeDtypeStruct((TOP_K, rows), F32),
            jax.ShapeDtypeStruct((TOP_K, rows), jnp.int32),
            jax.ShapeDtypeStruct((N_EXPERTS, LANES), F32),
        ],
        scratch_shapes=[pltpu.VMEM((N_EXPERTS, LANES), F32)],
        compiler_params=_cparams(("arbitrary",)),
        name="outproj_router",
    )(attn_n, lru_n, wo_bf, b_out, x2d, g_ffn, wr_t, br, tri)


def _zero_fill_rows(zero_ref, dst_hbm, first_row, n_blocks, sem):
    blk = zero_ref.shape[0]
    per = blk // SUB_ROWS

    def copy(j):
        row = pl.multiple_of((first_row + j * SUB_ROWS) * per, blk)
        return pltpu.make_async_copy(zero_ref, dst_hbm.at[pl.ds(row, blk)], sem)

    def start(j, c):
        copy(j).start()
        return c

    def wait(j, c):
        copy(j).wait()
        return c

    lax.fori_loop(0, n_blocks, start, 0)
    lax.fori_loop(0, n_blocks, wait, 0)


def _dispatch_kernel(tails_ref, pos_hbm, xp_ref, xs_hbm, pos_smem, zero_ref, sem, psem):
    i = pl.program_id(0)
    ns = TOK_WORDS // LANES
    tm = xp_ref.shape[0] // ns
    n = TOP_K * tm

    def pos_copy(step):
        half = pl.multiple_of((step % 2) * n, n)
        return pltpu.make_async_copy(pos_hbm.at[pl.ds(pl.multiple_of(step * n, n), n)],
                                     pos_smem.at[pl.ds(half, n)], psem.at[step % 2])

    @pl.when(i == 0)
    def _():
        pos_copy(i).start()
        zero_ref[...] = jnp.zeros_like(zero_ref)
        for e in range(N_EXPERTS):
            _zero_fill_rows(zero_ref, xs_hbm, tails_ref[e], 1, sem)
        _zero_fill_rows(zero_ref, xs_hbm, tails_ref[N_EXPERTS], tails_ref[N_EXPERTS + 1], sem)

    pos_copy(i).wait()

    @pl.when(i + 1 < pl.num_programs(0))
    def _():
        pos_copy(i + 1).start()

    base = (i % 2) * n
    for k in range(TOP_K):
        def issue(tt, c):
            for u in range(ROW_DMA_UNROLL):
                t = tt * ROW_DMA_UNROLL + u
                src = xp_ref.at[pl.ds(pl.multiple_of(t * ns, ns), ns)]
                dst = xs_hbm.at[pl.ds(pl.multiple_of(pos_smem[base + k * tm + t] * ns, ns), ns)]
                pltpu.make_async_copy(src, dst, sem).start(priority=u % 2)
            return c

        lax.fori_loop(0, tm // ROW_DMA_UNROLL, issue, 0)
    for _ in range(TOP_K):
        pltpu.make_async_copy(xp_ref, xs_hbm.at[pl.ds(0, tm * ns)], sem).wait()


def _dispatch(tails, pos_tiles, xp, p_alloc, tm):
    ns = TOK_WORDS // LANES
    rows = xp.shape[0] // ns
    return pl.pallas_call(
        _dispatch_kernel,
        grid_spec=pltpu.PrefetchScalarGridSpec(
            num_scalar_prefetch=1,
            grid=(rows // tm,),
            in_specs=[
                pl.BlockSpec(memory_space=pl.ANY),
                pl.BlockSpec((tm * ns, LANES), lambda i, s: (i, 0)),
            ],
            out_specs=pl.BlockSpec(memory_space=pl.ANY),
            scratch_shapes=[
                pltpu.SMEM((2 * TOP_K * tm,), jnp.int32),
                pltpu.VMEM((SUB_ROWS * ns, LANES), jnp.uint32),
                pltpu.SemaphoreType.DMA,
                pltpu.SemaphoreType.DMA((2,)),
            ],
        ),
        out_shape=jax.ShapeDtypeStruct((p_alloc * ns, LANES), jnp.uint32),
        compiler_params=_cparams(("arbitrary",)),
        name="dispatch",
    )(tails, pos_tiles, xp)


def _chunk_dma(cs_ref, cn_ref, src_hbm, dst_ref, per, sem):
    blk = SUB_ROWS * per
    nch = cs_ref.shape[0]

    def copy(ci, j):
        src = src_hbm.at[pl.ds(pl.multiple_of((cs_ref[ci] + j * SUB_ROWS) * per, blk), blk)]
        return pltpu.make_async_copy(src, dst_ref.at[pl.ds(pl.multiple_of(j * blk, blk), blk)], sem)

    def run(ci, wait):
        cc = jnp.minimum(ci, nch - 1)
        n = jnp.where(ci < nch, cn_ref[cc], 0)

        def body(j, c):
            if wait:
                copy(cc, j).wait()
            else:
                copy(cc, j).start()
            return c

        lax.fori_loop(0, n, body, 0)

    return (lambda ci: run(ci, False)), (lambda ci: run(ci, True))


def _tile_loop(nsub, prepare, compute, commit):
    def run(tiles):
        prepare(tiles)
        for t in tiles:
            compute(*t)
        commit(tiles)

    n8 = nsub // 8
    rem = nsub % 8
    base = n8 * 8 * SUB_ROWS
    r2 = base + (rem // 4) * 4 * SUB_ROWS
    r1 = r2 + ((rem % 4) // 2) * 2 * SUB_ROWS

    @pl.when(rem % 4 >= 2)
    def _():
        run([(pl.multiple_of(r2, SUB_ROWS), 2 * SUB_ROWS, 5)])

    @pl.when(rem % 2 == 1)
    def _():
        run([(pl.multiple_of(r1, SUB_ROWS), SUB_ROWS, 6)])

    @pl.when(rem >= 4)
    def _():
        run([(pl.multiple_of(base, 4 * SUB_ROWS), 4 * SUB_ROWS, 4)])

    def pair(j, c):
        r = pl.multiple_of(j * 8 * SUB_ROWS, 8 * SUB_ROWS)
        slot = (j % 2) * 2
        run([(r, 4 * SUB_ROWS, slot), (r + 4 * SUB_ROWS, 4 * SUB_ROWS, slot + 1)])
        return c

    lax.fori_loop(0, n8, pair, 0)


def _weight_stream(ce_ref, cn_ref, nf, w_hbms, tn, wf_ref, wsem):
    def copies(cc, f):
        cols = pl.ds(_mult(f * tn, tn), tn)
        return [pltpu.make_async_copy(w.at[ce_ref[cc], :, cols], wf_ref.at[f % 2, i], wsem.at[f % 2])
                for i, w in enumerate(w_hbms)]

    def first():
        for cp in copies(0, 0):
            cp.start()

    def advance(c, f):
        q = c * nf + f + 1
        cc = jnp.minimum(q // nf, cn_ref.shape[0] - 1)

        @pl.when(jnp.logical_and(q // nf < cn_ref.shape[0], cn_ref[cc] > 0))
        def _():
            for cp in copies(cc, q % nf):
                cp.start()

        for cp in copies(c, f):
            cp.wait()

    return first, advance


def _moe_up_kernel(ce_ref, cs_ref, cn_ref, tail_ref, xs_hbm, wg_hbm, bg_ref, wu_hbm, bu_ref, act_hbm,
                   land_ref, xb_ref, wf_ref, ab_ref, zero_ref, pend_ref, sem, wsem, osem):
    c = pl.program_id(0)
    nsub = cn_ref[c]
    start = pl.multiple_of(cs_ref[c], SUB_ROWS)
    tf = wf_ref.shape[3]
    nf = bg_ref.shape[2] // tf
    ns = TOK_WORDS // LANES
    start_chunk, wait_chunk = _chunk_dma(cs_ref, cn_ref, xs_hbm, land_ref, ns, sem)
    first_weights, next_weights = _weight_stream(ce_ref, cn_ref, nf, (wg_hbm, wu_hbm), tf, wf_ref, wsem)

    def out_copy(r, rows, slot, f):
        return pltpu.make_async_copy(
            ab_ref.at[slot, pl.ds(0, rows)],
            act_hbm.at[pl.ds(pl.multiple_of(start + r, SUB_ROWS), rows), pl.ds(_mult(f * tf, tf), tf)],
            osem.at[slot])

    @pl.when(c == 0)
    def _():
        zero_ref[...] = jnp.zeros_like(zero_ref)
        _zero_fill_rows(zero_ref, act_hbm, tail_ref[0], tail_ref[1], osem.at[0])
        for slot in range(len(SLOT_ROWS)):
            pend_ref[slot] = 0
        start_chunk(c)
        first_weights()

    @pl.when(nsub > 0)
    def _():
        wait_chunk(c)

        def unpack(j, carry):
            r = pl.multiple_of(j * SUB_ROWS, SUB_ROWS)
            for s in range(ns):
                w = land_ref[pl.ds(r * ns + s, SUB_ROWS, stride=ns), :]
                xb_ref[pl.ds(r, SUB_ROWS), s * LANES:(s + 1) * LANES] = pltpu.bitcast(w << 16, F32).astype(BF16)
                xb_ref[pl.ds(r, SUB_ROWS), TOK_WORDS + s * LANES:TOK_WORDS + (s + 1) * LANES] = (
                    pltpu.bitcast(w & jnp.uint32(0xFFFF0000), F32).astype(BF16))
            return carry

        lax.fori_loop(0, nsub, unpack, 0)
        start_chunk(c + 1)

        def column_block(f, carry):
            next_weights(c, f)
            cols = pl.ds(pl.multiple_of(f * tf, tf), tf)
            bg = bg_ref[0, :, cols]
            bu = bu_ref[0, :, cols]

            def prepare(tiles):
                for _, rows, slot in tiles:
                    @pl.when(pend_ref[slot] == 1)
                    def _():
                        out_copy(0, rows, slot, f).wait()

            def compute(r, rows, slot):
                x = xb_ref[pl.ds(r, rows), :]
                g = jnp.dot(x, wf_ref[f % 2, 0].astype(BF16), preferred_element_type=F32) + bg
                u = jnp.dot(x, wf_ref[f % 2, 1].astype(BF16), preferred_element_type=F32) + bu
                g = jnp.minimum(g, SWIGLU_LIMIT)
                u = jnp.clip(u, -SWIGLU_LIMIT, SWIGLU_LIMIT)
                ab_ref[slot, pl.ds(0, rows), :] = (g * jax.nn.sigmoid(SWIGLU_ALPHA * g) * (u + 1.0)).astype(BF16)

            def commit(tiles):
                for r, rows, slot in tiles:
                    out_copy(r, rows, slot, f).start()
                    pend_ref[slot] = 1

            _tile_loop(nsub, prepare, compute, commit)
            return carry

        lax.fori_loop(0, nf, column_block, 0)

    @pl.when(c == pl.num_programs(0) - 1)
    def _():
        for slot, rows in enumerate(SLOT_ROWS):
            @pl.when(pend_ref[slot] == 1)
            def _():
                out_copy(0, rows, slot, 0).wait()


def _moe_up(ch_e, ch_start, ch_nsub, tail, xs, w_gate, b_gate, w_up, b_up):
    ns = TOK_WORDS // LANES
    p_alloc = xs.shape[0] // ns
    d = 2 * TOK_WORDS
    dff = w_gate.shape[2]
    tf = MOE_TF
    assert (dff // tf) % 2 == 0
    nch = ch_e.shape[0]
    bmap = lambda c, e, s, n, t: (e[c], 0, 0)
    return pl.pallas_call(
        _moe_up_kernel,
        grid_spec=pltpu.PrefetchScalarGridSpec(
            num_scalar_prefetch=4,
            grid=(nch,),
            in_specs=[
                pl.BlockSpec(memory_space=pl.ANY),
                pl.BlockSpec(memory_space=pl.ANY),
                pl.BlockSpec((1, 1, dff), bmap),
                pl.BlockSpec(memory_space=pl.ANY),
                pl.BlockSpec((1, 1, dff), bmap),
            ],
            out_specs=pl.BlockSpec(memory_space=pl.ANY),
            scratch_shapes=[
                pltpu.VMEM((MOE_TM * ns, LANES), jnp.uint32),
                pltpu.VMEM((MOE_TM, d), BF16),
                pltpu.VMEM((2, 2, d, tf), F32),
                pltpu.VMEM((len(SLOT_ROWS), 4 * SUB_ROWS, tf), BF16),
                pltpu.VMEM((SUB_ROWS, dff), BF16),
                pltpu.SMEM((len(SLOT_ROWS),), jnp.int32),
                pltpu.SemaphoreType.DMA,
                pltpu.SemaphoreType.DMA((2,)),
                pltpu.SemaphoreType.DMA((len(SLOT_ROWS),)),
            ],
        ),
        out_shape=jax.ShapeDtypeStruct((p_alloc, dff), BF16),
        compiler_params=_cparams(("arbitrary",)),
        name="moe_up",
    )(ch_e, ch_start, ch_nsub, tail, xs, w_gate, b_gate.reshape(N_EXPERTS, 1, dff), w_up,
      b_up.reshape(N_EXPERTS, 1, dff))


def _moe_down_kernel(ce_ref, cs_ref, cn_ref, tail_ref, act_hbm, wd_hbm, bd_ref, ys_hbm,
                     x_ref, wf_ref, yb_ref, zero_ref, pend_ref, sem, wsem, osem):
    c = pl.program_id(0)
    nsub = cn_ref[c]
    start = pl.multiple_of(cs_ref[c], SUB_ROWS)
    tn = wf_ref.shape[3]
    nf = bd_ref.shape[2] // tn
    ns = TOK_WORDS // LANES
    nw = tn // 2 // LANES
    chunk = [_chunk_dma(cs_ref, cn_ref, act_hbm, x_ref.at[i], 1, sem.at[i]) for i in range(2)]
    first_weights, next_weights = _weight_stream(ce_ref, cn_ref, nf, (wd_hbm,), tn, wf_ref, wsem)

    def on_parity(ci, fn):
        for p in range(2):
            @pl.when(ci % 2 == p)
            def _():
                fn(p)

    def out_copy(r, rows, par):
        src = yb_ref.at[par, pl.ds(pl.multiple_of(r * ns, SUB_ROWS * ns), rows * ns)]
        dst = ys_hbm.at[pl.ds(pl.multiple_of((start + r) * ns, SUB_ROWS * ns), rows * ns)]
        return pltpu.make_async_copy(src, dst, osem.at[par])

    def drain(par):
        for kind, rows in enumerate(SLOT_ROWS[-3:]):
            def wait(j, carry):
                out_copy(0, rows, par).wait()
                return carry

            lax.fori_loop(0, pend_ref[par * 3 + kind], wait, 0)
            pend_ref[par * 3 + kind] = 0

    @pl.when(c == 0)
    def _():
        zero_ref[...] = jnp.zeros_like(zero_ref)
        _zero_fill_rows(zero_ref, ys_hbm, tail_ref[0], tail_ref[1], osem.at[0])
        for kind in range(6):
            pend_ref[kind] = 0
        chunk[0][0](c)
        first_weights()

    def run_chunk():
        for f in range(nf):
            next_weights(c, f)
            bd = bd_ref[0, :, f * tn:(f + 1) * tn]
            last = f == nf - 1
            if f == 0:
                drain(c % 2)

            def compute(r, rows, slot, f=f, bd=bd):
                y = jnp.dot(x_ref[c % 2, pl.ds(r, rows), :], wf_ref[f % 2, 0].astype(BF16),
                            preferred_element_type=F32) + bd
                lo = pltpu.bitcast(y[:, :tn // 2].astype(BF16).astype(F32), jnp.uint32)
                hi = pltpu.bitcast(y[:, tn // 2:].astype(BF16).astype(F32), jnp.uint32)
                packed = (hi & jnp.uint32(0xFFFF0000)) | (lo >> 16)
                for s in range(nw):
                    yb_ref[c % 2, pl.ds(r * ns + f * nw + s, rows, stride=ns), :] = (
                        packed[:, s * LANES:(s + 1) * LANES])

            def commit(tiles, last=last):
                if last:
                    for r, rows, slot in tiles:
                        out_copy(r, rows, c % 2).start()
                        kind = (c % 2) * 3 + SLOT_ROWS[-3:].index(rows)
                        pend_ref[kind] = pend_ref[kind] + 1

            _tile_loop(nsub, lambda tiles: None, compute, commit)

    @pl.when(nsub > 0)
    def _():
        on_parity(c, lambda p: chunk[p][1](c))
        on_parity(c + 1, lambda p: chunk[p][0](c + 1))
        run_chunk()

    @pl.when(c == pl.num_programs(0) - 1)
    def _():
        drain(0)
        drain(1)


def _moe_down(ch_e, ch_start, ch_nsub, tail, act, w_down, b_down):
    p_alloc, dff = act.shape
    d = w_down.shape[2]
    ns = TOK_WORDS // LANES
    tn = MOE_TN
    assert (d // tn) % 2 == 0
    nch = ch_e.shape[0]
    return pl.pallas_call(
        _moe_down_kernel,
        grid_spec=pltpu.PrefetchScalarGridSpec(
            num_scalar_prefetch=4,
            grid=(nch,),
            in_specs=[
                pl.BlockSpec(memory_space=pl.ANY),
                pl.BlockSpec(memory_space=pl.ANY),
                pl.BlockSpec((1, 1, d), lambda c, e, s, n, t: (e[c], 0, 0)),
            ],
            out_specs=pl.BlockSpec(memory_space=pl.ANY),
            scratch_shapes=[
                pltpu.VMEM((2, MOE_TM, dff), BF16),
                pltpu.VMEM((2, 1, dff, tn), F32),
                pltpu.VMEM((2, MOE_TM * ns, LANES), jnp.uint32),
                pltpu.VMEM((SUB_ROWS * ns, LANES), jnp.uint32),
                pltpu.SMEM((6,), jnp.int32),
                pltpu.SemaphoreType.DMA((2,)),
                pltpu.SemaphoreType.DMA((2,)),
                pltpu.SemaphoreType.DMA((2,)),
            ],
        ),
        out_shape=jax.ShapeDtypeStruct((p_alloc * ns, LANES), jnp.uint32),
        compiler_params=pltpu.CompilerParams(dimension_semantics=("arbitrary",), vmem_limit_bytes=VMEM_LIMIT_MAX),
        name="moe_down",
    )(ch_e, ch_start, ch_nsub, tail, act, w_down, b_down.reshape(N_EXPERTS, 1, d))


def _combine_kernel(pos_hbm, ys_hbm, h_ref, w_ref, g_ref, o_ref, pos_smem, yb_ref, sem, psem):
    i = pl.program_id(0)
    tm, d = h_ref.shape
    n = TOP_K * tm
    ns = TOK_WORDS // LANES

    def pos_copy(tile):
        third = pl.multiple_of((tile % 3) * n, n)
        return pltpu.make_async_copy(pos_hbm.at[pl.ds(pl.multiple_of(tile * n, n), n)],
                                     pos_smem.at[pl.ds(third, n)], psem.at[tile % 3])

    def fetch(tile, slot):
        pos_copy(tile).wait()

        @pl.when(tile + 1 < pl.num_programs(0))
        def _():
            pos_copy(tile + 1).start()

        base = (tile % 3) * n

        def issue(jj, c):
            for u in range(ROW_DMA_UNROLL):
                j = jj * ROW_DMA_UNROLL + u
                src = ys_hbm.at[pl.ds(pl.multiple_of(pos_smem[base + j] * ns, ns), ns)]
                dst = yb_ref.at[slot, pl.ds(pl.multiple_of(j * ns, ns), ns)]
                pltpu.make_async_copy(src, dst, sem.at[slot]).start(priority=u % 2)
            return c

        lax.fori_loop(0, n // ROW_DMA_UNROLL, issue, 0)

    @pl.when(i == 0)
    def _():
        pos_copy(i).start()
        fetch(i, 0)

    @pl.when(i + 1 < pl.num_programs(0))
    def _():
        for p in range(2):
            @pl.when((i + 1) % 2 == p)
            def _():
                fetch(i + 1, p)

    def reduce(slot):
        pltpu.make_async_copy(ys_hbm.at[pl.ds(0, n * ns)], yb_ref.at[slot], sem.at[slot]).wait()
        wb = [jnp.broadcast_to(w_ref[:, k:k + 1], (tm, LANES)) for k in range(TOP_K)]
        ssq = jnp.zeros((tm, 1), F32)
        nw = MOE_TN // 2 // LANES
        for s in range(ns):
            c0 = (s // nw) * MOE_TN + (s % nw) * LANES
            lo_cols = slice(c0, c0 + LANES)
            hi_cols = slice(c0 + MOE_TN // 2, c0 + MOE_TN // 2 + LANES)
            acc_lo = h_ref[:, lo_cols]
            acc_hi = h_ref[:, hi_cols]
            for k in range(TOP_K):
                w = yb_ref[slot, pl.ds(k * tm * ns + s, tm, stride=ns), :]
                acc_lo = acc_lo + wb[k] * pltpu.bitcast(w << 16, F32)
                acc_hi = acc_hi + wb[k] * pltpu.bitcast(w & jnp.uint32(0xFFFF0000), F32)
            o_ref[:, lo_cols] = acc_lo
            o_ref[:, hi_cols] = acc_hi
            ssq = ssq + jnp.sum(acc_lo * acc_lo + acc_hi * acc_hi, axis=-1, keepdims=True)
        o_ref[...] = o_ref[...] * lax.rsqrt(ssq * (1.0 / d) + NORM_EPS) * g_ref[...]

    for p in range(2):
        @pl.when(i % 2 == p)
        def _():
            reduce(p)


def _combine(pos_tiles, ys, h1, wts_t, g_final, tm):
    rows, d = h1.shape
    ns = TOK_WORDS // LANES
    return pl.pallas_call(
        _combine_kernel,
        grid=(rows // tm,),
        in_specs=[
            pl.BlockSpec(memory_space=pl.ANY),
            pl.BlockSpec(memory_space=pl.ANY),
            pl.BlockSpec((tm, d), lambda i: (i, 0)),
            pl.BlockSpec((tm, TOP_K), lambda i: (i, 0)),
            pl.BlockSpec((1, d), lambda i: (0, 0)),
        ],
        out_specs=pl.BlockSpec((tm, d), lambda i: (i, 0)),
        out_shape=jax.ShapeDtypeStruct((rows, d), F32),
        scratch_shapes=[
            pltpu.SMEM((3 * TOP_K * tm,), jnp.int32),
            pltpu.VMEM((2, TOP_K * tm * ns, LANES), jnp.uint32),
            pltpu.SemaphoreType.DMA((2,)),
            pltpu.SemaphoreType.DMA((3,)),
        ],
        compiler_params=_cparams(("arbitrary",)),
        name="combine",
    )(pos_tiles, ys, h1, wts_t, g_final)


def _rope_tables(n_pos):
    half = HEAD_DIM // 2
    inv = 1.0 / (ROPE_THETA ** (jnp.arange(half, dtype=F32) / half))
    ang = jnp.arange(n_pos, dtype=F32)[:, None] * inv[None, :]
    cos = jnp.tile(jnp.cos(ang), (1, LANES // half))
    sin = jnp.tile(jnp.concatenate([-jnp.sin(ang), jnp.sin(ang)], axis=1), (1, LANES // HEAD_DIM))
    return cos, sin


def _block_diag(w):
    per = LRU_GROUP // LRU_BLOCK
    w4 = w.reshape(D_LRU // LRU_GROUP, per, LRU_BLOCK, LRU_BLOCK)
    eye = jnp.eye(per, dtype=w.dtype)
    bd = jnp.einsum("gpcd,pq->gpcqd", w4, eye)
    return bd.reshape(D_LRU // LRU_GROUP, LRU_GROUP, LRU_GROUP).astype(BF16)


def _tile_positions(pos, tm):
    rows = pos.shape[1]
    return pos.reshape(TOP_K, rows // tm, tm).transpose(1, 0, 2).reshape(-1)


def _chunk_schedule(counts, n_rows):
    aligned = ((counts + SUB_ROWS - 1) // SUB_ROWS) * SUB_ROWS
    offs = jnp.cumsum(aligned) - aligned
    n_ch = (aligned + MOE_TM - 1) // MOE_TM
    cum = jnp.cumsum(n_ch)
    total = cum[-1]
    nch_max = N_EXPERTS + (n_rows + N_EXPERTS * SUB_ROWS) // MOE_TM
    c = jnp.arange(nch_max, dtype=jnp.int32)
    cc = jnp.minimum(c, total - 1)
    e = jnp.sum(cc[:, None] >= cum[None, :], axis=1).astype(jnp.int32)
    is_e = e[:, None] == jnp.arange(N_EXPERTS, dtype=jnp.int32)[None, :]
    of_e = lambda v: jnp.sum(jnp.where(is_e, v[None, :], 0), axis=1)
    j = cc - (of_e(cum) - of_e(n_ch))
    start = of_e(offs) + j * MOE_TM
    nsub = jnp.where(c < total, jnp.minimum(MOE_TM, of_e(aligned) - j * MOE_TM) // SUB_ROWS, 0)
    tails = offs + (counts // SUB_ROWS) * SUB_ROWS
    used = jnp.sum(aligned)
    tail = jnp.stack([used, (n_rows + N_EXPERTS * SUB_ROWS - used) // SUB_ROWS])
    tails = jnp.concatenate([tails, tail])
    return (offs, e, start.astype(jnp.int32), nsub.astype(jnp.int32), tails.astype(jnp.int32),
            tail.astype(jnp.int32))


def kernel(x, meta_tokens, norm_mix, w_in, b_in, sinks, conv_w, conv_b, w_a, b_a, w_i, b_i, lru_lambda,
           g_attn_out, g_lru_out, w_out, b_out, norm_ffn, w_router, b_router, w_gate, b_gate, w_up, b_up,
           w_down, b_down, final_norm):
    batch, seq, d = x.shape
    rows = batch * seq
    x2d = x.reshape(rows, d)
    row = lambda v: v.reshape(1, -1)

    w_in_bf = w_in[0].astype(BF16)
    w_out_bf = w_out[0].astype(BF16)
    cos, sin = _rope_tables(N_META + seq)
    wa_bd, wi_bd = _block_diag(w_a[0]), _block_diag(w_i[0])
    lru_args = (conv_w[0], row(conv_b[0]), wa_bd, row(b_a[0]), wi_bd, row(b_i[0]), row(lru_lambda[0]),
                row(g_lru_out[0]))

    _, kvm, xgm = _inproj(meta_tokens, row(norm_mix[0]), w_in_bf, row(b_in[0]), cos[:N_META], sin[:N_META],
                          N_META, 1)
    _, h0, tail0 = _lru(xgm, *lru_args, jnp.zeros((1, D_LRU), F32), jnp.zeros((8, D_LRU), F32), 1, N_META,
                        N_META)

    tm = ROW_TILE
    q, kv, xg = _inproj(x2d, row(norm_mix[0]), w_in_bf, row(b_in[0]), cos[N_META:], sin[N_META:], tm,
                        seq // tm)
    kvm = jnp.pad(kvm, ((0, BLOCK - N_META), (0, 0)))
    attn_n = _attention(sinks[0], q, kv, kvm, row(g_attn_out[0]), batch, seq)
    lru_n, _, _ = _lru(xg, *lru_args, h0, tail0, batch, seq, LRU_TIME_TILE)

    tri = (jnp.arange(tm)[:, None] <= jnp.arange(tm)[None, :]).astype(BF16)
    h1, xp, ids, wts, rank, cnt = _outproj(
        attn_n, lru_n, w_out_bf, row(b_out[0]), x2d, row(norm_ffn[0]), w_router[0].T.astype(BF16),
        b_router[0].reshape(N_EXPERTS, 1), tri, tm)

    counts = cnt[:, 0].astype(jnp.int32)
    n_rows = rows * TOP_K
    offs, ch_e, ch_start, ch_nsub, tails, tail = _chunk_schedule(counts, n_rows)
    onehot = ids[..., None] == jnp.arange(N_EXPERTS, dtype=jnp.int32)
    pos = jnp.sum(jnp.where(onehot, offs, 0), axis=-1) + rank
    p_alloc = n_rows + N_EXPERTS * SUB_ROWS

    xs = _dispatch(tails, _tile_positions(pos, ROW_TILE), xp, p_alloc, ROW_TILE)
    act = _moe_up(ch_e, ch_start, ch_nsub, tail, xs, w_gate[0], b_gate[0], w_up[0], b_up[0])
    ys = _moe_down(ch_e, ch_start, ch_nsub, tail, act, w_down[0], b_down[0])
    out = _combine(_tile_positions(pos, COMBINE_TILE), ys, h1, wts.T, row(final_norm), COMBINE_TILE)
    return out.reshape(batch, seq, d)
```

```python
import jax
import jax.numpy as jnp
from jax import lax
from jax.experimental import pallas as pl
from jax.experimental.pallas import tpu as pltpu

F32 = jnp.float32
BF16 = jnp.bfloat16

N_META = 16
HEAD_DIM = 64
N_Q_HEADS = 16
N_KV_HEADS = 4
GROUP = N_Q_HEADS // N_KV_HEADS
D_ATTN = N_Q_HEADS * HEAD_DIM
D_KV = N_KV_HEADS * HEAD_DIM
KV_COLS = 4 * D_KV
BLOCK = 128
ATTN_QBLOCKS = 8
ROPE_THETA = 10000.0
D_LRU = 1024
LRU_BLOCK = 64
LRU_GROUP = 256
CONV_WIDTH = 4
LRU_C = 8.0
N_EXPERTS = 32
TOP_K = 4
SWIGLU_LIMIT = 7.0
SWIGLU_ALPHA = 1.702
NORM_EPS = 1e-5
NEG_INF = -1e30

LANES = 128
D_MODEL = 2048
TOK_WORDS = D_MODEL // 2
SUB_ROWS = 128
MOE_TM = 2304
MOE_TF = 512
MOE_TN = 1024
ROW_TILE = 512
INPROJ_COLS = 512
LRU_TIME_TILE = 128
COMBINE_TILE = 256
ROW_DMA_UNROLL = 16
SLOT_ROWS = (4 * SUB_ROWS,) * 5 + (2 * SUB_ROWS, SUB_ROWS)
VMEM_LIMIT = 56 * 1024 * 1024
VMEM_LIMIT_MAX = 62 * 1024 * 1024


def _cparams(sem):
    return pltpu.CompilerParams(dimension_semantics=sem, vmem_limit_bytes=VMEM_LIMIT)


def _mult(x, m):
    return x if isinstance(x, int) else pl.multiple_of(x, m)


def _rms(x, g):
    return x * lax.rsqrt(jnp.mean(x * x, axis=-1, keepdims=True) + NORM_EPS) * g


def _inproj_kernel(x_ref, g_ref, w_ref, b_ref, cos_ref, sin_ref, q_ref, kv_ref, xg_ref):
    xn = _rms(x_ref[...], g_ref[...]).astype(BF16)
    cos = cos_ref[...]
    sin = sin_ref[...]
    lane = lax.broadcasted_iota(jnp.int32, (1, LANES), 1)
    first_half = (lane % HEAD_DIM) < (HEAD_DIM // 2)

    def rope(z):
        partner = jnp.where(first_half, pltpu.roll(z, LANES - HEAD_DIM // 2, 1),
                            pltpu.roll(z, HEAD_DIM // 2, 1))
        return z * cos + partner * sin

    cw = INPROJ_COLS
    for c in range(w_ref.shape[1] // cw):
        z = jnp.dot(xn, w_ref[:, c * cw:(c + 1) * cw], preferred_element_type=F32)
        z = z + b_ref[:, c * cw:(c + 1) * cw]
        if c < 2:
            for j in range(cw // LANES):
                zz = rope(z[:, j * LANES:(j + 1) * LANES]) * (HEAD_DIM ** -0.5)
                q_ref[:, c * cw + j * LANES:c * cw + (j + 1) * LANES] = zz.astype(BF16)
        elif c == 2:
            low = lane < HEAD_DIM
            for j in range(2 * D_KV // LANES):
                zz = z[:, j * LANES:(j + 1) * LANES]
                zz = rope(zz) if j < D_KV // LANES else zz
                sw = pltpu.roll(zz, HEAD_DIM, 1)
                kv_ref[:, 2 * j * LANES:(2 * j + 1) * LANES] = jnp.where(low, zz, sw).astype(BF16)
                kv_ref[:, (2 * j + 1) * LANES:(2 * j + 2) * LANES] = jnp.where(low, sw, zz).astype(BF16)
        else:
            xg_ref[:, (c - 3) * cw:(c - 2) * cw] = z


def _inproj(x2d, g, w_bf, b, cos, sin, tm, pos_blocks):
    rows, d = x2d.shape
    dz = w_bf.shape[1]
    return pl.pallas_call(
        _inproj_kernel,
        grid=(rows // tm,),
        in_specs=[
            pl.BlockSpec((tm, d), lambda m: (m, 0)),
            pl.BlockSpec((1, d), lambda m: (0, 0)),
            pl.BlockSpec((d, dz), lambda m: (0, 0)),
            pl.BlockSpec((1, dz), lambda m: (0, 0)),
            pl.BlockSpec((tm, LANES), lambda m: (m % pos_blocks, 0)),
            pl.BlockSpec((tm, LANES), lambda m: (m % pos_blocks, 0)),
        ],
        out_specs=[
            pl.BlockSpec((tm, D_ATTN), lambda m: (m, 0)),
            pl.BlockSpec((tm, KV_COLS), lambda m: (m, 0)),
            pl.BlockSpec((tm, 2 * D_LRU), lambda m: (m, 0)),
        ],
        out_shape=[
            jax.ShapeDtypeStruct((rows, D_ATTN), BF16),
            jax.ShapeDtypeStruct((rows, KV_COLS), BF16),
            jax.ShapeDtypeStruct((rows, 2 * D_LRU), F32),
        ],
        compiler_params=_cparams(("arbitrary",)),
        name="inproj",
    )(x2d, g, w_bf, b, cos, sin)


def _attn_block(sink_ref, q_ref, kvc_ref, kvp_ref, kvm_ref, g_ref, o_ref, n, rs):
    row = lax.broadcasted_iota(jnp.int32, (GROUP * BLOCK, BLOCK), 0) % BLOCK
    col = lax.broadcasted_iota(jnp.int32, (GROUP * BLOCK, BLOCK), 1)
    in_cur = col <= row
    in_prev = jnp.logical_and(col > row, n > 0)
    is_meta = col < N_META
    low = lax.broadcasted_iota(jnp.int32, (BLOCK, LANES), 1) < HEAD_DIM
    nt = (((1,), (1,)), ((), ()))
    zero = jnp.zeros((), BF16)
    outs = []
    for h in range(N_KV_HEADS):
        ks = slice(h * LANES, (h + 1) * LANES)
        vs = slice(N_KV_HEADS * LANES + h * LANES, N_KV_HEADS * LANES + (h + 1) * LANES)
        parts = []
        for j in range(GROUP // 2):
            qg = q_ref[rs, (h * GROUP // 2 + j) * LANES:(h * GROUP // 2 + j + 1) * LANES]
            parts += [jnp.where(low, qg, zero), jnp.where(low, zero, qg)]
        qs = jnp.concatenate(parts, axis=0)
        s_c = lax.dot_general(qs, kvc_ref[:, ks], nt, preferred_element_type=F32)
        s_p = lax.dot_general(qs, kvp_ref[:, ks], nt, preferred_element_type=F32)
        s_m = lax.dot_general(qs, kvm_ref[:, ks], nt, preferred_element_type=F32)
        s_b = jnp.where(in_prev, s_p, jnp.where(in_cur, s_c, NEG_INF))
        s_m = jnp.where(is_meta, s_m, NEG_INF)
        sink = jnp.concatenate(
            [jnp.full((BLOCK, 1), sink_ref[h * GROUP + g], F32) for g in range(GROUP)], axis=0)
        m = jnp.maximum(jnp.max(jnp.maximum(s_b, s_m), axis=-1, keepdims=True), sink)
        p_b = jnp.exp(s_b - m)
        p_m = jnp.exp(s_m - m)
        den = jnp.sum(p_b + p_m, axis=-1, keepdims=True) + jnp.exp(sink - m)
        r = (jnp.dot(jnp.where(in_cur, p_b, 0.0).astype(BF16), kvc_ref[:, vs], preferred_element_type=F32)
             + jnp.dot(jnp.where(in_cur, 0.0, p_b).astype(BF16), kvp_ref[:, vs], preferred_element_type=F32)
             + jnp.dot(p_m.astype(BF16), kvm_ref[:, vs], preferred_element_type=F32))
        r = r / den
        for j in range(GROUP // 2):
            outs.append(jnp.where(low, r[2 * j * BLOCK:(2 * j + 1) * BLOCK], r[(2 * j + 1) * BLOCK:(2 * j + 2) * BLOCK]))
    o_all = jnp.concatenate(outs, axis=1)
    o_ref[rs, :] = _rms(o_all, g_ref[...]).astype(BF16)


def _attn_kernel(sink_ref, q_ref, *rest):
    kv_refs, (kvm_ref, g_ref, o_ref) = rest[:ATTN_QBLOCKS + 1], rest[ATTN_QBLOCKS + 1:]
    n0 = pl.program_id(1) * ATTN_QBLOCKS
    for sub in range(ATTN_QBLOCKS):
        _attn_block(sink_ref, q_ref, kv_refs[sub + 1], kv_refs[sub], kvm_ref, g_ref, o_ref, n0 + sub,
                    slice(sub * BLOCK, (sub + 1) * BLOCK))


def _attention(sinks, q, kv, kvm, g_attn, batch, seq):
    nb = seq // BLOCK
    nq = ATTN_QBLOCKS
    kv_spec = lambda i: pl.BlockSpec((BLOCK, KV_COLS), lambda b, n, s: (b * nb + jnp.maximum(nq * n + i - 1, 0), 0))
    return pl.pallas_call(
        _attn_kernel,
        grid_spec=pltpu.PrefetchScalarGridSpec(
            num_scalar_prefetch=1,
            grid=(batch, nb // nq),
            in_specs=[
                pl.BlockSpec((nq * BLOCK, D_ATTN), lambda b, n, s: (b * (nb // nq) + n, 0)),
                *[kv_spec(i) for i in range(nq + 1)],
                pl.BlockSpec((BLOCK, KV_COLS), lambda b, n, s: (0, 0)),
                pl.BlockSpec((1, D_ATTN), lambda b, n, s: (0, 0)),
            ],
            out_specs=pl.BlockSpec((nq * BLOCK, D_ATTN), lambda b, n, s: (b * (nb // nq) + n, 0)),
        ),
        out_shape=jax.ShapeDtypeStruct((batch * seq, D_ATTN), BF16),
        compiler_params=_cparams(("arbitrary", "arbitrary")),
        name="attention",
    )(sinks, q, *([kv] * (ATTN_QBLOCKS + 1)), kvm, g_attn)


def _one_minus_sq(a, log_a):
    y = 2.0 * log_a
    p = 1.0 + y * (1.0 / 4.0)
    for k in (3, 2):
        p = 1.0 + y * p * (1.0 / k)
    return jnp.where(y > -1.0 / 64.0, -(y * p), 1.0 - a * a)


def _lru_kernel(xg_ref, cw_ref, cb_ref, wa_ref, ba_ref, wi_ref, bi_ref, lam_ref, g_ref, h0_ref, tail0_ref,
                o_ref, hout_ref, tailout_ref, ext_ref, tail_ref, a_ref, b_ref, h_ref):
    tt = pl.program_id(0)
    batch, rows = xg_ref.shape[0], xg_ref.shape[1]

    @pl.when(tt == 0)
    def _():
        for g in range(D_LRU // LANES):
            h_ref[g] = jnp.broadcast_to(h0_ref[:, g * LANES:(g + 1) * LANES], (batch, LANES))
        for s in range(batch):
            tail_ref[s] = tail0_ref[...]

    sp = jax.nn.softplus(-lam_ref[...])

    def gates(s, carry):
        ext_ref[0:8, :] = tail_ref[s]
        ext_ref[8:, :] = xg_ref[s, :, :D_LRU]
        tail_ref[s] = ext_ref[rows:rows + 8, :]
        xc = cb_ref[...] + sum(cw_ref[j:j + 1, :] * ext_ref[5 + j:5 + j + rows, :] for j in range(CONV_WIDTH))
        xcb = xc.astype(BF16)
        for c in range(D_LRU // LRU_GROUP):
            cs = slice(c * LRU_GROUP, (c + 1) * LRU_GROUP)
            r = jax.nn.sigmoid(jnp.dot(xcb[:, cs], wa_ref[c], preferred_element_type=F32) + ba_ref[:, cs])
            i = jax.nn.sigmoid(jnp.dot(xcb[:, cs], wi_ref[c], preferred_element_type=F32) + bi_ref[:, cs])
            log_a = -LRU_C * r * sp[:, cs]
            a = jnp.exp(log_a)
            b = jnp.sqrt(_one_minus_sq(a, log_a)) * i * xc[:, cs]
            for g in range(LRU_GROUP // LANES):
                lanes = slice(g * LANES, (g + 1) * LANES)
                a_ref[c * (LRU_GROUP // LANES) + g, pl.ds(s, rows, stride=batch), :] = a[:, lanes]
                b_ref[c * (LRU_GROUP // LANES) + g, pl.ds(s, rows, stride=batch), :] = b[:, lanes]
        return carry

    lax.fori_loop(0, batch, gates, 0)

    def step(t, h):
        r0 = pl.multiple_of(t * batch, batch)
        new = []
        for g in range(D_LRU // LANES):
            hg = a_ref[g, pl.ds(r0, batch), :] * h[g] + b_ref[g, pl.ds(r0, batch), :]
            b_ref[g, pl.ds(r0, batch), :] = hg
            new.append(hg)
        return tuple(new)

    h_last = lax.fori_loop(0, rows, step, tuple(h_ref[g] for g in range(D_LRU // LANES)))
    for g in range(D_LRU // LANES):
        h_ref[g] = h_last[g]
    hout_ref[...] = jnp.concatenate([h_last[g][0:1] for g in range(D_LRU // LANES)], axis=1)
    tailout_ref[...] = tail_ref[0]

    def finish(s, carry):
        hs = jnp.concatenate([b_ref[g, pl.ds(s, rows, stride=batch), :] for g in range(D_LRU // LANES)], axis=1)
        y = hs * jax.nn.gelu(xg_ref[s, :, D_LRU:])
        o_ref[s] = _rms(y, g_ref[...]).astype(BF16)
        return carry

    lax.fori_loop(0, batch, finish, 0)


def _lru(xg, conv_w, conv_b, wa_bd, b_a, wi_bd, b_i, lam, g_lru, h0, tail0, batch, seq, tt):
    ntt = seq // tt
    ng = D_LRU // LANES
    vec = pl.BlockSpec((1, D_LRU), lambda t: (0, 0))
    wspec = pl.BlockSpec((D_LRU // LRU_GROUP, LRU_GROUP, LRU_GROUP), lambda t: (0, 0, 0))
    out, h_last, tail = pl.pallas_call(
        _lru_kernel,
        grid=(ntt,),
        in_specs=[
            pl.BlockSpec((batch, tt, 2 * D_LRU), lambda t: (0, t, 0)),
            pl.BlockSpec((CONV_WIDTH, D_LRU), lambda t: (0, 0)),
            vec, wspec, vec, wspec, vec, vec, vec, vec,
            pl.BlockSpec((8, D_LRU), lambda t: (0, 0)),
        ],
        out_specs=[
            pl.BlockSpec((batch, tt, D_LRU), lambda t: (0, t, 0)),
            pl.BlockSpec((1, D_LRU), lambda t: (0, 0)),
            pl.BlockSpec((8, D_LRU), lambda t: (0, 0)),
        ],
        out_shape=[
            jax.ShapeDtypeStruct((batch, seq, D_LRU), BF16),
            jax.ShapeDtypeStruct((1, D_LRU), F32),
            jax.ShapeDtypeStruct((8, D_LRU), F32),
        ],
        scratch_shapes=[
            pltpu.VMEM((8 + tt, D_LRU), F32),
            pltpu.VMEM((batch, 8, D_LRU), F32),
            pltpu.VMEM((ng, tt * batch, LANES), F32),
            pltpu.VMEM((ng, tt * batch, LANES), F32),
            pltpu.VMEM((ng, batch, LANES), F32),
        ],
        compiler_params=_cparams(("arbitrary",)),
        name="rglru",
    )(xg.reshape(batch, seq, 2 * D_LRU), conv_w, conv_b, wa_bd, b_a, wi_bd, b_i, lam, g_lru, h0, tail0)
    return out.reshape(batch * seq, D_LRU), h_last, tail


def _outproj_kernel(a_ref, l_ref, wo_ref, bo_ref, x_ref, gf_ref, wr_ref, br_ref, tri_ref,
                    h_ref, xp_ref, ids_ref, wts_ref, rank_ref, cnt_ref, carry_ref):
    m = pl.program_id(0)
    tm = a_ref.shape[0]

    @pl.when(m == 0)
    def _():
        carry_ref[...] = jnp.zeros_like(carry_ref)

    mix = jnp.concatenate([a_ref[...], l_ref[...]], axis=1)
    h = jnp.dot(mix, wo_ref[...], preferred_element_type=F32) + bo_ref[...] + x_ref[...]
    h_ref[...] = h
    xn = _rms(h, gf_ref[...]).astype(BF16)
    half = xn.shape[1] // 2
    lo = pltpu.bitcast(xn[:, :half].astype(F32), jnp.uint32)
    hi = pltpu.bitcast(xn[:, half:].astype(F32), jnp.uint32)
    packed = (hi & jnp.uint32(0xFFFF0000)) | (lo >> 16)
    ns = half // LANES
    for s in range(ns):
        xp_ref[pl.ds(s, tm, stride=ns), :] = packed[:, s * LANES:(s + 1) * LANES]

    logits = lax.dot_general(wr_ref[...], xn, (((1,), (1,)), ((), ())), preferred_element_type=F32)
    logits = logits + br_ref[...]
    eidx = lax.broadcasted_iota(jnp.int32, (N_EXPERTS, tm), 0)
    work = logits
    vals, sels = [], []
    for k in range(TOP_K):
        v = jnp.max(work, axis=0, keepdims=True)
        idx = jnp.min(jnp.where(work == v, eidx, N_EXPERTS), axis=0, keepdims=True)
        sel = eidx == idx
        ids_ref[k:k + 1, :] = idx
        vals.append(v)
        sels.append(sel)
        work = jnp.where(sel, -jnp.inf, work)
    es = [jnp.exp(v - vals[0]) for v in vals]
    den = es[0] + es[1] + es[2] + es[3]
    for k in range(TOP_K):
        wts_ref[k:k + 1, :] = es[k] / den
    cnt = sum(s.astype(F32) for s in sels)
    incl = jnp.dot(cnt.astype(BF16), tri_ref[...], preferred_element_type=F32)
    before = incl - cnt + carry_ref[:, 0:1]
    for k in range(TOP_K):
        rk = jnp.sum(jnp.where(sels[k], before, 0.0), axis=0, keepdims=True)
        rank_ref[k:k + 1, :] = rk.astype(jnp.int32)
    carry_ref[...] = carry_ref[...] + incl[:, tm - 1:tm]
    cnt_ref[...] = carry_ref[...]


def _outproj(attn_n, lru_n, wo_bf, b_out, x2d, g_ffn, wr_t, br, tri, tm):
    rows, d = x2d.shape
    const = lambda shape: pl.BlockSpec(shape, lambda m: tuple(0 for _ in shape))
    return pl.pallas_call(
        _outproj_kernel,
        grid=(rows // tm,),
        in_specs=[
            pl.BlockSpec((tm, D_ATTN), lambda m: (m, 0)),
            pl.BlockSpec((tm, D_LRU), lambda m: (m, 0)),
            const((D_ATTN + D_LRU, d)), const((1, d)),
            pl.BlockSpec((tm, d), lambda m: (m, 0)),
            const((1, d)), const((N_EXPERTS, d)), const((N_EXPERTS, 1)), const((tm, tm)),
        ],
        out_specs=[
            pl.BlockSpec((tm, d), lambda m: (m, 0)),
            pl.BlockSpec((tm * (d // 2 // LANES), LANES), lambda m: (m, 0)),
            pl.BlockSpec((TOP_K, tm), lambda m: (0, m)),
            pl.BlockSpec((TOP_K, tm), lambda m: (0, m)),
            pl.BlockSpec((TOP_K, tm), lambda m: (0, m)),
            const((N_EXPERTS, LANES)),
        ],
        out_shape=[
            jax.ShapeDtypeStruct((rows, d), F32),
            jax.ShapeDtypeStruct((rows * (d // 2 // LANES), LANES), jnp.uint32),
            jax.ShapeDtypeStruct((TOP_K, rows), jnp.int32),
            jax.ShapeDtypeStruct((TOP_K, rows), F32),
            jax.ShapeDtypeStruct((TOP_K, rows), jnp.int32),
            jax.ShapeDtypeStruct((N_EXPERTS, LANES), F32),
        ],
        scratch_shapes=[pltpu.VMEM((N_EXPERTS, LANES), F32)],
        compiler_params=_cparams(("arbitrary",)),
        name="outproj_router",
    )(attn_n, lru_n, wo_bf, b_out, x2d, g_ffn, wr_t, br, tri)


def _zero_fill_rows(zero_ref, dst_hbm, first_row, n_blocks, sem):
    blk = zero_ref.shape[0]
    per = blk // SUB_ROWS

    def copy(j):
        row = pl.multiple_of((first_row + j * SUB_ROWS) * per, blk)
        return pltpu.make_async_copy(zero_ref, dst_hbm.at[pl.ds(row, blk)], sem)

    def start(j, c):
        copy(j).start()
        return c

    def wait(j, c):
        copy(j).wait()
        return c

    lax.fori_loop(0, n_blocks, start, 0)
    lax.fori_loop(0, n_blocks, wait, 0)


def _dispatch_kernel(tails_ref, pos_hbm, xp_ref, xs_hbm, pos_smem, zero_ref, sem, psem):
    i = pl.program_id(0)
    ns = TOK_WORDS // LANES
    tm = xp_ref.shape[0] // ns
    n = TOP_K * tm

    def pos_copy(step):
        half = pl.multiple_of((step % 2) * n, n)
        return pltpu.make_async_copy(pos_hbm.at[pl.ds(pl.multiple_of(step * n, n), n)],
                                     pos_smem.at[pl.ds(half, n)], psem.at[step % 2])

    @pl.when(i == 0)
    def _():
        pos_copy(i).start()
        zero_ref[...] = jnp.zeros_like(zero_ref)
        for e in range(N_EXPERTS):
            _zero_fill_rows(zero_ref, xs_hbm, tails_ref[e], 1, sem)
        _zero_fill_rows(zero_ref, xs_hbm, tails_ref[N_EXPERTS], tails_ref[N_EXPERTS + 1], sem)

    pos_copy(i).wait()

    @pl.when(i + 1 < pl.num_programs(0))
    def _():
        pos_copy(i + 1).start()

    base = (i % 2) * n
    for k in range(TOP_K):
        def issue(tt, c):
            for u in range(ROW_DMA_UNROLL):
                t = tt * ROW_DMA_UNROLL + u
                src = xp_ref.at[pl.ds(pl.multiple_of(t * ns, ns), ns)]
                dst = xs_hbm.at[pl.ds(pl.multiple_of(pos_smem[base + k * tm + t] * ns, ns), ns)]
                pltpu.make_async_copy(src, dst, sem).start(priority=u % 2)
            return c

        lax.fori_loop(0, tm // ROW_DMA_UNROLL, issue, 0)
    for _ in range(TOP_K):
        pltpu.make_async_copy(xp_ref, xs_hbm.at[pl.ds(0, tm * ns)], sem).wait()


def _dispatch(tails, pos_tiles, xp, p_alloc, tm):
    ns = TOK_WORDS // LANES
    rows = xp.shape[0] // ns
    return pl.pallas_call(
        _dispatch_kernel,
        grid_spec=pltpu.PrefetchScalarGridSpec(
            num_scalar_prefetch=1,
            grid=(rows // tm,),
            in_specs=[
                pl.BlockSpec(memory_space=pl.ANY),
                pl.BlockSpec((tm * ns, LANES), lambda i, s: (i, 0)),
            ],
            out_specs=pl.BlockSpec(memory_space=pl.ANY),
            scratch_shapes=[
                pltpu.SMEM((2 * TOP_K * tm,), jnp.int32),
                pltpu.VMEM((SUB_ROWS * ns, LANES), jnp.uint32),
                pltpu.SemaphoreType.DMA,
                pltpu.SemaphoreType.DMA((2,)),
            ],
        ),
        out_shape=jax.ShapeDtypeStruct((p_alloc * ns, LANES), jnp.uint32),
        compiler_params=_cparams(("arbitrary",)),
        name="dispatch",
    )(tails, pos_tiles, xp)


def _chunk_dma(cs_ref, cn_ref, src_hbm, dst_ref, per, sem):
    blk = SUB_ROWS * per
    nch = cs_ref.shape[0]

    def copy(ci, j):
        src = src_hbm.at[pl.ds(pl.multiple_of((cs_ref[ci] + j * SUB_ROWS) * per, blk), blk)]
        return pltpu.make_async_copy(src, dst_ref.at[pl.ds(pl.multiple_of(j * blk, blk), blk)], sem)

    def run(ci, wait):
        cc = jnp.minimum(ci, nch - 1)
        n = jnp.where(ci < nch, cn_ref[cc], 0)

        def body(j, c):
            if wait:
                copy(cc, j).wait()
            else:
                copy(cc, j).start()
            return c

        lax.fori_loop(0, n, body, 0)

    return (lambda ci: run(ci, False)), (lambda ci: run(ci, True))


def _tile_loop(nsub, prepare, compute, commit):
    def run(tiles):
        prepare(tiles)
        for t in tiles:
            compute(*t)
        commit(tiles)

    n8 = nsub // 8
    rem = nsub % 8
    base = n8 * 8 * SUB_ROWS
    r2 = base + (rem // 4) * 4 * SUB_ROWS
    r1 = r2 + ((rem % 4) // 2) * 2 * SUB_ROWS

    @pl.when(rem % 4 >= 2)
    def _():
        run([(pl.multiple_of(r2, SUB_ROWS), 2 * SUB_ROWS, 5)])

    @pl.when(rem % 2 == 1)
    def _():
        run([(pl.multiple_of(r1, SUB_ROWS), SUB_ROWS, 6)])

    @pl.when(rem >= 4)
    def _():
        run([(pl.multiple_of(base, 4 * SUB_ROWS), 4 * SUB_ROWS, 4)])

    def pair(j, c):
        r = pl.multiple_of(j * 8 * SUB_ROWS, 8 * SUB_ROWS)
        slot = (j % 2) * 2
        run([(r, 4 * SUB_ROWS, slot), (r + 4 * SUB_ROWS, 4 * SUB_ROWS, slot + 1)])
        return c

    lax.fori_loop(0, n8, pair, 0)


def _weight_stream(ce_ref, cn_ref, nf, w_hbms, tn, wf_ref, wsem):
    def copies(cc, f):
        cols = pl.ds(_mult(f * tn, tn), tn)
        return [pltpu.make_async_copy(w.at[ce_ref[cc], :, cols], wf_ref.at[f % 2, i], wsem.at[f % 2])
                for i, w in enumerate(w_hbms)]

    def first():
        for cp in copies(0, 0):
            cp.start()

    def advance(c, f):
        q = c * nf + f + 1
        cc = jnp.minimum(q // nf, cn_ref.shape[0] - 1)

        @pl.when(jnp.logical_and(q // nf < cn_ref.shape[0], cn_ref[cc] > 0))
        def _():
            for cp in copies(cc, q % nf):
                cp.start()

        for cp in copies(c, f):
            cp.wait()

    return first, advance


def _moe_up_kernel(ce_ref, cs_ref, cn_ref, tail_ref, xs_hbm, wg_hbm, bg_ref, wu_hbm, bu_ref, act_hbm,
                   land_ref, wf_ref, ab_ref, zero_ref, pend_ref, sem, wsem, osem):
    c = pl.program_id(0)
    nsub = cn_ref[c]
    start = pl.multiple_of(cs_ref[c], SUB_ROWS)
    tf = wf_ref.shape[3]
    nf = bg_ref.shape[2] // tf
    ns = TOK_WORDS // LANES
    chunk = [_chunk_dma(cs_ref, cn_ref, xs_hbm, land_ref.at[i], ns, sem.at[i]) for i in range(2)]

    def on_parity(ci, fn):
        for p in range(2):
            @pl.when(ci % 2 == p)
            def _():
                fn(p)
    first_weights, next_weights = _weight_stream(ce_ref, cn_ref, nf, (wg_hbm, wu_hbm), tf, wf_ref, wsem)

    def out_copy(r, rows, slot, f):
        return pltpu.make_async_copy(
            ab_ref.at[slot, pl.ds(0, rows)],
            act_hbm.at[pl.ds(pl.multiple_of(start + r, SUB_ROWS), rows), pl.ds(_mult(f * tf, tf), tf)],
            osem.at[slot])

    @pl.when(c == 0)
    def _():
        zero_ref[...] = jnp.zeros_like(zero_ref)
        _zero_fill_rows(zero_ref, act_hbm, tail_ref[0], tail_ref[1], osem.at[0])
        for slot in range(len(SLOT_ROWS)):
            pend_ref[slot] = 0
        chunk[0][0](c)
        first_weights()

    @pl.when(nsub > 0)
    def _():
        on_parity(c, lambda p: chunk[p][1](c))
        on_parity(c + 1, lambda p: chunk[p][0](c + 1))

        def column_block(f, carry):
            next_weights(c, f)
            cols = pl.ds(pl.multiple_of(f * tf, tf), tf)
            bg = bg_ref[0, :, cols]
            bu = bu_ref[0, :, cols]

            def prepare(tiles):
                for _, rows, slot in tiles:
                    @pl.when(pend_ref[slot] == 1)
                    def _():
                        out_copy(0, rows, slot, f).wait()

            def compute(r, rows, slot):
                words = [land_ref[c % 2, pl.ds(r * ns + s, rows, stride=ns), :] for s in range(ns)]
                x = jnp.concatenate([pltpu.bitcast(w << 16, F32).astype(BF16) for w in words]
                                    + [pltpu.bitcast(w & jnp.uint32(0xFFFF0000), F32).astype(BF16) for w in words],
                                    axis=1)
                g = jnp.dot(x, wf_ref[f % 2, 0].astype(BF16), preferred_element_type=F32) + bg
                u = jnp.dot(x, wf_ref[f % 2, 1].astype(BF16), preferred_element_type=F32) + bu
                g = jnp.minimum(g, SWIGLU_LIMIT)
                u = jnp.clip(u, -SWIGLU_LIMIT, SWIGLU_LIMIT)
                ab_ref[slot, pl.ds(0, rows), :] = (g * jax.nn.sigmoid(SWIGLU_ALPHA * g) * (u + 1.0)).astype(BF16)

            def commit(tiles):
                for r, rows, slot in tiles:
                    out_copy(r, rows, slot, f).start()
                    pend_ref[slot] = 1

            _tile_loop(nsub, prepare, compute, commit)
            return carry

        lax.fori_loop(0, nf, column_block, 0)

    @pl.when(c == pl.num_programs(0) - 1)
    def _():
        for slot, rows in enumerate(SLOT_ROWS):
            @pl.when(pend_ref[slot] == 1)
            def _():
                out_copy(0, rows, slot, 0).wait()


def _moe_up(ch_e, ch_start, ch_nsub, tail, xs, w_gate, b_gate, w_up, b_up):
    ns = TOK_WORDS // LANES
    p_alloc = xs.shape[0] // ns
    d = 2 * TOK_WORDS
    dff = w_gate.shape[2]
    tf = MOE_TF
    assert (dff // tf) % 2 == 0
    nch = ch_e.shape[0]
    bmap = lambda c, e, s, n, t: (e[c], 0, 0)
    return pl.pallas_call(
        _moe_up_kernel,
        grid_spec=pltpu.PrefetchScalarGridSpec(
            num_scalar_prefetch=4,
            grid=(nch,),
            in_specs=[
                pl.BlockSpec(memory_space=pl.ANY),
                pl.BlockSpec(memory_space=pl.ANY),
                pl.BlockSpec((1, 1, dff), bmap),
                pl.BlockSpec(memory_space=pl.ANY),
                pl.BlockSpec((1, 1, dff), bmap),
            ],
            out_specs=pl.BlockSpec(memory_space=pl.ANY),
            scratch_shapes=[
                pltpu.VMEM((2, MOE_TM * ns, LANES), jnp.uint32),
                pltpu.VMEM((2, 2, d, tf), F32),
                pltpu.VMEM((len(SLOT_ROWS), 4 * SUB_ROWS, tf), BF16),
                pltpu.VMEM((SUB_ROWS, dff), BF16),
                pltpu.SMEM((len(SLOT_ROWS),), jnp.int32),
                pltpu.SemaphoreType.DMA((2,)),
                pltpu.SemaphoreType.DMA((2,)),
                pltpu.SemaphoreType.DMA((len(SLOT_ROWS),)),
            ],
        ),
        out_shape=jax.ShapeDtypeStruct((p_alloc, dff), BF16),
        compiler_params=_cparams(("arbitrary",)),
        name="moe_up",
    )(ch_e, ch_start, ch_nsub, tail, xs, w_gate, b_gate.reshape(N_EXPERTS, 1, dff), w_up,
      b_up.reshape(N_EXPERTS, 1, dff))


def _moe_down_kernel(ce_ref, cs_ref, cn_ref, tail_ref, act_hbm, wd_hbm, bd_ref, ys_hbm,
                     x_ref, wf_ref, yb_ref, zero_ref, pend_ref, sem, wsem, osem):
    c = pl.program_id(0)
    nsub = cn_ref[c]
    start = pl.multiple_of(cs_ref[c], SUB_ROWS)
    tn = wf_ref.shape[3]
    nf = bd_ref.shape[2] // tn
    ns = TOK_WORDS // LANES
    nw = tn // 2 // LANES
    chunk = [_chunk_dma(cs_ref, cn_ref, act_hbm, x_ref.at[i], 1, sem.at[i]) for i in range(2)]
    first_weights, next_weights = _weight_stream(ce_ref, cn_ref, nf, (wd_hbm,), tn, wf_ref, wsem)

    def on_parity(ci, fn):
        for p in range(2):
            @pl.when(ci % 2 == p)
            def _():
                fn(p)

    def out_copy(r, rows, par):
        src = yb_ref.at[par, pl.ds(pl.multiple_of(r * ns, SUB_ROWS * ns), rows * ns)]
        dst = ys_hbm.at[pl.ds(pl.multiple_of((start + r) * ns, SUB_ROWS * ns), rows * ns)]
        return pltpu.make_async_copy(src, dst, osem.at[par])

    def drain(par):
        for kind, rows in enumerate(SLOT_ROWS[-3:]):
            def wait(j, carry):
                out_copy(0, rows, par).wait()
                return carry

            lax.fori_loop(0, pend_ref[par * 3 + kind], wait, 0)
            pend_ref[par * 3 + kind] = 0

    @pl.when(c == 0)
    def _():
        zero_ref[...] = jnp.zeros_like(zero_ref)
        _zero_fill_rows(zero_ref, ys_hbm, tail_ref[0], tail_ref[1], osem.at[0])
        for kind in range(6):
            pend_ref[kind] = 0
        chunk[0][0](c)
        first_weights()

    def run_chunk():
        for f in range(nf):
            next_weights(c, f)
            bd = bd_ref[0, :, f * tn:(f + 1) * tn]
            last = f == nf - 1
            if f == 0:
                drain(c % 2)

            def compute(r, rows, slot, f=f, bd=bd):
                y = jnp.dot(x_ref[c % 2, pl.ds(r, rows), :], wf_ref[f % 2, 0].astype(BF16),
                            preferred_element_type=F32) + bd
                lo = pltpu.bitcast(y[:, :tn // 2].astype(BF16).astype(F32), jnp.uint32)
                hi = pltpu.bitcast(y[:, tn // 2:].astype(BF16).astype(F32), jnp.uint32)
                packed = (hi & jnp.uint32(0xFFFF0000)) | (lo >> 16)
                for s in range(nw):
                    yb_ref[c % 2, pl.ds(r * ns + f * nw + s, rows, stride=ns), :] = (
                        packed[:, s * LANES:(s + 1) * LANES])

            def commit(tiles, last=last):
                if last:
                    for r, rows, slot in tiles:
                        out_copy(r, rows, c % 2).start()
                        kind = (c % 2) * 3 + SLOT_ROWS[-3:].index(rows)
                        pend_ref[kind] = pend_ref[kind] + 1

            _tile_loop(nsub, lambda tiles: None, compute, commit)

    @pl.when(nsub > 0)
    def _():
        on_parity(c, lambda p: chunk[p][1](c))
        on_parity(c + 1, lambda p: chunk[p][0](c + 1))
        run_chunk()

    @pl.when(c == pl.num_programs(0) - 1)
    def _():
        drain(0)
        drain(1)


def _moe_down(ch_e, ch_start, ch_nsub, tail, act, w_down, b_down):
    p_alloc, dff = act.shape
    d = w_down.shape[2]
    ns = TOK_WORDS // LANES
    tn = MOE_TN
    assert (d // tn) % 2 == 0
    nch = ch_e.shape[0]
    return pl.pallas_call(
        _moe_down_kernel,
        grid_spec=pltpu.PrefetchScalarGridSpec(
            num_scalar_prefetch=4,
            grid=(nch,),
            in_specs=[
                pl.BlockSpec(memory_space=pl.ANY),
                pl.BlockSpec(memory_space=pl.ANY),
                pl.BlockSpec((1, 1, d), lambda c, e, s, n, t: (e[c], 0, 0)),
            ],
            out_specs=pl.BlockSpec(memory_space=pl.ANY),
            scratch_shapes=[
                pltpu.VMEM((2, MOE_TM, dff), BF16),
                pltpu.VMEM((2, 1, dff, tn), F32),
                pltpu.VMEM((2, MOE_TM * ns, LANES), jnp.uint32),
                pltpu.VMEM((SUB_ROWS * ns, LANES), jnp.uint32),
                pltpu.SMEM((6,), jnp.int32),
                pltpu.SemaphoreType.DMA((2,)),
                pltpu.SemaphoreType.DMA((2,)),
                pltpu.SemaphoreType.DMA((2,)),
            ],
        ),
        out_shape=jax.ShapeDtypeStruct((p_alloc * ns, LANES), jnp.uint32),
        compiler_params=pltpu.CompilerParams(dimension_semantics=("arbitrary",), vmem_limit_bytes=VMEM_LIMIT_MAX),
        name="moe_down",
    )(ch_e, ch_start, ch_nsub, tail, act, w_down, b_down.reshape(N_EXPERTS, 1, d))


def _combine_kernel(pos_hbm, ys_hbm, h_ref, w_ref, g_ref, o_ref, pos_smem, yb_ref, sem, psem):
    i = pl.program_id(0)
    tm, d = h_ref.shape
    n = TOP_K * tm
    ns = TOK_WORDS // LANES

    def pos_copy(tile):
        third = pl.multiple_of((tile % 3) * n, n)
        return pltpu.make_async_copy(pos_hbm.at[pl.ds(pl.multiple_of(tile * n, n), n)],
                                     pos_smem.at[pl.ds(third, n)], psem.at[tile % 3])

    def fetch(tile, slot):
        pos_copy(tile).wait()

        @pl.when(tile + 1 < pl.num_programs(0))
        def _():
            pos_copy(tile + 1).start()

        base = (tile % 3) * n

        def issue(jj, c):
            for u in range(ROW_DMA_UNROLL):
                j = jj * ROW_DMA_UNROLL + u
                src = ys_hbm.at[pl.ds(pl.multiple_of(pos_smem[base + j] * ns, ns), ns)]
                dst = yb_ref.at[slot, pl.ds(pl.multiple_of(j * ns, ns), ns)]
                pltpu.make_async_copy(src, dst, sem.at[slot]).start(priority=u % 2)
            return c

        lax.fori_loop(0, n // ROW_DMA_UNROLL, issue, 0)

    @pl.when(i == 0)
    def _():
        pos_copy(i).start()
        fetch(i, 0)

    @pl.when(i + 1 < pl.num_programs(0))
    def _():
        for p in range(2):
            @pl.when((i + 1) % 2 == p)
            def _():
                fetch(i + 1, p)

    def reduce(slot):
        pltpu.make_async_copy(ys_hbm.at[pl.ds(0, n * ns)], yb_ref.at[slot], sem.at[slot]).wait()
        wb = [jnp.broadcast_to(w_ref[:, k:k + 1], (tm, LANES)) for k in range(TOP_K)]
        ssq = jnp.zeros((tm, 1), F32)
        nw = MOE_TN // 2 // LANES
        for s in range(ns):
            c0 = (s // nw) * MOE_TN + (s % nw) * LANES
            lo_cols = slice(c0, c0 + LANES)
            hi_cols = slice(c0 + MOE_TN // 2, c0 + MOE_TN // 2 + LANES)
            acc_lo = h_ref[:, lo_cols]
            acc_hi = h_ref[:, hi_cols]
            for k in range(TOP_K):
                w = yb_ref[slot, pl.ds(k * tm * ns + s, tm, stride=ns), :]
                acc_lo = acc_lo + wb[k] * pltpu.bitcast(w << 16, F32)
                acc_hi = acc_hi + wb[k] * pltpu.bitcast(w & jnp.uint32(0xFFFF0000), F32)
            o_ref[:, lo_cols] = acc_lo
            o_ref[:, hi_cols] = acc_hi
            ssq = ssq + jnp.sum(acc_lo * acc_lo + acc_hi * acc_hi, axis=-1, keepdims=True)
        o_ref[...] = o_ref[...] * lax.rsqrt(ssq * (1.0 / d) + NORM_EPS) * g_ref[...]

    for p in range(2):
        @pl.when(i % 2 == p)
        def _():
            reduce(p)


def _combine(pos_tiles, ys, h1, wts_t, g_final, tm):
    rows, d = h1.shape
    ns = TOK_WORDS // LANES
    return pl.pallas_call(
        _combine_kernel,
        grid=(rows // tm,),
        in_specs=[
            pl.BlockSpec(memory_space=pl.ANY),
            pl.BlockSpec(memory_space=pl.ANY),
            pl.BlockSpec((tm, d), lambda i: (i, 0)),
            pl.BlockSpec((tm, TOP_K), lambda i: (i, 0)),
            pl.BlockSpec((1, d), lambda i: (0, 0)),
        ],
        out_specs=pl.BlockSpec((tm, d), lambda i: (i, 0)),
        out_shape=jax.ShapeDtypeStruct((rows, d), F32),
        scratch_shapes=[
            pltpu.SMEM((3 * TOP_K * tm,), jnp.int32),
            pltpu.VMEM((2, TOP_K * tm * ns, LANES), jnp.uint32),
            pltpu.SemaphoreType.DMA((2,)),
            pltpu.SemaphoreType.DMA((3,)),
        ],
        compiler_params=_cparams(("arbitrary",)),
        name="combine",
    )(pos_tiles, ys, h1, wts_t, g_final)


def _rope_tables(n_pos):
    half = HEAD_DIM // 2
    inv = 1.0 / (ROPE_THETA ** (jnp.arange(half, dtype=F32) / half))
    ang = jnp.arange(n_pos, dtype=F32)[:, None] * inv[None, :]
    cos = jnp.tile(jnp.cos(ang), (1, LANES // half))
    sin = jnp.tile(jnp.concatenate([-jnp.sin(ang), jnp.sin(ang)], axis=1), (1, LANES // HEAD_DIM))
    return cos, sin


def _block_diag(w):
    per = LRU_GROUP // LRU_BLOCK
    w4 = w.reshape(D_LRU // LRU_GROUP, per, LRU_BLOCK, LRU_BLOCK)
    eye = jnp.eye(per, dtype=w.dtype)
    bd = jnp.einsum("gpcd,pq->gpcqd", w4, eye)
    return bd.reshape(D_LRU // LRU_GROUP, LRU_GROUP, LRU_GROUP).astype(BF16)


def _tile_positions(pos, tm):
    rows = pos.shape[1]
    return pos.reshape(TOP_K, rows // tm, tm).transpose(1, 0, 2).reshape(-1)


def _chunk_schedule(counts, n_rows):
    aligned = ((counts + SUB_ROWS - 1) // SUB_ROWS) * SUB_ROWS
    offs = jnp.cumsum(aligned) - aligned
    n_ch = (aligned + MOE_TM - 1) // MOE_TM
    cum = jnp.cumsum(n_ch)
    total = cum[-1]
    nch_max = N_EXPERTS + (n_rows + N_EXPERTS * SUB_ROWS) // MOE_TM
    c = jnp.arange(nch_max, dtype=jnp.int32)
    cc = jnp.minimum(c, total - 1)
    e = jnp.sum(cc[:, None] >= cum[None, :], axis=1).astype(jnp.int32)
    is_e = e[:, None] == jnp.arange(N_EXPERTS, dtype=jnp.int32)[None, :]
    of_e = lambda v: jnp.sum(jnp.where(is_e, v[None, :], 0), axis=1)
    j = cc - (of_e(cum) - of_e(n_ch))
    start = of_e(offs) + j * MOE_TM
    nsub = jnp.where(c < total, jnp.minimum(MOE_TM, of_e(aligned) - j * MOE_TM) // SUB_ROWS, 0)
    tails = offs + (counts // SUB_ROWS) * SUB_ROWS
    used = jnp.sum(aligned)
    tail = jnp.stack([used, (n_rows + N_EXPERTS * SUB_ROWS - used) // SUB_ROWS])
    tails = jnp.concatenate([tails, tail])
    return (offs, e, start.astype(jnp.int32), nsub.astype(jnp.int32), tails.astype(jnp.int32),
            tail.astype(jnp.int32))


def kernel(x, meta_tokens, norm_mix, w_in, b_in, sinks, conv_w, conv_b, w_a, b_a, w_i, b_i, lru_lambda,
           g_attn_out, g_lru_out, w_out, b_out, norm_ffn, w_router, b_router, w_gate, b_gate, w_up, b_up,
           w_down, b_down, final_norm):
    batch, seq, d = x.shape
    rows = batch * seq
    x2d = x.reshape(rows, d)
    row = lambda v: v.reshape(1, -1)

    w_in_bf = w_in[0].astype(BF16)
    w_out_bf = w_out[0].astype(BF16)
    cos, sin = _rope_tables(N_META + seq)
    wa_bd, wi_bd = _block_diag(w_a[0]), _block_diag(w_i[0])
    lru_args = (conv_w[0], row(conv_b[0]), wa_bd, row(b_a[0]), wi_bd, row(b_i[0]), row(lru_lambda[0]),
                row(g_lru_out[0]))

    _, kvm, xgm = _inproj(meta_tokens, row(norm_mix[0]), w_in_bf, row(b_in[0]), cos[:N_META], sin[:N_META],
                          N_META, 1)
    _, h0, tail0 = _lru(xgm, *lru_args, jnp.zeros((1, D_LRU), F32), jnp.zeros((8, D_LRU), F32), 1, N_META,
                        N_META)

    tm = ROW_TILE
    q, kv, xg = _inproj(x2d, row(norm_mix[0]), w_in_bf, row(b_in[0]), cos[N_META:], sin[N_META:], tm,
                        seq // tm)
    kvm = jnp.pad(kvm, ((0, BLOCK - N_META), (0, 0)))
    attn_n = _attention(sinks[0], q, kv, kvm, row(g_attn_out[0]), batch, seq)
    lru_n, _, _ = _lru(xg, *lru_args, h0, tail0, batch, seq, LRU_TIME_TILE)

    tri = (jnp.arange(tm)[:, None] <= jnp.arange(tm)[None, :]).astype(BF16)
    h1, xp, ids, wts, rank, cnt = _outproj(
        attn_n, lru_n, w_out_bf, row(b_out[0]), x2d, row(norm_ffn[0]), w_router[0].T.astype(BF16),
        b_router[0].reshape(N_EXPERTS, 1), tri, tm)

    counts = cnt[:, 0].astype(jnp.int32)
    n_rows = rows * TOP_K
    offs, ch_e, ch_start, ch_nsub, tails, tail = _chunk_schedule(counts, n_rows)
    onehot = ids[..., None] == jnp.arange(N_EXPERTS, dtype=jnp.int32)
    pos = jnp.sum(jnp.where(onehot, offs, 0), axis=-1) + rank
    p_alloc = n_rows + N_EXPERTS * SUB_ROWS

    xs = _dispatch(tails, _tile_positions(pos, ROW_TILE), xp, p_alloc, ROW_TILE)
    act = _moe_up(ch_e, ch_start, ch_nsub, tail, xs, w_gate[0], b_gate[0], w_up[0], b_up[0])
    ys = _moe_down(ch_e, ch_start, ch_nsub, tail, act, w_down[0], b_down[0])
    out = _combine(_tile_positions(pos, COMBINE_TILE), ys, h1, wts.T, row(final_norm), COMBINE_TILE)
    return out.reshape(batch, seq, d)
```
